```python
import math
import jax, jax.numpy as jnp
from jax import lax
import numpy as np

D_MODEL = 2048
BATCH = 8
SEQ = 2048
DEPTH = 2

MEM_LEN = 256
EPS = 1e-6
NEG_INF = -1e30
Q_BLOCK = 128

FOX_HEADS = 8
FOX_HEAD_DIM = 128
FORGET_BIAS_OFFSET = 2.0

MLA_HEADS = 8
MLA_NOPE_DIM = 128
MLA_ROPE_DIM = 64
MLA_V_DIM = 128
MLA_Q_RANK = 768
MLA_KV_RANK = 512
ROPE_THETA = 10000.0

NSA_HEADS = 8
NSA_GROUPS = 2
NSA_HPG = NSA_HEADS // NSA_GROUPS
NSA_DK = 192
NSA_DV = 128
CMP_BLOCK = 32
CMP_STRIDE = 16
SEL_BLOCK = 64
N_SEL = 8
WINDOW = 512
SEL_FORCE = 1e6

REL_BUCKETS = 32
REL_MAX_DIST = 128

N_BRANCH = 3
BRANCH_W = FOX_HEADS * FOX_HEAD_DIM

MEM_HEADS = 4
MEM_HEAD_DIM = 128

N_GROUPS = 4
EXPERTS_PER_GROUP = 8
N_EXPERTS = N_GROUPS * EXPERTS_PER_GROUP
TOP_K_IN_GROUP = 2
D_EXPERT = 1024
MOE_ROW_BLOCK = 128

IN_SPLITS = (
    FOX_HEADS * FOX_HEAD_DIM,
    FOX_HEADS * FOX_HEAD_DIM,
    FOX_HEADS * FOX_HEAD_DIM,
    FOX_HEADS,
    MLA_Q_RANK,
    MLA_KV_RANK,
    MLA_ROPE_DIM,
    NSA_HEADS * NSA_DK,
    NSA_GROUPS * NSA_DK,
    NSA_GROUPS * NSA_DV,
    NSA_GROUPS * NSA_DK,
    NSA_GROUPS * NSA_DV,
    NSA_GROUPS * NSA_DK,
    NSA_GROUPS * NSA_DV,
    NSA_HEADS * 3,
    N_BRANCH * D_MODEL,
)
D_IN = sum(IN_SPLITS)

kernel_name = "hybrid_fox_mla_nsa_hmoe_block"


def rmsnorm(x, g):
    xf = x.astype(jnp.float32)
    y = xf * lax.rsqrt(jnp.mean(xf * xf, axis=-1, keepdims=True) + EPS)
    return (y * g.astype(jnp.float32)).astype(x.dtype)


def t5_bucket(dist):
    dist = jnp.maximum(dist, 0)
    exact = REL_BUCKETS // 2
    df = jnp.maximum(dist, 1).astype(jnp.float32)
    large = exact + (jnp.log(df / exact) / math.log(REL_MAX_DIST / exact) * (REL_BUCKETS - exact)).astype(jnp.int32)
    large = jnp.minimum(large, REL_BUCKETS - 1)
    return jnp.where(dist < exact, dist, large)


def rope(x, pos):
    half = x.shape[-1] // 2
    inv = ROPE_THETA ** (-jnp.arange(half, dtype=jnp.float32) / half)
    ang = (pos.astype(jnp.float32)[:, None] * inv).reshape((pos.shape[0],) + (1,) * (x.ndim - 3) + (half,))
    c, s = jnp.cos(ang), jnp.sin(ang)
    xf = x.astype(jnp.float32)
    x1, x2 = xf[..., :half], xf[..., half:]
    return jnp.concatenate([x1 * c - x2 * s, x2 * c + x1 * s], axis=-1).astype(x.dtype)


def causal_block_attention(q, k, v, scale, log_decay=None):
    S = q.shape[1]
    outs = []
    for i in range(S // Q_BLOCK):
        q0, q1 = i * Q_BLOCK, (i + 1) * Q_BLOCK
        s = jnp.einsum("bqhd,bkhd->bhqk", q[:, q0:q1], k[:, :q1], preferred_element_type=jnp.float32) * scale
        if log_decay is not None:
            fq = jnp.transpose(log_decay[:, q0:q1], (0, 2, 1))[:, :, :, None]
            fk = jnp.transpose(log_decay[:, :q1], (0, 2, 1))[:, :, None, :]
            s = s + (fq - fk)
        mask = jnp.arange(q0, q1)[:, None] >= jnp.arange(q1)[None, :]
        s = jnp.where(mask, s, NEG_INF)
        p = jax.nn.softmax(s, axis=-1).astype(v.dtype)
        outs.append(jnp.einsum("bhqk,bkhd->bqhd", p, v[:, :q1]))
    return jnp.concatenate(outs, axis=1)


def fox_mixer(zq, zk, zv, zf, b_f):
    B, S, _ = zq.shape
    q = zq.reshape(B, S, FOX_HEADS, FOX_HEAD_DIM)
    k = zk.reshape(B, S, FOX_HEADS, FOX_HEAD_DIM)
    v = zv.reshape(B, S, FOX_HEADS, FOX_HEAD_DIM)
    log_f = jax.nn.log_sigmoid(zf.astype(jnp.float32) + b_f.astype(jnp.float32))
    cum = jnp.cumsum(log_f, axis=1)
    o = causal_block_attention(q, k, v, FOX_HEAD_DIM ** -0.5, cum)
    return o.reshape(B, S, FOX_HEADS * FOX_HEAD_DIM)


def mla_mixer(zcq, zckv, zkr, g_cq, g_ckv, w_uq, w_ukv, pos):
    B, S, _ = zcq.shape
    q = (rmsnorm(zcq, g_cq) @ w_uq).reshape(B, S, MLA_HEADS, MLA_NOPE_DIM + MLA_ROPE_DIM)
    kv = (rmsnorm(zckv, g_ckv) @ w_ukv).reshape(B, S, MLA_HEADS, MLA_NOPE_DIM + MLA_V_DIM)
    q_nope, q_rope = q[..., :MLA_NOPE_DIM], q[..., MLA_NOPE_DIM:]
    k_nope, v = kv[..., :MLA_NOPE_DIM], kv[..., MLA_NOPE_DIM:]
    k_rope = rope(zkr, pos)
    q = jnp.concatenate([q_nope, rope(q_rope, pos)], axis=-1)
    k = jnp.concatenate([k_nope, jnp.broadcast_to(k_rope[:, :, None, :], (B, S, MLA_HEADS, MLA_ROPE_DIM))], axis=-1)
    o = causal_block_attention(q, k, v, (MLA_NOPE_DIM + MLA_ROPE_DIM) ** -0.5)
    return o.reshape(B, S, MLA_HEADS * MLA_V_DIM)


def nsa_mixer(zq, zkc, zvc, zks, zvs, zkw, zvw, zg, pe_k, pe_v, w_ck1, w_ck2, w_cv1, w_cv2, rel_bias):
    B, S, _ = zq.shape
    G, Hg = NSA_GROUPS, NSA_HPG
    dt = zq.dtype
    scale = NSA_DK ** -0.5
    q = zq.reshape(B, S, G, Hg, NSA_DK)
    kc_raw = zkc.reshape(B, S, G, NSA_DK)
    vc_raw = zvc.reshape(B, S, G, NSA_DV)
    ks = zks.reshape(B, S, G, NSA_DK)
    vs = zvs.reshape(B, S, G, NSA_DV)
    kw = zkw.reshape(B, S, G, NSA_DK)
    vw = zvw.reshape(B, S, G, NSA_DV)
    t = jnp.arange(S)
    tab = jnp.transpose(rel_bias.reshape(REL_BUCKETS, G, Hg), (1, 0, 2))

    n_cmp = (S - CMP_BLOCK) // CMP_STRIDE + 1
    cidx = CMP_STRIDE * jnp.arange(n_cmp)[:, None] + jnp.arange(CMP_BLOCK)[None, :]

    def compress(raw, pe, w1, w2):
        d = raw.shape[-1]
        blk = raw[:, cidx] + pe[:, None, :]
        blk = jnp.transpose(blk, (0, 1, 3, 2, 4)).reshape(B, n_cmp, G, CMP_BLOCK * d)
        return jax.nn.gelu(blk @ w1) @ w2

    kc = compress(kc_raw, pe_k, w_ck1, w_ck2)
    vc = compress(vc_raw, pe_v, w_cv1, w_cv2)
    dist_c = t[:, None] - cidx[:, -1][None, :]
    valid_c = dist_c >= 0
    bias_c = jnp.transpose(tab[:, t5_bucket(dist_c)], (0, 3, 1, 2))
    s = jnp.einsum("bsghd,bcgd->bghsc", q, kc, preferred_element_type=jnp.float32) * scale + bias_c[None]
    s = jnp.where(valid_c, s, NEG_INF)
    p_c = jnp.where(valid_c, jax.nn.softmax(s, axis=-1), 0.0)
    o_cmp = jnp.einsum("bghsc,bcgd->bsghd", p_c.astype(dt), vc)

    n_blk = S // SEL_BLOCK
    sstart = SEL_BLOCK * jnp.arange(n_blk)
    overlap = jnp.clip(jnp.minimum(cidx[:, -1][:, None] + 1, sstart[None, :] + SEL_BLOCK)
                       - jnp.maximum(cidx[:, 0][:, None], sstart[None, :]), 0, None).astype(jnp.float32) / CMP_STRIDE
    imp = jnp.einsum("bghsc,cn->bgsn", p_c, overlap)
    cur = t // SEL_BLOCK
    blk_id = jnp.arange(n_blk)
    valid_b = blk_id[None, :] <= cur[:, None]
    forced = (blk_id[None, :] == 0) | (blk_id[None, :] == cur[:, None]) | (blk_id[None, :] == cur[:, None] - 1)
    score = jnp.where(valid_b, imp + jnp.where(forced, SEL_FORCE, 0.0), NEG_INF)
    n_sel = min(N_SEL, n_blk)
    _, sel = lax.top_k(score, n_sel)
    sel = jnp.transpose(sel, (0, 2, 1, 3))

    kw_pad = jnp.pad(kw, ((0, 0), (WINDOW, 0), (0, 0), (0, 0)))
    vw_pad = jnp.pad(vw, ((0, 0), (WINDOW, 0), (0, 0), (0, 0)))
    b_ix = jnp.arange(B)[:, None, None, None]
    g_ix = jnp.arange(G)[None, None, :, None]

    def block(i):
        q0 = i * Q_BLOCK
        qb = lax.dynamic_slice_in_dim(q, q0, Q_BLOCK, axis=1)
        tq = q0 + jnp.arange(Q_BLOCK)
        sb = lax.dynamic_slice_in_dim(sel, q0, Q_BLOCK, axis=1)
        tok = (sb[..., None] * SEL_BLOCK + jnp.arange(SEL_BLOCK)).reshape(B, Q_BLOCK, G, n_sel * SEL_BLOCK)
        k_g = ks[b_ix, tok, g_ix]
        v_g = vs[b_ix, tok, g_ix]
        dist = tq[None, :, None, None] - tok
        bias = jnp.transpose(tab[g_ix, t5_bucket(dist)], (0, 1, 2, 4, 3))
        s1 = jnp.einsum("bqghd,bqgkd->bqghk", qb, k_g, preferred_element_type=jnp.float32) * scale + bias
        s1 = jnp.where((dist >= 0)[:, :, :, None, :], s1, NEG_INF)
        o_s = jnp.einsum("bqghk,bqgkd->bqghd", jax.nn.softmax(s1, axis=-1).astype(dt), v_g)
        kwb = lax.dynamic_slice_in_dim(kw_pad, q0, Q_BLOCK + WINDOW, axis=1)
        vwb = lax.dynamic_slice_in_dim(vw_pad, q0, Q_BLOCK + WINDOW, axis=1)
        sk = q0 - WINDOW + jnp.arange(Q_BLOCK + WINDOW)
        dw = tq[:, None] - sk[None, :]
        wmask = (dw >= 0) & (dw < WINDOW) & (sk[None, :] >= 0)
        bias_w = jnp.transpose(tab[:, t5_bucket(dw)], (1, 0, 3, 2))
        s2 = jnp.einsum("bqghd,bkgd->bqghk", qb, kwb, preferred_element_type=jnp.float32) * scale + bias_w[None]
        s2 = jnp.where(wmask[:, None, None, :], s2, NEG_INF)
        o_w = jnp.einsum("bqghk,bkgd->bqghd", jax.nn.softmax(s2, axis=-1).astype(dt), vwb)
        return o_s, o_w

    o_slc, o_win = lax.map(block, jnp.arange(S // Q_BLOCK))
    o_slc = jnp.transpose(o_slc, (1, 0, 2, 3, 4, 5)).reshape(B, S, G, Hg, NSA_DV)
    o_win = jnp.transpose(o_win, (1, 0, 2, 3, 4, 5)).reshape(B, S, G, Hg, NSA_DV)
    gates = jax.nn.sigmoid(zg.astype(jnp.float32)).reshape(B, S, G, Hg, 3).astype(dt)
    o = gates[..., 0:1] * o_cmp + gates[..., 1:2] * o_slc + gates[..., 2:3] * o_win
    return o.reshape(B, S, NSA_HEADS * NSA_DV)


def memory_attention(u, mem_n, w_q, w_kv, w_o):
    B, S, _ = u.shape
    M = mem_n.shape[1]
    q = (u @ w_q).reshape(B, S, MEM_HEADS, MEM_HEAD_DIM)
    kv = (mem_n @ w_kv).reshape(B, M, 2, MEM_HEADS, MEM_HEAD_DIM)
    s = jnp.einsum("bshd,bmhd->bhsm", q, kv[:, :, 0], preferred_element_type=jnp.float32) * MEM_HEAD_DIM ** -0.5
    p = jax.nn.softmax(s, axis=-1).astype(u.dtype)
    o = jnp.einsum("bhsm,bmhd->bshd", p, kv[:, :, 1]).reshape(B, S, MEM_HEADS * MEM_HEAD_DIM)
    return o @ w_o


def hier_moe(u, w_rg, b_rg, w_re, b_re, w_g, w_u, w_d):
    B, S, D = u.shape
    T = B * S
    xt = u.reshape(T, D)
    glog = (xt @ w_rg).astype(jnp.float32) + b_rg.astype(jnp.float32)
    gprob = jax.nn.softmax(glog, axis=-1)
    gsel = jnp.argmax(glog, axis=-1).astype(jnp.int32)
    pg = jnp.max(gprob, axis=-1, keepdims=True)
    elog = ((xt @ w_re).astype(jnp.float32) + b_re.astype(jnp.float32)).reshape(T, N_GROUPS, EXPERTS_PER_GROUP)
    elog = jnp.take_along_axis(elog, gsel[:, None, None], axis=1)[:, 0]
    top_p, top_j = lax.top_k(jax.nn.softmax(elog, axis=-1), TOP_K_IN_GROUP)
    top_p = top_p / jnp.sum(top_p, axis=-1, keepdims=True)
    weight = (pg * top_p).astype(u.dtype)
    eid = gsel[:, None] * EXPERTS_PER_GROUP + top_j.astype(jnp.int32)
    TK = T * TOP_K_IN_GROUP
    flat_e = eid.reshape(-1)
    flat_tok = jnp.repeat(jnp.arange(T, dtype=jnp.int32), TOP_K_IN_GROUP)
    flat_w = weight.reshape(-1)
    order = jnp.argsort(flat_e)
    sorted_e = flat_e[order]
    counts = jnp.bincount(flat_e, length=N_EXPERTS)
    starts = jnp.cumsum(counts) - counts
    pcounts = ((counts + MOE_ROW_BLOCK - 1) // MOE_ROW_BLOCK) * MOE_ROW_BLOCK
    pends = jnp.cumsum(pcounts)
    pstarts = pends - pcounts
    dest = pstarts[sorted_e] + (jnp.arange(TK) - starts[sorted_e])
    P = TK + N_EXPERTS * MOE_ROW_BLOCK
    n_rb = P // MOE_ROW_BLOCK
    row_tok = jnp.full((P,), T, jnp.int32).at[dest].set(flat_tok[order])
    row_w = jnp.zeros((P,), u.dtype).at[dest].set(flat_w[order])
    blk_e = jnp.clip(jnp.searchsorted(pends, jnp.arange(n_rb) * MOE_ROW_BLOCK, side="right"), 0, N_EXPERTS - 1)
    x_pad = jnp.concatenate([xt, jnp.zeros((1, D), xt.dtype)], axis=0)
    xr = x_pad[row_tok].reshape(n_rb, MOE_ROW_BLOCK, D)

    def expert_block(args):
        xb, e = args
        hdn = jax.nn.silu(xb @ w_g[e]) * (xb @ w_u[e])
        return hdn @ w_d[e]

    y = lax.map(expert_block, (xr, blk_e)).reshape(P, D)
    out = jnp.zeros((T + 1, D), u.dtype).at[row_tok].add(y * row_w[:, None])[:T]
    return out.reshape(B, S, D)


def setup_inputs(seed: int = 0) -> dict:
    key = jax.random.key(seed)
    ks = jax.random.split(key, 32)
    L = DEPTH

    def nrm(k, shape, scale):
        return jax.random.normal(k, shape, jnp.float32) * scale

    def gain(k, shape):
        return 1.0 + nrm(k, shape, 0.01)

    return {
        "x": nrm(ks[0], (BATCH, SEQ, D_MODEL), 1.0),
        "mem": nrm(ks[1], (BATCH, MEM_LEN, D_MODEL), 1.0),
        "g_mix": gain(ks[2], (L, D_MODEL)),
        "w_in": nrm(ks[3], (L, D_MODEL, D_IN), D_MODEL ** -0.5),
        "b_forget": FORGET_BIAS_OFFSET + nrm(ks[4], (L, FOX_HEADS), 0.1),
        "g_cq": gain(ks[5], (L, MLA_Q_RANK)),
        "g_ckv": gain(ks[6], (L, MLA_KV_RANK)),
        "w_uq": nrm(ks[7], (L, MLA_Q_RANK, MLA_HEADS * (MLA_NOPE_DIM + MLA_ROPE_DIM)), MLA_Q_RANK ** -0.5),
        "w_ukv": nrm(ks[8], (L, MLA_KV_RANK, MLA_HEADS * (MLA_NOPE_DIM + MLA_V_DIM)), MLA_KV_RANK ** -0.5),
        "pe_k": nrm(ks[9], (L, CMP_BLOCK, NSA_DK), 0.1),
        "pe_v": nrm(ks[10], (L, CMP_BLOCK, NSA_DV), 0.1),
        "w_cmp_k1": nrm(ks[11], (L, CMP_BLOCK * NSA_DK, NSA_DK), (CMP_BLOCK * NSA_DK) ** -0.5),
        "w_cmp_k2": nrm(ks[12], (L, NSA_DK, NSA_DK), NSA_DK ** -0.5),
        "w_cmp_v1": nrm(ks[13], (L, CMP_BLOCK * NSA_DV, NSA_DV), (CMP_BLOCK * NSA_DV) ** -0.5),
        "w_cmp_v2": nrm(ks[14], (L, NSA_DV, NSA_DV), NSA_DV ** -0.5),
        "rel_bias": nrm(ks[15], (REL_BUCKETS, NSA_HEADS), 0.5),
        "w_branch": nrm(ks[16], (L, N_BRANCH, BRANCH_W, D_MODEL), BRANCH_W ** -0.5),
        "w_out": nrm(ks[17], (L, D_MODEL, D_MODEL), D_MODEL ** -0.5),
        "g_mem_q": gain(ks[18], (L, D_MODEL)),
        "g_mem_kv": gain(ks[19], (L, D_MODEL)),
        "w_mem_q": nrm(ks[20], (L, D_MODEL, MEM_HEADS * MEM_HEAD_DIM), D_MODEL ** -0.5),
        "w_mem_kv": nrm(ks[21], (L, D_MODEL, 2 * MEM_HEADS * MEM_HEAD_DIM), D_MODEL ** -0.5),
        "w_mem_o": nrm(ks[22], (L, MEM_HEADS * MEM_HEAD_DIM, D_MODEL), (MEM_HEADS * MEM_HEAD_DIM) ** -0.5),
        "g_moe": gain(ks[23], (L, D_MODEL)),
        "w_router_group": nrm(ks[24], (L, D_MODEL, N_GROUPS), D_MODEL ** -0.5),
        "b_router_group": nrm(ks[25], (L, N_GROUPS), 0.01),
        "w_router_expert": nrm(ks[26], (L, D_MODEL, N_EXPERTS), D_MODEL ** -0.5),
        "b_router_expert": nrm(ks[27], (L, N_EXPERTS), 0.01),
        "w_exp_gate": nrm(ks[28], (L, N_EXPERTS, D_MODEL, D_EXPERT), D_MODEL ** -0.5),
        "w_exp_up": nrm(ks[29], (L, N_EXPERTS, D_MODEL, D_EXPERT), D_MODEL ** -0.5),
        "w_exp_down": nrm(ks[30], (L, N_EXPERTS, D_EXPERT, D_MODEL), D_EXPERT ** -0.5),
        "g_final": gain(ks[31], (D_MODEL,)),
    }


def reference(x, mem, g_mix, w_in, b_forget, g_cq, g_ckv, w_uq, w_ukv, pe_k, pe_v, w_cmp_k1, w_cmp_k2,
              w_cmp_v1, w_cmp_v2, rel_bias, w_branch, w_out, g_mem_q, g_mem_kv, w_mem_q, w_mem_kv, w_mem_o,
              g_moe, w_router_group, b_router_group, w_router_expert, b_router_expert, w_exp_gate, w_exp_up,
              w_exp_down, g_final):
    B, S, D = x.shape
    pos = jnp.arange(S)
    offs = np.cumsum(IN_SPLITS)[:-1].tolist()
    h = x
    for l in range(DEPTH):
        u = rmsnorm(h, g_mix[l])
        z = u @ w_in[l]
        (fq, fk, fv, ff, mcq, mckv, mkr, nq, nkc, nvc, nks, nvs, nkw, nvw, ngt, mgate) = jnp.split(z, offs, axis=-1)
        o_fox = fox_mixer(fq, fk, fv, ff, b_forget[l])
        o_mla = mla_mixer(mcq, mckv, mkr, g_cq[l], g_ckv[l], w_uq[l], w_ukv[l], pos)
        o_nsa = nsa_mixer(nq, nkc, nvc, nks, nvs, nkw, nvw, ngt, pe_k[l], pe_v[l], w_cmp_k1[l], w_cmp_k2[l],
                          w_cmp_v1[l], w_cmp_v2[l], rel_bias)
        o = jnp.stack([o_fox, o_mla, o_nsa], axis=2)
        y = jnp.einsum("bsnc,ncd->bsnd", o, w_branch[l])
        gts = jax.nn.sigmoid(mgate.reshape(B, S, N_BRANCH, D))
        h = h + jnp.sum(gts * y, axis=2) @ w_out[l]
        h = h + memory_attention(rmsnorm(h, g_mem_q[l]), rmsnorm(mem, g_mem_kv[l]), w_mem_q[l], w_mem_kv[l], w_mem_o[l])
        h = h + hier_moe(rmsnorm(h, g_moe[l]), w_router_group[l], b_router_group[l], w_router_expert[l],
                         b_router_expert[l], w_exp_gate[l], w_exp_up[l], w_exp_down[l])
    return rmsnorm(h, g_final)
```

```python
import functools
import math

import jax
import jax.numpy as jnp
import numpy as np
from jax import lax
from jax.experimental import pallas as pl
from jax.experimental.pallas import tpu as pltpu

F32 = jnp.float32
BF16 = jnp.bfloat16

EPS = 1e-6
NEG_INF = -1e30
LANE = 128

FOX_HEADS, FOX_DH = 8, 128
MLA_HEADS, MLA_NOPE, MLA_ROPE, MLA_DV = 8, 128, 64, 128
MLA_Q_RANK, MLA_KV_RANK = 768, 512
ROPE_THETA = 10000.0
NSA_HEADS, NSA_GROUPS, NSA_DK, NSA_DV = 8, 2, 192, 128
NSA_HPG = NSA_HEADS // NSA_GROUPS
NSA_DKP = 256
CMP_BLOCK, CMP_STRIDE, SEL_BLOCK, N_SEL, WINDOW = 32, 16, 64, 8, 512
SEL_FORCE = 1e6
REL_BUCKETS, REL_MAX_DIST = 32, 128
N_BRANCH, BRANCH_W = 3, 1024
MEM_HEADS, MEM_DH = 4, 128
N_GROUPS, EXPERTS_PER_GROUP, TOP_K = 4, 8, 2
N_EXPERTS = N_GROUPS * EXPERTS_PER_GROUP
FORGET_COLS, NSA_GATE_COLS = FOX_HEADS, NSA_HEADS * 3

CB_CKV, CB_KR, CB_CQ = 0, 4, 6
CB_FQ, CB_FK, CB_FV = 12, 20, 28
CB_NKC, CB_NQ, CB_NKS, CB_NKW = 36, 40, 56, 60
CB_NVC, CB_NVS, CB_NVW = 64, 66, 68
CB_MG = 72
Z_BLOCKS = 120

ATT_T = 256
NSA_T = 128
MOE_TM = 256
VMEM_LIMIT = 56 * 1024 * 1024


def _cparams(sem):
    return pltpu.CompilerParams(dimension_semantics=sem, vmem_limit_bytes=VMEM_LIMIT)


def _sigmoid(x):
    return 1.0 / (1.0 + jnp.exp(-x))


def _rms_mm_kernel(x_ref, g_ref, w_ref, o_ref, u_ref, *, precise):
    @pl.when(pl.program_id(1) == 0)
    def _():
        x = x_ref[...].astype(F32)
        ms = jnp.mean(x * x, axis=-1, keepdims=True)
        u_ref[...] = (x * lax.rsqrt(ms + EPS) * g_ref[...]).astype(u_ref.dtype)

    prec = lax.Precision.HIGHEST if precise else None
    o_ref[...] = jnp.dot(u_ref[...], w_ref[...], preferred_element_type=F32, precision=prec).astype(o_ref.dtype)


def rms_matmul(x, xcol, g, w, out_dtype, tm, tn, precise=False):
    T = x.shape[0]
    K, N = w.shape
    return pl.pallas_call(
        functools.partial(_rms_mm_kernel, precise=precise),
        out_shape=jax.ShapeDtypeStruct((T, N), out_dtype),
        grid=(T // tm, N // tn),
        in_specs=[pl.BlockSpec((tm, K), lambda i, j: (i, xcol)),
                  pl.BlockSpec((1, K), lambda i, j: (0, 0)),
                  pl.BlockSpec((K, tn), lambda i, j: (0, j))],
        out_specs=pl.BlockSpec((tm, tn), lambda i, j: (i, j)),
        scratch_shapes=[pltpu.VMEM((tm, K), F32 if precise else BF16)],
        compiler_params=_cparams(("parallel", "arbitrary")),
        name="rms_matmul",
    )(x, g.reshape(1, K).astype(F32), w)


def _mla_q_kernel(x_ref, g_ref, w_ref, cos_ref, sin_ref, qn_ref, qr_ref):
    x = x_ref[...].astype(F32)
    ms = jnp.mean(x * x, axis=-1, keepdims=True)
    u = (x * lax.rsqrt(ms + EPS) * g_ref[...]).astype(BF16)
    y = jnp.dot(u, w_ref[...], preferred_element_type=F32)
    hw = MLA_HEADS * LANE
    qn_ref[...] = y[:, :hw].astype(qn_ref.dtype)
    cos = cos_ref[...]
    sin = sin_ref[...]
    for h in range(MLA_HEADS):
        a = y[:, hw + h * LANE: hw + (h + 1) * LANE]
        b = y[:, 2 * hw + h * LANE: 2 * hw + (h + 1) * LANE]
        qr_ref[:, h * LANE:(h + 1) * LANE] = (a * cos + b * sin).astype(qr_ref.dtype)


def mla_q_proj(z, g, w, cos, sin, S, tm):
    T = z.shape[0]
    K, N = w.shape
    hw = MLA_HEADS * LANE
    nsb = S // tm
    return pl.pallas_call(
        _mla_q_kernel,
        out_shape=(jax.ShapeDtypeStruct((T, hw), BF16), jax.ShapeDtypeStruct((T, hw), BF16)),
        grid=(T // tm,),
        in_specs=[pl.BlockSpec((tm, K), lambda i: (i, CB_CQ * LANE // MLA_Q_RANK)),
                  pl.BlockSpec((1, K), lambda i: (0, 0)),
                  pl.BlockSpec((K, N), lambda i: (0, 0)),
                  pl.BlockSpec((tm, LANE), lambda i: (i % nsb, 0)),
                  pl.BlockSpec((tm, LANE), lambda i: (i % nsb, 0))],
        out_specs=(pl.BlockSpec((tm, hw), lambda i: (i, 0)), pl.BlockSpec((tm, hw), lambda i: (i, 0))),
        compiler_params=_cparams(("parallel",)),
        name="mla_q_proj",
    )(z, g.reshape(1, K).astype(F32), w, cos, sin)


def _causal_attn_kernel(*refs, t, scale, two_part, decay):
    refs = list(refs)
    q_ref, k_ref, v_ref = refs[:3]
    pos = 3
    if two_part:
        q2_ref, k2_ref = refs[pos:pos + 2]
        pos += 2
    if decay:
        cq_ref, ck_ref = refs[pos:pos + 2]
        pos += 2
    o_ref = refs[pos]
    qi = pl.program_id(2)
    q = q_ref[0]
    dn = (((1,), (1,)), ((), ()))

    def scores(kb, masked):
        off = pl.multiple_of(kb * t, t)
        s = lax.dot_general(q, k_ref[0, pl.ds(off, t), :], dn, preferred_element_type=F32)
        if two_part:
            s = s + lax.dot_general(q2_ref[0], k2_ref[0, pl.ds(off, t), :], dn, preferred_element_type=F32)
        s = s * scale
        if decay:
            s = s + (cq_ref[0, 0] - ck_ref[0, 0, :, pl.ds(off, t)])
        if masked:
            r = lax.broadcasted_iota(jnp.int32, (t, t), 0)
            c = lax.broadcasted_iota(jnp.int32, (t, t), 1)
            s = jnp.where(r >= c, s, NEG_INF)
        return s, v_ref[0, pl.ds(off, t), :]

    def update(carry, sv):
        m, l, acc = carry
        s, v = sv
        m_new = jnp.maximum(m, jnp.max(s, axis=-1, keepdims=True))
        a = jnp.exp(m - m_new)
        p = jnp.exp(s - m_new)
        l = a * l + jnp.sum(p, axis=-1, keepdims=True)
        acc = a * acc + jnp.dot(p.astype(BF16), v, preferred_element_type=F32)
        return m_new, l, acc

    dv = v_ref.shape[-1]
    init = (jnp.full((t, 1), NEG_INF, F32), jnp.zeros((t, 1), F32), jnp.zeros((t, dv), F32))
    carry = lax.fori_loop(0, qi, lambda kb, c: update(c, scores(kb, False)), init)
    _, l, acc = update(carry, scores(qi, True))
    o_ref[0] = (acc / l).astype(o_ref.dtype)


def causal_attention(q, qcb, k, kcb, v, vcb, heads, scale, q2=None, k2=None, cum=None):
    B, S, _ = q.shape
    t = ATT_T
    two_part, decay = q2 is not None, cum is not None
    in_specs = [pl.BlockSpec((1, t, LANE), lambda b, h, i: (b, i, qcb + h)),
                pl.BlockSpec((1, S, LANE), lambda b, h, i: (b, 0, kcb + h)),
                pl.BlockSpec((1, S, LANE), lambda b, h, i: (b, 0, vcb + h))]
    args = [q, k, v]
    if two_part:
        in_specs += [pl.BlockSpec((1, t, LANE), lambda b, h, i: (b, i, h)),
                     pl.BlockSpec((1, S, LANE), lambda b, h, i: (b, 0, 0))]
        args += [q2, k2]
    if decay:
        in_specs += [pl.BlockSpec((1, 1, t, 1), lambda b, h, i: (b, h, i, 0)),
                     pl.BlockSpec((1, 1, 1, S), lambda b, h, i: (b, h, 0, 0))]
        args += [cum[:, :, :, None], cum[:, :, None, :]]
    return pl.pallas_call(
        functools.partial(_causal_attn_kernel, t=t, scale=scale, two_part=two_part, decay=decay),
        out_shape=jax.ShapeDtypeStruct((B, S, heads * LANE), BF16),
        grid=(B, heads, S // t),
        in_specs=in_specs,
        out_specs=pl.BlockSpec((1, t, LANE), lambda b, h, i: (b, i, h)),
        compiler_params=_cparams(("parallel", "parallel", "arbitrary")),
        name="causal_attention",
    )(*args)


def _gelu_tanh(x):
    return 0.5 * x * (1.0 + jnp.tanh(math.sqrt(2.0 / math.pi) * (x + 0.044715 * (x * x * x))))


def _compress_kernel(x_ref, w1a_ref, w1b_ref, pe_ref, w2_ref, o_ref):
    x = x_ref[0]
    a = jnp.dot(x, w1a_ref[...], preferred_element_type=F32)
    b = jnp.dot(x, w1b_ref[...], preferred_element_type=F32)
    nc = a.shape[0]
    b_next = pltpu.roll(b, nc - 1, 0)
    pe_term = jnp.dot(pe_ref[0:1, :], w1a_ref[...], preferred_element_type=F32) + \
        jnp.dot(pe_ref[1:2, :], w1b_ref[...], preferred_element_type=F32)
    hid = _gelu_tanh(a + b_next + pe_term)
    o_ref[0] = jnp.dot(hid.astype(BF16), w2_ref[...], preferred_element_type=F32).astype(o_ref.dtype)


def compress(x, w1a, w1b, pe2, w2):
    BG, NC, KD = x.shape
    dp = w2.shape[1]
    return pl.pallas_call(
        _compress_kernel,
        out_shape=jax.ShapeDtypeStruct((BG, NC, dp), BF16),
        grid=(BG,),
        in_specs=[pl.BlockSpec((1, NC, KD), lambda i: (i, 0, 0)),
                  pl.BlockSpec((KD, dp), lambda i: (0, 0)),
                  pl.BlockSpec((KD, dp), lambda i: (0, 0)),
                  pl.BlockSpec((2, KD), lambda i: (0, 0)),
                  pl.BlockSpec((dp, dp), lambda i: (0, 0))],
        out_specs=pl.BlockSpec((1, NC, dp), lambda i: (i, 0, 0)),
        compiler_params=_cparams(("parallel",)),
        name="nsa_compress",
    )(x, w1a, w1b, pe2, w2)


def _nsa_cmp_kernel(q_ref, kc_ref, vc_ref, bias_ref, ov_ref, o_ref, ind_ref, *, t, scale, n_sel):
    qi = pl.program_id(2)
    ncp = kc_ref.shape[2]
    kc = kc_ref[0, 0]
    vc = vc_ref[0, 0]
    row = lax.broadcasted_iota(jnp.int32, (t, ncp), 0) + qi * t
    col = lax.broadcasted_iota(jnp.int32, (t, ncp), 1)
    valid = row >= CMP_STRIDE * col + (CMP_BLOCK - 1)
    dn = (((1,), (1,)), ((), ()))
    psum = jnp.zeros((t, ncp), F32)
    for h in range(NSA_HPG):
        q = q_ref[0, :, h * NSA_DKP:(h + 1) * NSA_DKP]
        s = lax.dot_general(q, kc, dn, preferred_element_type=F32) * scale + bias_ref[h]
        s = jnp.where(valid, s, NEG_INF)
        m = jnp.max(s, axis=-1, keepdims=True)
        e = jnp.exp(s - m)
        p = jnp.where(valid, e / jnp.sum(e, axis=-1, keepdims=True), 0.0)
        o_ref[0, :, h * NSA_DV:(h + 1) * NSA_DV] = jnp.dot(p.astype(BF16), vc, preferred_element_type=F32).astype(o_ref.dtype)
        psum = psum + p
    p_hi = psum.astype(BF16)
    p_lo = (psum - p_hi.astype(F32)).astype(BF16)
    imp = jnp.dot(p_hi, ov_ref[...], preferred_element_type=F32) + jnp.dot(p_lo, ov_ref[...], preferred_element_type=F32)
    nbp = imp.shape[1]
    blk = lax.broadcasted_iota(jnp.int32, (t, nbp), 1)
    cur = (lax.broadcasted_iota(jnp.int32, (t, nbp), 0) + qi * t) // SEL_BLOCK
    forced = (blk == 0) | (blk == cur) | (blk == cur - 1)
    score = jnp.where(blk <= cur, imp + jnp.where(forced, SEL_FORCE, 0.0), NEG_INF)
    sel = jnp.zeros((t, nbp), F32)
    for _ in range(n_sel):
        mx = jnp.max(score, axis=-1, keepdims=True)
        first = jnp.min(jnp.where(score == mx, blk, nbp), axis=-1, keepdims=True)
        hit = blk == first
        sel = jnp.where(hit, 1.0, sel)
        score = jnp.where(hit, -jnp.inf, score)
    ind_ref[0, 0] = sel.astype(ind_ref.dtype)


def nsa_cmp_select(z3, kc, vc, bias_c, overlap, n_sel):
    B, S, _ = z3.shape
    G = NSA_GROUPS
    t = NSA_T
    ncp = kc.shape[2]
    nbp = overlap.shape[1]
    qw = NSA_HPG * NSA_DKP
    ow = NSA_HPG * NSA_DV
    return pl.pallas_call(
        functools.partial(_nsa_cmp_kernel, t=t, scale=NSA_DK ** -0.5, n_sel=n_sel),
        out_shape=(jax.ShapeDtypeStruct((B, S, G * ow), BF16), jax.ShapeDtypeStruct((B, G, S, nbp), BF16)),
        grid=(B, G, S // t),
        in_specs=[pl.BlockSpec((1, t, qw), lambda b, g, i: (b, i, CB_NQ * LANE // qw + g)),
                  pl.BlockSpec((1, 1, ncp, NSA_DKP), lambda b, g, i: (b, g, 0, 0)),
                  pl.BlockSpec((1, 1, ncp, NSA_DV), lambda b, g, i: (b, g, 0, 0)),
                  pl.BlockSpec((NSA_HPG, t, ncp), lambda b, g, i: (g, i, 0)),
                  pl.BlockSpec((ncp, nbp), lambda b, g, i: (0, 0))],
        out_specs=(pl.BlockSpec((1, t, ow), lambda b, g, i: (b, i, g)),
                   pl.BlockSpec((1, 1, t, nbp), lambda b, g, i: (b, g, i, 0))),
        compiler_params=_cparams(("parallel", "parallel", "arbitrary")),
        name="nsa_cmp_select",
    )(z3, kc, vc, bias_c, overlap)


def _nsa_sw_kernel(q_ref, ks_ref, vs_ref, kw_ref, vw_ref, ind_ref, e_ref, bias_ref, gate_ref, oc_ref, o_ref,
                   *, t, scale):
    qi = pl.program_id(2)
    hp = NSA_HPG
    q4 = jnp.concatenate([q_ref[0, :, h * NSA_DKP:(h + 1) * NSA_DKP] for h in range(hp)], axis=0)
    ind = ind_ref[0, 0]
    dn = (((1,), (1,)), ((), ()))
    ri = lax.broadcasted_iota(jnp.int32, (t, t), 0)
    ci = lax.broadcasted_iota(jnp.int32, (t, t), 1)

    def tile(kb, k_ref, v_ref, selected):
        off = pl.multiple_of(kb * t, t)
        s = lax.dot_general(q4, k_ref[0, pl.ds(off, t), :], dn, preferred_element_type=F32) * scale
        kind = jnp.minimum(qi - kb, 2)
        s = s + bias_ref[0, kind]
        d = (qi - kb) * t + ri - ci
        if selected:
            hit = jnp.dot(ind, e_ref[:, pl.ds(off, t)], preferred_element_type=F32)
            mask = (d >= 0) & (hit > 0.5)
        else:
            mask = (d >= 0) & (d < WINDOW)
        s = jnp.where(mask[None], s.reshape(hp, t, t), NEG_INF).reshape(hp * t, t)
        return s, v_ref[0, pl.ds(off, t), :]

    def update(carry, sv):
        m, l, acc = carry
        s, v = sv
        m_new = jnp.maximum(m, jnp.max(s, axis=-1, keepdims=True))
        a = jnp.exp(m - m_new)
        p = jnp.exp(s - m_new)
        l = a * l + jnp.sum(p, axis=-1, keepdims=True)
        acc = a * acc + jnp.dot(p.astype(BF16), v, preferred_element_type=F32)
        return m_new, l, acc

    init = (jnp.full((hp * t, 1), NEG_INF, F32), jnp.zeros((hp * t, 1), F32), jnp.zeros((hp * t, NSA_DV), F32))
    _, l_s, acc_s = lax.fori_loop(0, qi + 1, lambda kb, c: update(c, tile(kb, ks_ref, vs_ref, True)), init)
    lo = jnp.maximum(qi - WINDOW // t, 0)
    _, l_w, acc_w = lax.fori_loop(lo, qi + 1, lambda kb, c: update(c, tile(kb, kw_ref, vw_ref, False)), init)
    o_s = acc_s / l_s
    o_w = acc_w / l_w
    gates = _sigmoid(gate_ref[0, 0])
    for h in range(hp):
        r = slice(h * t, (h + 1) * t)
        c = slice(h * NSA_DV, (h + 1) * NSA_DV)
        o = gates[:, 3 * h:3 * h + 1] * oc_ref[0, :, c].astype(F32) + gates[:, 3 * h + 1:3 * h + 2] * o_s[r] + \
            gates[:, 3 * h + 2:3 * h + 3] * o_w[r]
        o_ref[0, :, c] = o.astype(o_ref.dtype)


def nsa_sel_win(z3, ind, expand, bias_sw, gates, o_cmp):
    B, S, _ = z3.shape
    G = NSA_GROUPS
    t = NSA_T
    qw = NSA_HPG * NSA_DKP
    ow = NSA_HPG * NSA_DV
    nbp = ind.shape[-1]
    kspec = lambda cb: pl.BlockSpec((1, S, NSA_DKP), lambda b, g, i: (b, 0, cb * LANE // NSA_DKP + g))
    vspec = lambda cb: pl.BlockSpec((1, S, NSA_DV), lambda b, g, i: (b, 0, cb + g))
    return pl.pallas_call(
        functools.partial(_nsa_sw_kernel, t=t, scale=NSA_DK ** -0.5),
        out_shape=jax.ShapeDtypeStruct((B, S, G * ow), BF16),
        grid=(B, G, S // t),
        in_specs=[pl.BlockSpec((1, t, qw), lambda b, g, i: (b, i, CB_NQ * LANE // qw + g)),
                  kspec(CB_NKS), vspec(CB_NVS), kspec(CB_NKW), vspec(CB_NVW),
                  pl.BlockSpec((1, 1, t, nbp), lambda b, g, i: (b, g, i, 0)),
                  pl.BlockSpec((nbp, S), lambda b, g, i: (0, 0)),
                  pl.BlockSpec((1, 3, NSA_HPG * t, t), lambda b, g, i: (g, 0, 0, 0)),
                  pl.BlockSpec((1, 1, t, LANE), lambda b, g, i: (b, g, i, 0)),
                  pl.BlockSpec((1, t, ow), lambda b, g, i: (b, i, g))],
        out_specs=pl.BlockSpec((1, t, ow), lambda b, g, i: (b, i, g)),
        compiler_params=_cparams(("parallel", "parallel", "arbitrary")),
        name="nsa_sel_win",
    )(z3, z3, z3, z3, z3, ind, expand, bias_sw, gates, o_cmp)


def _merge_kernel(of_ref, om_ref, on_ref, wb_ref, g0_ref, g1_ref, g2_ref, o_ref):
    acc = None
    for n, (o_r, g_r) in enumerate(((of_ref, g0_ref), (om_ref, g1_ref), (on_ref, g2_ref))):
        y = jnp.dot(o_r[...], wb_ref[n], preferred_element_type=F32)
        y = _sigmoid(g_r[...].astype(F32)) * y
        acc = y if acc is None else acc + y
    o_ref[...] = acc.astype(o_ref.dtype)


def merge_branches(o_fox, o_mla, o_nsa, wb, z, tm, tn):
    T = o_fox.shape[0]
    D = wb.shape[2]
    gspec = lambda n: pl.BlockSpec((tm, tn), lambda i, j: (i, (CB_MG * LANE + n * D) // tn + j))
    ospec = pl.BlockSpec((tm, BRANCH_W), lambda i, j: (i, 0))
    return pl.pallas_call(
        _merge_kernel,
        out_shape=jax.ShapeDtypeStruct((T, D), BF16),
        grid=(T // tm, D // tn),
        in_specs=[ospec, ospec, ospec,
                  pl.BlockSpec((N_BRANCH, BRANCH_W, tn), lambda i, j: (0, 0, j)),
                  gspec(0), gspec(1), gspec(2)],
        out_specs=pl.BlockSpec((tm, tn), lambda i, j: (i, j)),
        compiler_params=_cparams(("parallel", "arbitrary")),
        name="merge_branches",
    )(o_fox, o_mla, o_nsa, wb, z, z, z)


def _mm_res_kernel(a_ref, w_ref, r_ref, o_ref):
    o_ref[...] = r_ref[...] + jnp.dot(a_ref[...], w_ref[...], preferred_element_type=F32)


def matmul_residual(a, w, res, tm, tn):
    T, K = a.shape
    N = w.shape[1]
    return pl.pallas_call(
        _mm_res_kernel,
        out_shape=jax.ShapeDtypeStruct((T, N), F32),
        grid=(T // tm, N // tn),
        in_specs=[pl.BlockSpec((tm, K), lambda i, j: (i, 0)),
                  pl.BlockSpec((K, tn), lambda i, j: (0, j)),
                  pl.BlockSpec((tm, tn), lambda i, j: (i, j))],
        out_specs=pl.BlockSpec((tm, tn), lambda i, j: (i, j)),
        compiler_params=_cparams(("parallel", "arbitrary")),
        name="matmul_residual",
    )(a, w, res)


def _mem_attn_kernel(h_ref, g_ref, wq_ref, kv_ref, wo_ref, o_ref, *, scale):
    x = h_ref[0]
    ms = jnp.mean(x * x, axis=-1, keepdims=True)
    u = (x * lax.rsqrt(ms + EPS) * g_ref[...]).astype(BF16)
    q = jnp.dot(u, wq_ref[...], preferred_element_type=F32).astype(BF16)
    dn = (((1,), (1,)), ((), ()))
    hw = MEM_HEADS * MEM_DH
    outs = []
    for h in range(MEM_HEADS):
        c = slice(h * MEM_DH, (h + 1) * MEM_DH)
        k = kv_ref[0, :, c]
        v = kv_ref[0, :, hw + h * MEM_DH: hw + (h + 1) * MEM_DH]
        s = lax.dot_general(q[:, c], k, dn, preferred_element_type=F32) * scale
        m = jnp.max(s, axis=-1, keepdims=True)
        e = jnp.exp(s - m)
        p = e / jnp.sum(e, axis=-1, keepdims=True)
        outs.append(jnp.dot(p.astype(BF16), v, preferred_element_type=F32).astype(BF16))
    o = jnp.concatenate(outs, axis=1)
    o_ref[0] = x + jnp.dot(o, wo_ref[...], preferred_element_type=F32)


def memory_attention_block(h3, g, wq, kv, wo, tq):
    B, S, D = h3.shape
    M = kv.shape[1]
    hw = MEM_HEADS * MEM_DH
    return pl.pallas_call(
        functools.partial(_mem_attn_kernel, scale=MEM_DH ** -0.5),
        out_shape=jax.ShapeDtypeStruct((B, S, D), F32),
        grid=(B, S // tq),
        in_specs=[pl.BlockSpec((1, tq, D), lambda b, i: (b, i, 0)),
                  pl.BlockSpec((1, D), lambda b, i: (0, 0)),
                  pl.BlockSpec((D, hw), lambda b, i: (0, 0)),
                  pl.BlockSpec((1, M, 2 * hw), lambda b, i: (b, 0, 0)),
                  pl.BlockSpec((hw, D), lambda b, i: (0, 0))],
        out_specs=pl.BlockSpec((1, tq, D), lambda b, i: (b, i, 0)),
        compiler_params=_cparams(("parallel", "arbitrary")),
        name="memory_attention",
    )(h3, g.reshape(1, D).astype(F32), wq, kv, wo)


def _router_kernel(h_ref, g_ref, w_ref, b_ref, lg_ref, u_ref):
    x = h_ref[...]
    ms = jnp.mean(x * x, axis=-1, keepdims=True)
    u = x * lax.rsqrt(ms + EPS) * g_ref[...]
    u_ref[...] = u.astype(u_ref.dtype)
    lg_ref[...] = jnp.dot(u, w_ref[...], preferred_element_type=F32, precision=lax.Precision.HIGHEST) + b_ref[...]


def moe_router(h, g, w, b, tm):
    T, D = h.shape
    N = w.shape[1]
    return pl.pallas_call(
        _router_kernel,
        out_shape=(jax.ShapeDtypeStruct((T, N), F32), jax.ShapeDtypeStruct((T, D), BF16)),
        grid=(T // tm,),
        in_specs=[pl.BlockSpec((tm, D), lambda i: (i, 0)),
                  pl.BlockSpec((1, D), lambda i: (0, 0)),
                  pl.BlockSpec((D, N), lambda i: (0, 0)),
                  pl.BlockSpec((1, N), lambda i: (0, 0))],
        out_specs=(pl.BlockSpec((tm, N), lambda i: (i, 0)), pl.BlockSpec((tm, D), lambda i: (i, 0))),
        compiler_params=_cparams(("parallel",)),
        name="moe_router",
    )(h, g.reshape(1, D).astype(F32), w, b)


def _moe_kernel(be_ref, nu_ref, x_ref, wg_ref, wu_ref, wd_ref, rw_ref, o_ref):
    i = pl.program_id(0)

    @pl.when(i < nu_ref[0])
    def _():
        x = x_ref[...]
        a = jnp.dot(x, wg_ref[0], preferred_element_type=F32)
        b = jnp.dot(x, wu_ref[0], preferred_element_type=F32)
        hdn = (a * _sigmoid(a) * b).astype(BF16)
        y = jnp.dot(hdn, wd_ref[0], preferred_element_type=F32)
        o_ref[...] = (y * rw_ref[...]).astype(o_ref.dtype)

    @pl.when(i >= nu_ref[0])
    def _():
        o_ref[...] = jnp.zeros(o_ref.shape, o_ref.dtype)


def moe_experts(blk_e, n_used, xr, wg, wu, wd, row_w):
    P, D = xr.shape
    De = wg.shape[2]
    tm = MOE_TM
    grid_spec = pltpu.PrefetchScalarGridSpec(
        num_scalar_prefetch=2,
        grid=(P // tm,),
        in_specs=[pl.BlockSpec((tm, D), lambda i, be, nu: (i, 0)),
                  pl.BlockSpec((1, D, De), lambda i, be, nu: (be[i], 0, 0)),
                  pl.BlockSpec((1, D, De), lambda i, be, nu: (be[i], 0, 0)),
                  pl.BlockSpec((1, De, D), lambda i, be, nu: (be[i], 0, 0)),
                  pl.BlockSpec((tm, 1), lambda i, be, nu: (i, 0))],
        out_specs=pl.BlockSpec((tm, D), lambda i, be, nu: (i, 0)),
    )
    return pl.pallas_call(
        _moe_kernel,
        out_shape=jax.ShapeDtypeStruct((P, D), F32),
        grid_spec=grid_spec,
        compiler_params=_cparams(("arbitrary",)),
        name="moe_experts",
    )(blk_e, n_used, xr, wg, wu, wd, row_w)


def _rmsnorm_kernel(x_ref, g_ref, o_ref):
    x = x_ref[...]
    ms = jnp.mean(x * x, axis=-1, keepdims=True)
    o_ref[...] = x * lax.rsqrt(ms + EPS) * g_ref[...]


def rmsnorm_rows(x, g, tm):
    T, D = x.shape
    return pl.pallas_call(
        _rmsnorm_kernel,
        out_shape=jax.ShapeDtypeStruct((T, D), F32),
        grid=(T // tm,),
        in_specs=[pl.BlockSpec((tm, D), lambda i: (i, 0)), pl.BlockSpec((1, D), lambda i: (0, 0))],
        out_specs=pl.BlockSpec((tm, D), lambda i: (i, 0)),
        compiler_params=_cparams(("parallel",)),
        name="final_rmsnorm",
    )(x, g.reshape(1, D).astype(F32))


def _pad_cols(w, width):
    return jnp.pad(w, ((0, 0), (0, width - w.shape[1])))


def _pack_w_in(w):
    D = w.shape[0]
    offs = np.cumsum([0, 1024, 1024, 1024, FORGET_COLS, MLA_Q_RANK, MLA_KV_RANK, MLA_ROPE,
                      NSA_HEADS * NSA_DK, NSA_GROUPS * NSA_DK, NSA_GROUPS * NSA_DV, NSA_GROUPS * NSA_DK,
                      NSA_GROUPS * NSA_DV, NSA_GROUPS * NSA_DK, NSA_GROUPS * NSA_DV, NSA_GATE_COLS, N_BRANCH * D])
    seg = lambda i: w[:, offs[i]:offs[i + 1]]
    fq, fk, fv, ff, mcq, mckv, mkr, nq, nkc, nvc, nks, nvs, nkw, nvw, ngt, mg = [seg(i) for i in range(16)]
    half = MLA_ROPE // 2
    kr_rot = jnp.concatenate([-mkr[:, half:], mkr[:, :half]], axis=1)
    padk = lambda a, n: jnp.pad(a.reshape(D, n, NSA_DK), ((0, 0), (0, 0), (0, NSA_DKP - NSA_DK))).reshape(D, n * NSA_DKP)
    zeros = lambda nb: jnp.zeros((D, nb * LANE), w.dtype)
    cols = [mckv, mkr, kr_rot, zeros(1), mcq, fq, fk, fv, padk(nkc, NSA_GROUPS), padk(nq, NSA_HEADS),
            padk(nks, NSA_GROUPS), padk(nkw, NSA_GROUPS), nvc, nvs, nvw, zeros(2), mg]
    wz = jnp.concatenate(cols, axis=1).astype(BF16)
    assert wz.shape[1] == CB_MG * LANE + N_BRANCH * D
    w_small = _pad_cols(jnp.concatenate([ff, ngt], axis=1), LANE)
    return wz, w_small


def _pack_w_uq(w):
    K = w.shape[0]
    w3 = w.reshape(K, MLA_HEADS, MLA_NOPE + MLA_ROPE)
    nope = w3[:, :, :MLA_NOPE].reshape(K, MLA_HEADS * MLA_NOPE)
    r = w3[:, :, MLA_NOPE:]
    half = MLA_ROPE // 2
    r_rot = jnp.concatenate([-r[:, :, half:], r[:, :, :half]], axis=2)
    padr = lambda a: jnp.pad(a, ((0, 0), (0, 0), (0, LANE - MLA_ROPE))).reshape(K, MLA_HEADS * LANE)
    return jnp.concatenate([nope, padr(r), padr(r_rot)], axis=1).astype(BF16)


def _pack_w_ukv(w):
    K = w.shape[0]
    w3 = w.reshape(K, MLA_HEADS, MLA_NOPE + MLA_DV)
    return jnp.concatenate([w3[:, :, :MLA_NOPE].reshape(K, -1), w3[:, :, MLA_NOPE:].reshape(K, -1)], axis=1).astype(BF16)


def _t5_bucket(dist):
    dist = jnp.maximum(dist, 0)
    exact = REL_BUCKETS // 2
    df = jnp.maximum(dist, 1).astype(F32)
    large = exact + (jnp.log(df / exact) / math.log(REL_MAX_DIST / exact) * (REL_BUCKETS - exact)).astype(jnp.int32)
    large = jnp.minimum(large, REL_BUCKETS - 1)
    return jnp.where(dist < exact, dist, large)


def _position_tables(S, rel_bias):
    t = NSA_T
    half = MLA_ROPE // 2
    inv = ROPE_THETA ** (-jnp.arange(half, dtype=F32) / half)
    ang = jnp.arange(S, dtype=F32)[:, None] * inv
    c, s = jnp.cos(ang), jnp.sin(ang)
    cos = _pad_cols(jnp.concatenate([c, c], axis=1), LANE)
    sin = _pad_cols(jnp.concatenate([s, s], axis=1), LANE)
    ncp = max(S // CMP_STRIDE, LANE)
    pos = jnp.arange(S)
    dist_c = pos[:, None] - (CMP_STRIDE * jnp.arange(ncp)[None, :] + CMP_BLOCK - 1)
    bias_c = jnp.transpose(rel_bias[_t5_bucket(dist_c)], (2, 0, 1)).astype(F32)
    i = jnp.arange(t)
    tiles = [jnp.transpose(rel_bias[_t5_bucket(k * t + i[:, None] - i[None, :])], (2, 0, 1)) for k in range(3)]
    bias_sw = jnp.stack(tiles, axis=1).astype(F32)
    bias_sw = bias_sw.reshape(NSA_GROUPS, NSA_HPG, 3, t, t).transpose(0, 2, 1, 3, 4).reshape(NSA_GROUPS, 3, NSA_HPG * t, t)
    n_cmp = (S - CMP_BLOCK) // CMP_STRIDE + 1
    n_blk = S // SEL_BLOCK
    nbp = max(n_blk, LANE)
    cstart = CMP_STRIDE * jnp.arange(ncp)
    sstart = SEL_BLOCK * jnp.arange(nbp)
    ov = jnp.clip(jnp.minimum(cstart[:, None] + CMP_BLOCK, sstart[None, :] + SEL_BLOCK)
                  - jnp.maximum(cstart[:, None], sstart[None, :]), 0, None).astype(F32) / CMP_STRIDE
    ov = jnp.where((jnp.arange(ncp)[:, None] < n_cmp) & (jnp.arange(nbp)[None, :] < n_blk), ov, 0.0).astype(BF16)
    expand = (jnp.arange(nbp)[:, None] == (pos[None, :] // SEL_BLOCK)).astype(BF16)
    return cos, sin, bias_c, bias_sw, ov, expand


def _token_mixers(h, p, tabs, B, S):
    T, D = h.shape
    cos, sin, bias_c, bias_sw, overlap, expand = tabs
    wz, w_small = _pack_w_in(p["w_in"])
    z = rms_matmul(h, 0, p["g_mix"], wz, BF16, tm=min(T, 1024), tn=1024)
    zs = rms_matmul(h, 0, p["g_mix"], w_small, F32, tm=min(T, 512), tn=LANE, precise=True)
    z3 = z.reshape(B, S, Z_BLOCKS * LANE)

    log_f = jax.nn.log_sigmoid(zs[:, :FORGET_COLS] + p["b_forget"].astype(F32)).reshape(B, S, FOX_HEADS)
    cum = jnp.transpose(jnp.cumsum(log_f, axis=1), (0, 2, 1))
    o_fox = causal_attention(z3, CB_FQ, z3, CB_FK, z3, CB_FV, FOX_HEADS, FOX_DH ** -0.5, cum=cum)

    q_nope, q_rope = mla_q_proj(z, p["g_cq"], _pack_w_uq(p["w_uq"]), cos, sin, S, tm=min(S, 512))
    kv = rms_matmul(z, CB_CKV * LANE // MLA_KV_RANK, p["g_ckv"], _pack_w_ukv(p["w_ukv"]), BF16, tm=min(T, 1024), tn=1024)
    kr = z3[:, :, CB_KR * LANE:(CB_KR + 1) * LANE].astype(F32)
    kr = kr[..., :MLA_ROPE] * cos[None, :, :MLA_ROPE] + kr[..., MLA_ROPE:] * sin[None, :, :MLA_ROPE]
    k_rope = jnp.pad(kr, ((0, 0), (0, 0), (0, LANE - MLA_ROPE))).astype(BF16)
    hw = MLA_HEADS * LANE
    kv3 = kv.reshape(B, S, 2 * hw)
    o_mla = causal_attention(q_nope.reshape(B, S, hw), 0, kv3, 0, kv3, MLA_HEADS, MLA_HEADS,
                             (MLA_NOPE + MLA_ROPE) ** -0.5, q2=q_rope.reshape(B, S, hw), k2=k_rope)

    G = NSA_GROUPS
    NC = S // CMP_STRIDE
    ncp = bias_c.shape[2]

    def compress_branch(cb, dp, d, pe, w1, w2):
        raw = z3[:, :, cb * LANE: cb * LANE + G * dp].reshape(B, NC, CMP_STRIDE, G, dp)
        x = jnp.transpose(raw, (0, 3, 1, 2, 4)).reshape(B * G, NC, CMP_STRIDE * dp)
        w1p = jnp.pad(w1.reshape(CMP_BLOCK, d, d), ((0, 0), (0, dp - d), (0, dp - d))).astype(BF16)
        w1a = w1p[:CMP_STRIDE].reshape(CMP_STRIDE * dp, dp)
        w1b = w1p[CMP_STRIDE:].reshape(CMP_STRIDE * dp, dp)
        pe2 = jnp.pad(pe, ((0, 0), (0, dp - d))).reshape(2, CMP_STRIDE * dp).astype(BF16)
        w2p = jnp.pad(w2, ((0, dp - d), (0, dp - d))).astype(BF16)
        out = compress(x, w1a, w1b, pe2, w2p).reshape(B, G, NC, dp)
        return jnp.pad(out, ((0, 0), (0, 0), (0, ncp - NC), (0, 0)))

    kc = compress_branch(CB_NKC, NSA_DKP, NSA_DK, p["pe_k"], p["w_cmp_k1"], p["w_cmp_k2"])
    vc = compress_branch(CB_NVC, NSA_DV, NSA_DV, p["pe_v"], p["w_cmp_v1"], p["w_cmp_v2"])
    o_cmp, ind = nsa_cmp_select(z3, kc, vc, bias_c, overlap, min(N_SEL, S // SEL_BLOCK))
    gl = zs[:, FORGET_COLS:FORGET_COLS + NSA_GATE_COLS].reshape(B, S, G, NSA_HPG * 3)
    gl = jnp.pad(jnp.transpose(gl, (0, 2, 1, 3)), ((0, 0), (0, 0), (0, 0), (0, LANE - NSA_HPG * 3)))
    o_nsa = nsa_sel_win(z3, ind, expand, bias_sw, gl, o_cmp)

    merged = merge_branches(o_fox.reshape(T, -1), o_mla.reshape(T, -1), o_nsa.reshape(T, -1),
                            p["w_branch"].astype(BF16), z, tm=min(T, 1024), tn=512)
    return matmul_residual(merged, p["w_out"].astype(BF16), h, tm=min(T, 1024), tn=512)


def _memory_block(h, mem2, p, B, S):
    T, D = h.shape
    kv = rms_matmul(mem2, 0, p["g_mem_kv"], p["w_mem_kv"].astype(BF16), BF16, tm=min(mem2.shape[0], 512), tn=512)
    out = memory_attention_block(h.reshape(B, S, D), p["g_mem_q"], p["w_mem_q"].astype(BF16),
                                 kv.reshape(B, -1, kv.shape[1]), p["w_mem_o"].astype(BF16), tq=min(S, 512))
    return out.reshape(T, D)


def _moe_block(h, p):
    T, D = h.shape
    tm = MOE_TM
    w_r = _pad_cols(jnp.concatenate([p["w_router_group"], p["w_router_expert"]], axis=1), LANE).astype(F32)
    b_r = _pad_cols(jnp.concatenate([p["b_router_group"], p["b_router_expert"]])[None, :], LANE).astype(F32)
    logits, u = moe_router(h, p["g_moe"], w_r, b_r, tm=min(T, 512))
    glog = logits[:, :N_GROUPS]
    gsel = jnp.argmax(glog, axis=-1).astype(jnp.int32)
    pg = jnp.max(jax.nn.softmax(glog, axis=-1), axis=-1, keepdims=True)
    elog = logits[:, N_GROUPS:N_GROUPS + N_EXPERTS].reshape(T, N_GROUPS, EXPERTS_PER_GROUP)
    elog = jnp.take_along_axis(elog, gsel[:, None, None], axis=1)[:, 0]
    top_p, top_j = lax.top_k(jax.nn.softmax(elog, axis=-1), TOP_K)
    top_p = top_p / jnp.sum(top_p, axis=-1, keepdims=True)
    weight = pg * top_p
    flat_e = (gsel[:, None] * EXPERTS_PER_GROUP + top_j.astype(jnp.int32)).reshape(-1)
    TK = T * TOP_K
    onehot = (flat_e[:, None] == jnp.arange(N_EXPERTS, dtype=jnp.int32)[None, :]).astype(jnp.int32)
    rank = jnp.sum((jnp.cumsum(onehot, axis=0) - onehot) * onehot, axis=1)
    counts = jnp.sum(onehot, axis=0)
    pcounts = ((counts + tm - 1) // tm) * tm
    pends = jnp.cumsum(pcounts)
    dest = (pends - pcounts)[flat_e] + rank
    P = TK + N_EXPERTS * tm
    n_rb = P // tm
    row_tok = jnp.full((P,), T, jnp.int32).at[dest].set(jnp.repeat(jnp.arange(T, dtype=jnp.int32), TOP_K))
    row_w = jnp.zeros((P,), F32).at[dest].set(weight.reshape(-1))
    blk_e = jnp.clip(jnp.searchsorted(pends, jnp.arange(n_rb) * tm, side="right"), 0, N_EXPERTS - 1).astype(jnp.int32)
    n_used = (pends[-1] // tm).astype(jnp.int32).reshape(1)
    xr = jnp.concatenate([u, jnp.zeros((1, D), u.dtype)], axis=0)[row_tok]
    y = moe_experts(blk_e, n_used, xr, p["w_exp_gate"].astype(BF16), p["w_exp_up"].astype(BF16),
                    p["w_exp_down"].astype(BF16), row_w[:, None])
    d2 = dest.reshape(T, TOP_K)
    return h + (y[d2[:, 0]] + y[d2[:, 1]])


_LAYER_KEYS = ("g_mix", "w_in", "b_forget", "g_cq", "g_ckv", "w_uq", "w_ukv", "pe_k", "pe_v", "w_cmp_k1", "w_cmp_k2",
               "w_cmp_v1", "w_cmp_v2", "w_branch", "w_out", "g_mem_q", "g_mem_kv", "w_mem_q", "w_mem_kv", "w_mem_o",
               "g_moe", "w_router_group", "b_router_group", "w_router_expert", "b_router_expert", "w_exp_gate",
               "w_exp_up", "w_exp_down")


def kernel(x, mem, g_mix, w_in, b_forget, g_cq, g_ckv, w_uq, w_ukv, pe_k, pe_v, w_cmp_k1, w_cmp_k2, w_cmp_v1, w_cmp_v2, rel_bias, w_branch, w_out, g_mem_q, g_mem_kv, w_mem_q, w_mem_kv, w_mem_o, g_moe, w_router_group, b_router_group, w_router_expert, b_router_expert, w_exp_gate, w_exp_up, w_exp_down, g_final):
    B, S, D = x.shape
    T = B * S
    stacked = dict(g_mix=g_mix, w_in=w_in, b_forget=b_forget, g_cq=g_cq, g_ckv=g_ckv, w_uq=w_uq, w_ukv=w_ukv,
                   pe_k=pe_k, pe_v=pe_v, w_cmp_k1=w_cmp_k1, w_cmp_k2=w_cmp_k2, w_cmp_v1=w_cmp_v1, w_cmp_v2=w_cmp_v2,
                   w_branch=w_branch, w_out=w_out, g_mem_q=g_mem_q, g_mem_kv=g_mem_kv, w_mem_q=w_mem_q,
                   w_mem_kv=w_mem_kv, w_mem_o=w_mem_o, g_moe=g_moe, w_router_group=w_router_group,
                   b_router_group=b_router_group, w_router_expert=w_router_expert, b_router_expert=b_router_expert,
                   w_exp_gate=w_exp_gate, w_exp_up=w_exp_up, w_exp_down=w_exp_down)
    tabs = _position_tables(S, rel_bias.astype(F32))
    h = x.reshape(T, D).astype(F32)
    mem2 = mem.reshape(-1, D).astype(F32)
    for l in range(w_in.shape[0]):
        p = {k: stacked[k][l] for k in _LAYER_KEYS}
        h = _token_mixers(h, p, tabs, B, S)
        h = _memory_block(h, mem2, p, B, S)
        h = _moe_block(h, p)
    return rmsnorm_rows(h, g_final, tm=min(T, 512)).reshape(B, S, D)
```

```python
import functools
import math

import jax
import jax.numpy as jnp
import numpy as np
from jax import lax
from jax.experimental import pallas as pl
from jax.experimental.pallas import tpu as pltpu

F32 = jnp.float32
BF16 = jnp.bfloat16

EPS = 1e-6
NEG_INF = -1e30
LANE = 128

FOX_HEADS, FOX_DH = 8, 128
MLA_HEADS, MLA_NOPE, MLA_ROPE, MLA_DV = 8, 128, 64, 128
MLA_Q_RANK, MLA_KV_RANK = 768, 512
ROPE_THETA = 10000.0
NSA_HEADS, NSA_GROUPS, NSA_DK, NSA_DV = 8, 2, 192, 128
NSA_HPG = NSA_HEADS // NSA_GROUPS
NSA_DKP = 256
CMP_BLOCK, CMP_STRIDE, SEL_BLOCK, N_SEL, WINDOW = 32, 16, 64, 8, 512
SEL_FORCE = 1e6
REL_BUCKETS, REL_MAX_DIST = 32, 128
N_BRANCH, BRANCH_W = 3, 1024
MEM_HEADS, MEM_DH = 4, 128
N_GROUPS, EXPERTS_PER_GROUP, TOP_K = 4, 8, 2
N_EXPERTS = N_GROUPS * EXPERTS_PER_GROUP
FORGET_COLS, NSA_GATE_COLS = FOX_HEADS, NSA_HEADS * 3

CB_CKV, CB_KR, CB_CQ = 0, 4, 6
CB_NKC, CB_NQ = 12, 16
CB_FQ, CB_FK, CB_FV = 32, 40, 48
CB_NKS, CB_NKW = 56, 60
CB_NVC, CB_NVS, CB_NVW = 64, 66, 68
CB_MG = 72
Z_BLOCKS = 120

ATT_T = 256
ATT_HG = 4
NSA_T = 256
NSA_CMP_T = 512
MOE_TM = 256
VMEM_LIMIT = 56 * 1024 * 1024


def _cparams(sem):
    return pltpu.CompilerParams(dimension_semantics=sem, vmem_limit_bytes=VMEM_LIMIT)


def _sigmoid(x):
    return 1.0 / (1.0 + jnp.exp(-x))


def _rms_mm_kernel(x_ref, g_ref, w_ref, o_ref, u_ref, *, precise):
    @pl.when(pl.program_id(1) == 0)
    def _():
        x = x_ref[...].astype(F32)
        ms = jnp.mean(x * x, axis=-1, keepdims=True)
        u_ref[...] = (x * lax.rsqrt(ms + EPS) * g_ref[...]).astype(u_ref.dtype)

    prec = lax.Precision.HIGHEST if precise else None
    o_ref[...] = jnp.dot(u_ref[...], w_ref[...], preferred_element_type=F32, precision=prec).astype(o_ref.dtype)


def rms_matmul(x, xcol, g, w, out_dtype, tm, tn, precise=False):
    T = x.shape[0]
    K, N = w.shape
    return pl.pallas_call(
        functools.partial(_rms_mm_kernel, precise=precise),
        out_shape=jax.ShapeDtypeStruct((T, N), out_dtype),
        grid=(T // tm, N // tn),
        in_specs=[pl.BlockSpec((tm, K), lambda i, j: (i, xcol)),
                  pl.BlockSpec((1, K), lambda i, j: (0, 0)),
                  pl.BlockSpec((K, tn), lambda i, j: (0, j))],
        out_specs=pl.BlockSpec((tm, tn), lambda i, j: (i, j)),
        scratch_shapes=[pltpu.VMEM((tm, K), F32 if precise else BF16)],
        compiler_params=_cparams(("parallel", "arbitrary")),
        name="rms_matmul",
    )(x, g.reshape(1, K).astype(F32), w)


def _mla_q_kernel(x_ref, g_ref, w_ref, cos_ref, sin_ref, qn_ref, qr_ref):
    x = x_ref[...].astype(F32)
    ms = jnp.mean(x * x, axis=-1, keepdims=True)
    u = (x * lax.rsqrt(ms + EPS) * g_ref[...]).astype(BF16)
    y = jnp.dot(u, w_ref[...], preferred_element_type=F32)
    hw = MLA_HEADS * LANE
    qn_ref[...] = y[:, :hw].astype(qn_ref.dtype)
    cos = cos_ref[...]
    sin = sin_ref[...]
    for h in range(MLA_HEADS):
        a = y[:, hw + h * LANE: hw + (h + 1) * LANE]
        b = y[:, 2 * hw + h * LANE: 2 * hw + (h + 1) * LANE]
        qr_ref[:, h * LANE:(h + 1) * LANE] = (a * cos + b * sin).astype(qr_ref.dtype)


def mla_q_proj(z, g, w, cos, sin, S, tm):
    T = z.shape[0]
    K, N = w.shape
    hw = MLA_HEADS * LANE
    nsb = S // tm
    return pl.pallas_call(
        _mla_q_kernel,
        out_shape=(jax.ShapeDtypeStruct((T, hw), BF16), jax.ShapeDtypeStruct((T, hw), BF16)),
        grid=(T // tm,),
        in_specs=[pl.BlockSpec((tm, K), lambda i: (i, CB_CQ * LANE // MLA_Q_RANK)),
                  pl.BlockSpec((1, K), lambda i: (0, 0)),
                  pl.BlockSpec((K, N), lambda i: (0, 0)),
                  pl.BlockSpec((tm, LANE), lambda i: (i % nsb, 0)),
                  pl.BlockSpec((tm, LANE), lambda i: (i % nsb, 0))],
        out_specs=(pl.BlockSpec((tm, hw), lambda i: (i, 0)), pl.BlockSpec((tm, hw), lambda i: (i, 0))),
        compiler_params=_cparams(("parallel",)),
        name="mla_q_proj",
    )(z, g.reshape(1, K).astype(F32), w, cos, sin)


def _causal_attn_kernel(*refs, t, hg, two_part, decay):
    refs = list(refs)
    q_ref, k_ref, v_ref = refs[:3]
    pos = 3
    if two_part:
        q2_ref, k2_ref = refs[pos:pos + 2]
        pos += 2
    if decay:
        ck_ref = refs[pos]
        pos += 1
    o_ref = refs[pos]
    if two_part:
        kcat_ref = refs[pos + 1]
    hgi = pl.program_id(1)
    qi = pl.program_id(2)
    dn = (((1,), (1,)), ((), ()))

    if two_part:
        @pl.when(qi == 0)
        def _():
            for j in range(hg):
                kcat_ref[j, :, :LANE] = k_ref[0, :, j * LANE:(j + 1) * LANE]
                kcat_ref[j, :, LANE:] = k2_ref[0]

    qs = []
    for j in range(hg):
        qj = q_ref[0, :, j * LANE:(j + 1) * LANE]
        if two_part:
            qj = jnp.concatenate([qj, q2_ref[0, :, j * LANE:(j + 1) * LANE]], axis=1)
        qs.append(qj)

    def step(kb, carry, masked):
        off = pl.multiple_of(kb * t, t)
        out = []
        for j in range(hg):
            m, l, acc = carry[j]
            k = kcat_ref[j, pl.ds(off, t), :] if two_part else k_ref[0, pl.ds(off, t), j * LANE:(j + 1) * LANE]
            s = lax.dot_general(qs[j], k, dn, preferred_element_type=F32)
            if decay:
                s = s - ck_ref[0, pl.ds(hgi * hg + j, 1), pl.ds(off, t)]
            if masked:
                r = lax.broadcasted_iota(jnp.int32, (t, t), 0)
                c = lax.broadcasted_iota(jnp.int32, (t, t), 1)
                s = jnp.where(r >= c, s, NEG_INF)
            m_new = jnp.maximum(m, jnp.max(s, axis=-1, keepdims=True))
            a = jnp.exp(m - m_new)
            p = jnp.exp(s - m_new)
            l = a * l + jnp.sum(p, axis=-1, keepdims=True)
            v = v_ref[0, pl.ds(off, t), j * LANE:(j + 1) * LANE]
            acc = a * acc + jnp.dot(p.astype(BF16), v, preferred_element_type=F32)
            out.append((m_new, l, acc))
        return tuple(out)

    init = tuple((jnp.full((t, 1), NEG_INF, F32), jnp.zeros((t, 1), F32), jnp.zeros((t, LANE), F32))
                 for _ in range(hg))
    carry = lax.fori_loop(0, qi, lambda kb, c: step(kb, c, False), init)
    carry = step(qi, carry, True)
    for j in range(hg):
        _, l, acc = carry[j]
        o_ref[0, :, j * LANE:(j + 1) * LANE] = (acc / l).astype(o_ref.dtype)


def causal_attention(q, qcb, k, kcb, v, vcb, heads, q2=None, k2=None, cum=None):
    B, S, _ = q.shape
    t, hg = ATT_T, ATT_HG
    w = hg * LANE
    two_part, decay = q2 is not None, cum is not None
    in_specs = [pl.BlockSpec((1, t, w), lambda b, h, i: (b, i, qcb // hg + h)),
                pl.BlockSpec((1, S, w), lambda b, h, i: (b, 0, kcb // hg + h)),
                pl.BlockSpec((1, S, w), lambda b, h, i: (b, 0, vcb // hg + h))]
    args = [q, k, v]
    scratch = []
    if two_part:
        in_specs += [pl.BlockSpec((1, t, w), lambda b, h, i: (b, i, h)),
                     pl.BlockSpec((1, S, LANE), lambda b, h, i: (b, 0, 0))]
        args += [q2, k2]
        scratch = [pltpu.VMEM((hg, S, 2 * LANE), BF16)]
    if decay:
        in_specs += [pl.BlockSpec((1, heads, S), lambda b, h, i: (b, 0, 0))]
        args += [cum]
    return pl.pallas_call(
        functools.partial(_causal_attn_kernel, t=t, hg=hg, two_part=two_part, decay=decay),
        out_shape=jax.ShapeDtypeStruct((B, S, heads * LANE), BF16),
        grid=(B, heads // hg, S // t),
        in_specs=in_specs,
        out_specs=pl.BlockSpec((1, t, w), lambda b, h, i: (b, i, h)),
        scratch_shapes=scratch,
        compiler_params=_cparams(("parallel", "parallel", "arbitrary")),
        name="causal_attention",
    )(*args)


def _gelu_tanh(x):
    return 0.5 * x * (1.0 + jnp.tanh(math.sqrt(2.0 / math.pi) * (x + 0.044715 * (x * x * x))))


def _compress_kernel(x_ref, w1a_ref, w1b_ref, pe_ref, w2_ref, o_ref):
    x = x_ref[0]
    a = jnp.dot(x, w1a_ref[...], preferred_element_type=F32)
    b = jnp.dot(x, w1b_ref[...], preferred_element_type=F32)
    nc = a.shape[0]
    b_next = pltpu.roll(b, nc - 1, 0)
    pe_term = jnp.dot(pe_ref[0:1, :], w1a_ref[...], preferred_element_type=F32) + \
        jnp.dot(pe_ref[1:2, :], w1b_ref[...], preferred_element_type=F32)
    hid = _gelu_tanh(a + b_next + pe_term)
    o_ref[0] = jnp.dot(hid.astype(BF16), w2_ref[...], preferred_element_type=F32).astype(o_ref.dtype)


def compress(x, w1a, w1b, pe2, w2):
    BG, NC, KD = x.shape
    dp = w2.shape[1]
    return pl.pallas_call(
        _compress_kernel,
        out_shape=jax.ShapeDtypeStruct((BG, NC, dp), BF16),
        grid=(BG,),
        in_specs=[pl.BlockSpec((1, NC, KD), lambda i: (i, 0, 0)),
                  pl.BlockSpec((KD, dp), lambda i: (0, 0)),
                  pl.BlockSpec((KD, dp), lambda i: (0, 0)),
                  pl.BlockSpec((2, KD), lambda i: (0, 0)),
                  pl.BlockSpec((dp, dp), lambda i: (0, 0))],
        out_specs=pl.BlockSpec((1, NC, dp), lambda i: (i, 0, 0)),
        compiler_params=_cparams(("parallel",)),
        name="nsa_compress",
    )(x, w1a, w1b, pe2, w2)


def _nsa_cmp_kernel(q_ref, kc_ref, vc_ref, bias_ref, ov_ref, o_ref, ind_ref, *, t, n_sel):
    qi = pl.program_id(2)
    ncp = kc_ref.shape[2]
    kc = kc_ref[0, 0]
    vc = vc_ref[0, 0]
    row = lax.broadcasted_iota(jnp.int32, (t, ncp), 0) + qi * t
    col = lax.broadcasted_iota(jnp.int32, (t, ncp), 1)
    valid = row >= CMP_STRIDE * col + (CMP_BLOCK - 1)
    dn = (((1,), (1,)), ((), ()))
    psum = jnp.zeros((t, ncp), F32)
    for h in range(NSA_HPG):
        q = q_ref[0, :, h * NSA_DKP:(h + 1) * NSA_DKP]
        s = lax.dot_general(q, kc, dn, preferred_element_type=F32) + bias_ref[h]
        s = jnp.where(valid, s, NEG_INF)
        m = jnp.max(s, axis=-1, keepdims=True)
        e = jnp.exp(s - m)
        p = jnp.where(valid, e / jnp.sum(e, axis=-1, keepdims=True), 0.0)
        o_ref[0, :, h * NSA_DV:(h + 1) * NSA_DV] = jnp.dot(p.astype(BF16), vc, preferred_element_type=F32).astype(o_ref.dtype)
        psum = psum + p
    p_hi = psum.astype(BF16)
    p_lo = (psum - p_hi.astype(F32)).astype(BF16)
    imp = jnp.dot(p_hi, ov_ref[...], preferred_element_type=F32) + jnp.dot(p_lo, ov_ref[...], preferred_element_type=F32)
    nbp = imp.shape[1]
    blk = lax.broadcasted_iota(jnp.int32, (t, nbp), 1)
    cur = (lax.broadcasted_iota(jnp.int32, (t, nbp), 0) + qi * t) // SEL_BLOCK
    forced = (blk == 0) | (blk == cur) | (blk == cur - 1)
    score = jnp.where(blk <= cur, imp + jnp.where(forced, SEL_FORCE, 0.0), NEG_INF)
    sel = jnp.zeros((t, nbp), F32)
    for _ in range(n_sel):
        mx = jnp.max(score, axis=-1, keepdims=True)
        first = jnp.min(jnp.where(score == mx, blk, nbp), axis=-1, keepdims=True)
        hit = blk == first
        sel = jnp.where(hit, 1.0, sel)
        score = jnp.where(hit, -jnp.inf, score)
    ind_ref[0, 0] = sel.astype(ind_ref.dtype)


def nsa_cmp_select(z3, kc, vc, bias_c, overlap, n_sel):
    B, S, _ = z3.shape
    G = NSA_GROUPS
    t = min(NSA_CMP_T, S)
    ncp = kc.shape[2]
    nbp = overlap.shape[1]
    qw = NSA_HPG * NSA_DKP
    ow = NSA_HPG * NSA_DV
    return pl.pallas_call(
        functools.partial(_nsa_cmp_kernel, t=t, n_sel=n_sel),
        out_shape=(jax.ShapeDtypeStruct((B, S, G * ow), BF16), jax.ShapeDtypeStruct((B, G, S, nbp), BF16)),
        grid=(B, G, S // t),
        in_specs=[pl.BlockSpec((1, t, qw), lambda b, g, i: (b, i, CB_NQ * LANE // qw + g)),
                  pl.BlockSpec((1, 1, ncp, NSA_DKP), lambda b, g, i: (b, g, 0, 0)),
                  pl.BlockSpec((1, 1, ncp, NSA_DV), lambda b, g, i: (b, g, 0, 0)),
                  pl.BlockSpec((NSA_HPG, t, ncp), lambda b, g, i: (g, i, 0)),
                  pl.BlockSpec((ncp, nbp), lambda b, g, i: (0, 0))],
        out_specs=(pl.BlockSpec((1, t, ow), lambda b, g, i: (b, i, g)),
                   pl.BlockSpec((1, 1, t, nbp), lambda b, g, i: (b, g, i, 0))),
        compiler_params=_cparams(("parallel", "parallel", "arbitrary")),
        name="nsa_cmp_select",
    )(z3, kc, vc, bias_c, overlap)


def _nsa_sw_kernel(q_ref, ks_ref, vs_ref, kw_ref, vw_ref, ind_ref, e_ref, bias_ref, gate_ref, oc_ref, o_ref,
                   *, t):
    qi = pl.program_id(1)
    hp, G = NSA_HPG, NSA_GROUPS
    q4 = [jnp.concatenate([q_ref[0, :, (g * hp + h) * NSA_DKP:(g * hp + h + 1) * NSA_DKP] for h in range(hp)], axis=0)
          for g in range(G)]
    inds = [ind_ref[0, g] for g in range(G)]
    dn = (((1,), (1,)), ((), ()))
    ri = lax.broadcasted_iota(jnp.int32, (t, t), 0)
    ci = lax.broadcasted_iota(jnp.int32, (t, t), 1)

    def step(kb, carry, k_ref, v_ref, selected):
        off = pl.multiple_of(kb * t, t)
        kind = jnp.minimum(qi - kb, 2)
        d = (qi - kb) * t + ri - ci
        near = (d >= 0) if selected else (d >= 0) & (d < WINDOW)
        out = []
        for g in range(G):
            m, l, acc = carry[g]
            k = k_ref[0, pl.ds(off, t), g * NSA_DKP:(g + 1) * NSA_DKP]
            s = lax.dot_general(q4[g], k, dn, preferred_element_type=F32) + bias_ref[g, kind]
            mask = near
            if selected:
                hit = jnp.dot(inds[g], e_ref[:, pl.ds(off, t)], preferred_element_type=F32)
                mask = near & (hit > 0.5)
            s = jnp.where(mask[None], s.reshape(hp, t, t), NEG_INF).reshape(hp * t, t)
            m_new = jnp.maximum(m, jnp.max(s, axis=-1, keepdims=True))
            a = jnp.exp(m - m_new)
            p = jnp.exp(s - m_new)
            l = a * l + jnp.sum(p, axis=-1, keepdims=True)
            v = v_ref[0, pl.ds(off, t), g * NSA_DV:(g + 1) * NSA_DV]
            acc = a * acc + jnp.dot(p.astype(BF16), v, preferred_element_type=F32)
            out.append((m_new, l, acc))
        return tuple(out)

    init = tuple((jnp.full((hp * t, 1), NEG_INF, F32), jnp.zeros((hp * t, 1), F32), jnp.zeros((hp * t, NSA_DV), F32))
                 for _ in range(G))
    sel = lax.fori_loop(0, qi + 1, lambda kb, c: step(kb, c, ks_ref, vs_ref, True), init)
    lo = jnp.maximum(qi - WINDOW // t, 0)
    win = lax.fori_loop(lo, qi + 1, lambda kb, c: step(kb, c, kw_ref, vw_ref, False), init)
    for g in range(G):
        o_s = sel[g][2] / sel[g][1]
        o_w = win[g][2] / win[g][1]
        gates = _sigmoid(gate_ref[0, g])
        for h in range(hp):
            r = slice(h * t, (h + 1) * t)
            c = slice((g * hp + h) * NSA_DV, (g * hp + h + 1) * NSA_DV)
            o = gates[:, 3 * h:3 * h + 1] * oc_ref[0, :, c].astype(F32) + gates[:, 3 * h + 1:3 * h + 2] * o_s[r] + \
                gates[:, 3 * h + 2:3 * h + 3] * o_w[r]
            o_ref[0, :, c] = o.astype(o_ref.dtype)


def nsa_sel_win(z3, ind, expand, bias_sw, gates, o_cmp):
    B, S, _ = z3.shape
    G = NSA_GROUPS
    t = NSA_T
    qw = NSA_HEADS * NSA_DKP
    ow = NSA_HEADS * NSA_DV
    kw = G * NSA_DKP
    vw = G * NSA_DV
    nbp = ind.shape[-1]
    kspec = lambda cb: pl.BlockSpec((1, S, kw), lambda b, i: (b, 0, cb * LANE // kw))
    vspec = lambda cb: pl.BlockSpec((1, S, vw), lambda b, i: (b, 0, cb * LANE // vw))
    return pl.pallas_call(
        functools.partial(_nsa_sw_kernel, t=t),
        out_shape=jax.ShapeDtypeStruct((B, S, ow), BF16),
        grid=(B, S // t),
        in_specs=[pl.BlockSpec((1, t, qw), lambda b, i: (b, i, CB_NQ * LANE // qw)),
                  kspec(CB_NKS), vspec(CB_NVS), kspec(CB_NKW), vspec(CB_NVW),
                  pl.BlockSpec((1, G, t, nbp), lambda b, i: (b, 0, i, 0)),
                  pl.BlockSpec((nbp, S), lambda b, i: (0, 0)),
                  pl.BlockSpec((G, 3, NSA_HPG * t, t), lambda b, i: (0, 0, 0, 0)),
                  pl.BlockSpec((1, G, t, LANE), lambda b, i: (b, 0, i, 0)),
                  pl.BlockSpec((1, t, ow), lambda b, i: (b, i, 0))],
        out_specs=pl.BlockSpec((1, t, ow), lambda b, i: (b, i, 0)),
        compiler_params=_cparams(("parallel", "arbitrary")),
        name="nsa_sel_win",
    )(z3, z3, z3, z3, z3, ind, expand, bias_sw, gates, o_cmp)


def _merge_kernel(of_ref, om_ref, on_ref, wb_ref, g0_ref, g1_ref, g2_ref, o_ref):
    acc = None
    for n, (o_r, g_r) in enumerate(((of_ref, g0_ref), (om_ref, g1_ref), (on_ref, g2_ref))):
        y = jnp.dot(o_r[...], wb_ref[n], preferred_element_type=F32)
        y = _sigmoid(g_r[...].astype(F32)) * y
        acc = y if acc is None else acc + y
    o_ref[...] = acc.astype(o_ref.dtype)


def merge_branches(o_fox, o_mla, o_nsa, wb, z, tm, tn):
    T = o_fox.shape[0]
    D = wb.shape[2]
    gspec = lambda n: pl.BlockSpec((tm, tn), lambda i, j: (i, (CB_MG * LANE + n * D) // tn + j))
    ospec = pl.BlockSpec((tm, BRANCH_W), lambda i, j: (i, 0))
    return pl.pallas_call(
        _merge_kernel,
        out_shape=jax.ShapeDtypeStruct((T, D), BF16),
        grid=(T // tm, D // tn),
        in_specs=[ospec, ospec, ospec,
                  pl.BlockSpec((N_BRANCH, BRANCH_W, tn), lambda i, j: (0, 0, j)),
                  gspec(0), gspec(1), gspec(2)],
        out_specs=pl.BlockSpec((tm, tn), lambda i, j: (i, j)),
        compiler_params=_cparams(("parallel", "arbitrary")),
        name="merge_branches",
    )(o_fox, o_mla, o_nsa, wb, z, z, z)


def _mm_res_kernel(a_ref, w_ref, r_ref, o_ref):
    o_ref[...] = r_ref[...] + jnp.dot(a_ref[...], w_ref[...], preferred_element_type=F32)


def matmul_residual(a, w, res, tm, tn):
    T, K = a.shape
    N = w.shape[1]
    return pl.pallas_call(
        _mm_res_kernel,
        out_shape=jax.ShapeDtypeStruct((T, N), F32),
        grid=(T // tm, N // tn),
        in_specs=[pl.BlockSpec((tm, K), lambda i, j: (i, 0)),
                  pl.BlockSpec((K, tn), lambda i, j: (0, j)),
                  pl.BlockSpec((tm, tn), lambda i, j: (i, j))],
        out_specs=pl.BlockSpec((tm, tn), lambda i, j: (i, j)),
        compiler_params=_cparams(("parallel", "arbitrary")),
        name="matmul_residual",
    )(a, w, res)


def _mem_attn_kernel(h_ref, g_ref, wq_ref, kv_ref, wo_ref, o_ref):
    x = h_ref[0]
    ms = jnp.mean(x * x, axis=-1, keepdims=True)
    u = (x * lax.rsqrt(ms + EPS) * g_ref[...]).astype(BF16)
    q = jnp.dot(u, wq_ref[...], preferred_element_type=F32).astype(BF16)
    dn = (((1,), (1,)), ((), ()))
    hw = MEM_HEADS * MEM_DH
    outs = []
    for h in range(MEM_HEADS):
        c = slice(h * MEM_DH, (h + 1) * MEM_DH)
        k = kv_ref[0, :, c]
        v = kv_ref[0, :, hw + h * MEM_DH: hw + (h + 1) * MEM_DH]
        s = lax.dot_general(q[:, c], k, dn, preferred_element_type=F32)
        m = jnp.max(s, axis=-1, keepdims=True)
        e = jnp.exp(s - m)
        p = e / jnp.sum(e, axis=-1, keepdims=True)
        outs.append(jnp.dot(p.astype(BF16), v, preferred_element_type=F32).astype(BF16))
    o = jnp.concatenate(outs, axis=1)
    o_ref[0] = x + jnp.dot(o, wo_ref[...], preferred_element_type=F32)


def memory_attention_block(h3, g, wq, kv, wo, tq):
    B, S, D = h3.shape
    M = kv.shape[1]
    hw = MEM_HEADS * MEM_DH
    return pl.pallas_call(
        _mem_attn_kernel,
        out_shape=jax.ShapeDtypeStruct((B, S, D), F32),
        grid=(B, S // tq),
        in_specs=[pl.BlockSpec((1, tq, D), lambda b, i: (b, i, 0)),
                  pl.BlockSpec((1, D), lambda b, i: (0, 0)),
                  pl.BlockSpec((D, hw), lambda b, i: (0, 0)),
                  pl.BlockSpec((1, M, 2 * hw), lambda b, i: (b, 0, 0)),
                  pl.BlockSpec((hw, D), lambda b, i: (0, 0))],
        out_specs=pl.BlockSpec((1, tq, D), lambda b, i: (b, i, 0)),
        compiler_params=_cparams(("parallel", "arbitrary")),
        name="memory_attention",
    )(h3, g.reshape(1, D).astype(F32), wq, kv, wo)


def _router_kernel(h_ref, g_ref, w_ref, b_ref, lg_ref, u_ref):
    x = h_ref[...]
    ms = jnp.mean(x * x, axis=-1, keepdims=True)
    u = x * lax.rsqrt(ms + EPS) * g_ref[...]
    u_ref[...] = u.astype(u_ref.dtype)
    lg_ref[...] = jnp.dot(u, w_ref[...], preferred_element_type=F32, precision=lax.Precision.HIGHEST) + b_ref[...]


def moe_router(h, g, w, b, tm):
    T, D = h.shape
    N = w.shape[1]
    return pl.pallas_call(
        _router_kernel,
        out_shape=(jax.ShapeDtypeStruct((T, N), F32), jax.ShapeDtypeStruct((T, D), BF16)),
        grid=(T // tm,),
        in_specs=[pl.BlockSpec((tm, D), lambda i: (i, 0)),
                  pl.BlockSpec((1, D), lambda i: (0, 0)),
                  pl.BlockSpec((D, N), lambda i: (0, 0)),
                  pl.BlockSpec((1, N), lambda i: (0, 0))],
        out_specs=(pl.BlockSpec((tm, N), lambda i: (i, 0)), pl.BlockSpec((tm, D), lambda i: (i, 0))),
        compiler_params=_cparams(("parallel",)),
        name="moe_router",
    )(h, g.reshape(1, D).astype(F32), w, b)


def _moe_kernel(be_ref, nu_ref, x_ref, wg_ref, wu_ref, wd_ref, o_ref):
    i = pl.program_id(0)

    @pl.when(i < nu_ref[0])
    def _():
        x = x_ref[...]
        a = jnp.dot(x, wg_ref[0], preferred_element_type=F32)
        b = jnp.dot(x, wu_ref[0], preferred_element_type=F32)
        hdn = (a * _sigmoid(a) * b).astype(BF16)
        o_ref[...] = jnp.dot(hdn, wd_ref[0], preferred_element_type=F32).astype(o_ref.dtype)

    @pl.when(i >= nu_ref[0])
    def _():
        o_ref[...] = jnp.zeros(o_ref.shape, o_ref.dtype)


def moe_experts(blk_e, n_used, xr, wg, wu, wd):
    P, D = xr.shape
    De = wg.shape[2]
    tm = MOE_TM
    grid_spec = pltpu.PrefetchScalarGridSpec(
        num_scalar_prefetch=2,
        grid=(P // tm,),
        in_specs=[pl.BlockSpec((tm, D), lambda i, be, nu: (i, 0)),
                  pl.BlockSpec((1, D, De), lambda i, be, nu: (be[i], 0, 0)),
                  pl.BlockSpec((1, D, De), lambda i, be, nu: (be[i], 0, 0)),
                  pl.BlockSpec((1, De, D), lambda i, be, nu: (be[i], 0, 0))],
        out_specs=pl.BlockSpec((tm, D), lambda i, be, nu: (i, 0)),
    )
    return pl.pallas_call(
        _moe_kernel,
        out_shape=jax.ShapeDtypeStruct((P, D), BF16),
        grid_spec=grid_spec,
        compiler_params=_cparams(("arbitrary",)),
        name="moe_experts",
    )(blk_e, n_used, xr, wg, wu, wd)


def _rmsnorm_kernel(x_ref, g_ref, o_ref):
    x = x_ref[...]
    ms = jnp.mean(x * x, axis=-1, keepdims=True)
    o_ref[...] = x * lax.rsqrt(ms + EPS) * g_ref[...]


def rmsnorm_rows(x, g, tm):
    T, D = x.shape
    return pl.pallas_call(
        _rmsnorm_kernel,
        out_shape=jax.ShapeDtypeStruct((T, D), F32),
        grid=(T // tm,),
        in_specs=[pl.BlockSpec((tm, D), lambda i: (i, 0)), pl.BlockSpec((1, D), lambda i: (0, 0))],
        out_specs=pl.BlockSpec((tm, D), lambda i: (i, 0)),
        compiler_params=_cparams(("parallel",)),
        name="final_rmsnorm",
    )(x, g.reshape(1, D).astype(F32))


def _pad_cols(w, width):
    return jnp.pad(w, ((0, 0), (0, width - w.shape[1])))


def _pack_w_in(w):
    D = w.shape[0]
    offs = np.cumsum([0, 1024, 1024, 1024, FORGET_COLS, MLA_Q_RANK, MLA_KV_RANK, MLA_ROPE,
                      NSA_HEADS * NSA_DK, NSA_GROUPS * NSA_DK, NSA_GROUPS * NSA_DV, NSA_GROUPS * NSA_DK,
                      NSA_GROUPS * NSA_DV, NSA_GROUPS * NSA_DK, NSA_GROUPS * NSA_DV, NSA_GATE_COLS, N_BRANCH * D])
    seg = lambda i: w[:, offs[i]:offs[i + 1]]
    fq, fk, fv, ff, mcq, mckv, mkr, nq, nkc, nvc, nks, nvs, nkw, nvw, ngt, mg = [seg(i) for i in range(16)]
    half = MLA_ROPE // 2
    kr_rot = jnp.concatenate([-mkr[:, half:], mkr[:, :half]], axis=1)
    padk = lambda a, n: jnp.pad(a.reshape(D, n, NSA_DK), ((0, 0), (0, 0), (0, NSA_DKP - NSA_DK))).reshape(D, n * NSA_DKP)
    zeros = lambda nb: jnp.zeros((D, nb * LANE), w.dtype)
    cols = [mckv, mkr, kr_rot, zeros(1), mcq, padk(nkc, NSA_GROUPS), padk(nq * NSA_DK ** -0.5, NSA_HEADS),
            fq * FOX_DH ** -0.5, fk, fv, padk(nks, NSA_GROUPS), padk(nkw, NSA_GROUPS), nvc, nvs, nvw, zeros(2), mg]
    wz = jnp.concatenate(cols, axis=1).astype(BF16)
    assert wz.shape[1] == CB_MG * LANE + N_BRANCH * D
    w_small = _pad_cols(jnp.concatenate([ff, ngt], axis=1), LANE)
    return wz, w_small


def _pack_w_uq(w):
    K = w.shape[0]
    w3 = w.reshape(K, MLA_HEADS, MLA_NOPE + MLA_ROPE) * (MLA_NOPE + MLA_ROPE) ** -0.5
    nope = w3[:, :, :MLA_NOPE].reshape(K, MLA_HEADS * MLA_NOPE)
    r = w3[:, :, MLA_NOPE:]
    half = MLA_ROPE // 2
    r_rot = jnp.concatenate([-r[:, :, half:], r[:, :, :half]], axis=2)
    padr = lambda a: jnp.pad(a, ((0, 0), (0, 0), (0, LANE - MLA_ROPE))).reshape(K, MLA_HEADS * LANE)
    return jnp.concatenate([nope, padr(r), padr(r_rot)], axis=1).astype(BF16)


def _pack_w_ukv(w):
    K = w.shape[0]
    w3 = w.reshape(K, MLA_HEADS, MLA_NOPE + MLA_DV)
    return jnp.concatenate([w3[:, :, :MLA_NOPE].reshape(K, -1), w3[:, :, MLA_NOPE:].reshape(K, -1)], axis=1).astype(BF16)


def _t5_bucket(dist):
    dist = jnp.maximum(dist, 0)
    exact = REL_BUCKETS // 2
    df = jnp.maximum(dist, 1).astype(F32)
    large = exact + (jnp.log(df / exact) / math.log(REL_MAX_DIST / exact) * (REL_BUCKETS - exact)).astype(jnp.int32)
    large = jnp.minimum(large, REL_BUCKETS - 1)
    return jnp.where(dist < exact, dist, large)


def _position_tables(S, rel_bias):
    t = NSA_T
    half = MLA_ROPE // 2
    inv = ROPE_THETA ** (-jnp.arange(half, dtype=F32) / half)
    ang = jnp.arange(S, dtype=F32)[:, None] * inv
    c, s = jnp.cos(ang), jnp.sin(ang)
    cos = _pad_cols(jnp.concatenate([c, c], axis=1), LANE)
    sin = _pad_cols(jnp.concatenate([s, s], axis=1), LANE)
    ncp = max(S // CMP_STRIDE, LANE)
    pos = jnp.arange(S)

    def bias_of(dist):
        onehot = jax.nn.one_hot(_t5_bucket(dist), REL_BUCKETS, dtype=F32)
        return jnp.einsum("...b,bh->h...", onehot, rel_bias, precision=lax.Precision.HIGHEST)

    bias_c = bias_of(pos[:, None] - (CMP_STRIDE * jnp.arange(ncp)[None, :] + CMP_BLOCK - 1))
    i = jnp.arange(t)
    bias_sw = jnp.stack([bias_of(k * t + i[:, None] - i[None, :]) for k in range(3)], axis=1)
    bias_sw = bias_sw.reshape(NSA_GROUPS, NSA_HPG, 3, t, t).transpose(0, 2, 1, 3, 4).reshape(NSA_GROUPS, 3, NSA_HPG * t, t)
    n_cmp = (S - CMP_BLOCK) // CMP_STRIDE + 1
    n_blk = S // SEL_BLOCK
    nbp = max(n_blk, LANE)
    cstart = CMP_STRIDE * jnp.arange(ncp)
    sstart = SEL_BLOCK * jnp.arange(nbp)
    ov = jnp.clip(jnp.minimum(cstart[:, None] + CMP_BLOCK, sstart[None, :] + SEL_BLOCK)
                  - jnp.maximum(cstart[:, None], sstart[None, :]), 0, None).astype(F32) / CMP_STRIDE
    ov = jnp.where((jnp.arange(ncp)[:, None] < n_cmp) & (jnp.arange(nbp)[None, :] < n_blk), ov, 0.0).astype(BF16)
    expand = (jnp.arange(nbp)[:, None] == (pos[None, :] // SEL_BLOCK)).astype(BF16)
    return cos, sin, bias_c, bias_sw, ov, expand


def _token_mixers(h, p, tabs, B, S):
    T, D = h.shape
    cos, sin, bias_c, bias_sw, overlap, expand = tabs
    wz, w_small = _pack_w_in(p["w_in"])
    z = rms_matmul(h, 0, p["g_mix"], wz, BF16, tm=min(T, 1024), tn=1024)
    zs = rms_matmul(h, 0, p["g_mix"], w_small, F32, tm=min(T, 512), tn=LANE, precise=True)
    z3 = z.reshape(B, S, Z_BLOCKS * LANE)

    log_f = jax.nn.log_sigmoid(zs[:, :FORGET_COLS] + p["b_forget"].astype(F32)).reshape(B, S, FOX_HEADS)
    cum = jnp.transpose(jnp.cumsum(log_f, axis=1), (0, 2, 1))
    o_fox = causal_attention(z3, CB_FQ, z3, CB_FK, z3, CB_FV, FOX_HEADS, cum=cum)

    q_nope, q_rope = mla_q_proj(z, p["g_cq"], _pack_w_uq(p["w_uq"]), cos, sin, S, tm=min(S, 512))
    kv = rms_matmul(z, CB_CKV * LANE // MLA_KV_RANK, p["g_ckv"], _pack_w_ukv(p["w_ukv"]), BF16, tm=min(T, 1024), tn=1024)
    kr = z3[:, :, CB_KR * LANE:(CB_KR + 1) * LANE].astype(F32)
    kr = kr[..., :MLA_ROPE] * cos[None, :, :MLA_ROPE] + kr[..., MLA_ROPE:] * sin[None, :, :MLA_ROPE]
    k_rope = jnp.pad(kr, ((0, 0), (0, 0), (0, LANE - MLA_ROPE))).astype(BF16)
    hw = MLA_HEADS * LANE
    kv3 = kv.reshape(B, S, 2 * hw)
    o_mla = causal_attention(q_nope.reshape(B, S, hw), 0, kv3, 0, kv3, MLA_HEADS, MLA_HEADS,
                             q2=q_rope.reshape(B, S, hw), k2=k_rope)

    G = NSA_GROUPS
    NC = S // CMP_STRIDE
    ncp = bias_c.shape[2]

    def compress_branch(cb, dp, d, pe, w1, w2):
        raw = z3[:, :, cb * LANE: cb * LANE + G * dp].reshape(B, NC, CMP_STRIDE, G, dp)
        x = jnp.transpose(raw, (0, 3, 1, 2, 4)).reshape(B * G, NC, CMP_STRIDE * dp)
        w1p = jnp.pad(w1.reshape(CMP_BLOCK, d, d), ((0, 0), (0, dp - d), (0, dp - d))).astype(BF16)
        w1a = w1p[:CMP_STRIDE].reshape(CMP_STRIDE * dp, dp)
        w1b = w1p[CMP_STRIDE:].reshape(CMP_STRIDE * dp, dp)
        pe2 = jnp.pad(pe, ((0, 0), (0, dp - d))).reshape(2, CMP_STRIDE * dp).astype(BF16)
        w2p = jnp.pad(w2, ((0, dp - d), (0, dp - d))).astype(BF16)
        out = compress(x, w1a, w1b, pe2, w2p).reshape(B, G, NC, dp)
        return jnp.pad(out, ((0, 0), (0, 0), (0, ncp - NC), (0, 0)))

    kc = compress_branch(CB_NKC, NSA_DKP, NSA_DK, p["pe_k"], p["w_cmp_k1"], p["w_cmp_k2"])
    vc = compress_branch(CB_NVC, NSA_DV, NSA_DV, p["pe_v"], p["w_cmp_v1"], p["w_cmp_v2"])
    o_cmp, ind = nsa_cmp_select(z3, kc, vc, bias_c, overlap, min(N_SEL, S // SEL_BLOCK))
    gl = zs[:, FORGET_COLS:FORGET_COLS + NSA_GATE_COLS].reshape(B, S, G, NSA_HPG * 3)
    gl = jnp.pad(jnp.transpose(gl, (0, 2, 1, 3)), ((0, 0), (0, 0), (0, 0), (0, LANE - NSA_HPG * 3)))
    o_nsa = nsa_sel_win(z3, ind, expand, bias_sw, gl, o_cmp)

    merged = merge_branches(o_fox.reshape(T, -1), o_mla.reshape(T, -1), o_nsa.reshape(T, -1),
                            p["w_branch"].astype(BF16), z, tm=min(T, 1024), tn=512)
    return matmul_residual(merged, p["w_out"].astype(BF16), h, tm=min(T, 1024), tn=512)


def _memory_block(h, mem2, p, B, S):
    T, D = h.shape
    kv = rms_matmul(mem2, 0, p["g_mem_kv"], p["w_mem_kv"].astype(BF16), BF16, tm=min(mem2.shape[0], 512), tn=512)
    out = memory_attention_block(h.reshape(B, S, D), p["g_mem_q"], (p["w_mem_q"] * MEM_DH ** -0.5).astype(BF16),
                                 kv.reshape(B, -1, kv.shape[1]), p["w_mem_o"].astype(BF16), tq=min(S, 512))
    return out.reshape(T, D)


def _moe_block(h, p):
    T, D = h.shape
    tm = MOE_TM
    w_r = _pad_cols(jnp.concatenate([p["w_router_group"], p["w_router_expert"]], axis=1), LANE).astype(F32)
    b_r = _pad_cols(jnp.concatenate([p["b_router_group"], p["b_router_expert"]])[None, :], LANE).astype(F32)
    logits, u = moe_router(h, p["g_moe"], w_r, b_r, tm=min(T, 512))
    glog = logits[:, :N_GROUPS]
    gsel = jnp.argmax(glog, axis=-1).astype(jnp.int32)
    pg = jnp.max(jax.nn.softmax(glog, axis=-1), axis=-1, keepdims=True)
    elog = logits[:, N_GROUPS:N_GROUPS + N_EXPERTS].reshape(T, N_GROUPS, EXPERTS_PER_GROUP)
    elog = jnp.take_along_axis(elog, gsel[:, None, None], axis=1)[:, 0]
    top_p, top_j = lax.top_k(jax.nn.softmax(elog, axis=-1), TOP_K)
    top_p = top_p / jnp.sum(top_p, axis=-1, keepdims=True)
    weight = pg * top_p
    flat_e = (gsel[:, None] * EXPERTS_PER_GROUP + top_j.astype(jnp.int32)).reshape(-1)
    TK = T * TOP_K
    onehot = (flat_e[:, None] == jnp.arange(N_EXPERTS, dtype=jnp.int32)[None, :]).astype(jnp.int32)
    rank = jnp.sum((jnp.cumsum(onehot, axis=0) - onehot) * onehot, axis=1)
    counts = jnp.sum(onehot, axis=0)
    pcounts = ((counts + tm - 1) // tm) * tm
    pends = jnp.cumsum(pcounts)
    dest = (pends - pcounts)[flat_e] + rank
    P = TK + N_EXPERTS * tm
    n_rb = P // tm
    row_tok = jnp.full((P,), T, jnp.int32).at[dest].set(jnp.repeat(jnp.arange(T, dtype=jnp.int32), TOP_K))
    blk_e = jnp.sum((pends[None, :] <= (jnp.arange(n_rb, dtype=jnp.int32) * tm)[:, None]).astype(jnp.int32), axis=1)
    blk_e = jnp.minimum(blk_e, N_EXPERTS - 1).astype(jnp.int32)
    n_used = (pends[-1] // tm).astype(jnp.int32).reshape(1)
    xr = jnp.concatenate([u, jnp.zeros((1, D), u.dtype)], axis=0)[row_tok]
    y = moe_experts(blk_e, n_used, xr, p["w_exp_gate"].astype(BF16), p["w_exp_up"].astype(BF16),
                    p["w_exp_down"].astype(BF16))
    d2 = dest.reshape(T, TOP_K)
    return h + (weight[:, 0:1] * y[d2[:, 0]].astype(F32) + weight[:, 1:2] * y[d2[:, 1]].astype(F32))


_LAYER_KEYS = ("g_mix", "w_in", "b_forget", "g_cq", "g_ckv", "w_uq", "w_ukv", "pe_k", "pe_v", "w_cmp_k1", "w_cmp_k2",
               "w_cmp_v1", "w_cmp_v2", "w_branch", "w_out", "g_mem_q", "g_mem_kv", "w_mem_q", "w_mem_kv", "w_mem_o",
               "g_moe", "w_router_group", "b_router_group", "w_router_expert", "b_router_expert", "w_exp_gate",
               "w_exp_up", "w_exp_down")


def kernel(x, mem, g_mix, w_in, b_forget, g_cq, g_ckv, w_uq, w_ukv, pe_k, pe_v, w_cmp_k1, w_cmp_k2, w_cmp_v1, w_cmp_v2, rel_bias, w_branch, w_out, g_mem_q, g_mem_kv, w_mem_q, w_mem_kv, w_mem_o, g_moe, w_router_group, b_router_group, w_router_expert, b_router_expert, w_exp_gate, w_exp_up, w_exp_down, g_final):
    B, S, D = x.shape
    T = B * S
    stacked = dict(g_mix=g_mix, w_in=w_in, b_forget=b_forget, g_cq=g_cq, g_ckv=g_ckv, w_uq=w_uq, w_ukv=w_ukv,
                   pe_k=pe_k, pe_v=pe_v, w_cmp_k1=w_cmp_k1, w_cmp_k2=w_cmp_k2, w_cmp_v1=w_cmp_v1, w_cmp_v2=w_cmp_v2,
                   w_branch=w_branch, w_out=w_out, g_mem_q=g_mem_q, g_mem_kv=g_mem_kv, w_mem_q=w_mem_q,
                   w_mem_kv=w_mem_kv, w_mem_o=w_mem_o, g_moe=g_moe, w_router_group=w_router_group,
                   b_router_group=b_router_group, w_router_expert=w_router_expert, b_router_expert=b_router_expert,
                   w_exp_gate=w_exp_gate, w_exp_up=w_exp_up, w_exp_down=w_exp_down)
    tabs = _position_tables(S, rel_bias.astype(F32))
    h = x.reshape(T, D).astype(F32)
    mem2 = mem.reshape(-1, D).astype(F32)
    for l in range(w_in.shape[0]):
        p = {k: stacked[k][l] for k in _LAYER_KEYS}
        h = _token_mixers(h, p, tabs, B, S)
        h = _memory_block(h, mem2, p, B, S)
        h = _moe_block(h, p)
    return rmsnorm_rows(h, g_final, tm=min(T, 512)).reshape(B, S, D)
```

```python
import functools
import math

import jax
import jax.numpy as jnp
import numpy as np
from jax import lax
from jax.experimental import pallas as pl
from jax.experimental.pallas import tpu as pltpu

F32 = jnp.float32
BF16 = jnp.bfloat16

EPS = 1e-6
NEG_INF = -1e30
LANE = 128

FOX_HEADS, FOX_DH = 8, 128
MLA_HEADS, MLA_NOPE, MLA_ROPE, MLA_DV = 8, 128, 64, 128
MLA_Q_RANK, MLA_KV_RANK = 768, 512
ROPE_THETA = 10000.0
NSA_HEADS, NSA_GROUPS, NSA_DK, NSA_DV = 8, 2, 192, 128
NSA_HPG = NSA_HEADS // NSA_GROUPS
NSA_DKP = 256
CMP_BLOCK, CMP_STRIDE, SEL_BLOCK, N_SEL, WINDOW = 32, 16, 64, 8, 512
SEL_FORCE = 1e6
REL_BUCKETS, REL_MAX_DIST = 32, 128
N_BRANCH, BRANCH_W = 3, 1024
MEM_HEADS, MEM_DH = 4, 128
N_GROUPS, EXPERTS_PER_GROUP, TOP_K = 4, 8, 2
N_EXPERTS = N_GROUPS * EXPERTS_PER_GROUP
FORGET_COLS, NSA_GATE_COLS = FOX_HEADS, NSA_HEADS * 3

CB_CKV, CB_KR, CB_CQ = 0, 4, 6
CB_NKC, CB_NQ = 12, 16
CB_FQ, CB_FK, CB_FV = 32, 40, 48
CB_NKS, CB_NKW = 56, 60
CB_NVC, CB_NVS, CB_NVW = 64, 66, 68
CB_MG = 72
Z_BLOCKS = 120

ATT_T = 256
ATT_HG = 4
NSA_T = 256
NSA_CMP_T = 512
MOE_TM = 512
MOE_DC = 256
VMEM_LIMIT = 56 * 1024 * 1024


def _cparams(sem):
    return pltpu.CompilerParams(dimension_semantics=sem, vmem_limit_bytes=VMEM_LIMIT)


def _sigmoid(x):
    return 1.0 / (1.0 + jnp.exp(-x))


def _rms_mm_kernel(x_ref, g_ref, w_ref, o_ref, u_ref, *, precise):
    @pl.when(pl.program_id(1) == 0)
    def _():
        x = x_ref[...].astype(F32)
        ms = jnp.mean(x * x, axis=-1, keepdims=True)
        u_ref[...] = (x * lax.rsqrt(ms + EPS) * g_ref[...]).astype(u_ref.dtype)

    prec = lax.Precision.HIGHEST if precise else None
    o_ref[...] = jnp.dot(u_ref[...], w_ref[...], preferred_element_type=F32, precision=prec).astype(o_ref.dtype)


def rms_matmul(x, xcol, g, w, out_dtype, tm, tn, precise=False):
    T = x.shape[0]
    K, N = w.shape
    return pl.pallas_call(
        functools.partial(_rms_mm_kernel, precise=precise),
        out_shape=jax.ShapeDtypeStruct((T, N), out_dtype),
        grid=(T // tm, N // tn),
        in_specs=[pl.BlockSpec((tm, K), lambda i, j: (i, xcol)),
                  pl.BlockSpec((1, K), lambda i, j: (0, 0)),
                  pl.BlockSpec((K, tn), lambda i, j: (0, j))],
        out_specs=pl.BlockSpec((tm, tn), lambda i, j: (i, j)),
        scratch_shapes=[pltpu.VMEM((tm, K), F32 if precise else BF16)],
        compiler_params=_cparams(("parallel", "arbitrary")),
        name="rms_matmul",
    )(x, g.reshape(1, K).astype(F32), w)


def _mla_q_kernel(x_ref, g_ref, w_ref, cos_ref, sin_ref, qn_ref, qr_ref):
    x = x_ref[...].astype(F32)
    ms = jnp.mean(x * x, axis=-1, keepdims=True)
    u = (x * lax.rsqrt(ms + EPS) * g_ref[...]).astype(BF16)
    y = jnp.dot(u, w_ref[...], preferred_element_type=F32)
    hw = MLA_HEADS * LANE
    qn_ref[...] = y[:, :hw].astype(qn_ref.dtype)
    cos = cos_ref[...]
    sin = sin_ref[...]
    for h in range(MLA_HEADS):
        a = y[:, hw + h * LANE: hw + (h + 1) * LANE]
        b = y[:, 2 * hw + h * LANE: 2 * hw + (h + 1) * LANE]
        qr_ref[:, h * LANE:(h + 1) * LANE] = (a * cos + b * sin).astype(qr_ref.dtype)


def mla_q_proj(z, g, w, cos, sin, S, tm):
    T = z.shape[0]
    K, N = w.shape
    hw = MLA_HEADS * LANE
    nsb = S // tm
    return pl.pallas_call(
        _mla_q_kernel,
        out_shape=(jax.ShapeDtypeStruct((T, hw), BF16), jax.ShapeDtypeStruct((T, hw), BF16)),
        grid=(T // tm,),
        in_specs=[pl.BlockSpec((tm, K), lambda i: (i, CB_CQ * LANE // MLA_Q_RANK)),
                  pl.BlockSpec((1, K), lambda i: (0, 0)),
                  pl.BlockSpec((K, N), lambda i: (0, 0)),
                  pl.BlockSpec((tm, LANE), lambda i: (i % nsb, 0)),
                  pl.BlockSpec((tm, LANE), lambda i: (i % nsb, 0))],
        out_specs=(pl.BlockSpec((tm, hw), lambda i: (i, 0)), pl.BlockSpec((tm, hw), lambda i: (i, 0))),
        compiler_params=_cparams(("parallel",)),
        name="mla_q_proj",
    )(z, g.reshape(1, K).astype(F32), w, cos, sin)


def _causal_attn_kernel(*refs, t, hg, two_part, decay):
    refs = list(refs)
    q_ref, k_ref, v_ref = refs[:3]
    pos = 3
    if two_part:
        q2_ref, k2_ref = refs[pos:pos + 2]
        pos += 2
    if decay:
        ck_ref = refs[pos]
        pos += 1
    o_ref = refs[pos]
    if two_part:
        kcat_ref = refs[pos + 1]
    qi = pl.program_id(2)
    dn = (((1,), (1,)), ((), ()))

    if two_part:
        @pl.when(qi == 0)
        def _():
            for j in range(hg):
                kcat_ref[j, :, :LANE] = k_ref[0, :, j * LANE:(j + 1) * LANE]
                kcat_ref[j, :, LANE:] = k2_ref[0]

    qs = []
    for j in range(hg):
        qj = q_ref[0, :, j * LANE:(j + 1) * LANE]
        if two_part:
            qj = jnp.concatenate([qj, q2_ref[0, :, j * LANE:(j + 1) * LANE]], axis=1)
        qs.append(qj)

    def step(kb, carry, masked):
        off = pl.multiple_of(kb * t, t)
        heads = range(hg)
        ss = []
        for j in heads:
            k = kcat_ref[j, pl.ds(off, t), :] if two_part else k_ref[0, pl.ds(off, t), j * LANE:(j + 1) * LANE]
            ss.append(lax.dot_general(k, qs[j], dn, preferred_element_type=F32))
        if decay:
            ss = [ss[j] - ck_ref[0, 0, pl.ds(off, t), j:j + 1] for j in heads]
        if masked:
            r = lax.broadcasted_iota(jnp.int32, (t, t), 0)
            c = lax.broadcasted_iota(jnp.int32, (t, t), 1)
            ss = [jnp.where(r <= c, s, NEG_INF) for s in ss]
        ms = [jnp.maximum(carry[j][0], jnp.max(ss[j], axis=0, keepdims=True)) for j in heads]
        ps = [jnp.exp(ss[j] - ms[j]) for j in heads]
        out = []
        for j in heads:
            m, l, acc = carry[j]
            a = jnp.exp(m - ms[j])
            l = a * l + jnp.sum(ps[j], axis=0, keepdims=True)
            v = v_ref[0, pl.ds(off, t), j * LANE:(j + 1) * LANE]
            acc = a * acc + lax.dot_general(v, ps[j].astype(BF16), (((0,), (0,)), ((), ())),
                                            preferred_element_type=F32)
            out.append((ms[j], l, acc))
        return tuple(out)

    init = tuple((jnp.full((1, t), NEG_INF, F32), jnp.zeros((1, t), F32), jnp.zeros((LANE, t), F32))
                 for _ in range(hg))
    carry = lax.fori_loop(0, qi, lambda kb, c: step(kb, c, False), init)
    carry = step(qi, carry, True)
    for j in range(hg):
        _, l, acc = carry[j]
        o_ref[0, :, j * LANE:(j + 1) * LANE] = (acc / l).T.astype(o_ref.dtype)


def causal_attention(q, qcb, k, kcb, v, vcb, heads, q2=None, k2=None, cum=None):
    B, S, _ = q.shape
    t, hg = ATT_T, ATT_HG
    w = hg * LANE
    two_part, decay = q2 is not None, cum is not None
    in_specs = [pl.BlockSpec((1, t, w), lambda b, h, i: (b, i, qcb // hg + h)),
                pl.BlockSpec((1, S, w), lambda b, h, i: (b, 0, kcb // hg + h)),
                pl.BlockSpec((1, S, w), lambda b, h, i: (b, 0, vcb // hg + h))]
    args = [q, k, v]
    scratch = []
    if two_part:
        in_specs += [pl.BlockSpec((1, t, w), lambda b, h, i: (b, i, h)),
                     pl.BlockSpec((1, S, LANE), lambda b, h, i: (b, 0, 0))]
        args += [q2, k2]
        scratch = [pltpu.VMEM((hg, S, 2 * LANE), BF16)]
    if decay:
        in_specs += [pl.BlockSpec((1, 1, S, hg), lambda b, h, i: (b, h, 0, 0))]
        args += [jnp.transpose(cum.reshape(B, S, heads // hg, hg), (0, 2, 1, 3))]
    return pl.pallas_call(
        functools.partial(_causal_attn_kernel, t=t, hg=hg, two_part=two_part, decay=decay),
        out_shape=jax.ShapeDtypeStruct((B, S, heads * LANE), BF16),
        grid=(B, heads // hg, S // t),
        in_specs=in_specs,
        out_specs=pl.BlockSpec((1, t, w), lambda b, h, i: (b, i, h)),
        scratch_shapes=scratch,
        compiler_params=_cparams(("parallel", "parallel", "arbitrary")),
        name="causal_attention",
    )(*args)


def _gelu_tanh(x):
    return 0.5 * x * (1.0 + jnp.tanh(math.sqrt(2.0 / math.pi) * (x + 0.044715 * (x * x * x))))


def _compress_kernel(x_ref, w1a_ref, w1b_ref, pe_ref, w2_ref, o_ref):
    x = x_ref[0]
    a = jnp.dot(x, w1a_ref[...], preferred_element_type=F32)
    b = jnp.dot(x, w1b_ref[...], preferred_element_type=F32)
    nc = a.shape[0]
    b_next = pltpu.roll(b, nc - 1, 0)
    pe_term = jnp.dot(pe_ref[0:1, :], w1a_ref[...], preferred_element_type=F32) + \
        jnp.dot(pe_ref[1:2, :], w1b_ref[...], preferred_element_type=F32)
    hid = _gelu_tanh(a + b_next + pe_term)
    o_ref[0] = jnp.dot(hid.astype(BF16), w2_ref[...], preferred_element_type=F32).astype(o_ref.dtype)


def compress(x, w1a, w1b, pe2, w2):
    BG, NC, KD = x.shape
    dp = w2.shape[1]
    return pl.pallas_call(
        _compress_kernel,
        out_shape=jax.ShapeDtypeStruct((BG, NC, dp), BF16),
        grid=(BG,),
        in_specs=[pl.BlockSpec((1, NC, KD), lambda i: (i, 0, 0)),
                  pl.BlockSpec((KD, dp), lambda i: (0, 0)),
                  pl.BlockSpec((KD, dp), lambda i: (0, 0)),
                  pl.BlockSpec((2, KD), lambda i: (0, 0)),
                  pl.BlockSpec((dp, dp), lambda i: (0, 0))],
        out_specs=pl.BlockSpec((1, NC, dp), lambda i: (i, 0, 0)),
        compiler_params=_cparams(("parallel",)),
        name="nsa_compress",
    )(x, w1a, w1b, pe2, w2)


def _nsa_cmp_kernel(q_ref, kc_ref, vc_ref, bias_ref, ov_ref, o_ref, ind_ref, *, t, n_sel):
    qi = pl.program_id(2)
    ncp = kc_ref.shape[2]
    kc = kc_ref[0, 0]
    vc = vc_ref[0, 0]
    row = lax.broadcasted_iota(jnp.int32, (t, ncp), 0) + qi * t
    col = lax.broadcasted_iota(jnp.int32, (t, ncp), 1)
    valid = row >= CMP_STRIDE * col + (CMP_BLOCK - 1)
    dn = (((1,), (1,)), ((), ()))
    heads = range(NSA_HPG)
    ss = [lax.dot_general(q_ref[0, :, h * NSA_DKP:(h + 1) * NSA_DKP], kc, dn, preferred_element_type=F32)
          for h in heads]
    ss = [jnp.where(valid, ss[h] + bias_ref[h], NEG_INF) for h in heads]
    es = [jnp.exp(ss[h] - jnp.max(ss[h], axis=-1, keepdims=True)) for h in heads]
    ps = [jnp.where(valid, es[h] / jnp.sum(es[h], axis=-1, keepdims=True), 0.0) for h in heads]
    for h in heads:
        o_ref[0, :, h * NSA_DV:(h + 1) * NSA_DV] = jnp.dot(ps[h].astype(BF16), vc,
                                                          preferred_element_type=F32).astype(o_ref.dtype)
    psum = functools.reduce(lambda x, y: x + y, ps)
    p_hi = psum.astype(BF16)
    p_lo = (psum - p_hi.astype(F32)).astype(BF16)
    imp = jnp.dot(p_hi, ov_ref[...], preferred_element_type=F32) + jnp.dot(p_lo, ov_ref[...], preferred_element_type=F32)
    nbp = imp.shape[1]
    blk = lax.broadcasted_iota(jnp.int32, (t, nbp), 1)
    cur = (lax.broadcasted_iota(jnp.int32, (t, nbp), 0) + qi * t) // SEL_BLOCK
    forced = (blk == 0) | (blk == cur) | (blk == cur - 1)
    score = jnp.where(blk <= cur, imp + jnp.where(forced, SEL_FORCE, 0.0), NEG_INF)
    sel = jnp.zeros((t, nbp), F32)
    for _ in range(n_sel):
        mx = jnp.max(score, axis=-1, keepdims=True)
        first = jnp.min(jnp.where(score == mx, blk, nbp), axis=-1, keepdims=True)
        hit = blk == first
        sel = jnp.where(hit, 1.0, sel)
        score = jnp.where(hit, -jnp.inf, score)
    ind_ref[0, 0] = sel.astype(ind_ref.dtype)


def nsa_cmp_select(z3, kc, vc, bias_c, overlap, n_sel):
    B, S, _ = z3.shape
    G = NSA_GROUPS
    t = min(NSA_CMP_T, S)
    ncp = kc.shape[2]
    nbp = overlap.shape[1]
    qw = NSA_HPG * NSA_DKP
    ow = NSA_HPG * NSA_DV
    return pl.pallas_call(
        functools.partial(_nsa_cmp_kernel, t=t, n_sel=n_sel),
        out_shape=(jax.ShapeDtypeStruct((B, S, G * ow), BF16), jax.ShapeDtypeStruct((B, G, S, nbp), BF16)),
        grid=(B, G, S // t),
        in_specs=[pl.BlockSpec((1, t, qw), lambda b, g, i: (b, i, CB_NQ * LANE // qw + g)),
                  pl.BlockSpec((1, 1, ncp, NSA_DKP), lambda b, g, i: (b, g, 0, 0)),
                  pl.BlockSpec((1, 1, ncp, NSA_DV), lambda b, g, i: (b, g, 0, 0)),
                  pl.BlockSpec((NSA_HPG, t, ncp), lambda b, g, i: (g, i, 0)),
                  pl.BlockSpec((ncp, nbp), lambda b, g, i: (0, 0))],
        out_specs=(pl.BlockSpec((1, t, ow), lambda b, g, i: (b, i, g)),
                   pl.BlockSpec((1, 1, t, nbp), lambda b, g, i: (b, g, i, 0))),
        compiler_params=_cparams(("parallel", "parallel", "arbitrary")),
        name="nsa_cmp_select",
    )(z3, kc, vc, bias_c, overlap)


def _nsa_sw_kernel(q_ref, ks_ref, vs_ref, kw_ref, vw_ref, ind_ref, e_ref, bias_ref, gate_ref, oc_ref, o_ref,
                   *, t):
    qi = pl.program_id(1)
    hp, G = NSA_HPG, NSA_GROUPS
    q4 = [jnp.concatenate([q_ref[0, :, (g * hp + h) * NSA_DKP:(g * hp + h + 1) * NSA_DKP] for h in range(hp)], axis=0)
          for g in range(G)]
    inds = [ind_ref[0, g] for g in range(G)]
    dn = (((1,), (1,)), ((), ()))
    ri = lax.broadcasted_iota(jnp.int32, (t, t), 0)
    ci = lax.broadcasted_iota(jnp.int32, (t, t), 1)

    def step(kb, carry, k_ref, v_ref, selected):
        off = pl.multiple_of(kb * t, t)
        kind = jnp.minimum(qi - kb, 2)
        d = (qi - kb) * t + ci - ri
        near = (d >= 0) if selected else (d >= 0) & (d < WINDOW)
        groups = range(G)
        ss = [lax.dot_general(k_ref[0, pl.ds(off, t), g * NSA_DKP:(g + 1) * NSA_DKP], q4[g], dn,
                              preferred_element_type=F32) for g in groups]
        negs = []
        for g in groups:
            mask = near
            if selected:
                hit = jnp.dot(e_ref[pl.ds(off, t), :], inds[g], preferred_element_type=F32)
                mask = near & (hit > 0.5)
            neg = jnp.where(mask, 0.0, NEG_INF)
            negs.append(jnp.concatenate([neg] * hp, axis=1))
        ss = [ss[g] + bias_ref[g, kind] + negs[g] for g in groups]
        ms = [jnp.maximum(carry[g][0], jnp.max(ss[g], axis=0, keepdims=True)) for g in groups]
        ps = [jnp.exp(ss[g] - ms[g]) for g in groups]
        out = []
        for g in groups:
            m, l, acc = carry[g]
            a = jnp.exp(m - ms[g])
            l = a * l + jnp.sum(ps[g], axis=0, keepdims=True)
            v = v_ref[0, pl.ds(off, t), g * NSA_DV:(g + 1) * NSA_DV]
            acc = a * acc + lax.dot_general(v, ps[g].astype(BF16), (((0,), (0,)), ((), ())),
                                            preferred_element_type=F32)
            out.append((ms[g], l, acc))
        return tuple(out)

    init = tuple((jnp.full((1, hp * t), NEG_INF, F32), jnp.zeros((1, hp * t), F32), jnp.zeros((NSA_DV, hp * t), F32))
                 for _ in range(G))
    sel = lax.fori_loop(0, qi + 1, lambda kb, c: step(kb, c, ks_ref, vs_ref, True), init)
    lo = jnp.maximum(qi - WINDOW // t, 0)
    win = lax.fori_loop(lo, qi + 1, lambda kb, c: step(kb, c, kw_ref, vw_ref, False), init)
    for g in range(G):
        o_s = sel[g][2] / sel[g][1]
        o_w = win[g][2] / win[g][1]
        gates = _sigmoid(gate_ref[0, g])
        for h in range(hp):
            r = slice(h * t, (h + 1) * t)
            c = slice((g * hp + h) * NSA_DV, (g * hp + h + 1) * NSA_DV)
            o = gates[:, 3 * h:3 * h + 1] * oc_ref[0, :, c].astype(F32) + gates[:, 3 * h + 1:3 * h + 2] * o_s[:, r].T + \
                gates[:, 3 * h + 2:3 * h + 3] * o_w[:, r].T
            o_ref[0, :, c] = o.astype(o_ref.dtype)


def nsa_sel_win(z3, ind, expand, bias_sw, gates, o_cmp):
    B, S, _ = z3.shape
    G = NSA_GROUPS
    t = NSA_T
    qw = NSA_HEADS * NSA_DKP
    ow = NSA_HEADS * NSA_DV
    kw = G * NSA_DKP
    vw = G * NSA_DV
    nbp = ind.shape[2]
    kspec = lambda cb: pl.BlockSpec((1, S, kw), lambda b, i: (b, 0, cb * LANE // kw))
    vspec = lambda cb: pl.BlockSpec((1, S, vw), lambda b, i: (b, 0, cb * LANE // vw))
    return pl.pallas_call(
        functools.partial(_nsa_sw_kernel, t=t),
        out_shape=jax.ShapeDtypeStruct((B, S, ow), BF16),
        grid=(B, S // t),
        in_specs=[pl.BlockSpec((1, t, qw), lambda b, i: (b, i, CB_NQ * LANE // qw)),
                  kspec(CB_NKS), vspec(CB_NVS), kspec(CB_NKW), vspec(CB_NVW),
                  pl.BlockSpec((1, G, nbp, t), lambda b, i: (b, 0, 0, i)),
                  pl.BlockSpec((S, nbp), lambda b, i: (0, 0)),
                  pl.BlockSpec((G, 3, t, NSA_HPG * t), lambda b, i: (0, 0, 0, 0)),
                  pl.BlockSpec((1, G, t, LANE), lambda b, i: (b, 0, i, 0)),
                  pl.BlockSpec((1, t, ow), lambda b, i: (b, i, 0))],
        out_specs=pl.BlockSpec((1, t, ow), lambda b, i: (b, i, 0)),
        compiler_params=_cparams(("parallel", "arbitrary")),
        name="nsa_sel_win",
    )(z3, z3, z3, z3, z3, ind, expand, bias_sw, gates, o_cmp)


def _merge_kernel(of_ref, om_ref, on_ref, wb_ref, g0_ref, g1_ref, g2_ref, o_ref):
    acc = None
    for n, (o_r, g_r) in enumerate(((of_ref, g0_ref), (om_ref, g1_ref), (on_ref, g2_ref))):
        y = jnp.dot(o_r[...], wb_ref[n], preferred_element_type=F32)
        y = _sigmoid(g_r[...].astype(F32)) * y
        acc = y if acc is None else acc + y
    o_ref[...] = acc.astype(o_ref.dtype)


def merge_branches(o_fox, o_mla, o_nsa, wb, z, tm, tn):
    T = o_fox.shape[0]
    D = wb.shape[2]
    gspec = lambda n: pl.BlockSpec((tm, tn), lambda i, j: (i, (CB_MG * LANE + n * D) // tn + j))
    ospec = pl.BlockSpec((tm, BRANCH_W), lambda i, j: (i, 0))
    return pl.pallas_call(
        _merge_kernel,
        out_shape=jax.ShapeDtypeStruct((T, D), BF16),
        grid=(T // tm, D // tn),
        in_specs=[ospec, ospec, ospec,
                  pl.BlockSpec((N_BRANCH, BRANCH_W, tn), lambda i, j: (0, 0, j)),
                  gspec(0), gspec(1), gspec(2)],
        out_specs=pl.BlockSpec((tm, tn), lambda i, j: (i, j)),
        compiler_params=_cparams(("parallel", "arbitrary")),
        name="merge_branches",
    )(o_fox, o_mla, o_nsa, wb, z, z, z)


def _mm_res_kernel(a_ref, w_ref, r_ref, o_ref):
    o_ref[...] = r_ref[...] + jnp.dot(a_ref[...], w_ref[...], preferred_element_type=F32)


def matmul_residual(a, w, res, tm, tn):
    T, K = a.shape
    N = w.shape[1]
    return pl.pallas_call(
        _mm_res_kernel,
        out_shape=jax.ShapeDtypeStruct((T, N), F32),
        grid=(T // tm, N // tn),
        in_specs=[pl.BlockSpec((tm, K), lambda i, j: (i, 0)),
                  pl.BlockSpec((K, tn), lambda i, j: (0, j)),
                  pl.BlockSpec((tm, tn), lambda i, j: (i, j))],
        out_specs=pl.BlockSpec((tm, tn), lambda i, j: (i, j)),
        compiler_params=_cparams(("parallel", "arbitrary")),
        name="matmul_residual",
    )(a, w, res)


def _mem_attn_kernel(h_ref, g_ref, wq_ref, kv_ref, wo_ref, o_ref):
    x = h_ref[0]
    ms = jnp.mean(x * x, axis=-1, keepdims=True)
    u = (x * lax.rsqrt(ms + EPS) * g_ref[...]).astype(BF16)
    q = jnp.dot(u, wq_ref[...], preferred_element_type=F32).astype(BF16)
    dn = (((1,), (1,)), ((), ()))
    hw = MEM_HEADS * MEM_DH
    heads = range(MEM_HEADS)
    cs = [slice(h * MEM_DH, (h + 1) * MEM_DH) for h in heads]
    ss = [lax.dot_general(q[:, cs[h]], kv_ref[0, :, cs[h]], dn, preferred_element_type=F32) for h in heads]
    es = [jnp.exp(ss[h] - jnp.max(ss[h], axis=-1, keepdims=True)) for h in heads]
    ps = [es[h] / jnp.sum(es[h], axis=-1, keepdims=True) for h in heads]
    outs = [jnp.dot(ps[h].astype(BF16), kv_ref[0, :, hw + h * MEM_DH: hw + (h + 1) * MEM_DH],
                    preferred_element_type=F32).astype(BF16) for h in heads]
    o = jnp.concatenate(outs, axis=1)
    o_ref[0] = x + jnp.dot(o, wo_ref[...], preferred_element_type=F32)


def memory_attention_block(h3, g, wq, kv, wo, tq):
    B, S, D = h3.shape
    M = kv.shape[1]
    hw = MEM_HEADS * MEM_DH
    return pl.pallas_call(
        _mem_attn_kernel,
        out_shape=jax.ShapeDtypeStruct((B, S, D), F32),
        grid=(B, S // tq),
        in_specs=[pl.BlockSpec((1, tq, D), lambda b, i: (b, i, 0)),
                  pl.BlockSpec((1, D), lambda b, i: (0, 0)),
                  pl.BlockSpec((D, hw), lambda b, i: (0, 0)),
                  pl.BlockSpec((1, M, 2 * hw), lambda b, i: (b, 0, 0)),
                  pl.BlockSpec((hw, D), lambda b, i: (0, 0))],
        out_specs=pl.BlockSpec((1, tq, D), lambda b, i: (b, i, 0)),
        compiler_params=_cparams(("parallel", "arbitrary")),
        name="memory_attention",
    )(h3, g.reshape(1, D).astype(F32), wq, kv, wo)


def _router_kernel(h_ref, g_ref, w_ref, b_ref, lg_ref, u_ref):
    x = h_ref[...]
    ms = jnp.mean(x * x, axis=-1, keepdims=True)
    u = x * lax.rsqrt(ms + EPS) * g_ref[...]
    u_ref[...] = u.astype(u_ref.dtype)
    lg_ref[...] = jnp.dot(u, w_ref[...], preferred_element_type=F32, precision=lax.Precision.HIGHEST) + b_ref[...]


def moe_router(h, g, w, b, tm):
    T, D = h.shape
    N = w.shape[1]
    return pl.pallas_call(
        _router_kernel,
        out_shape=(jax.ShapeDtypeStruct((T, N), F32), jax.ShapeDtypeStruct((T, D), BF16)),
        grid=(T // tm,),
        in_specs=[pl.BlockSpec((tm, D), lambda i: (i, 0)),
                  pl.BlockSpec((1, D), lambda i: (0, 0)),
                  pl.BlockSpec((D, N), lambda i: (0, 0)),
                  pl.BlockSpec((1, N), lambda i: (0, 0))],
        out_specs=(pl.BlockSpec((tm, N), lambda i: (i, 0)), pl.BlockSpec((tm, D), lambda i: (i, 0))),
        compiler_params=_cparams(("parallel",)),
        name="moe_router",
    )(h, g.reshape(1, D).astype(F32), w, b)


def _moe_kernel(be_ref, nu_ref, x_ref, wg_ref, wu_ref, wd_ref, o_ref, acc_ref):
    i = pl.program_id(0)
    j = pl.program_id(1)
    last = pl.num_programs(1) - 1
    used = i < nu_ref[0]

    @pl.when(used)
    def _():
        x = x_ref[...]
        a = jnp.dot(x, wg_ref[0, 0].astype(BF16), preferred_element_type=F32)
        b = jnp.dot(x, wu_ref[0, 0].astype(BF16), preferred_element_type=F32)
        hdn = (a * _sigmoid(a) * b).astype(BF16)
        y = jnp.dot(hdn, wd_ref[0, 0].astype(BF16), preferred_element_type=F32)

        @pl.when(j == 0)
        def _():
            acc_ref[...] = y

        @pl.when(j > 0)
        def _():
            acc_ref[...] += y

        @pl.when(j == last)
        def _():
            o_ref[...] = acc_ref[...].astype(o_ref.dtype)

    @pl.when(jnp.logical_not(used) & (j == last))
    def _():
        o_ref[...] = jnp.zeros(o_ref.shape, o_ref.dtype)


def moe_experts(blk_e, n_used, xr, wg, wu, wd, layer):
    P, D = xr.shape
    De = wg.shape[3]
    tm, dc = MOE_TM, MOE_DC
    nj = De // dc
    jj = lambda i, j, nu: jnp.where(i < nu[0], j, nj - 1)
    grid_spec = pltpu.PrefetchScalarGridSpec(
        num_scalar_prefetch=2,
        grid=(P // tm, nj),
        in_specs=[pl.BlockSpec((tm, D), lambda i, j, be, nu: (i, 0)),
                  pl.BlockSpec((1, 1, D, dc), lambda i, j, be, nu: (layer, be[i], 0, jj(i, j, nu))),
                  pl.BlockSpec((1, 1, D, dc), lambda i, j, be, nu: (layer, be[i], 0, jj(i, j, nu))),
                  pl.BlockSpec((1, 1, dc, D), lambda i, j, be, nu: (layer, be[i], jj(i, j, nu), 0))],
        out_specs=pl.BlockSpec((tm, D), lambda i, j, be, nu: (i, 0)),
        scratch_shapes=[pltpu.VMEM((tm, D), F32)],
    )
    return pl.pallas_call(
        _moe_kernel,
        out_shape=jax.ShapeDtypeStruct((P, D), BF16),
        grid_spec=grid_spec,
        compiler_params=_cparams(("arbitrary", "arbitrary")),
        name="moe_experts",
    )(blk_e, n_used, xr, wg, wu, wd)


def _rmsnorm_kernel(x_ref, g_ref, o_ref):
    x = x_ref[...]
    ms = jnp.mean(x * x, axis=-1, keepdims=True)
    o_ref[...] = x * lax.rsqrt(ms + EPS) * g_ref[...]


def rmsnorm_rows(x, g, tm):
    T, D = x.shape
    return pl.pallas_call(
        _rmsnorm_kernel,
        out_shape=jax.ShapeDtypeStruct((T, D), F32),
        grid=(T // tm,),
        in_specs=[pl.BlockSpec((tm, D), lambda i: (i, 0)), pl.BlockSpec((1, D), lambda i: (0, 0))],
        out_specs=pl.BlockSpec((tm, D), lambda i: (i, 0)),
        compiler_params=_cparams(("parallel",)),
        name="final_rmsnorm",
    )(x, g.reshape(1, D).astype(F32))


def _pad_cols(w, width):
    return jnp.pad(w, ((0, 0), (0, width - w.shape[1])))


def _pack_w_in(w):
    D = w.shape[0]
    offs = np.cumsum([0, 1024, 1024, 1024, FORGET_COLS, MLA_Q_RANK, MLA_KV_RANK, MLA_ROPE,
                      NSA_HEADS * NSA_DK, NSA_GROUPS * NSA_DK, NSA_GROUPS * NSA_DV, NSA_GROUPS * NSA_DK,
                      NSA_GROUPS * NSA_DV, NSA_GROUPS * NSA_DK, NSA_GROUPS * NSA_DV, NSA_GATE_COLS, N_BRANCH * D])
    seg = lambda i: w[:, offs[i]:offs[i + 1]]
    fq, fk, fv, ff, mcq, mckv, mkr, nq, nkc, nvc, nks, nvs, nkw, nvw, ngt, mg = [seg(i) for i in range(16)]
    half = MLA_ROPE // 2
    kr_rot = jnp.concatenate([-mkr[:, half:], mkr[:, :half]], axis=1)
    padk = lambda a, n: jnp.pad(a.reshape(D, n, NSA_DK), ((0, 0), (0, 0), (0, NSA_DKP - NSA_DK))).reshape(D, n * NSA_DKP)
    zeros = lambda nb: jnp.zeros((D, nb * LANE), w.dtype)
    cols = [mckv, mkr, kr_rot, zeros(1), mcq, padk(nkc, NSA_GROUPS), padk(nq * NSA_DK ** -0.5, NSA_HEADS),
            fq * FOX_DH ** -0.5, fk, fv, padk(nks, NSA_GROUPS), padk(nkw, NSA_GROUPS), nvc, nvs, nvw, zeros(2), mg]
    wz = jnp.concatenate(cols, axis=1).astype(BF16)
    assert wz.shape[1] == CB_MG * LANE + N_BRANCH * D
    w_small = _pad_cols(jnp.concatenate([ff, ngt], axis=1), LANE)
    return wz, w_small


def _pack_w_uq(w):
    K = w.shape[0]
    w3 = w.reshape(K, MLA_HEADS, MLA_NOPE + MLA_ROPE) * (MLA_NOPE + MLA_ROPE) ** -0.5
    nope = w3[:, :, :MLA_NOPE].reshape(K, MLA_HEADS * MLA_NOPE)
    r = w3[:, :, MLA_NOPE:]
    half = MLA_ROPE // 2
    r_rot = jnp.concatenate([-r[:, :, half:], r[:, :, :half]], axis=2)
    padr = lambda a: jnp.pad(a, ((0, 0), (0, 0), (0, LANE - MLA_ROPE))).reshape(K, MLA_HEADS * LANE)
    return jnp.concatenate([nope, padr(r), padr(r_rot)], axis=1).astype(BF16)


def _pack_w_ukv(w):
    K = w.shape[0]
    w3 = w.reshape(K, MLA_HEADS, MLA_NOPE + MLA_DV)
    return jnp.concatenate([w3[:, :, :MLA_NOPE].reshape(K, -1), w3[:, :, MLA_NOPE:].reshape(K, -1)], axis=1).astype(BF16)


def _t5_bucket(dist):
    dist = jnp.maximum(dist, 0)
    exact = REL_BUCKETS // 2
    df = jnp.maximum(dist, 1).astype(F32)
    large = exact + (jnp.log(df / exact) / math.log(REL_MAX_DIST / exact) * (REL_BUCKETS - exact)).astype(jnp.int32)
    large = jnp.minimum(large, REL_BUCKETS - 1)
    return jnp.where(dist < exact, dist, large)


def _position_tables(S, rel_bias):
    t = NSA_T
    half = MLA_ROPE // 2
    inv = ROPE_THETA ** (-jnp.arange(half, dtype=F32) / half)
    ang = jnp.arange(S, dtype=F32)[:, None] * inv
    c, s = jnp.cos(ang), jnp.sin(ang)
    cos = _pad_cols(jnp.concatenate([c, c], axis=1), LANE)
    sin = _pad_cols(jnp.concatenate([s, s], axis=1), LANE)
    ncp = max(S // CMP_STRIDE, LANE)
    pos = jnp.arange(S)

    def bias_of(dist):
        onehot = jax.nn.one_hot(_t5_bucket(dist), REL_BUCKETS, dtype=F32)
        return jnp.einsum("...b,bh->h...", onehot, rel_bias, precision=lax.Precision.HIGHEST)

    bias_c = bias_of(pos[:, None] - (CMP_STRIDE * jnp.arange(ncp)[None, :] + CMP_BLOCK - 1))
    i = jnp.arange(t)
    bias_sw = jnp.stack([bias_of(k * t + i[:, None] - i[None, :]) for k in range(3)], axis=1)
    bias_sw = bias_sw.reshape(NSA_GROUPS, NSA_HPG, 3, t, t).transpose(0, 2, 4, 1, 3).reshape(NSA_GROUPS, 3, t, NSA_HPG * t)
    n_cmp = (S - CMP_BLOCK) // CMP_STRIDE + 1
    n_blk = S // SEL_BLOCK
    nbp = max(n_blk, LANE)
    cstart = CMP_STRIDE * jnp.arange(ncp)
    sstart = SEL_BLOCK * jnp.arange(nbp)
    ov = jnp.clip(jnp.minimum(cstart[:, None] + CMP_BLOCK, sstart[None, :] + SEL_BLOCK)
                  - jnp.maximum(cstart[:, None], sstart[None, :]), 0, None).astype(F32) / CMP_STRIDE
    ov = jnp.where((jnp.arange(ncp)[:, None] < n_cmp) & (jnp.arange(nbp)[None, :] < n_blk), ov, 0.0).astype(BF16)
    expand = ((pos[:, None] // SEL_BLOCK) == jnp.arange(nbp)[None, :]).astype(BF16)
    return cos, sin, bias_c, bias_sw, ov, expand


def _token_mixers(h, p, tabs, B, S):
    T, D = h.shape
    cos, sin, bias_c, bias_sw, overlap, expand = tabs
    wz, w_small = _pack_w_in(p["w_in"])
    z = rms_matmul(h, 0, p["g_mix"], wz, BF16, tm=min(T, 1024), tn=1024)
    zs = rms_matmul(h, 0, p["g_mix"], w_small, F32, tm=min(T, 512), tn=LANE, precise=True)
    z3 = z.reshape(B, S, Z_BLOCKS * LANE)

    log_f = jax.nn.log_sigmoid(zs[:, :FORGET_COLS] + p["b_forget"].astype(F32)).reshape(B, S, FOX_HEADS)
    cum = jnp.cumsum(log_f, axis=1)
    o_fox = causal_attention(z3, CB_FQ, z3, CB_FK, z3, CB_FV, FOX_HEADS, cum=cum)

    q_nope, q_rope = mla_q_proj(z, p["g_cq"], _pack_w_uq(p["w_uq"]), cos, sin, S, tm=min(S, 512))
    kv = rms_matmul(z, CB_CKV * LANE // MLA_KV_RANK, p["g_ckv"], _pack_w_ukv(p["w_ukv"]), BF16, tm=min(T, 1024), tn=1024)
    kr = z3[:, :, CB_KR * LANE:(CB_KR + 1) * LANE].astype(F32)
    kr = kr[..., :MLA_ROPE] * cos[None, :, :MLA_ROPE] + kr[..., MLA_ROPE:] * sin[None, :, :MLA_ROPE]
    k_rope = jnp.pad(kr, ((0, 0), (0, 0), (0, LANE - MLA_ROPE))).astype(BF16)
    hw = MLA_HEADS * LANE
    kv3 = kv.reshape(B, S, 2 * hw)
    o_mla = causal_attention(q_nope.reshape(B, S, hw), 0, kv3, 0, kv3, MLA_HEADS, MLA_HEADS,
                             q2=q_rope.reshape(B, S, hw), k2=k_rope)

    G = NSA_GROUPS
    NC = S // CMP_STRIDE
    ncp = bias_c.shape[2]

    def compress_branch(cb, dp, d, pe, w1, w2):
        raw = z3[:, :, cb * LANE: cb * LANE + G * dp].reshape(B, NC, CMP_STRIDE, G, dp)
        x = jnp.transpose(raw, (0, 3, 1, 2, 4)).reshape(B * G, NC, CMP_STRIDE * dp)
        w1p = jnp.pad(w1.reshape(CMP_BLOCK, d, d), ((0, 0), (0, dp - d), (0, dp - d))).astype(BF16)
        w1a = w1p[:CMP_STRIDE].reshape(CMP_STRIDE * dp, dp)
        w1b = w1p[CMP_STRIDE:].reshape(CMP_STRIDE * dp, dp)
        pe2 = jnp.pad(pe, ((0, 0), (0, dp - d))).reshape(2, CMP_STRIDE * dp).astype(BF16)
        w2p = jnp.pad(w2, ((0, dp - d), (0, dp - d))).astype(BF16)
        out = compress(x, w1a, w1b, pe2, w2p).reshape(B, G, NC, dp)
        return jnp.pad(out, ((0, 0), (0, 0), (0, ncp - NC), (0, 0)))

    kc = compress_branch(CB_NKC, NSA_DKP, NSA_DK, p["pe_k"], p["w_cmp_k1"], p["w_cmp_k2"])
    vc = compress_branch(CB_NVC, NSA_DV, NSA_DV, p["pe_v"], p["w_cmp_v1"], p["w_cmp_v2"])
    o_cmp, ind = nsa_cmp_select(z3, kc, vc, bias_c, overlap, min(N_SEL, S // SEL_BLOCK))
    gl = zs[:, FORGET_COLS:FORGET_COLS + NSA_GATE_COLS].reshape(B, S, G, NSA_HPG * 3)
    gl = jnp.pad(jnp.transpose(gl, (0, 2, 1, 3)), ((0, 0), (0, 0), (0, 0), (0, LANE - NSA_HPG * 3)))
    o_nsa = nsa_sel_win(z3, jnp.swapaxes(ind, 2, 3), expand, bias_sw, gl, o_cmp)

    merged = merge_branches(o_fox.reshape(T, -1), o_mla.reshape(T, -1), o_nsa.reshape(T, -1),
                            p["w_branch"].astype(BF16), z, tm=min(T, 1024), tn=512)
    return matmul_residual(merged, p["w_out"].astype(BF16), h, tm=min(T, 1024), tn=512)


def _memory_block(h, mem2, p, B, S):
    T, D = h.shape
    kv = rms_matmul(mem2, 0, p["g_mem_kv"], p["w_mem_kv"].astype(BF16), BF16, tm=min(mem2.shape[0], 512), tn=512)
    out = memory_attention_block(h.reshape(B, S, D), p["g_mem_q"], (p["w_mem_q"] * MEM_DH ** -0.5).astype(BF16),
                                 kv.reshape(B, -1, kv.shape[1]), p["w_mem_o"].astype(BF16), tq=min(S, 512))
    return out.reshape(T, D)


def _moe_block(h, p, experts, layer):
    T, D = h.shape
    tm = MOE_TM
    w_r = _pad_cols(jnp.concatenate([p["w_router_group"], p["w_router_expert"]], axis=1), LANE).astype(F32)
    b_r = _pad_cols(jnp.concatenate([p["b_router_group"], p["b_router_expert"]])[None, :], LANE).astype(F32)
    logits, u = moe_router(h, p["g_moe"], w_r, b_r, tm=min(T, 512))
    glog = logits[:, :N_GROUPS]
    gsel = jnp.argmax(glog, axis=-1).astype(jnp.int32)
    pg = jnp.max(jax.nn.softmax(glog, axis=-1), axis=-1, keepdims=True)
    elog = logits[:, N_GROUPS:N_GROUPS + N_EXPERTS].reshape(T, N_GROUPS, EXPERTS_PER_GROUP)
    elog = jnp.take_along_axis(elog, gsel[:, None, None], axis=1)[:, 0]
    top_p, top_j = lax.top_k(jax.nn.softmax(elog, axis=-1), TOP_K)
    top_p = top_p / jnp.sum(top_p, axis=-1, keepdims=True)
    weight = pg * top_p
    flat_e = (gsel[:, None] * EXPERTS_PER_GROUP + top_j.astype(jnp.int32)).reshape(-1)
    TK = T * TOP_K
    onehot = (flat_e[:, None] == jnp.arange(N_EXPERTS, dtype=jnp.int32)[None, :]).astype(jnp.int32)
    rank = jnp.sum((jnp.cumsum(onehot, axis=0) - onehot) * onehot, axis=1)
    counts = jnp.sum(onehot, axis=0)
    pcounts = ((counts + tm - 1) // tm) * tm
    pends = jnp.cumsum(pcounts)
    dest = (pends - pcounts)[flat_e] + rank
    P = TK + N_EXPERTS * tm
    n_rb = P // tm
    row_tok = jnp.full((P,), T, jnp.int32).at[dest].set(jnp.repeat(jnp.arange(T, dtype=jnp.int32), TOP_K))
    blk_e = jnp.sum((pends[None, :] <= (jnp.arange(n_rb, dtype=jnp.int32) * tm)[:, None]).astype(jnp.int32), axis=1)
    blk_e = jnp.minimum(blk_e, N_EXPERTS - 1).astype(jnp.int32)
    n_used = (pends[-1] // tm).astype(jnp.int32).reshape(1)
    xr = jnp.concatenate([u, jnp.zeros((1, D), u.dtype)], axis=0)[row_tok]
    y = moe_experts(blk_e, n_used, xr, experts[0].astype(F32), experts[1].astype(F32), experts[2].astype(F32), layer)
    d2 = dest.reshape(T, TOP_K)
    return h + (weight[:, 0:1] * y[d2[:, 0]].astype(F32) + weight[:, 1:2] * y[d2[:, 1]].astype(F32))


_LAYER_KEYS = ("g_mix", "w_in", "b_forget", "g_cq", "g_ckv", "w_uq", "w_ukv", "pe_k", "pe_v", "w_cmp_k1", "w_cmp_k2",
               "w_cmp_v1", "w_cmp_v2", "w_branch", "w_out", "g_mem_q", "g_mem_kv", "w_mem_q", "w_mem_kv", "w_mem_o",
               "g_moe", "w_router_group", "b_router_group", "w_router_expert", "b_router_expert")


def kernel(x, mem, g_mix, w_in, b_forget, g_cq, g_ckv, w_uq, w_ukv, pe_k, pe_v, w_cmp_k1, w_cmp_k2, w_cmp_v1, w_cmp_v2, rel_bias, w_branch, w_out, g_mem_q, g_mem_kv, w_mem_q, w_mem_kv, w_mem_o, g_moe, w_router_group, b_router_group, w_router_expert, b_router_expert, w_exp_gate, w_exp_up, w_exp_down, g_final):
    B, S, D = x.shape
    T = B * S
    stacked = dict(g_mix=g_mix, w_in=w_in, b_forget=b_forget, g_cq=g_cq, g_ckv=g_ckv, w_uq=w_uq, w_ukv=w_ukv,
                   pe_k=pe_k, pe_v=pe_v, w_cmp_k1=w_cmp_k1, w_cmp_k2=w_cmp_k2, w_cmp_v1=w_cmp_v1, w_cmp_v2=w_cmp_v2,
                   w_branch=w_branch, w_out=w_out, g_mem_q=g_mem_q, g_mem_kv=g_mem_kv, w_mem_q=w_mem_q,
                   w_mem_kv=w_mem_kv, w_mem_o=w_mem_o, g_moe=g_moe, w_router_group=w_router_group,
                   b_router_group=b_router_group, w_router_expert=w_router_expert, b_router_expert=b_router_expert,
                   w_exp_gate=w_exp_gate, w_exp_up=w_exp_up, w_exp_down=w_exp_down)
    tabs = _position_tables(S, rel_bias.astype(F32))
    h = x.reshape(T, D).astype(F32)
    mem2 = mem.reshape(-1, D).astype(F32)
    for l in range(w_in.shape[0]):
        p = {k: stacked[k][l] for k in _LAYER_KEYS}
        h = _token_mixers(h, p, tabs, B, S)
        h = _memory_block(h, mem2, p, B, S)
        h = _moe_block(h, p, (w_exp_gate, w_exp_up, w_exp_down), l)
    return rmsnorm_rows(h, g_final, tm=min(T, 512)).reshape(B, S, D)
```

```python
import functools
import math

import jax
import jax.numpy as jnp
import numpy as np
from jax import lax
from jax.experimental import pallas as pl
from jax.experimental.pallas import tpu as pltpu

F32 = jnp.float32
BF16 = jnp.bfloat16

EPS = 1e-6
NEG_INF = -1e30
LOG2E = math.log2(math.e)
LANE = 128

FOX_HEADS, FOX_DH = 8, 128
MLA_HEADS, MLA_NOPE, MLA_ROPE, MLA_DV = 8, 128, 64, 128
MLA_Q_RANK, MLA_KV_RANK = 768, 512
ROPE_THETA = 10000.0
NSA_HEADS, NSA_GROUPS, NSA_DK, NSA_DV = 8, 2, 192, 128
NSA_HPG = NSA_HEADS // NSA_GROUPS
NSA_DKP = 256
CMP_BLOCK, CMP_STRIDE, SEL_BLOCK, N_SEL, WINDOW = 32, 16, 64, 8, 512
SEL_FORCE = 1e6
REL_BUCKETS, REL_MAX_DIST = 32, 128
N_BRANCH, BRANCH_W = 3, 1024
MEM_HEADS, MEM_DH = 4, 128
N_GROUPS, EXPERTS_PER_GROUP, TOP_K = 4, 8, 2
N_EXPERTS = N_GROUPS * EXPERTS_PER_GROUP
FORGET_COLS, NSA_GATE_COLS = FOX_HEADS, NSA_HEADS * 3

CB_CKV, CB_KR, CB_CQ = 0, 4, 6
CB_NKC, CB_NQ = 12, 16
CB_FQ, CB_FK, CB_FV = 32, 40, 48
CB_NKS, CB_NKW = 56, 60
CB_NVC, CB_NVS, CB_NVW = 64, 66, 68
CB_MG = 72
Z_BLOCKS = 120

ATT_T = 256
ATT_HG = 4
NSA_T = 256
NSA_CMP_T = 512
MOE_TM = 512
MOE_DC = 256
VMEM_LIMIT = 56 * 1024 * 1024


def _cparams(sem):
    return pltpu.CompilerParams(dimension_semantics=sem, vmem_limit_bytes=VMEM_LIMIT)


def _sigmoid(x):
    return 1.0 / (1.0 + jnp.exp(-x))


def _rms_mm_kernel(x_ref, g_ref, w_ref, o_ref, u_ref, *, precise):
    @pl.when(pl.program_id(1) == 0)
    def _():
        x = x_ref[...].astype(F32)
        ms = jnp.mean(x * x, axis=-1, keepdims=True)
        u_ref[...] = (x * lax.rsqrt(ms + EPS) * g_ref[...]).astype(u_ref.dtype)

    prec = lax.Precision.HIGHEST if precise else None
    o_ref[...] = jnp.dot(u_ref[...], w_ref[...], preferred_element_type=F32, precision=prec).astype(o_ref.dtype)


def rms_matmul(x, xcol, g, w, out_dtype, tm, tn, precise=False):
    T = x.shape[0]
    K, N = w.shape
    return pl.pallas_call(
        functools.partial(_rms_mm_kernel, precise=precise),
        out_shape=jax.ShapeDtypeStruct((T, N), out_dtype),
        grid=(T // tm, N // tn),
        in_specs=[pl.BlockSpec((tm, K), lambda i, j: (i, xcol)),
                  pl.BlockSpec((1, K), lambda i, j: (0, 0)),
                  pl.BlockSpec((K, tn), lambda i, j: (0, j))],
        out_specs=pl.BlockSpec((tm, tn), lambda i, j: (i, j)),
        scratch_shapes=[pltpu.VMEM((tm, K), F32 if precise else BF16)],
        compiler_params=_cparams(("parallel", "arbitrary")),
        name="rms_matmul",
    )(x, g.reshape(1, K).astype(F32), w)


def _mla_q_kernel(x_ref, g_ref, w_ref, cos_ref, sin_ref, qn_ref, qr_ref):
    x = x_ref[...].astype(F32)
    ms = jnp.mean(x * x, axis=-1, keepdims=True)
    u = (x * lax.rsqrt(ms + EPS) * g_ref[...]).astype(BF16)
    y = jnp.dot(u, w_ref[...], preferred_element_type=F32)
    hw = MLA_HEADS * LANE
    qn_ref[...] = y[:, :hw].astype(qn_ref.dtype)
    cos = cos_ref[...]
    sin = sin_ref[...]
    for h in range(MLA_HEADS):
        a = y[:, hw + h * LANE: hw + (h + 1) * LANE]
        b = y[:, 2 * hw + h * LANE: 2 * hw + (h + 1) * LANE]
        qr_ref[:, h * LANE:(h + 1) * LANE] = (a * cos + b * sin).astype(qr_ref.dtype)


def mla_q_proj(z, g, w, cos, sin, S, tm):
    T = z.shape[0]
    K, N = w.shape
    hw = MLA_HEADS * LANE
    nsb = S // tm
    return pl.pallas_call(
        _mla_q_kernel,
        out_shape=(jax.ShapeDtypeStruct((T, hw), BF16), jax.ShapeDtypeStruct((T, hw), BF16)),
        grid=(T // tm,),
        in_specs=[pl.BlockSpec((tm, K), lambda i: (i, CB_CQ * LANE // MLA_Q_RANK)),
                  pl.BlockSpec((1, K), lambda i: (0, 0)),
                  pl.BlockSpec((K, N), lambda i: (0, 0)),
                  pl.BlockSpec((tm, LANE), lambda i: (i % nsb, 0)),
                  pl.BlockSpec((tm, LANE), lambda i: (i % nsb, 0))],
        out_specs=(pl.BlockSpec((tm, hw), lambda i: (i, 0)), pl.BlockSpec((tm, hw), lambda i: (i, 0))),
        compiler_params=_cparams(("parallel",)),
        name="mla_q_proj",
    )(z, g.reshape(1, K).astype(F32), w, cos, sin)


def _causal_attn_kernel(*refs, t, hg, two_part, decay):
    refs = list(refs)
    q_ref, k_ref, v_ref = refs[:3]
    pos = 3
    if two_part:
        q2_ref, k2_ref = refs[pos:pos + 2]
        pos += 2
    if decay:
        ck_ref = refs[pos]
        pos += 1
    o_ref = refs[pos]
    if two_part:
        kcat_ref = refs[pos + 1]
    qi = pl.program_id(2)
    dn = (((1,), (1,)), ((), ()))

    if two_part:
        @pl.when(qi == 0)
        def _():
            for j in range(hg):
                kcat_ref[j, :, :LANE] = k_ref[0, :, j * LANE:(j + 1) * LANE]
                kcat_ref[j, :, LANE:] = k2_ref[0]

    qs = []
    for j in range(hg):
        qj = q_ref[0, :, j * LANE:(j + 1) * LANE]
        if two_part:
            qj = jnp.concatenate([qj, q2_ref[0, :, j * LANE:(j + 1) * LANE]], axis=1)
        qs.append(qj)

    def step(kb, carry, masked):
        off = pl.multiple_of(kb * t, t)
        heads = range(hg)
        ss = []
        for j in heads:
            k = kcat_ref[j, pl.ds(off, t), :] if two_part else k_ref[0, pl.ds(off, t), j * LANE:(j + 1) * LANE]
            ss.append(lax.dot_general(k, qs[j], dn, preferred_element_type=F32))
        if decay:
            ss = [ss[j] - ck_ref[0, 0, pl.ds(off, t), j:j + 1] for j in heads]
        if masked:
            r = lax.broadcasted_iota(jnp.int32, (t, t), 0)
            c = lax.broadcasted_iota(jnp.int32, (t, t), 1)
            ss = [jnp.where(r <= c, s, NEG_INF) for s in ss]
        ms = [jnp.maximum(carry[j][0], jnp.max(ss[j], axis=0, keepdims=True)) for j in heads]
        ps = [jnp.exp2(ss[j] - ms[j]) for j in heads]
        out = []
        for j in heads:
            m, l, acc = carry[j]
            a = jnp.exp2(m - ms[j])
            l = a * l + jnp.sum(ps[j], axis=0, keepdims=True)
            v = v_ref[0, pl.ds(off, t), j * LANE:(j + 1) * LANE]
            acc = a * acc + lax.dot_general(v, ps[j].astype(BF16), (((0,), (0,)), ((), ())),
                                            preferred_element_type=F32)
            out.append((ms[j], l, acc))
        return tuple(out)

    init = tuple((jnp.full((1, t), NEG_INF, F32), jnp.zeros((1, t), F32), jnp.zeros((LANE, t), F32))
                 for _ in range(hg))
    carry = lax.fori_loop(0, qi, lambda kb, c: step(kb, c, False), init)
    carry = step(qi, carry, True)
    for j in range(hg):
        _, l, acc = carry[j]
        o_ref[0, :, j * LANE:(j + 1) * LANE] = (acc / l).T.astype(o_ref.dtype)


def causal_attention(q, qcb, k, kcb, v, vcb, heads, q2=None, k2=None, cum=None):
    B, S, _ = q.shape
    t, hg = ATT_T, ATT_HG
    w = hg * LANE
    two_part, decay = q2 is not None, cum is not None
    in_specs = [pl.BlockSpec((1, t, w), lambda b, h, i: (b, i, qcb // hg + h)),
                pl.BlockSpec((1, S, w), lambda b, h, i: (b, 0, kcb // hg + h)),
                pl.BlockSpec((1, S, w), lambda b, h, i: (b, 0, vcb // hg + h))]
    args = [q, k, v]
    scratch = []
    if two_part:
        in_specs += [pl.BlockSpec((1, t, w), lambda b, h, i: (b, i, h)),
                     pl.BlockSpec((1, S, LANE), lambda b, h, i: (b, 0, 0))]
        args += [q2, k2]
        scratch = [pltpu.VMEM((hg, S, 2 * LANE), BF16)]
    if decay:
        in_specs += [pl.BlockSpec((1, 1, S, hg), lambda b, h, i: (b, h, 0, 0))]
        args += [jnp.transpose(cum.reshape(B, S, heads // hg, hg), (0, 2, 1, 3))]
    return pl.pallas_call(
        functools.partial(_causal_attn_kernel, t=t, hg=hg, two_part=two_part, decay=decay),
        out_shape=jax.ShapeDtypeStruct((B, S, heads * LANE), BF16),
        grid=(B, heads // hg, S // t),
        in_specs=in_specs,
        out_specs=pl.BlockSpec((1, t, w), lambda b, h, i: (b, i, h)),
        scratch_shapes=scratch,
        compiler_params=_cparams(("parallel", "parallel", "arbitrary")),
        name="causal_attention",
    )(*args)


def _gelu_tanh(x):
    return 0.5 * x * (1.0 + jnp.tanh(math.sqrt(2.0 / math.pi) * (x + 0.044715 * (x * x * x))))


def _compress_kernel(x_ref, w1a_ref, w1b_ref, pe_ref, w2_ref, o_ref):
    x = x_ref[0]
    a = jnp.dot(x, w1a_ref[...], preferred_element_type=F32)
    b = jnp.dot(x, w1b_ref[...], preferred_element_type=F32)
    nc = a.shape[0]
    b_next = pltpu.roll(b, nc - 1, 0)
    pe_term = jnp.dot(pe_ref[0:1, :], w1a_ref[...], preferred_element_type=F32) + \
        jnp.dot(pe_ref[1:2, :], w1b_ref[...], preferred_element_type=F32)
    hid = _gelu_tanh(a + b_next + pe_term)
    o_ref[0] = jnp.dot(hid.astype(BF16), w2_ref[...], preferred_element_type=F32).astype(o_ref.dtype)


def compress(x, w1a, w1b, pe2, w2):
    BG, NC, KD = x.shape
    dp = w2.shape[1]
    return pl.pallas_call(
        _compress_kernel,
        out_shape=jax.ShapeDtypeStruct((BG, NC, dp), BF16),
        grid=(BG,),
        in_specs=[pl.BlockSpec((1, NC, KD), lambda i: (i, 0, 0)),
                  pl.BlockSpec((KD, dp), lambda i: (0, 0)),
                  pl.BlockSpec((KD, dp), lambda i: (0, 0)),
                  pl.BlockSpec((2, KD), lambda i: (0, 0)),
                  pl.BlockSpec((dp, dp), lambda i: (0, 0))],
        out_specs=pl.BlockSpec((1, NC, dp), lambda i: (i, 0, 0)),
        compiler_params=_cparams(("parallel",)),
        name="nsa_compress",
    )(x, w1a, w1b, pe2, w2)


def _nsa_cmp_kernel(q_ref, kc_ref, vc_ref, bias_ref, ov_ref, gate_ref, o_ref, ind_ref, *, t, n_sel):
    qi = pl.program_id(2)
    ncp = kc_ref.shape[2]
    kc = kc_ref[0, 0]
    vc = vc_ref[0, 0]
    row = lax.broadcasted_iota(jnp.int32, (t, ncp), 0) + qi * t
    col = lax.broadcasted_iota(jnp.int32, (t, ncp), 1)
    valid = row >= CMP_STRIDE * col + (CMP_BLOCK - 1)
    dn = (((1,), (1,)), ((), ()))
    heads = range(NSA_HPG)
    ss = [lax.dot_general(q_ref[0, :, h * NSA_DKP:(h + 1) * NSA_DKP], kc, dn, preferred_element_type=F32)
          for h in heads]
    ss = [jnp.where(valid, ss[h] + bias_ref[h], NEG_INF) for h in heads]
    es = [jnp.exp2(ss[h] - jnp.max(ss[h], axis=-1, keepdims=True)) for h in heads]
    ps = [jnp.where(valid, es[h] / jnp.sum(es[h], axis=-1, keepdims=True), 0.0) for h in heads]
    gates = _sigmoid(gate_ref[0, 0])
    for h in heads:
        o = jnp.dot(ps[h].astype(BF16), vc, preferred_element_type=F32)
        o_ref[0, :, h * NSA_DV:(h + 1) * NSA_DV] = (gates[:, 3 * h:3 * h + 1] * o).astype(o_ref.dtype)
    psum = functools.reduce(lambda x, y: x + y, ps)
    p_hi = psum.astype(BF16)
    p_lo = (psum - p_hi.astype(F32)).astype(BF16)
    imp = jnp.dot(p_hi, ov_ref[...], preferred_element_type=F32) + jnp.dot(p_lo, ov_ref[...], preferred_element_type=F32)
    nbp = imp.shape[1]
    blk = lax.broadcasted_iota(jnp.int32, (t, nbp), 1)
    cur = (lax.broadcasted_iota(jnp.int32, (t, nbp), 0) + qi * t) // SEL_BLOCK
    forced = (blk == 0) | (blk == cur) | (blk == cur - 1)
    score = jnp.where(blk <= cur, imp + jnp.where(forced, SEL_FORCE, 0.0), NEG_INF)
    sel = jnp.zeros((t, nbp), F32)
    for _ in range(n_sel):
        mx = jnp.max(score, axis=-1, keepdims=True)
        first = jnp.min(jnp.where(score == mx, blk, nbp), axis=-1, keepdims=True)
        hit = blk == first
        sel = jnp.where(hit, 1.0, sel)
        score = jnp.where(hit, -jnp.inf, score)
    ind_ref[0, 0] = sel.astype(ind_ref.dtype)


def nsa_cmp_select(z3, kc, vc, bias_c, overlap, gates, n_sel):
    B, S, _ = z3.shape
    G = NSA_GROUPS
    t = min(NSA_CMP_T, S)
    ncp = kc.shape[2]
    nbp = overlap.shape[1]
    qw = NSA_HPG * NSA_DKP
    ow = NSA_HPG * NSA_DV
    return pl.pallas_call(
        functools.partial(_nsa_cmp_kernel, t=t, n_sel=n_sel),
        out_shape=(jax.ShapeDtypeStruct((B, S, G * ow), BF16), jax.ShapeDtypeStruct((B, G, S, nbp), BF16)),
        grid=(B, G, S // t),
        in_specs=[pl.BlockSpec((1, t, qw), lambda b, g, i: (b, i, CB_NQ * LANE // qw + g)),
                  pl.BlockSpec((1, 1, ncp, NSA_DKP), lambda b, g, i: (b, g, 0, 0)),
                  pl.BlockSpec((1, 1, ncp, NSA_DV), lambda b, g, i: (b, g, 0, 0)),
                  pl.BlockSpec((NSA_HPG, t, ncp), lambda b, g, i: (g, i, 0)),
                  pl.BlockSpec((ncp, nbp), lambda b, g, i: (0, 0)),
                  pl.BlockSpec((1, 1, t, LANE), lambda b, g, i: (b, g, i, 0))],
        out_specs=(pl.BlockSpec((1, t, ow), lambda b, g, i: (b, i, g)),
                   pl.BlockSpec((1, 1, t, nbp), lambda b, g, i: (b, g, i, 0))),
        compiler_params=_cparams(("parallel", "parallel", "arbitrary")),
        name="nsa_cmp_select",
    )(z3, kc, vc, bias_c, overlap, gates)


def _nsa_sw_kernel(q_ref, ks_ref, vs_ref, kw_ref, vw_ref, ind_ref, e_ref, bias_ref, gate_ref, oc_ref, o_ref,
                   *, t):
    qi = pl.program_id(1)
    hp, G = NSA_HPG, NSA_GROUPS
    q4 = [jnp.concatenate([q_ref[0, :, (g * hp + h) * NSA_DKP:(g * hp + h + 1) * NSA_DKP] for h in range(hp)], axis=0)
          for g in range(G)]
    inds = [ind_ref[0, g] for g in range(G)]
    dn = (((1,), (1,)), ((), ()))
    ri = lax.broadcasted_iota(jnp.int32, (t, t), 0)
    ci = lax.broadcasted_iota(jnp.int32, (t, t), 1)

    def step(kb, carry, k_ref, v_ref, selected):
        off = pl.multiple_of(kb * t, t)
        kind = jnp.minimum(qi - kb, 2)
        d = (qi - kb) * t + ci - ri
        near = (d >= 0) if selected else (d >= 0) & (d < WINDOW)
        groups = range(G)
        ss = [lax.dot_general(k_ref[0, pl.ds(off, t), g * NSA_DKP:(g + 1) * NSA_DKP], q4[g], dn,
                              preferred_element_type=F32) for g in groups]
        negs = []
        for g in groups:
            mask = near
            if selected:
                hit = jnp.dot(e_ref[pl.ds(off, t), :], inds[g], preferred_element_type=F32)
                mask = near & (hit > 0.5)
            neg = jnp.where(mask, 0.0, NEG_INF)
            negs.append(jnp.concatenate([neg] * hp, axis=1))
        ss = [ss[g] + bias_ref[g, kind] + negs[g] for g in groups]
        ms = [jnp.maximum(carry[g][0], jnp.max(ss[g], axis=0, keepdims=True)) for g in groups]
        ps = [jnp.exp2(ss[g] - ms[g]) for g in groups]
        out = []
        for g in groups:
            m, l, acc = carry[g]
            a = jnp.exp2(m - ms[g])
            l = a * l + jnp.sum(ps[g], axis=0, keepdims=True)
            v = v_ref[0, pl.ds(off, t), g * NSA_DV:(g + 1) * NSA_DV]
            acc = a * acc + lax.dot_general(v, ps[g].astype(BF16), (((0,), (0,)), ((), ())),
                                            preferred_element_type=F32)
            out.append((ms[g], l, acc))
        return tuple(out)

    init = tuple((jnp.full((1, hp * t), NEG_INF, F32), jnp.zeros((1, hp * t), F32), jnp.zeros((NSA_DV, hp * t), F32))
                 for _ in range(G))
    sel = lax.fori_loop(0, qi + 1, lambda kb, c: step(kb, c, ks_ref, vs_ref, True), init)
    lo = jnp.maximum(qi - WINDOW // t, 0)
    win = lax.fori_loop(lo, qi + 1, lambda kb, c: step(kb, c, kw_ref, vw_ref, False), init)
    for g in range(G):
        o_s = sel[g][2] / sel[g][1]
        o_w = win[g][2] / win[g][1]
        gates = _sigmoid(gate_ref[0, g])
        for h in range(hp):
            r = slice(h * t, (h + 1) * t)
            c = slice((g * hp + h) * NSA_DV, (g * hp + h + 1) * NSA_DV)
            o = gates[3 * h + 1:3 * h + 2, :] * o_s[:, r] + gates[3 * h + 2:3 * h + 3, :] * o_w[:, r]
            o_ref[0, :, c] = (oc_ref[0, :, c].astype(F32) + o.T).astype(o_ref.dtype)


def nsa_sel_win(z3, ind, expand, bias_sw, gates, o_cmp):
    B, S, _ = z3.shape
    G = NSA_GROUPS
    t = NSA_T
    qw = NSA_HEADS * NSA_DKP
    ow = NSA_HEADS * NSA_DV
    kw = G * NSA_DKP
    vw = G * NSA_DV
    nbp = ind.shape[2]
    kspec = lambda cb: pl.BlockSpec((1, S, kw), lambda b, i: (b, 0, cb * LANE // kw))
    vspec = lambda cb: pl.BlockSpec((1, S, vw), lambda b, i: (b, 0, cb * LANE // vw))
    return pl.pallas_call(
        functools.partial(_nsa_sw_kernel, t=t),
        out_shape=jax.ShapeDtypeStruct((B, S, ow), BF16),
        grid=(B, S // t),
        in_specs=[pl.BlockSpec((1, t, qw), lambda b, i: (b, i, CB_NQ * LANE // qw)),
                  kspec(CB_NKS), vspec(CB_NVS), kspec(CB_NKW), vspec(CB_NVW),
                  pl.BlockSpec((1, G, nbp, t), lambda b, i: (b, 0, 0, i)),
                  pl.BlockSpec((S, nbp), lambda b, i: (0, 0)),
                  pl.BlockSpec((G, 3, t, NSA_HPG * t), lambda b, i: (0, 0, 0, 0)),
                  pl.BlockSpec((1, G, gates.shape[2], t), lambda b, i: (b, 0, 0, i)),
                  pl.BlockSpec((1, t, ow), lambda b, i: (b, i, 0))],
        out_specs=pl.BlockSpec((1, t, ow), lambda b, i: (b, i, 0)),
        compiler_params=_cparams(("parallel", "arbitrary")),
        name="nsa_sel_win",
    )(z3, z3, z3, z3, z3, ind, expand, bias_sw, gates, o_cmp)


def _merge_kernel(of_ref, om_ref, on_ref, wb_ref, g0_ref, g1_ref, g2_ref, o_ref):
    acc = None
    for n, (o_r, g_r) in enumerate(((of_ref, g0_ref), (om_ref, g1_ref), (on_ref, g2_ref))):
        y = jnp.dot(o_r[...], wb_ref[n], preferred_element_type=F32)
        y = _sigmoid(g_r[...].astype(F32)) * y
        acc = y if acc is None else acc + y
    o_ref[...] = acc.astype(o_ref.dtype)


def merge_branches(o_fox, o_mla, o_nsa, wb, z, tm, tn):
    T = o_fox.shape[0]
    D = wb.shape[2]
    gspec = lambda n: pl.BlockSpec((tm, tn), lambda i, j: (i, (CB_MG * LANE + n * D) // tn + j))
    ospec = pl.BlockSpec((tm, BRANCH_W), lambda i, j: (i, 0))
    return pl.pallas_call(
        _merge_kernel,
        out_shape=jax.ShapeDtypeStruct((T, D), BF16),
        grid=(T // tm, D // tn),
        in_specs=[ospec, ospec, ospec,
                  pl.BlockSpec((N_BRANCH, BRANCH_W, tn), lambda i, j: (0, 0, j)),
                  gspec(0), gspec(1), gspec(2)],
        out_specs=pl.BlockSpec((tm, tn), lambda i, j: (i, j)),
        compiler_params=_cparams(("parallel", "arbitrary")),
        name="merge_branches",
    )(o_fox, o_mla, o_nsa, wb, z, z, z)


def _mm_res_kernel(a_ref, w_ref, r_ref, o_ref):
    o_ref[...] = r_ref[...] + jnp.dot(a_ref[...], w_ref[...], preferred_element_type=F32)


def matmul_residual(a, w, res, tm, tn):
    T, K = a.shape
    N = w.shape[1]
    return pl.pallas_call(
        _mm_res_kernel,
        out_shape=jax.ShapeDtypeStruct((T, N), F32),
        grid=(T // tm, N // tn),
        in_specs=[pl.BlockSpec((tm, K), lambda i, j: (i, 0)),
                  pl.BlockSpec((K, tn), lambda i, j: (0, j)),
                  pl.BlockSpec((tm, tn), lambda i, j: (i, j))],
        out_specs=pl.BlockSpec((tm, tn), lambda i, j: (i, j)),
        compiler_params=_cparams(("parallel", "arbitrary")),
        name="matmul_residual",
    )(a, w, res)


def _mem_attn_kernel(h_ref, g_ref, wq_ref, kv_ref, wo_ref, o_ref):
    x = h_ref[0]
    ms = jnp.mean(x * x, axis=-1, keepdims=True)
    u = (x * lax.rsqrt(ms + EPS) * g_ref[...]).astype(BF16)
    q = jnp.dot(u, wq_ref[...], preferred_element_type=F32).astype(BF16)
    dn = (((1,), (1,)), ((), ()))
    hw = MEM_HEADS * MEM_DH
    heads = range(MEM_HEADS)
    cs = [slice(h * MEM_DH, (h + 1) * MEM_DH) for h in heads]
    ss = [lax.dot_general(q[:, cs[h]], kv_ref[0, :, cs[h]], dn, preferred_element_type=F32) for h in heads]
    es = [jnp.exp2(ss[h] - jnp.max(ss[h], axis=-1, keepdims=True)) for h in heads]
    ps = [es[h] / jnp.sum(es[h], axis=-1, keepdims=True) for h in heads]
    outs = [jnp.dot(ps[h].astype(BF16), kv_ref[0, :, hw + h * MEM_DH: hw + (h + 1) * MEM_DH],
                    preferred_element_type=F32).astype(BF16) for h in heads]
    o = jnp.concatenate(outs, axis=1)
    o_ref[0] = x + jnp.dot(o, wo_ref[...], preferred_element_type=F32)


def memory_attention_block(h3, g, wq, kv, wo, tq):
    B, S, D = h3.shape
    M = kv.shape[1]
    hw = MEM_HEADS * MEM_DH
    return pl.pallas_call(
        _mem_attn_kernel,
        out_shape=jax.ShapeDtypeStruct((B, S, D), F32),
        grid=(B, S // tq),
        in_specs=[pl.BlockSpec((1, tq, D), lambda b, i: (b, i, 0)),
                  pl.BlockSpec((1, D), lambda b, i: (0, 0)),
                  pl.BlockSpec((D, hw), lambda b, i: (0, 0)),
                  pl.BlockSpec((1, M, 2 * hw), lambda b, i: (b, 0, 0)),
                  pl.BlockSpec((hw, D), lambda b, i: (0, 0))],
        out_specs=pl.BlockSpec((1, tq, D), lambda b, i: (b, i, 0)),
        compiler_params=_cparams(("parallel", "arbitrary")),
        name="memory_attention",
    )(h3, g.reshape(1, D).astype(F32), wq, kv, wo)


def _router_kernel(h_ref, g_ref, w_ref, b_ref, lg_ref, u_ref):
    x = h_ref[...]
    ms = jnp.mean(x * x, axis=-1, keepdims=True)
    u = x * lax.rsqrt(ms + EPS) * g_ref[...]
    u_ref[...] = u.astype(u_ref.dtype)
    lg_ref[...] = jnp.dot(u, w_ref[...], preferred_element_type=F32, precision=lax.Precision.HIGHEST) + b_ref[...]


def moe_router(h, g, w, b, tm):
    T, D = h.shape
    N = w.shape[1]
    return pl.pallas_call(
        _router_kernel,
        out_shape=(jax.ShapeDtypeStruct((T, N), F32), jax.ShapeDtypeStruct((T, D), BF16)),
        grid=(T // tm,),
        in_specs=[pl.BlockSpec((tm, D), lambda i: (i, 0)),
                  pl.BlockSpec((1, D), lambda i: (0, 0)),
                  pl.BlockSpec((D, N), lambda i: (0, 0)),
                  pl.BlockSpec((1, N), lambda i: (0, 0))],
        out_specs=(pl.BlockSpec((tm, N), lambda i: (i, 0)), pl.BlockSpec((tm, D), lambda i: (i, 0))),
        compiler_params=_cparams(("parallel",)),
        name="moe_router",
    )(h, g.reshape(1, D).astype(F32), w, b)


def _moe_kernel(be_ref, nu_ref, x_ref, wg_ref, wu_ref, wd_ref, o_ref, acc_ref):
    i = pl.program_id(0)
    j = pl.program_id(1)
    last = pl.num_programs(1) - 1
    used = i < nu_ref[0]

    @pl.when(used)
    def _():
        x = x_ref[...]
        a = jnp.dot(x, wg_ref[0, 0].astype(BF16), preferred_element_type=F32)
        b = jnp.dot(x, wu_ref[0, 0].astype(BF16), preferred_element_type=F32)
        hdn = (a * _sigmoid(a) * b).astype(BF16)
        y = jnp.dot(hdn, wd_ref[0, 0].astype(BF16), preferred_element_type=F32)

        @pl.when(j == 0)
        def _():
            acc_ref[...] = y

        @pl.when(j > 0)
        def _():
            acc_ref[...] += y

        @pl.when(j == last)
        def _():
            o_ref[...] = acc_ref[...].astype(o_ref.dtype)

    @pl.when(jnp.logical_not(used) & (j == last))
    def _():
        o_ref[...] = jnp.zeros(o_ref.shape, o_ref.dtype)


def moe_experts(blk_e, n_used, xr, wg, wu, wd, layer):
    P, D = xr.shape
    De = wg.shape[3]
    tm, dc = MOE_TM, MOE_DC
    nj = De // dc
    jj = lambda i, j, nu: jnp.where(i < nu[0], j, nj - 1)
    grid_spec = pltpu.PrefetchScalarGridSpec(
        num_scalar_prefetch=2,
        grid=(P // tm, nj),
        in_specs=[pl.BlockSpec((tm, D), lambda i, j, be, nu: (i, 0)),
                  pl.BlockSpec((1, 1, D, dc), lambda i, j, be, nu: (layer, be[i], 0, jj(i, j, nu))),
                  pl.BlockSpec((1, 1, D, dc), lambda i, j, be, nu: (layer, be[i], 0, jj(i, j, nu))),
                  pl.BlockSpec((1, 1, dc, D), lambda i, j, be, nu: (layer, be[i], jj(i, j, nu), 0))],
        out_specs=pl.BlockSpec((tm, D), lambda i, j, be, nu: (i, 0)),
        scratch_shapes=[pltpu.VMEM((tm, D), F32)],
    )
    return pl.pallas_call(
        _moe_kernel,
        out_shape=jax.ShapeDtypeStruct((P, D), BF16),
        grid_spec=grid_spec,
        compiler_params=_cparams(("arbitrary", "arbitrary")),
        name="moe_experts",
    )(blk_e, n_used, xr, wg, wu, wd)


def _rmsnorm_kernel(x_ref, g_ref, o_ref):
    x = x_ref[...]
    ms = jnp.mean(x * x, axis=-1, keepdims=True)
    o_ref[...] = x * lax.rsqrt(ms + EPS) * g_ref[...]


def rmsnorm_rows(x, g, tm):
    T, D = x.shape
    return pl.pallas_call(
        _rmsnorm_kernel,
        out_shape=jax.ShapeDtypeStruct((T, D), F32),
        grid=(T // tm,),
        in_specs=[pl.BlockSpec((tm, D), lambda i: (i, 0)), pl.BlockSpec((1, D), lambda i: (0, 0))],
        out_specs=pl.BlockSpec((tm, D), lambda i: (i, 0)),
        compiler_params=_cparams(("parallel",)),
        name="final_rmsnorm",
    )(x, g.reshape(1, D).astype(F32))


def _pad_cols(w, width):
    return jnp.pad(w, ((0, 0), (0, width - w.shape[1])))


def _pack_w_in(w):
    D = w.shape[0]
    offs = np.cumsum([0, 1024, 1024, 1024, FORGET_COLS, MLA_Q_RANK, MLA_KV_RANK, MLA_ROPE,
                      NSA_HEADS * NSA_DK, NSA_GROUPS * NSA_DK, NSA_GROUPS * NSA_DV, NSA_GROUPS * NSA_DK,
                      NSA_GROUPS * NSA_DV, NSA_GROUPS * NSA_DK, NSA_GROUPS * NSA_DV, NSA_GATE_COLS, N_BRANCH * D])
    seg = lambda i: w[:, offs[i]:offs[i + 1]]
    fq, fk, fv, ff, mcq, mckv, mkr, nq, nkc, nvc, nks, nvs, nkw, nvw, ngt, mg = [seg(i) for i in range(16)]
    half = MLA_ROPE // 2
    kr_rot = jnp.concatenate([-mkr[:, half:], mkr[:, :half]], axis=1)
    padk = lambda a, n: jnp.pad(a.reshape(D, n, NSA_DK), ((0, 0), (0, 0), (0, NSA_DKP - NSA_DK))).reshape(D, n * NSA_DKP)
    zeros = lambda nb: jnp.zeros((D, nb * LANE), w.dtype)
    cols = [mckv, mkr, kr_rot, zeros(1), mcq, padk(nkc, NSA_GROUPS), padk(nq * (NSA_DK ** -0.5 * LOG2E), NSA_HEADS),
            fq * (FOX_DH ** -0.5 * LOG2E), fk, fv, padk(nks, NSA_GROUPS), padk(nkw, NSA_GROUPS), nvc, nvs, nvw, zeros(2), mg]
    wz = jnp.concatenate(cols, axis=1).astype(BF16)
    assert wz.shape[1] == CB_MG * LANE + N_BRANCH * D
    w_small = _pad_cols(jnp.concatenate([ff, ngt], axis=1), LANE)
    return wz, w_small


def _pack_w_uq(w):
    K = w.shape[0]
    w3 = w.reshape(K, MLA_HEADS, MLA_NOPE + MLA_ROPE) * ((MLA_NOPE + MLA_ROPE) ** -0.5 * LOG2E)
    nope = w3[:, :, :MLA_NOPE].reshape(K, MLA_HEADS * MLA_NOPE)
    r = w3[:, :, MLA_NOPE:]
    half = MLA_ROPE // 2
    r_rot = jnp.concatenate([-r[:, :, half:], r[:, :, :half]], axis=2)
    padr = lambda a: jnp.pad(a, ((0, 0), (0, 0), (0, LANE - MLA_ROPE))).reshape(K, MLA_HEADS * LANE)
    return jnp.concatenate([nope, padr(r), padr(r_rot)], axis=1).astype(BF16)


def _pack_w_ukv(w):
    K = w.shape[0]
    w3 = w.reshape(K, MLA_HEADS, MLA_NOPE + MLA_DV)
    return jnp.concatenate([w3[:, :, :MLA_NOPE].reshape(K, -1), w3[:, :, MLA_NOPE:].reshape(K, -1)], axis=1).astype(BF16)


def _t5_bucket(dist):
    dist = jnp.maximum(dist, 0)
    exact = REL_BUCKETS // 2
    df = jnp.maximum(dist, 1).astype(F32)
    large = exact + (jnp.log(df / exact) / math.log(REL_MAX_DIST / exact) * (REL_BUCKETS - exact)).astype(jnp.int32)
    large = jnp.minimum(large, REL_BUCKETS - 1)
    return jnp.where(dist < exact, dist, large)


def _position_tables(S, rel_bias):
    t = NSA_T
    half = MLA_ROPE // 2
    inv = ROPE_THETA ** (-jnp.arange(half, dtype=F32) / half)
    ang = jnp.arange(S, dtype=F32)[:, None] * inv
    c, s = jnp.cos(ang), jnp.sin(ang)
    cos = _pad_cols(jnp.concatenate([c, c], axis=1), LANE)
    sin = _pad_cols(jnp.concatenate([s, s], axis=1), LANE)
    ncp = max(S // CMP_STRIDE, LANE)
    pos = jnp.arange(S)

    def bias_of(dist):
        onehot = jax.nn.one_hot(_t5_bucket(dist), REL_BUCKETS, dtype=F32)
        return jnp.einsum("...b,bh->h...", onehot, rel_bias, precision=lax.Precision.HIGHEST)

    bias_c = bias_of(pos[:, None] - (CMP_STRIDE * jnp.arange(ncp)[None, :] + CMP_BLOCK - 1))
    i = jnp.arange(t)
    bias_sw = jnp.stack([bias_of(k * t + i[:, None] - i[None, :]) for k in range(3)], axis=1)
    bias_sw = bias_sw.reshape(NSA_GROUPS, NSA_HPG, 3, t, t).transpose(0, 2, 4, 1, 3).reshape(NSA_GROUPS, 3, t, NSA_HPG * t)
    n_cmp = (S - CMP_BLOCK) // CMP_STRIDE + 1
    n_blk = S // SEL_BLOCK
    nbp = max(n_blk, LANE)
    cstart = CMP_STRIDE * jnp.arange(ncp)
    sstart = SEL_BLOCK * jnp.arange(nbp)
    ov = jnp.clip(jnp.minimum(cstart[:, None] + CMP_BLOCK, sstart[None, :] + SEL_BLOCK)
                  - jnp.maximum(cstart[:, None], sstart[None, :]), 0, None).astype(F32) / CMP_STRIDE
    ov = jnp.where((jnp.arange(ncp)[:, None] < n_cmp) & (jnp.arange(nbp)[None, :] < n_blk), ov, 0.0).astype(BF16)
    expand = ((pos[:, None] // SEL_BLOCK) == jnp.arange(nbp)[None, :]).astype(BF16)
    return cos, sin, bias_c, bias_sw, ov, expand


def _token_mixers(h, p, tabs, B, S):
    T, D = h.shape
    cos, sin, bias_c, bias_sw, overlap, expand = tabs
    wz, w_small = _pack_w_in(p["w_in"])
    z = rms_matmul(h, 0, p["g_mix"], wz, BF16, tm=min(T, 1024), tn=1024)
    zs = rms_matmul(h, 0, p["g_mix"], w_small, F32, tm=min(T, 512), tn=LANE, precise=True)
    z3 = z.reshape(B, S, Z_BLOCKS * LANE)

    log_f = jax.nn.log_sigmoid(zs[:, :FORGET_COLS] + p["b_forget"].astype(F32)).reshape(B, S, FOX_HEADS)
    cum = jnp.cumsum(log_f, axis=1) * LOG2E
    o_fox = causal_attention(z3, CB_FQ, z3, CB_FK, z3, CB_FV, FOX_HEADS, cum=cum)

    q_nope, q_rope = mla_q_proj(z, p["g_cq"], _pack_w_uq(p["w_uq"]), cos, sin, S, tm=min(S, 512))
    kv = rms_matmul(z, CB_CKV * LANE // MLA_KV_RANK, p["g_ckv"], _pack_w_ukv(p["w_ukv"]), BF16, tm=min(T, 1024), tn=1024)
    kr = z3[:, :, CB_KR * LANE:(CB_KR + 1) * LANE].astype(F32)
    kr = kr[..., :MLA_ROPE] * cos[None, :, :MLA_ROPE] + kr[..., MLA_ROPE:] * sin[None, :, :MLA_ROPE]
    k_rope = jnp.pad(kr, ((0, 0), (0, 0), (0, LANE - MLA_ROPE))).astype(BF16)
    hw = MLA_HEADS * LANE
    kv3 = kv.reshape(B, S, 2 * hw)
    o_mla = causal_attention(q_nope.reshape(B, S, hw), 0, kv3, 0, kv3, MLA_HEADS, MLA_HEADS,
                             q2=q_rope.reshape(B, S, hw), k2=k_rope)

    G = NSA_GROUPS
    NC = S // CMP_STRIDE
    ncp = bias_c.shape[2]

    def compress_branch(cb, dp, d, pe, w1, w2):
        raw = z3[:, :, cb * LANE: cb * LANE + G * dp].reshape(B, NC, CMP_STRIDE, G, dp)
        x = jnp.transpose(raw, (0, 3, 1, 2, 4)).reshape(B * G, NC, CMP_STRIDE * dp)
        w1p = jnp.pad(w1.reshape(CMP_BLOCK, d, d), ((0, 0), (0, dp - d), (0, dp - d))).astype(BF16)
        w1a = w1p[:CMP_STRIDE].reshape(CMP_STRIDE * dp, dp)
        w1b = w1p[CMP_STRIDE:].reshape(CMP_STRIDE * dp, dp)
        pe2 = jnp.pad(pe, ((0, 0), (0, dp - d))).reshape(2, CMP_STRIDE * dp).astype(BF16)
        w2p = jnp.pad(w2, ((0, dp - d), (0, dp - d))).astype(BF16)
        out = compress(x, w1a, w1b, pe2, w2p).reshape(B, G, NC, dp)
        return jnp.pad(out, ((0, 0), (0, 0), (0, ncp - NC), (0, 0)))

    kc = compress_branch(CB_NKC, NSA_DKP, NSA_DK, p["pe_k"], p["w_cmp_k1"], p["w_cmp_k2"])
    vc = compress_branch(CB_NVC, NSA_DV, NSA_DV, p["pe_v"], p["w_cmp_v1"], p["w_cmp_v2"])
    gl = zs[:, FORGET_COLS:FORGET_COLS + NSA_GATE_COLS].reshape(B, S, G, NSA_HPG * 3)
    gl = jnp.transpose(gl, (0, 2, 1, 3))
    gl_rows = jnp.pad(gl, ((0, 0), (0, 0), (0, 0), (0, LANE - NSA_HPG * 3)))
    gl_cols = jnp.pad(jnp.swapaxes(gl, 2, 3), ((0, 0), (0, 0), (0, 16 - NSA_HPG * 3), (0, 0)))
    o_cmp, ind = nsa_cmp_select(z3, kc, vc, bias_c, overlap, gl_rows, min(N_SEL, S // SEL_BLOCK))
    o_nsa = nsa_sel_win(z3, jnp.swapaxes(ind, 2, 3), expand, bias_sw, gl_cols, o_cmp)

    merged = merge_branches(o_fox.reshape(T, -1), o_mla.reshape(T, -1), o_nsa.reshape(T, -1),
                            p["w_branch"].astype(BF16), z, tm=min(T, 1024), tn=512)
    return matmul_residual(merged, p["w_out"].astype(BF16), h, tm=min(T, 1024), tn=512)


def _memory_block(h, mem2, p, B, S):
    T, D = h.shape
    kv = rms_matmul(mem2, 0, p["g_mem_kv"], p["w_mem_kv"].astype(BF16), BF16, tm=min(mem2.shape[0], 512), tn=512)
    out = memory_attention_block(h.reshape(B, S, D), p["g_mem_q"], (p["w_mem_q"] * (MEM_DH ** -0.5 * LOG2E)).astype(BF16),
                                 kv.reshape(B, -1, kv.shape[1]), p["w_mem_o"].astype(BF16), tq=min(S, 512))
    return out.reshape(T, D)


def _moe_block(h, p, experts, layer):
    T, D = h.shape
    tm = MOE_TM
    w_r = _pad_cols(jnp.concatenate([p["w_router_group"], p["w_router_expert"]], axis=1), LANE).astype(F32)
    b_r = _pad_cols(jnp.concatenate([p["b_router_group"], p["b_router_expert"]])[None, :], LANE).astype(F32)
    logits, u = moe_router(h, p["g_moe"], w_r, b_r, tm=min(T, 512))
    glog = logits[:, :N_GROUPS]
    gsel = jnp.argmax(glog, axis=-1).astype(jnp.int32)
    pg = jnp.max(jax.nn.softmax(glog, axis=-1), axis=-1, keepdims=True)
    elog = logits[:, N_GROUPS:N_GROUPS + N_EXPERTS].reshape(T, N_GROUPS, EXPERTS_PER_GROUP)
    elog = jnp.take_along_axis(elog, gsel[:, None, None], axis=1)[:, 0]
    top_p, top_j = lax.top_k(jax.nn.softmax(elog, axis=-1), TOP_K)
    top_p = top_p / jnp.sum(top_p, axis=-1, keepdims=True)
    weight = pg * top_p
    flat_e = (gsel[:, None] * EXPERTS_PER_GROUP + top_j.astype(jnp.int32)).reshape(-1)
    TK = T * TOP_K
    onehot = (flat_e[:, None] == jnp.arange(N_EXPERTS, dtype=jnp.int32)[None, :]).astype(jnp.int32)
    rank = jnp.sum((jnp.cumsum(onehot, axis=0) - onehot) * onehot, axis=1)
    counts = jnp.sum(onehot, axis=0)
    pcounts = ((counts + tm - 1) // tm) * tm
    pends = jnp.cumsum(pcounts)
    dest = (pends - pcounts)[flat_e] + rank
    P = TK + N_EXPERTS * tm
    n_rb = P // tm
    row_tok = (jnp.arange(P, dtype=jnp.int32) % T).at[dest].set(jnp.repeat(jnp.arange(T, dtype=jnp.int32), TOP_K))
    blk_e = jnp.sum((pends[None, :] <= (jnp.arange(n_rb, dtype=jnp.int32) * tm)[:, None]).astype(jnp.int32), axis=1)
    blk_e = jnp.minimum(blk_e, N_EXPERTS - 1).astype(jnp.int32)
    n_used = (pends[-1] // tm).astype(jnp.int32).reshape(1)
    xr = u[row_tok]
    y = moe_experts(blk_e, n_used, xr, experts[0].astype(F32), experts[1].astype(F32), experts[2].astype(F32), layer)
    d2 = dest.reshape(T, TOP_K)
    return h + (weight[:, 0:1] * y[d2[:, 0]].astype(F32) + weight[:, 1:2] * y[d2[:, 1]].astype(F32))


_LAYER_KEYS = ("g_mix", "w_in", "b_forget", "g_cq", "g_ckv", "w_uq", "w_ukv", "pe_k", "pe_v", "w_cmp_k1", "w_cmp_k2",
               "w_cmp_v1", "w_cmp_v2", "w_branch", "w_out", "g_mem_q", "g_mem_kv", "w_mem_q", "w_mem_kv", "w_mem_o",
               "g_moe", "w_router_group", "b_router_group", "w_router_expert", "b_router_expert")


def kernel(x, mem, g_mix, w_in, b_forget, g_cq, g_ckv, w_uq, w_ukv, pe_k, pe_v, w_cmp_k1, w_cmp_k2, w_cmp_v1, w_cmp_v2, rel_bias, w_branch, w_out, g_mem_q, g_mem_kv, w_mem_q, w_mem_kv, w_mem_o, g_moe, w_router_group, b_router_group, w_router_expert, b_router_expert, w_exp_gate, w_exp_up, w_exp_down, g_final):
    B, S, D = x.shape
    T = B * S
    stacked = dict(g_mix=g_mix, w_in=w_in, b_forget=b_forget, g_cq=g_cq, g_ckv=g_ckv, w_uq=w_uq, w_ukv=w_ukv,
                   pe_k=pe_k, pe_v=pe_v, w_cmp_k1=w_cmp_k1, w_cmp_k2=w_cmp_k2, w_cmp_v1=w_cmp_v1, w_cmp_v2=w_cmp_v2,
                   w_branch=w_branch, w_out=w_out, g_mem_q=g_mem_q, g_mem_kv=g_mem_kv, w_mem_q=w_mem_q,
                   w_mem_kv=w_mem_kv, w_mem_o=w_mem_o, g_moe=g_moe, w_router_group=w_router_group,
                   b_router_group=b_router_group, w_router_expert=w_router_expert, b_router_expert=b_router_expert,
                   w_exp_gate=w_exp_gate, w_exp_up=w_exp_up, w_exp_down=w_exp_down)
    tabs = _position_tables(S, rel_bias.astype(F32) * LOG2E)
    h = x.reshape(T, D).astype(F32)
    mem2 = mem.reshape(-1, D).astype(F32)
    for l in range(w_in.shape[0]):
        p = {k: stacked[k][l] for k in _LAYER_KEYS}
        h = _token_mixers(h, p, tabs, B, S)
        h = _memory_block(h, mem2, p, B, S)
        h = _moe_block(h, p, (w_exp_gate, w_exp_up, w_exp_down), l)
    return rmsnorm_rows(h, g_final, tm=min(T, 512)).reshape(B, S, D)
```

```python
import functools
import math

import jax
import jax.numpy as jnp
import numpy as np
from jax import lax
from jax.experimental import pallas as pl
from jax.experimental.pallas import tpu as pltpu

F32 = jnp.float32
BF16 = jnp.bfloat16

EPS = 1e-6
NEG_INF = -1e30
LOG2E = math.log2(math.e)
LANE = 128

FOX_HEADS, FOX_DH = 8, 128
MLA_HEADS, MLA_NOPE, MLA_ROPE, MLA_DV = 8, 128, 64, 128
MLA_Q_RANK, MLA_KV_RANK = 768, 512
ROPE_THETA = 10000.0
NSA_HEADS, NSA_GROUPS, NSA_DK, NSA_DV = 8, 2, 192, 128
NSA_HPG = NSA_HEADS // NSA_GROUPS
NSA_DKP = 256
CMP_BLOCK, CMP_STRIDE, SEL_BLOCK, N_SEL, WINDOW = 32, 16, 64, 8, 512
SEL_FORCE = 1e6
REL_BUCKETS, REL_MAX_DIST = 32, 128
N_BRANCH, BRANCH_W = 3, 1024
MEM_HEADS, MEM_DH = 4, 128
N_GROUPS, EXPERTS_PER_GROUP, TOP_K = 4, 8, 2
N_EXPERTS = N_GROUPS * EXPERTS_PER_GROUP
FORGET_COLS, NSA_GATE_COLS = FOX_HEADS, NSA_HEADS * 3

CB_CKV, CB_KR, CB_CQ = 0, 4, 6
CB_NKC, CB_NQ = 12, 16
CB_FQ, CB_FK, CB_FV = 32, 40, 48
CB_NKS, CB_NKW = 56, 60
CB_NVC, CB_NVS, CB_NVW = 64, 66, 68
CB_MG = 72
Z_BLOCKS = 120

ATT_T = 256
ATT_HG = 8
NSA_T = 256
NSA_CMP_T = 512
MOE_TM = 512
MOE_DC = 512
VMEM_LIMIT = 56 * 1024 * 1024


def _cparams(sem):
    return pltpu.CompilerParams(dimension_semantics=sem, vmem_limit_bytes=VMEM_LIMIT)


def _sigmoid(x):
    return 1.0 / (1.0 + jnp.exp(-x))


def _rms_mm_kernel(x_ref, g_ref, w_ref, o_ref, u_ref, *, precise):
    @pl.when(pl.program_id(1) == 0)
    def _():
        x = x_ref[...].astype(F32)
        ms = jnp.mean(x * x, axis=-1, keepdims=True)
        u_ref[...] = (x * lax.rsqrt(ms + EPS) * g_ref[...]).astype(u_ref.dtype)

    prec = lax.Precision.HIGHEST if precise else None
    o_ref[...] = jnp.dot(u_ref[...], w_ref[...], preferred_element_type=F32, precision=prec).astype(o_ref.dtype)


def rms_matmul(x, xcol, g, w, out_dtype, tm, tn, precise=False):
    T = x.shape[0]
    K, N = w.shape
    return pl.pallas_call(
        functools.partial(_rms_mm_kernel, precise=precise),
        out_shape=jax.ShapeDtypeStruct((T, N), out_dtype),
        grid=(T // tm, N // tn),
        in_specs=[pl.BlockSpec((tm, K), lambda i, j: (i, xcol)),
                  pl.BlockSpec((1, K), lambda i, j: (0, 0)),
                  pl.BlockSpec((K, tn), lambda i, j: (0, j))],
        out_specs=pl.BlockSpec((tm, tn), lambda i, j: (i, j)),
        scratch_shapes=[pltpu.VMEM((tm, K), F32 if precise else BF16)],
        compiler_params=_cparams(("parallel", "arbitrary")),
        name="rms_matmul",
    )(x, g.reshape(1, K).astype(F32), w)


def _mla_q_kernel(x_ref, g_ref, w_ref, cos_ref, sin_ref, qn_ref, qr_ref):
    x = x_ref[...].astype(F32)
    ms = jnp.mean(x * x, axis=-1, keepdims=True)
    u = (x * lax.rsqrt(ms + EPS) * g_ref[...]).astype(BF16)
    y = jnp.dot(u, w_ref[...], preferred_element_type=F32)
    hw = MLA_HEADS * LANE
    qn_ref[...] = y[:, :hw].astype(qn_ref.dtype)
    cos = cos_ref[...]
    sin = sin_ref[...]
    for h in range(MLA_HEADS):
        a = y[:, hw + h * LANE: hw + (h + 1) * LANE]
        b = y[:, 2 * hw + h * LANE: 2 * hw + (h + 1) * LANE]
        qr_ref[:, h * LANE:(h + 1) * LANE] = (a * cos + b * sin).astype(qr_ref.dtype)


def mla_q_proj(z, g, w, cos, sin, S, tm):
    T = z.shape[0]
    K, N = w.shape
    hw = MLA_HEADS * LANE
    nsb = S // tm
    return pl.pallas_call(
        _mla_q_kernel,
        out_shape=(jax.ShapeDtypeStruct((T, hw), BF16), jax.ShapeDtypeStruct((T, hw), BF16)),
        grid=(T // tm,),
        in_specs=[pl.BlockSpec((tm, K), lambda i: (i, CB_CQ * LANE // MLA_Q_RANK)),
                  pl.BlockSpec((1, K), lambda i: (0, 0)),
                  pl.BlockSpec((K, N), lambda i: (0, 0)),
                  pl.BlockSpec((tm, LANE), lambda i: (i % nsb, 0)),
                  pl.BlockSpec((tm, LANE), lambda i: (i % nsb, 0))],
        out_specs=(pl.BlockSpec((tm, hw), lambda i: (i, 0)), pl.BlockSpec((tm, hw), lambda i: (i, 0))),
        compiler_params=_cparams(("parallel",)),
        name="mla_q_proj",
    )(z, g.reshape(1, K).astype(F32), w, cos, sin)


def _causal_attn_kernel(*refs, t, hg, two_part, decay):
    refs = list(refs)
    q_ref, k_ref, v_ref = refs[:3]
    pos = 3
    if two_part:
        q2_ref, k2_ref = refs[pos:pos + 2]
        pos += 2
    if decay:
        ck_ref = refs[pos]
        pos += 1
    o_ref = refs[pos]
    if two_part:
        kcat_ref = refs[pos + 1]
    qi = pl.program_id(2)
    dn = (((1,), (1,)), ((), ()))

    if two_part:
        @pl.when(qi == 0)
        def _():
            for j in range(hg):
                kcat_ref[j, :, :LANE] = k_ref[0, :, j * LANE:(j + 1) * LANE]
                kcat_ref[j, :, LANE:] = k2_ref[0]

    qs = []
    for j in range(hg):
        qj = q_ref[0, :, j * LANE:(j + 1) * LANE]
        if two_part:
            qj = jnp.concatenate([qj, q2_ref[0, :, j * LANE:(j + 1) * LANE]], axis=1)
        qs.append(qj)

    def step(kb, carry, masked, width=1):
        off = pl.multiple_of(kb * t, t)
        tk = width * t
        heads = range(hg)
        ss = []
        for j in heads:
            k = kcat_ref[j, pl.ds(off, tk), :] if two_part else k_ref[0, pl.ds(off, tk), j * LANE:(j + 1) * LANE]
            ss.append(lax.dot_general(k, qs[j], dn, preferred_element_type=F32))
        if decay:
            ss = [ss[j] - ck_ref[0, 0, pl.ds(off, tk), j:j + 1] for j in heads]
        if masked:
            r = lax.broadcasted_iota(jnp.int32, (tk, t), 0)
            c = lax.broadcasted_iota(jnp.int32, (tk, t), 1)
            ss = [jnp.where(r <= c, s, NEG_INF) for s in ss]
        ms = [jnp.maximum(carry[j][0], jnp.max(ss[j], axis=0, keepdims=True)) for j in heads]
        ps = [jnp.exp2(ss[j] - ms[j]) for j in heads]
        out = []
        for j in heads:
            m, l, acc = carry[j]
            a = jnp.exp2(m - ms[j])
            l = a * l + jnp.sum(ps[j], axis=0, keepdims=True)
            v = v_ref[0, pl.ds(off, tk), j * LANE:(j + 1) * LANE]
            acc = a * acc + lax.dot_general(v, ps[j].astype(BF16), (((0,), (0,)), ((), ())),
                                            preferred_element_type=F32)
            out.append((ms[j], l, acc))
        return tuple(out)

    init = tuple((jnp.full((1, t), NEG_INF, F32), jnp.zeros((1, t), F32), jnp.zeros((LANE, t), F32))
                 for _ in range(hg))
    carry = lax.fori_loop(0, qi // 2, lambda kp, c: step(2 * kp, c, False, width=2), init)
    carry = lax.cond(qi % 2 == 1, lambda c: step(qi - 1, c, False), lambda c: c, carry)
    carry = step(qi, carry, True)
    for j in range(hg):
        _, l, acc = carry[j]
        o_ref[0, :, j * LANE:(j + 1) * LANE] = (acc / l).T.astype(o_ref.dtype)


def causal_attention(q, qcb, k, kcb, v, vcb, heads, q2=None, k2=None, cum=None):
    B, S, _ = q.shape
    t, hg = ATT_T, ATT_HG
    w = hg * LANE
    two_part, decay = q2 is not None, cum is not None
    in_specs = [pl.BlockSpec((1, t, w), lambda b, h, i: (b, i, qcb // hg + h)),
                pl.BlockSpec((1, S, w), lambda b, h, i: (b, 0, kcb // hg + h)),
                pl.BlockSpec((1, S, w), lambda b, h, i: (b, 0, vcb // hg + h))]
    args = [q, k, v]
    scratch = []
    if two_part:
        in_specs += [pl.BlockSpec((1, t, w), lambda b, h, i: (b, i, h)),
                     pl.BlockSpec((1, S, LANE), lambda b, h, i: (b, 0, 0))]
        args += [q2, k2]
        scratch = [pltpu.VMEM((hg, S, 2 * LANE), BF16)]
    if decay:
        in_specs += [pl.BlockSpec((1, 1, S, hg), lambda b, h, i: (b, h, 0, 0))]
        args += [jnp.transpose(cum.reshape(B, S, heads // hg, hg), (0, 2, 1, 3))]
    return pl.pallas_call(
        functools.partial(_causal_attn_kernel, t=t, hg=hg, two_part=two_part, decay=decay),
        out_shape=jax.ShapeDtypeStruct((B, S, heads * LANE), BF16),
        grid=(B, heads // hg, S // t),
        in_specs=in_specs,
        out_specs=pl.BlockSpec((1, t, w), lambda b, h, i: (b, i, h)),
        scratch_shapes=scratch,
        compiler_params=_cparams(("parallel", "parallel", "arbitrary")),
        name="causal_attention",
    )(*args)


def _gelu_tanh(x):
    return 0.5 * x * (1.0 + jnp.tanh(math.sqrt(2.0 / math.pi) * (x + 0.044715 * (x * x * x))))


def _compress_kernel(x_ref, w1a_ref, w1b_ref, pe_ref, w2_ref, o_ref):
    x = x_ref[0]
    a = jnp.dot(x, w1a_ref[...], preferred_element_type=F32)
    b = jnp.dot(x, w1b_ref[...], preferred_element_type=F32)
    nc = a.shape[0]
    b_next = pltpu.roll(b, nc - 1, 0)
    pe_term = jnp.dot(pe_ref[0:1, :], w1a_ref[...], preferred_element_type=F32) + \
        jnp.dot(pe_ref[1:2, :], w1b_ref[...], preferred_element_type=F32)
    hid = _gelu_tanh(a + b_next + pe_term)
    o_ref[0] = jnp.dot(hid.astype(BF16), w2_ref[...], preferred_element_type=F32).astype(o_ref.dtype)


def compress(x, w1a, w1b, pe2, w2):
    BG, NC, KD = x.shape
    dp = w2.shape[1]
    return pl.pallas_call(
        _compress_kernel,
        out_shape=jax.ShapeDtypeStruct((BG, NC, dp), BF16),
        grid=(BG,),
        in_specs=[pl.BlockSpec((1, NC, KD), lambda i: (i, 0, 0)),
                  pl.BlockSpec((KD, dp), lambda i: (0, 0)),
                  pl.BlockSpec((KD, dp), lambda i: (0, 0)),
                  pl.BlockSpec((2, KD), lambda i: (0, 0)),
                  pl.BlockSpec((dp, dp), lambda i: (0, 0))],
        out_specs=pl.BlockSpec((1, NC, dp), lambda i: (i, 0, 0)),
        compiler_params=_cparams(("parallel",)),
        name="nsa_compress",
    )(x, w1a, w1b, pe2, w2)


def _nsa_cmp_kernel(q_ref, kc_ref, vc_ref, bias_ref, ov_ref, gate_ref, o_ref, ind_ref, *, t, n_sel):
    qi = pl.program_id(2)
    ncp = kc_ref.shape[2]
    kc = kc_ref[0, 0]
    vc = vc_ref[0, 0]
    row = lax.broadcasted_iota(jnp.int32, (t, ncp), 0) + qi * t
    col = lax.broadcasted_iota(jnp.int32, (t, ncp), 1)
    valid = row >= CMP_STRIDE * col + (CMP_BLOCK - 1)
    dn = (((1,), (1,)), ((), ()))
    heads = range(NSA_HPG)
    ss = [lax.dot_general(q_ref[0, :, h * NSA_DKP:(h + 1) * NSA_DKP], kc, dn, preferred_element_type=F32)
          for h in heads]
    ss = [jnp.where(valid, ss[h] + bias_ref[h], NEG_INF) for h in heads]
    es = [jnp.exp2(ss[h] - jnp.max(ss[h], axis=-1, keepdims=True)) for h in heads]
    ps = [jnp.where(valid, es[h] / jnp.sum(es[h], axis=-1, keepdims=True), 0.0) for h in heads]
    gates = _sigmoid(gate_ref[0, 0])
    for h in heads:
        o = jnp.dot(ps[h].astype(BF16), vc, preferred_element_type=F32)
        o_ref[0, :, h * NSA_DV:(h + 1) * NSA_DV] = (gates[:, 3 * h:3 * h + 1] * o).astype(o_ref.dtype)
    psum = functools.reduce(lambda x, y: x + y, ps)
    p_hi = psum.astype(BF16)
    p_lo = (psum - p_hi.astype(F32)).astype(BF16)
    imp = jnp.dot(p_hi, ov_ref[...], preferred_element_type=F32) + jnp.dot(p_lo, ov_ref[...], preferred_element_type=F32)
    nbp = imp.shape[1]
    blk = lax.broadcasted_iota(jnp.int32, (t, nbp), 1)
    cur = (lax.broadcasted_iota(jnp.int32, (t, nbp), 0) + qi * t) // SEL_BLOCK
    forced = (blk == 0) | (blk == cur) | (blk == cur - 1)
    score = jnp.where(blk <= cur, imp + jnp.where(forced, SEL_FORCE, 0.0), NEG_INF)
    sel = jnp.zeros((t, nbp), F32)
    for _ in range(n_sel):
        mx = jnp.max(score, axis=-1, keepdims=True)
        first = jnp.min(jnp.where(score == mx, blk, nbp), axis=-1, keepdims=True)
        hit = blk == first
        sel = jnp.where(hit, 1.0, sel)
        score = jnp.where(hit, -jnp.inf, score)
    ind_ref[0, 0] = sel.astype(ind_ref.dtype)


def nsa_cmp_select(z3, kc, vc, bias_c, overlap, gates, n_sel):
    B, S, _ = z3.shape
    G = NSA_GROUPS
    t = min(NSA_CMP_T, S)
    ncp = kc.shape[2]
    nbp = overlap.shape[1]
    qw = NSA_HPG * NSA_DKP
    ow = NSA_HPG * NSA_DV
    return pl.pallas_call(
        functools.partial(_nsa_cmp_kernel, t=t, n_sel=n_sel),
        out_shape=(jax.ShapeDtypeStruct((B, S, G * ow), BF16), jax.ShapeDtypeStruct((B, G, S, nbp), BF16)),
        grid=(B, G, S // t),
        in_specs=[pl.BlockSpec((1, t, qw), lambda b, g, i: (b, i, CB_NQ * LANE // qw + g)),
                  pl.BlockSpec((1, 1, ncp, NSA_DKP), lambda b, g, i: (b, g, 0, 0)),
                  pl.BlockSpec((1, 1, ncp, NSA_DV), lambda b, g, i: (b, g, 0, 0)),
                  pl.BlockSpec((NSA_HPG, t, ncp), lambda b, g, i: (g, i, 0)),
                  pl.BlockSpec((ncp, nbp), lambda b, g, i: (0, 0)),
                  pl.BlockSpec((1, 1, t, LANE), lambda b, g, i: (b, g, i, 0))],
        out_specs=(pl.BlockSpec((1, t, ow), lambda b, g, i: (b, i, g)),
                   pl.BlockSpec((1, 1, t, nbp), lambda b, g, i: (b, g, i, 0))),
        compiler_params=_cparams(("parallel", "parallel", "arbitrary")),
        name="nsa_cmp_select",
    )(z3, kc, vc, bias_c, overlap, gates)


def _nsa_sw_kernel(q_ref, ks_ref, vs_ref, kw_ref, vw_ref, ind_ref, e_ref, bias_ref, gate_ref, oc_ref, o_ref,
                   *, t):
    qi = pl.program_id(1)
    hp, G = NSA_HPG, NSA_GROUPS
    q4 = [jnp.concatenate([q_ref[0, :, (g * hp + h) * NSA_DKP:(g * hp + h + 1) * NSA_DKP] for h in range(hp)], axis=0)
          for g in range(G)]
    inds = [ind_ref[0, g] for g in range(G)]
    dn = (((1,), (1,)), ((), ()))

    def step(kb, carry, k_ref, v_ref, selected, width=1):
        off = pl.multiple_of(kb * t, t)
        tk = width * t
        ri = lax.broadcasted_iota(jnp.int32, (tk, t), 0)
        ci = lax.broadcasted_iota(jnp.int32, (tk, t), 1)
        d = (qi - kb) * t + ci - ri
        near = (d >= 0) if selected else (d >= 0) & (d < WINDOW)
        groups = range(G)
        ss = [lax.dot_general(k_ref[0, pl.ds(off, tk), g * NSA_DKP:(g + 1) * NSA_DKP], q4[g], dn,
                              preferred_element_type=F32) for g in groups]
        negs = []
        for g in groups:
            mask = near
            if selected:
                hit = jnp.dot(e_ref[pl.ds(off, tk), :], inds[g], preferred_element_type=F32)
                mask = near & (hit > 0.5)
            neg = jnp.where(mask, 0.0, NEG_INF)
            negs.append(jnp.concatenate([neg] * hp, axis=1))
        bias = [jnp.concatenate([bias_ref[g, jnp.minimum(qi - kb - w, 2)] for w in range(width)], axis=0)
                if width > 1 else bias_ref[g, jnp.minimum(qi - kb, 2)] for g in groups]
        ss = [ss[g] + bias[g] + negs[g] for g in groups]
        ms = [jnp.maximum(carry[g][0], jnp.max(ss[g], axis=0, keepdims=True)) for g in groups]
        ps = [jnp.exp2(ss[g] - ms[g]) for g in groups]
        out = []
        for g in groups:
            m, l, acc = carry[g]
            a = jnp.exp2(m - ms[g])
            l = a * l + jnp.sum(ps[g], axis=0, keepdims=True)
            v = v_ref[0, pl.ds(off, tk), g * NSA_DV:(g + 1) * NSA_DV]
            acc = a * acc + lax.dot_general(v, ps[g].astype(BF16), (((0,), (0,)), ((), ())),
                                            preferred_element_type=F32)
            out.append((ms[g], l, acc))
        return tuple(out)

    init = tuple((jnp.full((1, hp * t), NEG_INF, F32), jnp.zeros((1, hp * t), F32), jnp.zeros((NSA_DV, hp * t), F32))
                 for _ in range(G))
    sel = lax.fori_loop(0, (qi + 1) // 2, lambda kp, c: step(2 * kp, c, ks_ref, vs_ref, True, width=2), init)
    sel = lax.cond(qi % 2 == 0, lambda c: step(qi, c, ks_ref, vs_ref, True), lambda c: c, sel)
    lo = jnp.maximum(qi - WINDOW // t, 0)
    win = lax.fori_loop(lo, qi + 1, lambda kb, c: step(kb, c, kw_ref, vw_ref, False), init)
    for g in range(G):
        o_s = sel[g][2] / sel[g][1]
        o_w = win[g][2] / win[g][1]
        gates = _sigmoid(gate_ref[0, g])
        for h in range(hp):
            r = slice(h * t, (h + 1) * t)
            c = slice((g * hp + h) * NSA_DV, (g * hp + h + 1) * NSA_DV)
            o = gates[3 * h + 1:3 * h + 2, :] * o_s[:, r] + gates[3 * h + 2:3 * h + 3, :] * o_w[:, r]
            o_ref[0, :, c] = (oc_ref[0, :, c].astype(F32) + o.T).astype(o_ref.dtype)


def nsa_sel_win(z3, ind, expand, bias_sw, gates, o_cmp):
    B, S, _ = z3.shape
    G = NSA_GROUPS
    t = NSA_T
    qw = NSA_HEADS * NSA_DKP
    ow = NSA_HEADS * NSA_DV
    kw = G * NSA_DKP
    vw = G * NSA_DV
    nbp = ind.shape[2]
    kspec = lambda cb: pl.BlockSpec((1, S, kw), lambda b, i: (b, 0, cb * LANE // kw))
    vspec = lambda cb: pl.BlockSpec((1, S, vw), lambda b, i: (b, 0, cb * LANE // vw))
    return pl.pallas_call(
        functools.partial(_nsa_sw_kernel, t=t),
        out_shape=jax.ShapeDtypeStruct((B, S, ow), BF16),
        grid=(B, S // t),
        in_specs=[pl.BlockSpec((1, t, qw), lambda b, i: (b, i, CB_NQ * LANE // qw)),
                  kspec(CB_NKS), vspec(CB_NVS), kspec(CB_NKW), vspec(CB_NVW),
                  pl.BlockSpec((1, G, nbp, t), lambda b, i: (b, 0, 0, i)),
                  pl.BlockSpec((S, nbp), lambda b, i: (0, 0)),
                  pl.BlockSpec((G, 3, t, NSA_HPG * t), lambda b, i: (0, 0, 0, 0)),
                  pl.BlockSpec((1, G, gates.shape[2], t), lambda b, i: (b, 0, 0, i)),
                  pl.BlockSpec((1, t, ow), lambda b, i: (b, i, 0))],
        out_specs=pl.BlockSpec((1, t, ow), lambda b, i: (b, i, 0)),
        compiler_params=_cparams(("parallel", "arbitrary")),
        name="nsa_sel_win",
    )(z3, z3, z3, z3, z3, ind, expand, bias_sw, gates, o_cmp)


def _merge_kernel(of_ref, om_ref, on_ref, wb_ref, g0_ref, g1_ref, g2_ref, o_ref):
    acc = None
    for n, (o_r, g_r) in enumerate(((of_ref, g0_ref), (om_ref, g1_ref), (on_ref, g2_ref))):
        y = jnp.dot(o_r[...], wb_ref[n], preferred_element_type=F32)
        y = _sigmoid(g_r[...].astype(F32)) * y
        acc = y if acc is None else acc + y
    o_ref[...] = acc.astype(o_ref.dtype)


def merge_branches(o_fox, o_mla, o_nsa, wb, z, tm, tn):
    T = o_fox.shape[0]
    D = wb.shape[2]
    gspec = lambda n: pl.BlockSpec((tm, tn), lambda i, j: (i, (CB_MG * LANE + n * D) // tn + j))
    ospec = pl.BlockSpec((tm, BRANCH_W), lambda i, j: (i, 0))
    return pl.pallas_call(
        _merge_kernel,
        out_shape=jax.ShapeDtypeStruct((T, D), BF16),
        grid=(T // tm, D // tn),
        in_specs=[ospec, ospec, ospec,
                  pl.BlockSpec((N_BRANCH, BRANCH_W, tn), lambda i, j: (0, 0, j)),
                  gspec(0), gspec(1), gspec(2)],
        out_specs=pl.BlockSpec((tm, tn), lambda i, j: (i, j)),
        compiler_params=_cparams(("parallel", "arbitrary")),
        name="merge_branches",
    )(o_fox, o_mla, o_nsa, wb, z, z, z)


def _mm_res_kernel(a_ref, w_ref, r_ref, o_ref):
    o_ref[...] = r_ref[...] + jnp.dot(a_ref[...], w_ref[...], preferred_element_type=F32)


def matmul_residual(a, w, res, tm, tn):
    T, K = a.shape
    N = w.shape[1]
    return pl.pallas_call(
        _mm_res_kernel,
        out_shape=jax.ShapeDtypeStruct((T, N), F32),
        grid=(T // tm, N // tn),
        in_specs=[pl.BlockSpec((tm, K), lambda i, j: (i, 0)),
                  pl.BlockSpec((K, tn), lambda i, j: (0, j)),
                  pl.BlockSpec((tm, tn), lambda i, j: (i, j))],
        out_specs=pl.BlockSpec((tm, tn), lambda i, j: (i, j)),
        compiler_params=_cparams(("parallel", "arbitrary")),
        name="matmul_residual",
    )(a, w, res)


def _mem_attn_kernel(h_ref, g_ref, wq_ref, kv_ref, wo_ref, o_ref):
    x = h_ref[0]
    ms = jnp.mean(x * x, axis=-1, keepdims=True)
    u = (x * lax.rsqrt(ms + EPS) * g_ref[...]).astype(BF16)
    q = jnp.dot(u, wq_ref[...], preferred_element_type=F32).astype(BF16)
    dn = (((1,), (1,)), ((), ()))
    hw = MEM_HEADS * MEM_DH
    heads = range(MEM_HEADS)
    cs = [slice(h * MEM_DH, (h + 1) * MEM_DH) for h in heads]
    ss = [lax.dot_general(q[:, cs[h]], kv_ref[0, :, cs[h]], dn, preferred_element_type=F32) for h in heads]
    es = [jnp.exp2(ss[h] - jnp.max(ss[h], axis=-1, keepdims=True)) for h in heads]
    ps = [es[h] / jnp.sum(es[h], axis=-1, keepdims=True) for h in heads]
    outs = [jnp.dot(ps[h].astype(BF16), kv_ref[0, :, hw + h * MEM_DH: hw + (h + 1) * MEM_DH],
                    preferred_element_type=F32).astype(BF16) for h in heads]
    o = jnp.concatenate(outs, axis=1)
    o_ref[0] = x + jnp.dot(o, wo_ref[...], preferred_element_type=F32)


def memory_attention_block(h3, g, wq, kv, wo, tq):
    B, S, D = h3.shape
    M = kv.shape[1]
    hw = MEM_HEADS * MEM_DH
    return pl.pallas_call(
        _mem_attn_kernel,
        out_shape=jax.ShapeDtypeStruct((B, S, D), F32),
        grid=(B, S // tq),
        in_specs=[pl.BlockSpec((1, tq, D), lambda b, i: (b, i, 0)),
                  pl.BlockSpec((1, D), lambda b, i: (0, 0)),
                  pl.BlockSpec((D, hw), lambda b, i: (0, 0)),
                  pl.BlockSpec((1, M, 2 * hw), lambda b, i: (b, 0, 0)),
                  pl.BlockSpec((hw, D), lambda b, i: (0, 0))],
        out_specs=pl.BlockSpec((1, tq, D), lambda b, i: (b, i, 0)),
        compiler_params=_cparams(("parallel", "arbitrary")),
        name="memory_attention",
    )(h3, g.reshape(1, D).astype(F32), wq, kv, wo)


def _router_kernel(h_ref, g_ref, w_ref, b_ref, lg_ref, u_ref):
    x = h_ref[...]
    ms = jnp.mean(x * x, axis=-1, keepdims=True)
    u = x * lax.rsqrt(ms + EPS) * g_ref[...]
    u_ref[...] = u.astype(u_ref.dtype)
    lg_ref[...] = jnp.dot(u, w_ref[...], preferred_element_type=F32, precision=lax.Precision.HIGHEST) + b_ref[...]


def moe_router(h, g, w, b, tm):
    T, D = h.shape
    N = w.shape[1]
    return pl.pallas_call(
        _router_kernel,
        out_shape=(jax.ShapeDtypeStruct((T, N), F32), jax.ShapeDtypeStruct((T, D), BF16)),
        grid=(T // tm,),
        in_specs=[pl.BlockSpec((tm, D), lambda i: (i, 0)),
                  pl.BlockSpec((1, D), lambda i: (0, 0)),
                  pl.BlockSpec((D, N), lambda i: (0, 0)),
                  pl.BlockSpec((1, N), lambda i: (0, 0))],
        out_specs=(pl.BlockSpec((tm, N), lambda i: (i, 0)), pl.BlockSpec((tm, D), lambda i: (i, 0))),
        compiler_params=_cparams(("parallel",)),
        name="moe_router",
    )(h, g.reshape(1, D).astype(F32), w, b)


def _moe_kernel(be_ref, nu_ref, x_ref, wg_ref, wu_ref, wd_ref, o_ref, acc_ref):
    i = pl.program_id(0)
    j = pl.program_id(1)
    last = pl.num_programs(1) - 1
    used = i < nu_ref[0]

    @pl.when(used)
    def _():
        x = x_ref[...]
        a = jnp.dot(x, wg_ref[0, 0].astype(BF16), preferred_element_type=F32)
        b = jnp.dot(x, wu_ref[0, 0].astype(BF16), preferred_element_type=F32)
        hdn = (a * _sigmoid(a) * b).astype(BF16)
        y = jnp.dot(hdn, wd_ref[0, 0].astype(BF16), preferred_element_type=F32)

        @pl.when(j == 0)
        def _():
            acc_ref[...] = y

        @pl.when(j > 0)
        def _():
            acc_ref[...] += y

        @pl.when(j == last)
        def _():
            o_ref[...] = acc_ref[...].astype(o_ref.dtype)

    @pl.when(jnp.logical_not(used) & (j == last))
    def _():
        o_ref[...] = jnp.zeros(o_ref.shape, o_ref.dtype)


def moe_experts(blk_e, n_used, xr, wg, wu, wd, layer):
    P, D = xr.shape
    De = wg.shape[3]
    tm, dc = MOE_TM, MOE_DC
    nj = De // dc
    jj = lambda i, j, nu: jnp.where(i < nu[0], j, nj - 1)
    grid_spec = pltpu.PrefetchScalarGridSpec(
        num_scalar_prefetch=2,
        grid=(P // tm, nj),
        in_specs=[pl.BlockSpec((tm, D), lambda i, j, be, nu: (i, 0)),
                  pl.BlockSpec((1, 1, D, dc), lambda i, j, be, nu: (layer, be[i], 0, jj(i, j, nu))),
                  pl.BlockSpec((1, 1, D, dc), lambda i, j, be, nu: (layer, be[i], 0, jj(i, j, nu))),
                  pl.BlockSpec((1, 1, dc, D), lambda i, j, be, nu: (layer, be[i], jj(i, j, nu), 0))],
        out_specs=pl.BlockSpec((tm, D), lambda i, j, be, nu: (i, 0)),
        scratch_shapes=[pltpu.VMEM((tm, D), F32)],
    )
    return pl.pallas_call(
        _moe_kernel,
        out_shape=jax.ShapeDtypeStruct((P, D), BF16),
        grid_spec=grid_spec,
        compiler_params=_cparams(("arbitrary", "arbitrary")),
        name="moe_experts",
    )(blk_e, n_used, xr, wg, wu, wd)


def _rmsnorm_kernel(x_ref, g_ref, o_ref):
    x = x_ref[...]
    ms = jnp.mean(x * x, axis=-1, keepdims=True)
    o_ref[...] = x * lax.rsqrt(ms + EPS) * g_ref[...]


def rmsnorm_rows(x, g, tm):
    T, D = x.shape
    return pl.pallas_call(
        _rmsnorm_kernel,
        out_shape=jax.ShapeDtypeStruct((T, D), F32),
        grid=(T // tm,),
        in_specs=[pl.BlockSpec((tm, D), lambda i: (i, 0)), pl.BlockSpec((1, D), lambda i: (0, 0))],
        out_specs=pl.BlockSpec((tm, D), lambda i: (i, 0)),
        compiler_params=_cparams(("parallel",)),
        name="final_rmsnorm",
    )(x, g.reshape(1, D).astype(F32))


def _pad_cols(w, width):
    return jnp.pad(w, ((0, 0), (0, width - w.shape[1])))


def _pack_w_in(w):
    D = w.shape[0]
    offs = np.cumsum([0, 1024, 1024, 1024, FORGET_COLS, MLA_Q_RANK, MLA_KV_RANK, MLA_ROPE,
                      NSA_HEADS * NSA_DK, NSA_GROUPS * NSA_DK, NSA_GROUPS * NSA_DV, NSA_GROUPS * NSA_DK,
                      NSA_GROUPS * NSA_DV, NSA_GROUPS * NSA_DK, NSA_GROUPS * NSA_DV, NSA_GATE_COLS, N_BRANCH * D])
    seg = lambda i: w[:, offs[i]:offs[i + 1]]
    fq, fk, fv, ff, mcq, mckv, mkr, nq, nkc, nvc, nks, nvs, nkw, nvw, ngt, mg = [seg(i) for i in range(16)]
    half = MLA_ROPE // 2
    kr_rot = jnp.concatenate([-mkr[:, half:], mkr[:, :half]], axis=1)
    padk = lambda a, n: jnp.pad(a.reshape(D, n, NSA_DK), ((0, 0), (0, 0), (0, NSA_DKP - NSA_DK))).reshape(D, n * NSA_DKP)
    zeros = lambda nb: jnp.zeros((D, nb * LANE), w.dtype)
    cols = [mckv, mkr, kr_rot, zeros(1), mcq, padk(nkc, NSA_GROUPS), padk(nq * (NSA_DK ** -0.5 * LOG2E), NSA_HEADS),
            fq * (FOX_DH ** -0.5 * LOG2E), fk, fv, padk(nks, NSA_GROUPS), padk(nkw, NSA_GROUPS), nvc, nvs, nvw, zeros(2), mg]
    wz = jnp.concatenate(cols, axis=1).astype(BF16)
    assert wz.shape[1] == CB_MG * LANE + N_BRANCH * D
    w_small = _pad_cols(jnp.concatenate([ff, ngt], axis=1), LANE)
    return wz, w_small


def _pack_w_uq(w):
    K = w.shape[0]
    w3 = w.reshape(K, MLA_HEADS, MLA_NOPE + MLA_ROPE) * ((MLA_NOPE + MLA_ROPE) ** -0.5 * LOG2E)
    nope = w3[:, :, :MLA_NOPE].reshape(K, MLA_HEADS * MLA_NOPE)
    r = w3[:, :, MLA_NOPE:]
    half = MLA_ROPE // 2
    r_rot = jnp.concatenate([-r[:, :, half:], r[:, :, :half]], axis=2)
    padr = lambda a: jnp.pad(a, ((0, 0), (0, 0), (0, LANE - MLA_ROPE))).reshape(K, MLA_HEADS * LANE)
    return jnp.concatenate([nope, padr(r), padr(r_rot)], axis=1).astype(BF16)


def _pack_w_ukv(w):
    K = w.shape[0]
    w3 = w.reshape(K, MLA_HEADS, MLA_NOPE + MLA_DV)
    return jnp.concatenate([w3[:, :, :MLA_NOPE].reshape(K, -1), w3[:, :, MLA_NOPE:].reshape(K, -1)], axis=1).astype(BF16)


def _t5_bucket(dist):
    dist = jnp.maximum(dist, 0)
    exact = REL_BUCKETS // 2
    df = jnp.maximum(dist, 1).astype(F32)
    large = exact + (jnp.log(df / exact) / math.log(REL_MAX_DIST / exact) * (REL_BUCKETS - exact)).astype(jnp.int32)
    large = jnp.minimum(large, REL_BUCKETS - 1)
    return jnp.where(dist < exact, dist, large)


def _position_tables(S, rel_bias):
    t = NSA_T
    half = MLA_ROPE // 2
    inv = ROPE_THETA ** (-jnp.arange(half, dtype=F32) / half)
    ang = jnp.arange(S, dtype=F32)[:, None] * inv
    c, s = jnp.cos(ang), jnp.sin(ang)
    cos = _pad_cols(jnp.concatenate([c, c], axis=1), LANE)
    sin = _pad_cols(jnp.concatenate([s, s], axis=1), LANE)
    ncp = max(S // CMP_STRIDE, LANE)
    pos = jnp.arange(S)

    def bias_of(dist):
        onehot = jax.nn.one_hot(_t5_bucket(dist), REL_BUCKETS, dtype=F32)
        return jnp.einsum("...b,bh->h...", onehot, rel_bias, precision=lax.Precision.HIGHEST)

    bias_c = bias_of(pos[:, None] - (CMP_STRIDE * jnp.arange(ncp)[None, :] + CMP_BLOCK - 1))
    i = jnp.arange(t)
    bias_sw = jnp.stack([bias_of(k * t + i[:, None] - i[None, :]) for k in range(3)], axis=1)
    bias_sw = bias_sw.reshape(NSA_GROUPS, NSA_HPG, 3, t, t).transpose(0, 2, 4, 1, 3).reshape(NSA_GROUPS, 3, t, NSA_HPG * t)
    n_cmp = (S - CMP_BLOCK) // CMP_STRIDE + 1
    n_blk = S // SEL_BLOCK
    nbp = max(n_blk, LANE)
    cstart = CMP_STRIDE * jnp.arange(ncp)
    sstart = SEL_BLOCK * jnp.arange(nbp)
    ov = jnp.clip(jnp.minimum(cstart[:, None] + CMP_BLOCK, sstart[None, :] + SEL_BLOCK)
                  - jnp.maximum(cstart[:, None], sstart[None, :]), 0, None).astype(F32) / CMP_STRIDE
    ov = jnp.where((jnp.arange(ncp)[:, None] < n_cmp) & (jnp.arange(nbp)[None, :] < n_blk), ov, 0.0).astype(BF16)
    expand = ((pos[:, None] // SEL_BLOCK) == jnp.arange(nbp)[None, :]).astype(BF16)
    return cos, sin, bias_c, bias_sw, ov, expand


def _token_mixers(h, p, tabs, B, S):
    T, D = h.shape
    cos, sin, bias_c, bias_sw, overlap, expand = tabs
    wz, w_small = _pack_w_in(p["w_in"])
    z = rms_matmul(h, 0, p["g_mix"], wz, BF16, tm=min(T, 1024), tn=1024)
    zs = rms_matmul(h, 0, p["g_mix"], w_small, F32, tm=min(T, 512), tn=LANE, precise=True)
    z3 = z.reshape(B, S, Z_BLOCKS * LANE)

    log_f = jax.nn.log_sigmoid(zs[:, :FORGET_COLS] + p["b_forget"].astype(F32)).reshape(B, S, FOX_HEADS)
    cum = jnp.cumsum(log_f, axis=1) * LOG2E
    o_fox = causal_attention(z3, CB_FQ, z3, CB_FK, z3, CB_FV, FOX_HEADS, cum=cum)

    q_nope, q_rope = mla_q_proj(z, p["g_cq"], _pack_w_uq(p["w_uq"]), cos, sin, S, tm=min(S, 512))
    kv = rms_matmul(z, CB_CKV * LANE // MLA_KV_RANK, p["g_ckv"], _pack_w_ukv(p["w_ukv"]), BF16, tm=min(T, 1024), tn=1024)
    kr = z3[:, :, CB_KR * LANE:(CB_KR + 1) * LANE].astype(F32)
    kr = kr[..., :MLA_ROPE] * cos[None, :, :MLA_ROPE] + kr[..., MLA_ROPE:] * sin[None, :, :MLA_ROPE]
    k_rope = jnp.pad(kr, ((0, 0), (0, 0), (0, LANE - MLA_ROPE))).astype(BF16)
    hw = MLA_HEADS * LANE
    kv3 = kv.reshape(B, S, 2 * hw)
    o_mla = causal_attention(q_nope.reshape(B, S, hw), 0, kv3, 0, kv3, MLA_HEADS, MLA_HEADS,
                             q2=q_rope.reshape(B, S, hw), k2=k_rope)

    G = NSA_GROUPS
    NC = S // CMP_STRIDE
    ncp = bias_c.shape[2]

    def compress_branch(cb, dp, d, pe, w1, w2):
        raw = z3[:, :, cb * LANE: cb * LANE + G * dp].reshape(B, NC, CMP_STRIDE, G, dp)
        x = jnp.transpose(raw, (0, 3, 1, 2, 4)).reshape(B * G, NC, CMP_STRIDE * dp)
        w1p = jnp.pad(w1.reshape(CMP_BLOCK, d, d), ((0, 0), (0, dp - d), (0, dp - d))).astype(BF16)
        w1a = w1p[:CMP_STRIDE].reshape(CMP_STRIDE * dp, dp)
        w1b = w1p[CMP_STRIDE:].reshape(CMP_STRIDE * dp, dp)
        pe2 = jnp.pad(pe, ((0, 0), (0, dp - d))).reshape(2, CMP_STRIDE * dp).astype(BF16)
        w2p = jnp.pad(w2, ((0, dp - d), (0, dp - d))).astype(BF16)
        out = compress(x, w1a, w1b, pe2, w2p).reshape(B, G, NC, dp)
        return jnp.pad(out, ((0, 0), (0, 0), (0, ncp - NC), (0, 0)))

    kc = compress_branch(CB_NKC, NSA_DKP, NSA_DK, p["pe_k"], p["w_cmp_k1"], p["w_cmp_k2"])
    vc = compress_branch(CB_NVC, NSA_DV, NSA_DV, p["pe_v"], p["w_cmp_v1"], p["w_cmp_v2"])
    gl = zs[:, FORGET_COLS:FORGET_COLS + NSA_GATE_COLS].reshape(B, S, G, NSA_HPG * 3)
    gl = jnp.transpose(gl, (0, 2, 1, 3))
    gl_rows = jnp.pad(gl, ((0, 0), (0, 0), (0, 0), (0, LANE - NSA_HPG * 3)))
    gl_cols = jnp.pad(jnp.swapaxes(gl, 2, 3), ((0, 0), (0, 0), (0, 16 - NSA_HPG * 3), (0, 0)))
    o_cmp, ind = nsa_cmp_select(z3, kc, vc, bias_c, overlap, gl_rows, min(N_SEL, S // SEL_BLOCK))
    o_nsa = nsa_sel_win(z3, jnp.swapaxes(ind, 2, 3), expand, bias_sw, gl_cols, o_cmp)

    merged = merge_branches(o_fox.reshape(T, -1), o_mla.reshape(T, -1), o_nsa.reshape(T, -1),
                            p["w_branch"].astype(BF16), z, tm=min(T, 1024), tn=512)
    return matmul_residual(merged, p["w_out"].astype(BF16), h, tm=min(T, 1024), tn=512)


def _memory_block(h, mem2, p, B, S):
    T, D = h.shape
    kv = rms_matmul(mem2, 0, p["g_mem_kv"], p["w_mem_kv"].astype(BF16), BF16, tm=min(mem2.shape[0], 512), tn=512)
    out = memory_attention_block(h.reshape(B, S, D), p["g_mem_q"], (p["w_mem_q"] * (MEM_DH ** -0.5 * LOG2E)).astype(BF16),
                                 kv.reshape(B, -1, kv.shape[1]), p["w_mem_o"].astype(BF16), tq=min(S, 512))
    return out.reshape(T, D)


def _moe_block(h, p, experts, layer):
    T, D = h.shape
    tm = MOE_TM
    w_r = _pad_cols(jnp.concatenate([p["w_router_group"], p["w_router_expert"]], axis=1), LANE).astype(F32)
    b_r = _pad_cols(jnp.concatenate([p["b_router_group"], p["b_router_expert"]])[None, :], LANE).astype(F32)
    logits, u = moe_router(h, p["g_moe"], w_r, b_r, tm=min(T, 512))
    glog = logits[:, :N_GROUPS]
    gsel = jnp.argmax(glog, axis=-1).astype(jnp.int32)
    pg = jnp.max(jax.nn.softmax(glog, axis=-1), axis=-1, keepdims=True)
    elog = logits[:, N_GROUPS:N_GROUPS + N_EXPERTS].reshape(T, N_GROUPS, EXPERTS_PER_GROUP)
    elog = jnp.take_along_axis(elog, gsel[:, None, None], axis=1)[:, 0]
    eprob = jax.nn.softmax(elog, axis=-1)
    j0 = jnp.argmax(eprob, axis=-1).astype(jnp.int32)
    lane = jnp.arange(EXPERTS_PER_GROUP, dtype=jnp.int32)[None, :]
    j1 = jnp.argmax(jnp.where(lane == j0[:, None], -jnp.inf, eprob), axis=-1).astype(jnp.int32)
    top_j = jnp.stack([j0, j1], axis=-1)
    top_p = jnp.take_along_axis(eprob, top_j, axis=-1)
    top_p = top_p / jnp.sum(top_p, axis=-1, keepdims=True)
    weight = pg * top_p
    flat_e = (gsel[:, None] * EXPERTS_PER_GROUP + top_j.astype(jnp.int32)).reshape(-1)
    TK = T * TOP_K
    onehot = (flat_e[:, None] == jnp.arange(N_EXPERTS, dtype=jnp.int32)[None, :]).astype(jnp.int32)
    rank = jnp.sum((jnp.cumsum(onehot, axis=0) - onehot) * onehot, axis=1)
    counts = jnp.sum(onehot, axis=0)
    pcounts = ((counts + tm - 1) // tm) * tm
    pends = jnp.cumsum(pcounts)
    dest = (pends - pcounts)[flat_e] + rank
    P = TK + N_EXPERTS * tm
    n_rb = P // tm
    row_tok = (jnp.arange(P, dtype=jnp.int32) % T).at[dest].set(jnp.repeat(jnp.arange(T, dtype=jnp.int32), TOP_K))
    blk_e = jnp.sum((pends[None, :] <= (jnp.arange(n_rb, dtype=jnp.int32) * tm)[:, None]).astype(jnp.int32), axis=1)
    blk_e = jnp.minimum(blk_e, N_EXPERTS - 1).astype(jnp.int32)
    n_used = (pends[-1] // tm).astype(jnp.int32).reshape(1)
    xr = u[row_tok]
    y = moe_experts(blk_e, n_used, xr, experts[0].astype(F32), experts[1].astype(F32), experts[2].astype(F32), layer)
    d2 = dest.reshape(T, TOP_K)
    return h + (weight[:, 0:1] * y[d2[:, 0]].astype(F32) + weight[:, 1:2] * y[d2[:, 1]].astype(F32))


_LAYER_KEYS = ("g_mix", "w_in", "b_forget", "g_cq", "g_ckv", "w_uq", "w_ukv", "pe_k", "pe_v", "w_cmp_k1", "w_cmp_k2",
               "w_cmp_v1", "w_cmp_v2", "w_branch", "w_out", "g_mem_q", "g_mem_kv", "w_mem_q", "w_mem_kv", "w_mem_o",
               "g_moe", "w_router_group", "b_router_group", "w_router_expert", "b_router_expert")


def kernel(x, mem, g_mix, w_in, b_forget, g_cq, g_ckv, w_uq, w_ukv, pe_k, pe_v, w_cmp_k1, w_cmp_k2, w_cmp_v1, w_cmp_v2, rel_bias, w_branch, w_out, g_mem_q, g_mem_kv, w_mem_q, w_mem_kv, w_mem_o, g_moe, w_router_group, b_router_group, w_router_expert, b_router_expert, w_exp_gate, w_exp_up, w_exp_down, g_final):
    B, S, D = x.shape
    T = B * S
    stacked = dict(g_mix=g_mix, w_in=w_in, b_forget=b_forget, g_cq=g_cq, g_ckv=g_ckv, w_uq=w_uq, w_ukv=w_ukv,
                   pe_k=pe_k, pe_v=pe_v, w_cmp_k1=w_cmp_k1, w_cmp_k2=w_cmp_k2, w_cmp_v1=w_cmp_v1, w_cmp_v2=w_cmp_v2,
                   w_branch=w_branch, w_out=w_out, g_mem_q=g_mem_q, g_mem_kv=g_mem_kv, w_mem_q=w_mem_q,
                   w_mem_kv=w_mem_kv, w_mem_o=w_mem_o, g_moe=g_moe, w_router_group=w_router_group,
                   b_router_group=b_router_group, w_router_expert=w_router_expert, b_router_expert=b_router_expert,
                   w_exp_gate=w_exp_gate, w_exp_up=w_exp_up, w_exp_down=w_exp_down)
    tabs = _position_tables(S, rel_bias.astype(F32) * LOG2E)
    h = x.reshape(T, D).astype(F32)
    mem2 = mem.reshape(-1, D).astype(F32)
    for l in range(w_in.shape[0]):
        p = {k: stacked[k][l] for k in _LAYER_KEYS}
        h = _token_mixers(h, p, tabs, B, S)
        h = _memory_block(h, mem2, p, B, S)
        h = _moe_block(h, p, (w_exp_gate, w_exp_up, w_exp_down), l)
    return rmsnorm_rows(h, g_final, tm=min(T, 512)).reshape(B, S, D)
```

```python
import functools
import math

import jax
import jax.numpy as jnp
import numpy as np
from jax import lax
from jax.experimental import pallas as pl
from jax.experimental.pallas import tpu as pltpu

F32 = jnp.float32
BF16 = jnp.bfloat16

EPS = 1e-6
NEG_INF = -1e30
LOG2E = math.log2(math.e)
LANE = 128

FOX_HEADS, FOX_DH = 8, 128
MLA_HEADS, MLA_NOPE, MLA_ROPE, MLA_DV = 8, 128, 64, 128
MLA_Q_RANK, MLA_KV_RANK = 768, 512
ROPE_THETA = 10000.0
NSA_HEADS, NSA_GROUPS, NSA_DK, NSA_DV = 8, 2, 192, 128
NSA_HPG = NSA_HEADS // NSA_GROUPS
NSA_DKP = 256
CMP_BLOCK, CMP_STRIDE, SEL_BLOCK, N_SEL, WINDOW = 32, 16, 64, 8, 512
SEL_FORCE = 1e6
REL_BUCKETS, REL_MAX_DIST = 32, 128
N_BRANCH, BRANCH_W = 3, 1024
MEM_HEADS, MEM_DH = 4, 128
N_GROUPS, EXPERTS_PER_GROUP, TOP_K = 4, 8, 2
N_EXPERTS = N_GROUPS * EXPERTS_PER_GROUP
FORGET_COLS, NSA_GATE_COLS = FOX_HEADS, NSA_HEADS * 3

CB_CKV, CB_KR, CB_CQ = 0, 4, 6
CB_NKC, CB_NQ = 12, 16
CB_FQ, CB_FK, CB_FV = 32, 40, 48
CB_NKS, CB_NKW = 56, 60
CB_NVC, CB_NVS, CB_NVW = 64, 66, 68
CB_MG = 72
Z_BLOCKS = 120

ATT_T = 256
ATT_HG = 8
NSA_T = 256
NSA_CMP_T = 512
MOE_TM = 512
MOE_DC = 512
VMEM_LIMIT = 56 * 1024 * 1024


def _cparams(sem):
    return pltpu.CompilerParams(dimension_semantics=sem, vmem_limit_bytes=VMEM_LIMIT)


def _sigmoid(x):
    return 1.0 / (1.0 + jnp.exp(-x))


def _rms_mm_kernel(x_ref, g_ref, w_ref, o_ref, u_ref):
    @pl.when(pl.program_id(1) == 0)
    def _():
        x = x_ref[...].astype(F32)
        ms = jnp.mean(x * x, axis=-1, keepdims=True)
        u_ref[...] = (x * lax.rsqrt(ms + EPS) * g_ref[...]).astype(u_ref.dtype)

    o_ref[...] = jnp.dot(u_ref[...], w_ref[...], preferred_element_type=F32).astype(o_ref.dtype)


def rms_matmul(x, xcol, g, w, out_dtype, tm, tn):
    T = x.shape[0]
    K, N = w.shape
    return pl.pallas_call(
        _rms_mm_kernel,
        out_shape=jax.ShapeDtypeStruct((T, N), out_dtype),
        grid=(T // tm, N // tn),
        in_specs=[pl.BlockSpec((tm, K), lambda i, j: (i, xcol)),
                  pl.BlockSpec((1, K), lambda i, j: (0, 0)),
                  pl.BlockSpec((K, tn), lambda i, j: (0, j))],
        out_specs=pl.BlockSpec((tm, tn), lambda i, j: (i, j)),
        scratch_shapes=[pltpu.VMEM((tm, K), BF16)],
        compiler_params=_cparams(("parallel", "arbitrary")),
        name="rms_matmul",
    )(x, g.reshape(1, K).astype(F32), w)


def _in_proj_kernel(x_ref, g_ref, w_ref, ws_ref, o_ref, os_ref, u_ref):
    @pl.when(pl.program_id(1) == 0)
    def _():
        x = x_ref[...]
        ms = jnp.mean(x * x, axis=-1, keepdims=True)
        u = x * lax.rsqrt(ms + EPS) * g_ref[...]
        u_ref[...] = u.astype(u_ref.dtype)
        os_ref[...] = jnp.dot(u, ws_ref[...], preferred_element_type=F32, precision=lax.Precision.HIGHEST)

    o_ref[...] = jnp.dot(u_ref[...], w_ref[...], preferred_element_type=F32).astype(o_ref.dtype)


def input_projection(x, g, w, w_small, tm, tn):
    T, K = x.shape
    N = w.shape[1]
    Ns = w_small.shape[1]
    return pl.pallas_call(
        _in_proj_kernel,
        out_shape=(jax.ShapeDtypeStruct((T, N), BF16), jax.ShapeDtypeStruct((T, Ns), F32)),
        grid=(T // tm, N // tn),
        in_specs=[pl.BlockSpec((tm, K), lambda i, j: (i, 0)),
                  pl.BlockSpec((1, K), lambda i, j: (0, 0)),
                  pl.BlockSpec((K, tn), lambda i, j: (0, j)),
                  pl.BlockSpec((K, Ns), lambda i, j: (0, 0))],
        out_specs=(pl.BlockSpec((tm, tn), lambda i, j: (i, j)), pl.BlockSpec((tm, Ns), lambda i, j: (i, 0))),
        scratch_shapes=[pltpu.VMEM((tm, K), BF16)],
        compiler_params=_cparams(("parallel", "arbitrary")),
        name="input_projection",
    )(x, g.reshape(1, K).astype(F32), w, w_small)


def _mla_q_kernel(x_ref, g_ref, w_ref, cos_ref, sin_ref, qn_ref, qr_ref):
    x = x_ref[...].astype(F32)
    ms = jnp.mean(x * x, axis=-1, keepdims=True)
    u = (x * lax.rsqrt(ms + EPS) * g_ref[...]).astype(BF16)
    y = jnp.dot(u, w_ref[...], preferred_element_type=F32)
    hw = MLA_HEADS * LANE
    qn_ref[...] = y[:, :hw].astype(qn_ref.dtype)
    cos = cos_ref[...]
    sin = sin_ref[...]
    for h in range(MLA_HEADS):
        a = y[:, hw + h * LANE: hw + (h + 1) * LANE]
        b = y[:, 2 * hw + h * LANE: 2 * hw + (h + 1) * LANE]
        qr_ref[:, h * LANE:(h + 1) * LANE] = (a * cos + b * sin).astype(qr_ref.dtype)


def mla_q_proj(z, g, w, cos, sin, S, tm):
    T = z.shape[0]
    K, N = w.shape
    hw = MLA_HEADS * LANE
    nsb = S // tm
    return pl.pallas_call(
        _mla_q_kernel,
        out_shape=(jax.ShapeDtypeStruct((T, hw), BF16), jax.ShapeDtypeStruct((T, hw), BF16)),
        grid=(T // tm,),
        in_specs=[pl.BlockSpec((tm, K), lambda i: (i, CB_CQ * LANE // MLA_Q_RANK)),
                  pl.BlockSpec((1, K), lambda i: (0, 0)),
                  pl.BlockSpec((K, N), lambda i: (0, 0)),
                  pl.BlockSpec((tm, LANE), lambda i: (i % nsb, 0)),
                  pl.BlockSpec((tm, LANE), lambda i: (i % nsb, 0))],
        out_specs=(pl.BlockSpec((tm, hw), lambda i: (i, 0)), pl.BlockSpec((tm, hw), lambda i: (i, 0))),
        compiler_params=_cparams(("parallel",)),
        name="mla_q_proj",
    )(z, g.reshape(1, K).astype(F32), w, cos, sin)


def _causal_attn_kernel(*refs, t, hg, two_part, decay):
    refs = list(refs)
    q_ref, k_ref, v_ref = refs[:3]
    pos = 3
    if two_part:
        q2_ref, k2_ref = refs[pos:pos + 2]
        pos += 2
    if decay:
        ck_ref = refs[pos]
        pos += 1
    o_ref = refs[pos]
    if two_part:
        kcat_ref = refs[pos + 1]
    qi = pl.program_id(2)
    dn = (((1,), (1,)), ((), ()))

    if two_part:
        @pl.when(qi == 0)
        def _():
            for j in range(hg):
                kcat_ref[j, :, :LANE] = k_ref[0, :, j * LANE:(j + 1) * LANE]
                kcat_ref[j, :, LANE:] = k2_ref[0]

    qs = []
    for j in range(hg):
        qj = q_ref[0, :, j * LANE:(j + 1) * LANE]
        if two_part:
            qj = jnp.concatenate([qj, q2_ref[0, :, j * LANE:(j + 1) * LANE]], axis=1)
        qs.append(qj)

    def step(kb, carry, masked, width=1):
        off = pl.multiple_of(kb * t, t)
        tk = width * t
        heads = range(hg)
        ss = []
        for j in heads:
            k = kcat_ref[j, pl.ds(off, tk), :] if two_part else k_ref[0, pl.ds(off, tk), j * LANE:(j + 1) * LANE]
            ss.append(lax.dot_general(k, qs[j], dn, preferred_element_type=F32))
        if decay:
            ss = [ss[j] - ck_ref[0, 0, pl.ds(off, tk), j:j + 1] for j in heads]
        if masked:
            r = lax.broadcasted_iota(jnp.int32, (tk, t), 0)
            c = lax.broadcasted_iota(jnp.int32, (tk, t), 1)
            ss = [jnp.where(r <= c, s, NEG_INF) for s in ss]
        ms = [jnp.maximum(carry[j][0], jnp.max(ss[j], axis=0, keepdims=True)) for j in heads]
        ps = [jnp.exp2(ss[j] - ms[j]) for j in heads]
        out = []
        for j in heads:
            m, l, acc = carry[j]
            a = jnp.exp2(m - ms[j])
            l = a * l + jnp.sum(ps[j], axis=0, keepdims=True)
            v = v_ref[0, pl.ds(off, tk), j * LANE:(j + 1) * LANE]
            acc = a * acc + lax.dot_general(v, ps[j].astype(BF16), (((0,), (0,)), ((), ())),
                                            preferred_element_type=F32)
            out.append((ms[j], l, acc))
        return tuple(out)

    init = tuple((jnp.full((1, t), NEG_INF, F32), jnp.zeros((1, t), F32), jnp.zeros((LANE, t), F32))
                 for _ in range(hg))
    carry = lax.fori_loop(0, qi // 2, lambda kp, c: step(2 * kp, c, False, width=2), init)
    carry = lax.cond(qi % 2 == 1, lambda c: step(qi - 1, c, False), lambda c: c, carry)
    carry = step(qi, carry, True)
    for j in range(hg):
        _, l, acc = carry[j]
        o_ref[0, :, j * LANE:(j + 1) * LANE] = (acc / l).T.astype(o_ref.dtype)


def causal_attention(q, qcb, k, kcb, v, vcb, heads, q2=None, k2=None, cum=None):
    B, S, _ = q.shape
    t, hg = ATT_T, ATT_HG
    w = hg * LANE
    two_part, decay = q2 is not None, cum is not None
    in_specs = [pl.BlockSpec((1, t, w), lambda b, h, i: (b, i, qcb // hg + h)),
                pl.BlockSpec((1, S, w), lambda b, h, i: (b, 0, kcb // hg + h)),
                pl.BlockSpec((1, S, w), lambda b, h, i: (b, 0, vcb // hg + h))]
    args = [q, k, v]
    scratch = []
    if two_part:
        in_specs += [pl.BlockSpec((1, t, w), lambda b, h, i: (b, i, h)),
                     pl.BlockSpec((1, S, LANE), lambda b, h, i: (b, 0, 0))]
        args += [q2, k2]
        scratch = [pltpu.VMEM((hg, S, 2 * LANE), BF16)]
    if decay:
        in_specs += [pl.BlockSpec((1, 1, S, hg), lambda b, h, i: (b, h, 0, 0))]
        args += [jnp.transpose(cum.reshape(B, S, heads // hg, hg), (0, 2, 1, 3))]
    return pl.pallas_call(
        functools.partial(_causal_attn_kernel, t=t, hg=hg, two_part=two_part, decay=decay),
        out_shape=jax.ShapeDtypeStruct((B, S, heads * LANE), BF16),
        grid=(B, heads // hg, S // t),
        in_specs=in_specs,
        out_specs=pl.BlockSpec((1, t, w), lambda b, h, i: (b, i, h)),
        scratch_shapes=scratch,
        compiler_params=_cparams(("parallel", "parallel", "arbitrary")),
        name="causal_attention",
    )(*args)


def _gelu_tanh(x):
    return 0.5 * x * (1.0 + jnp.tanh(math.sqrt(2.0 / math.pi) * (x + 0.044715 * (x * x * x))))


def _compress_kernel(x_ref, w1a_ref, w1b_ref, pe_ref, w2_ref, o_ref):
    x = x_ref[0]
    a = jnp.dot(x, w1a_ref[...], preferred_element_type=F32)
    b = jnp.dot(x, w1b_ref[...], preferred_element_type=F32)
    nc = a.shape[0]
    b_next = pltpu.roll(b, nc - 1, 0)
    pe_term = jnp.dot(pe_ref[0:1, :], w1a_ref[...], preferred_element_type=F32) + \
        jnp.dot(pe_ref[1:2, :], w1b_ref[...], preferred_element_type=F32)
    hid = _gelu_tanh(a + b_next + pe_term)
    o_ref[0] = jnp.dot(hid.astype(BF16), w2_ref[...], preferred_element_type=F32).astype(o_ref.dtype)


def compress(x, w1a, w1b, pe2, w2):
    BG, NC, KD = x.shape
    dp = w2.shape[1]
    return pl.pallas_call(
        _compress_kernel,
        out_shape=jax.ShapeDtypeStruct((BG, NC, dp), BF16),
        grid=(BG,),
        in_specs=[pl.BlockSpec((1, NC, KD), lambda i: (i, 0, 0)),
                  pl.BlockSpec((KD, dp), lambda i: (0, 0)),
                  pl.BlockSpec((KD, dp), lambda i: (0, 0)),
                  pl.BlockSpec((2, KD), lambda i: (0, 0)),
                  pl.BlockSpec((dp, dp), lambda i: (0, 0))],
        out_specs=pl.BlockSpec((1, NC, dp), lambda i: (i, 0, 0)),
        compiler_params=_cparams(("parallel",)),
        name="nsa_compress",
    )(x, w1a, w1b, pe2, w2)


def _nsa_cmp_kernel(q_ref, kc_ref, vc_ref, bias_ref, ov_ref, gate_ref, o_ref, ind_ref, *, t, n_sel):
    qi = pl.program_id(2)
    ncp = kc_ref.shape[1]
    kc = kc_ref[0]
    vc = vc_ref[0]
    row = lax.broadcasted_iota(jnp.int32, (t, ncp), 0) + qi * t
    col = lax.broadcasted_iota(jnp.int32, (t, ncp), 1)
    valid = row >= CMP_STRIDE * col + (CMP_BLOCK - 1)
    dn = (((1,), (1,)), ((), ()))
    heads = range(NSA_HPG)
    ss = [lax.dot_general(q_ref[0, :, h * NSA_DKP:(h + 1) * NSA_DKP], kc, dn, preferred_element_type=F32)
          for h in heads]
    ss = [jnp.where(valid, ss[h] + bias_ref[h], NEG_INF) for h in heads]
    es = [jnp.exp2(ss[h] - jnp.max(ss[h], axis=-1, keepdims=True)) for h in heads]
    ps = [jnp.where(valid, es[h] / jnp.sum(es[h], axis=-1, keepdims=True), 0.0) for h in heads]
    gates = _sigmoid(gate_ref[0, 0])
    for h in heads:
        o = jnp.dot(ps[h].astype(BF16), vc, preferred_element_type=F32)
        o_ref[0, :, h * NSA_DV:(h + 1) * NSA_DV] = (gates[:, 3 * h:3 * h + 1] * o).astype(o_ref.dtype)
    psum = functools.reduce(lambda x, y: x + y, ps)
    p_hi = psum.astype(BF16)
    p_lo = (psum - p_hi.astype(F32)).astype(BF16)
    imp = jnp.dot(p_hi, ov_ref[...], preferred_element_type=F32) + jnp.dot(p_lo, ov_ref[...], preferred_element_type=F32)
    nbp = imp.shape[1]
    blk = lax.broadcasted_iota(jnp.int32, (t, nbp), 1)
    cur = (lax.broadcasted_iota(jnp.int32, (t, nbp), 0) + qi * t) // SEL_BLOCK
    forced = (blk == 0) | (blk == cur) | (blk == cur - 1)
    score = jnp.where(blk <= cur, imp + jnp.where(forced, SEL_FORCE, 0.0), NEG_INF)
    sel = jnp.zeros((t, nbp), F32)
    for _ in range(n_sel):
        mx = jnp.max(score, axis=-1, keepdims=True)
        first = jnp.min(jnp.where(score == mx, blk, nbp), axis=-1, keepdims=True)
        hit = blk == first
        sel = jnp.where(hit, 1.0, sel)
        score = jnp.where(hit, -jnp.inf, score)
    ind_ref[0, 0] = sel.astype(ind_ref.dtype)


def nsa_cmp_select(z3, kc, vc, bias_c, overlap, gates, n_sel):
    B, S, _ = z3.shape
    G = NSA_GROUPS
    t = min(NSA_CMP_T, S)
    ncp = kc.shape[1]
    nbp = overlap.shape[1]
    qw = NSA_HPG * NSA_DKP
    ow = NSA_HPG * NSA_DV
    return pl.pallas_call(
        functools.partial(_nsa_cmp_kernel, t=t, n_sel=n_sel),
        out_shape=(jax.ShapeDtypeStruct((B, S, G * ow), BF16), jax.ShapeDtypeStruct((B, G, S, nbp), BF16)),
        grid=(B, G, S // t),
        in_specs=[pl.BlockSpec((1, t, qw), lambda b, g, i: (b, i, CB_NQ * LANE // qw + g)),
                  pl.BlockSpec((1, ncp, NSA_DKP), lambda b, g, i: (b, 0, g)),
                  pl.BlockSpec((1, ncp, NSA_DV), lambda b, g, i: (b, 0, g)),
                  pl.BlockSpec((NSA_HPG, t, ncp), lambda b, g, i: (g, i, 0)),
                  pl.BlockSpec((ncp, nbp), lambda b, g, i: (0, 0)),
                  pl.BlockSpec((1, 1, t, LANE), lambda b, g, i: (b, g, i, 0))],
        out_specs=(pl.BlockSpec((1, t, ow), lambda b, g, i: (b, i, g)),
                   pl.BlockSpec((1, 1, t, nbp), lambda b, g, i: (b, g, i, 0))),
        compiler_params=_cparams(("parallel", "parallel", "arbitrary")),
        name="nsa_cmp_select",
    )(z3, kc, vc, bias_c, overlap, gates)


def _nsa_sw_kernel(q_ref, ks_ref, vs_ref, kw_ref, vw_ref, ind_ref, e_ref, bias_ref, gate_ref, oc_ref, o_ref,
                   *, t):
    qi = pl.program_id(1)
    hp, G = NSA_HPG, NSA_GROUPS
    q4 = [jnp.concatenate([q_ref[0, :, (g * hp + h) * NSA_DKP:(g * hp + h + 1) * NSA_DKP] for h in range(hp)], axis=0)
          for g in range(G)]
    inds = [ind_ref[0, g] for g in range(G)]
    dn = (((1,), (1,)), ((), ()))

    def step(kb, carry, k_ref, v_ref, selected, width=1):
        off = pl.multiple_of(kb * t, t)
        tk = width * t
        ri = lax.broadcasted_iota(jnp.int32, (tk, t), 0)
        ci = lax.broadcasted_iota(jnp.int32, (tk, t), 1)
        d = (qi - kb) * t + ci - ri
        near = (d >= 0) if selected else (d >= 0) & (d < WINDOW)
        groups = range(G)
        ss = [lax.dot_general(k_ref[0, pl.ds(off, tk), g * NSA_DKP:(g + 1) * NSA_DKP], q4[g], dn,
                              preferred_element_type=F32) for g in groups]
        negs = []
        for g in groups:
            mask = near
            if selected:
                hit = jnp.dot(e_ref[pl.ds(off, tk), :], inds[g], preferred_element_type=F32)
                mask = near & (hit > 0.5)
            neg = jnp.where(mask, 0.0, NEG_INF)
            negs.append(jnp.concatenate([neg] * hp, axis=1))
        bias = [jnp.concatenate([bias_ref[g, jnp.minimum(qi - kb - w, 2)] for w in range(width)], axis=0)
                if width > 1 else bias_ref[g, jnp.minimum(qi - kb, 2)] for g in groups]
        ss = [ss[g] + bias[g] + negs[g] for g in groups]
        ms = [jnp.maximum(carry[g][0], jnp.max(ss[g], axis=0, keepdims=True)) for g in groups]
        ps = [jnp.exp2(ss[g] - ms[g]) for g in groups]
        out = []
        for g in groups:
            m, l, acc = carry[g]
            a = jnp.exp2(m - ms[g])
            l = a * l + jnp.sum(ps[g], axis=0, keepdims=True)
            v = v_ref[0, pl.ds(off, tk), g * NSA_DV:(g + 1) * NSA_DV]
            acc = a * acc + lax.dot_general(v, ps[g].astype(BF16), (((0,), (0,)), ((), ())),
                                            preferred_element_type=F32)
            out.append((ms[g], l, acc))
        return tuple(out)

    init = tuple((jnp.full((1, hp * t), NEG_INF, F32), jnp.zeros((1, hp * t), F32), jnp.zeros((NSA_DV, hp * t), F32))
                 for _ in range(G))
    sel = lax.fori_loop(0, (qi + 1) // 2, lambda kp, c: step(2 * kp, c, ks_ref, vs_ref, True, width=2), init)
    sel = lax.cond(qi % 2 == 0, lambda c: step(qi, c, ks_ref, vs_ref, True), lambda c: c, sel)
    lo = jnp.maximum(qi - WINDOW // t, 0)
    win = lax.fori_loop(lo, qi + 1, lambda kb, c: step(kb, c, kw_ref, vw_ref, False), init)
    for g in range(G):
        o_s = sel[g][2] / sel[g][1]
        o_w = win[g][2] / win[g][1]
        gates = _sigmoid(gate_ref[0, g])
        for h in range(hp):
            r = slice(h * t, (h + 1) * t)
            c = slice((g * hp + h) * NSA_DV, (g * hp + h + 1) * NSA_DV)
            o = gates[3 * h + 1:3 * h + 2, :] * o_s[:, r] + gates[3 * h + 2:3 * h + 3, :] * o_w[:, r]
            o_ref[0, :, c] = (oc_ref[0, :, c].astype(F32) + o.T).astype(o_ref.dtype)


def nsa_sel_win(z3, ind, expand, bias_sw, gates, o_cmp):
    B, S, _ = z3.shape
    G = NSA_GROUPS
    t = NSA_T
    qw = NSA_HEADS * NSA_DKP
    ow = NSA_HEADS * NSA_DV
    kw = G * NSA_DKP
    vw = G * NSA_DV
    nbp = ind.shape[2]
    kspec = lambda cb: pl.BlockSpec((1, S, kw), lambda b, i: (b, 0, cb * LANE // kw))
    vspec = lambda cb: pl.BlockSpec((1, S, vw), lambda b, i: (b, 0, cb * LANE // vw))
    return pl.pallas_call(
        functools.partial(_nsa_sw_kernel, t=t),
        out_shape=jax.ShapeDtypeStruct((B, S, ow), BF16),
        grid=(B, S // t),
        in_specs=[pl.BlockSpec((1, t, qw), lambda b, i: (b, i, CB_NQ * LANE // qw)),
                  kspec(CB_NKS), vspec(CB_NVS), kspec(CB_NKW), vspec(CB_NVW),
                  pl.BlockSpec((1, G, nbp, t), lambda b, i: (b, 0, 0, i)),
                  pl.BlockSpec((S, nbp), lambda b, i: (0, 0)),
                  pl.BlockSpec((G, 3, t, NSA_HPG * t), lambda b, i: (0, 0, 0, 0)),
                  pl.BlockSpec((1, G, gates.shape[2], t), lambda b, i: (b, 0, 0, i)),
                  pl.BlockSpec((1, t, ow), lambda b, i: (b, i, 0))],
        out_specs=pl.BlockSpec((1, t, ow), lambda b, i: (b, i, 0)),
        compiler_params=_cparams(("parallel", "arbitrary")),
        name="nsa_sel_win",
    )(z3, z3, z3, z3, z3, ind, expand, bias_sw, gates, o_cmp)


def _merge_kernel(of_ref, om_ref, on_ref, wb_ref, g0_ref, g1_ref, g2_ref, o_ref):
    acc = None
    for n, (o_r, g_r) in enumerate(((of_ref, g0_ref), (om_ref, g1_ref), (on_ref, g2_ref))):
        y = jnp.dot(o_r[...], wb_ref[n], preferred_element_type=F32)
        y = _sigmoid(g_r[...].astype(F32)) * y
        acc = y if acc is None else acc + y
    o_ref[...] = acc.astype(o_ref.dtype)


def merge_branches(o_fox, o_mla, o_nsa, wb, z, tm, tn):
    T = o_fox.shape[0]
    D = wb.shape[2]
    gspec = lambda n: pl.BlockSpec((tm, tn), lambda i, j: (i, (CB_MG * LANE + n * D) // tn + j))
    ospec = pl.BlockSpec((tm, BRANCH_W), lambda i, j: (i, 0))
    return pl.pallas_call(
        _merge_kernel,
        out_shape=jax.ShapeDtypeStruct((T, D), BF16),
        grid=(T // tm, D // tn),
        in_specs=[ospec, ospec, ospec,
                  pl.BlockSpec((N_BRANCH, BRANCH_W, tn), lambda i, j: (0, 0, j)),
                  gspec(0), gspec(1), gspec(2)],
        out_specs=pl.BlockSpec((tm, tn), lambda i, j: (i, j)),
        compiler_params=_cparams(("parallel", "arbitrary")),
        name="merge_branches",
    )(o_fox, o_mla, o_nsa, wb, z, z, z)


def _mm_res_kernel(a_ref, w_ref, r_ref, o_ref):
    o_ref[...] = r_ref[...] + jnp.dot(a_ref[...], w_ref[...], preferred_element_type=F32)


def matmul_residual(a, w, res, tm, tn):
    T, K = a.shape
    N = w.shape[1]
    return pl.pallas_call(
        _mm_res_kernel,
        out_shape=jax.ShapeDtypeStruct((T, N), F32),
        grid=(T // tm, N // tn),
        in_specs=[pl.BlockSpec((tm, K), lambda i, j: (i, 0)),
                  pl.BlockSpec((K, tn), lambda i, j: (0, j)),
                  pl.BlockSpec((tm, tn), lambda i, j: (i, j))],
        out_specs=pl.BlockSpec((tm, tn), lambda i, j: (i, j)),
        compiler_params=_cparams(("parallel", "arbitrary")),
        name="matmul_residual",
    )(a, w, res)


def _mem_attn_kernel(h_ref, g_ref, wq_ref, kv_ref, wo_ref, o_ref):
    x = h_ref[0]
    ms = jnp.mean(x * x, axis=-1, keepdims=True)
    u = (x * lax.rsqrt(ms + EPS) * g_ref[...]).astype(BF16)
    q = jnp.dot(u, wq_ref[...], preferred_element_type=F32).astype(BF16)
    dn = (((1,), (1,)), ((), ()))
    hw = MEM_HEADS * MEM_DH
    heads = range(MEM_HEADS)
    cs = [slice(h * MEM_DH, (h + 1) * MEM_DH) for h in heads]
    ss = [lax.dot_general(q[:, cs[h]], kv_ref[0, :, cs[h]], dn, preferred_element_type=F32) for h in heads]
    es = [jnp.exp2(ss[h] - jnp.max(ss[h], axis=-1, keepdims=True)) for h in heads]
    ps = [es[h] / jnp.sum(es[h], axis=-1, keepdims=True) for h in heads]
    outs = [jnp.dot(ps[h].astype(BF16), kv_ref[0, :, hw + h * MEM_DH: hw + (h + 1) * MEM_DH],
                    preferred_element_type=F32).astype(BF16) for h in heads]
    o = jnp.concatenate(outs, axis=1)
    o_ref[0] = x + jnp.dot(o, wo_ref[...], preferred_element_type=F32)


def memory_attention_block(h3, g, wq, kv, wo, tq):
    B, S, D = h3.shape
    M = kv.shape[1]
    hw = MEM_HEADS * MEM_DH
    return pl.pallas_call(
        _mem_attn_kernel,
        out_shape=jax.ShapeDtypeStruct((B, S, D), F32),
        grid=(B, S // tq),
        in_specs=[pl.BlockSpec((1, tq, D), lambda b, i: (b, i, 0)),
                  pl.BlockSpec((1, D), lambda b, i: (0, 0)),
                  pl.BlockSpec((D, hw), lambda b, i: (0, 0)),
                  pl.BlockSpec((1, M, 2 * hw), lambda b, i: (b, 0, 0)),
                  pl.BlockSpec((hw, D), lambda b, i: (0, 0))],
        out_specs=pl.BlockSpec((1, tq, D), lambda b, i: (b, i, 0)),
        compiler_params=_cparams(("parallel", "arbitrary")),
        name="memory_attention",
    )(h3, g.reshape(1, D).astype(F32), wq, kv, wo)


def _router_kernel(h_ref, g_ref, w_ref, b_ref, lg_ref, u_ref):
    x = h_ref[...]
    ms = jnp.mean(x * x, axis=-1, keepdims=True)
    u = x * lax.rsqrt(ms + EPS) * g_ref[...]
    u_ref[...] = u.astype(u_ref.dtype)
    lg_ref[...] = jnp.dot(u, w_ref[...], preferred_element_type=F32, precision=lax.Precision.HIGHEST) + b_ref[...]


def moe_router(h, g, w, b, tm):
    T, D = h.shape
    N = w.shape[1]
    return pl.pallas_call(
        _router_kernel,
        out_shape=(jax.ShapeDtypeStruct((T, N), F32), jax.ShapeDtypeStruct((T, D), BF16)),
        grid=(T // tm,),
        in_specs=[pl.BlockSpec((tm, D), lambda i: (i, 0)),
                  pl.BlockSpec((1, D), lambda i: (0, 0)),
                  pl.BlockSpec((D, N), lambda i: (0, 0)),
                  pl.BlockSpec((1, N), lambda i: (0, 0))],
        out_specs=(pl.BlockSpec((tm, N), lambda i: (i, 0)), pl.BlockSpec((tm, D), lambda i: (i, 0))),
        compiler_params=_cparams(("parallel",)),
        name="moe_router",
    )(h, g.reshape(1, D).astype(F32), w, b)


def _moe_kernel(be_ref, nu_ref, x_ref, wg_ref, wu_ref, wd_ref, o_ref, acc_ref):
    i = pl.program_id(0)
    j = pl.program_id(1)
    last = pl.num_programs(1) - 1
    used = i < nu_ref[0]

    @pl.when(used)
    def _():
        x = x_ref[...]
        a = jnp.dot(x, wg_ref[0, 0].astype(BF16), preferred_element_type=F32)
        b = jnp.dot(x, wu_ref[0, 0].astype(BF16), preferred_element_type=F32)
        hdn = (a * _sigmoid(a) * b).astype(BF16)
        y = jnp.dot(hdn, wd_ref[0, 0].astype(BF16), preferred_element_type=F32)

        @pl.when(j == 0)
        def _():
            acc_ref[...] = y

        @pl.when(j > 0)
        def _():
            acc_ref[...] += y

        @pl.when(j == last)
        def _():
            o_ref[...] = acc_ref[...].astype(o_ref.dtype)

    @pl.when(jnp.logical_not(used) & (j == last))
    def _():
        o_ref[...] = jnp.zeros(o_ref.shape, o_ref.dtype)


def moe_experts(blk_e, n_used, xr, wg, wu, wd, layer):
    P, D = xr.shape
    De = wg.shape[3]
    tm, dc = MOE_TM, MOE_DC
    nj = De // dc
    jj = lambda i, j, nu: jnp.where(i < nu[0], j, nj - 1)
    grid_spec = pltpu.PrefetchScalarGridSpec(
        num_scalar_prefetch=2,
        grid=(P // tm, nj),
        in_specs=[pl.BlockSpec((tm, D), lambda i, j, be, nu: (i, 0)),
                  pl.BlockSpec((1, 1, D, dc), lambda i, j, be, nu: (layer, be[i], 0, jj(i, j, nu))),
                  pl.BlockSpec((1, 1, D, dc), lambda i, j, be, nu: (layer, be[i], 0, jj(i, j, nu))),
                  pl.BlockSpec((1, 1, dc, D), lambda i, j, be, nu: (layer, be[i], jj(i, j, nu), 0))],
        out_specs=pl.BlockSpec((tm, D), lambda i, j, be, nu: (i, 0)),
        scratch_shapes=[pltpu.VMEM((tm, D), F32)],
    )
    return pl.pallas_call(
        _moe_kernel,
        out_shape=jax.ShapeDtypeStruct((P, D), BF16),
        grid_spec=grid_spec,
        compiler_params=_cparams(("arbitrary", "arbitrary")),
        name="moe_experts",
    )(blk_e, n_used, xr, wg, wu, wd)


def _rmsnorm_kernel(x_ref, g_ref, o_ref):
    x = x_ref[...]
    ms = jnp.mean(x * x, axis=-1, keepdims=True)
    o_ref[...] = x * lax.rsqrt(ms + EPS) * g_ref[...]


def rmsnorm_rows(x, g, tm):
    T, D = x.shape
    return pl.pallas_call(
        _rmsnorm_kernel,
        out_shape=jax.ShapeDtypeStruct((T, D), F32),
        grid=(T // tm,),
        in_specs=[pl.BlockSpec((tm, D), lambda i: (i, 0)), pl.BlockSpec((1, D), lambda i: (0, 0))],
        out_specs=pl.BlockSpec((tm, D), lambda i: (i, 0)),
        compiler_params=_cparams(("parallel",)),
        name="final_rmsnorm",
    )(x, g.reshape(1, D).astype(F32))


def _pad_cols(w, width):
    return jnp.pad(w, ((0, 0), (0, width - w.shape[1])))


def _w_in_segments(D):
    names = ("fq", "fk", "fv", "ff", "mcq", "mckv", "mkr", "nq", "nkc", "nvc", "nks", "nvs", "nkw", "nvw", "ngt", "mg")
    widths = (1024, 1024, 1024, FORGET_COLS, MLA_Q_RANK, MLA_KV_RANK, MLA_ROPE, NSA_HEADS * NSA_DK,
              NSA_GROUPS * NSA_DK, NSA_GROUPS * NSA_DV, NSA_GROUPS * NSA_DK, NSA_GROUPS * NSA_DV,
              NSA_GROUPS * NSA_DK, NSA_GROUPS * NSA_DV, NSA_GATE_COLS, N_BRANCH * D)
    src = dict(zip(names, np.cumsum((0,) + widths[:-1]).tolist()))
    wid = dict(zip(names, widths))
    segs = []
    plain = lambda name, cb, f=1.0: segs.append((cb * LANE, src[name], wid[name], f))

    def padded_k(name, cb, n, f=1.0):
        for i in range(n):
            segs.append((cb * LANE + i * NSA_DKP, src[name] + i * NSA_DK, NSA_DK, f))

    half = MLA_ROPE // 2
    plain("mckv", CB_CKV)
    plain("mkr", CB_KR)
    segs.append((CB_KR * LANE + MLA_ROPE, src["mkr"] + half, half, -1.0))
    segs.append((CB_KR * LANE + MLA_ROPE + half, src["mkr"], half, 1.0))
    plain("mcq", CB_CQ)
    padded_k("nkc", CB_NKC, NSA_GROUPS)
    padded_k("nq", CB_NQ, NSA_HEADS, NSA_DK ** -0.5 * LOG2E)
    plain("fq", CB_FQ, FOX_DH ** -0.5 * LOG2E)
    plain("fk", CB_FK)
    plain("fv", CB_FV)
    padded_k("nks", CB_NKS, NSA_GROUPS)
    padded_k("nkw", CB_NKW, NSA_GROUPS)
    plain("nvc", CB_NVC)
    plain("nvs", CB_NVS)
    plain("nvw", CB_NVW)
    plain("mg", CB_MG)
    return segs, src


def _pack_w_in_kernel(w_ref, o_ref, *, segs):
    x = w_ref[0]
    o_ref[...] = jnp.zeros(o_ref.shape, o_ref.dtype)
    for dst, s, n, f in segs:
        v = x[:, s:s + n]
        if f != 1.0:
            v = v * f
        o_ref[:, dst:dst + n] = v.astype(o_ref.dtype)


def _pack_w_in(w_all, layer):
    _, D, d_in = w_all.shape
    segs, src = _w_in_segments(D)
    tr = 128
    wz = pl.pallas_call(
        functools.partial(_pack_w_in_kernel, segs=tuple(segs)),
        out_shape=jax.ShapeDtypeStruct((D, Z_BLOCKS * LANE), BF16),
        grid=(D // tr,),
        in_specs=[pl.BlockSpec((1, tr, d_in), lambda i: (layer, i, 0))],
        out_specs=pl.BlockSpec((tr, Z_BLOCKS * LANE), lambda i: (i, 0)),
        compiler_params=_cparams(("parallel",)),
        name="pack_w_in",
    )(w_all)
    w = w_all[layer]
    w_small = _pad_cols(jnp.concatenate([w[:, src["ff"]:src["ff"] + FORGET_COLS],
                                         w[:, src["ngt"]:src["ngt"] + NSA_GATE_COLS]], axis=1), LANE)
    return wz, w_small


def _pack_w_uq(w):
    K = w.shape[0]
    w3 = w.reshape(K, MLA_HEADS, MLA_NOPE + MLA_ROPE) * ((MLA_NOPE + MLA_ROPE) ** -0.5 * LOG2E)
    nope = w3[:, :, :MLA_NOPE].reshape(K, MLA_HEADS * MLA_NOPE)
    r = w3[:, :, MLA_NOPE:]
    half = MLA_ROPE // 2
    r_rot = jnp.concatenate([-r[:, :, half:], r[:, :, :half]], axis=2)
    padr = lambda a: jnp.pad(a, ((0, 0), (0, 0), (0, LANE - MLA_ROPE))).reshape(K, MLA_HEADS * LANE)
    return jnp.concatenate([nope, padr(r), padr(r_rot)], axis=1).astype(BF16)


def _pack_w_ukv(w):
    K = w.shape[0]
    w3 = w.reshape(K, MLA_HEADS, MLA_NOPE + MLA_DV)
    return jnp.concatenate([w3[:, :, :MLA_NOPE].reshape(K, -1), w3[:, :, MLA_NOPE:].reshape(K, -1)], axis=1).astype(BF16)


def _t5_bucket(dist):
    dist = jnp.maximum(dist, 0)
    exact = REL_BUCKETS // 2
    df = jnp.maximum(dist, 1).astype(F32)
    large = exact + (jnp.log(df / exact) / math.log(REL_MAX_DIST / exact) * (REL_BUCKETS - exact)).astype(jnp.int32)
    large = jnp.minimum(large, REL_BUCKETS - 1)
    return jnp.where(dist < exact, dist, large)


def _position_tables(S, rel_bias):
    t = NSA_T
    half = MLA_ROPE // 2
    inv = ROPE_THETA ** (-jnp.arange(half, dtype=F32) / half)
    ang = jnp.arange(S, dtype=F32)[:, None] * inv
    c, s = jnp.cos(ang), jnp.sin(ang)
    cos = _pad_cols(jnp.concatenate([c, c], axis=1), LANE)
    sin = _pad_cols(jnp.concatenate([s, s], axis=1), LANE)
    ncp = max(S // CMP_STRIDE, LANE)
    pos = jnp.arange(S)

    def bias_of(dist):
        onehot = jax.nn.one_hot(_t5_bucket(dist), REL_BUCKETS, dtype=F32)
        return jnp.einsum("...b,bh->h...", onehot, rel_bias, precision=lax.Precision.HIGHEST)

    bias_c = bias_of(pos[:, None] - (CMP_STRIDE * jnp.arange(ncp)[None, :] + CMP_BLOCK - 1))
    i = jnp.arange(t)
    bias_sw = jnp.stack([bias_of(k * t + i[:, None] - i[None, :]) for k in range(3)], axis=1)
    bias_sw = bias_sw.reshape(NSA_GROUPS, NSA_HPG, 3, t, t).transpose(0, 2, 4, 1, 3).reshape(NSA_GROUPS, 3, t, NSA_HPG * t)
    n_cmp = (S - CMP_BLOCK) // CMP_STRIDE + 1
    n_blk = S // SEL_BLOCK
    nbp = max(n_blk, LANE)
    cstart = CMP_STRIDE * jnp.arange(ncp)
    sstart = SEL_BLOCK * jnp.arange(nbp)
    ov = jnp.clip(jnp.minimum(cstart[:, None] + CMP_BLOCK, sstart[None, :] + SEL_BLOCK)
                  - jnp.maximum(cstart[:, None], sstart[None, :]), 0, None).astype(F32) / CMP_STRIDE
    ov = jnp.where((jnp.arange(ncp)[:, None] < n_cmp) & (jnp.arange(nbp)[None, :] < n_blk), ov, 0.0).astype(BF16)
    expand = ((pos[:, None] // SEL_BLOCK) == jnp.arange(nbp)[None, :]).astype(BF16)
    return cos, sin, bias_c, bias_sw, ov, expand


def _token_mixers(h, p, w_in_all, layer, tabs, B, S):
    T, D = h.shape
    cos, sin, bias_c, bias_sw, overlap, expand = tabs
    wz, w_small = _pack_w_in(w_in_all, layer)
    z, zs = input_projection(h, p["g_mix"], wz, w_small, tm=min(T, 1024), tn=1024)
    z3 = z.reshape(B, S, Z_BLOCKS * LANE)

    log_f = jax.nn.log_sigmoid(zs[:, :FORGET_COLS] + p["b_forget"].astype(F32)).reshape(B, S, FOX_HEADS)
    cum = jnp.cumsum(log_f, axis=1) * LOG2E
    o_fox = causal_attention(z3, CB_FQ, z3, CB_FK, z3, CB_FV, FOX_HEADS, cum=cum)

    q_nope, q_rope = mla_q_proj(z, p["g_cq"], _pack_w_uq(p["w_uq"]), cos, sin, S, tm=min(S, 512))
    kv = rms_matmul(z, CB_CKV * LANE // MLA_KV_RANK, p["g_ckv"], _pack_w_ukv(p["w_ukv"]), BF16, tm=min(T, 1024), tn=1024)
    kr = z3[:, :, CB_KR * LANE:(CB_KR + 1) * LANE].astype(F32)
    kr = kr[..., :MLA_ROPE] * cos[None, :, :MLA_ROPE] + kr[..., MLA_ROPE:] * sin[None, :, :MLA_ROPE]
    k_rope = jnp.pad(kr, ((0, 0), (0, 0), (0, LANE - MLA_ROPE))).astype(BF16)
    hw = MLA_HEADS * LANE
    kv3 = kv.reshape(B, S, 2 * hw)
    o_mla = causal_attention(q_nope.reshape(B, S, hw), 0, kv3, 0, kv3, MLA_HEADS, MLA_HEADS,
                             q2=q_rope.reshape(B, S, hw), k2=k_rope)

    G = NSA_GROUPS
    NC = S // CMP_STRIDE
    ncp = bias_c.shape[2]

    def compress_branch(cb, dp, d, pe, w1, w2):
        x = z3[:, :, cb * LANE: cb * LANE + G * dp].reshape(B, NC, CMP_STRIDE * G * dp)
        eye = jnp.eye(G, dtype=F32)
        w1p = jnp.pad(w1.reshape(CMP_BLOCK, d, d), ((0, 0), (0, dp - d), (0, dp - d)))
        w1g = jnp.einsum("lij,gh->lgihj", w1p, eye).reshape(CMP_BLOCK, G * dp, G * dp).astype(BF16)
        w1a = w1g[:CMP_STRIDE].reshape(CMP_STRIDE * G * dp, G * dp)
        w1b = w1g[CMP_STRIDE:].reshape(CMP_STRIDE * G * dp, G * dp)
        pe_g = jnp.tile(jnp.pad(pe, ((0, 0), (0, dp - d)))[:, None, :], (1, G, 1))
        pe2 = pe_g.reshape(2, CMP_STRIDE * G * dp).astype(BF16)
        w2p = jnp.pad(w2, ((0, dp - d), (0, dp - d)))
        w2g = jnp.einsum("ij,gh->gihj", w2p, eye).reshape(G * dp, G * dp).astype(BF16)
        out = compress(x, w1a, w1b, pe2, w2g)
        return jnp.pad(out, ((0, 0), (0, ncp - NC), (0, 0)))

    kc = compress_branch(CB_NKC, NSA_DKP, NSA_DK, p["pe_k"], p["w_cmp_k1"], p["w_cmp_k2"])
    vc = compress_branch(CB_NVC, NSA_DV, NSA_DV, p["pe_v"], p["w_cmp_v1"], p["w_cmp_v2"])
    gl = zs[:, FORGET_COLS:FORGET_COLS + NSA_GATE_COLS].reshape(B, S, G, NSA_HPG * 3)
    gl = jnp.transpose(gl, (0, 2, 1, 3))
    gl_rows = jnp.pad(gl, ((0, 0), (0, 0), (0, 0), (0, LANE - NSA_HPG * 3)))
    gl_cols = jnp.pad(jnp.swapaxes(gl, 2, 3), ((0, 0), (0, 0), (0, 16 - NSA_HPG * 3), (0, 0)))
    o_cmp, ind = nsa_cmp_select(z3, kc, vc, bias_c, overlap, gl_rows, min(N_SEL, S // SEL_BLOCK))
    o_nsa = nsa_sel_win(z3, jnp.swapaxes(ind, 2, 3), expand, bias_sw, gl_cols, o_cmp)

    merged = merge_branches(o_fox.reshape(T, -1), o_mla.reshape(T, -1), o_nsa.reshape(T, -1),
                            p["w_branch"].astype(BF16), z, tm=min(T, 1024), tn=512)
    return matmul_residual(merged, p["w_out"].astype(BF16), h, tm=min(T, 1024), tn=512)


def _memory_block(h, mem2, p, B, S):
    T, D = h.shape
    kv = rms_matmul(mem2, 0, p["g_mem_kv"], p["w_mem_kv"].astype(BF16), BF16, tm=min(mem2.shape[0], 512), tn=512)
    out = memory_attention_block(h.reshape(B, S, D), p["g_mem_q"], (p["w_mem_q"] * (MEM_DH ** -0.5 * LOG2E)).astype(BF16),
                                 kv.reshape(B, -1, kv.shape[1]), p["w_mem_o"].astype(BF16), tq=min(S, 512))
    return out.reshape(T, D)


def _moe_block(h, p, experts, layer):
    T, D = h.shape
    tm = MOE_TM
    w_r = _pad_cols(jnp.concatenate([p["w_router_group"], p["w_router_expert"]], axis=1), LANE).astype(F32)
    b_r = _pad_cols(jnp.concatenate([p["b_router_group"], p["b_router_expert"]])[None, :], LANE).astype(F32)
    logits, u = moe_router(h, p["g_moe"], w_r, b_r, tm=min(T, 512))
    glog = logits[:, :N_GROUPS]
    gsel = jnp.argmax(glog, axis=-1).astype(jnp.int32)
    pg = jnp.max(jax.nn.softmax(glog, axis=-1), axis=-1, keepdims=True)
    elog = logits[:, N_GROUPS:N_GROUPS + N_EXPERTS].reshape(T, N_GROUPS, EXPERTS_PER_GROUP)
    elog = jnp.take_along_axis(elog, gsel[:, None, None], axis=1)[:, 0]
    eprob = jax.nn.softmax(elog, axis=-1)
    j0 = jnp.argmax(eprob, axis=-1).astype(jnp.int32)
    lane = jnp.arange(EXPERTS_PER_GROUP, dtype=jnp.int32)[None, :]
    j1 = jnp.argmax(jnp.where(lane == j0[:, None], -jnp.inf, eprob), axis=-1).astype(jnp.int32)
    top_j = jnp.stack([j0, j1], axis=-1)
    top_p = jnp.take_along_axis(eprob, top_j, axis=-1)
    top_p = top_p / jnp.sum(top_p, axis=-1, keepdims=True)
    weight = pg * top_p
    flat_e = (gsel[:, None] * EXPERTS_PER_GROUP + top_j.astype(jnp.int32)).reshape(-1)
    TK = T * TOP_K
    onehot = (flat_e[:, None] == jnp.arange(N_EXPERTS, dtype=jnp.int32)[None, :]).astype(jnp.int32)
    rank = jnp.sum((jnp.cumsum(onehot, axis=0) - onehot) * onehot, axis=1)
    counts = jnp.sum(onehot, axis=0)
    pcounts = ((counts + tm - 1) // tm) * tm
    pends = jnp.cumsum(pcounts)
    dest = (pends - pcounts)[flat_e] + rank
    P = TK + N_EXPERTS * tm
    n_rb = P // tm
    row_tok = (jnp.arange(P, dtype=jnp.int32) % T).at[dest].set(jnp.repeat(jnp.arange(T, dtype=jnp.int32), TOP_K))
    blk_e = jnp.sum((pends[None, :] <= (jnp.arange(n_rb, dtype=jnp.int32) * tm)[:, None]).astype(jnp.int32), axis=1)
    blk_e = jnp.minimum(blk_e, N_EXPERTS - 1).astype(jnp.int32)
    n_used = (pends[-1] // tm).astype(jnp.int32).reshape(1)
    xr = u[row_tok]
    y = moe_experts(blk_e, n_used, xr, experts[0].astype(F32), experts[1].astype(F32), experts[2].astype(F32), layer)
    d2 = dest.reshape(T, TOP_K)
    return h + (weight[:, 0:1] * y[d2[:, 0]].astype(F32) + weight[:, 1:2] * y[d2[:, 1]].astype(F32))


_LAYER_KEYS = ("g_mix", "w_in", "b_forget", "g_cq", "g_ckv", "w_uq", "w_ukv", "pe_k", "pe_v", "w_cmp_k1", "w_cmp_k2",
               "w_cmp_v1", "w_cmp_v2", "w_branch", "w_out", "g_mem_q", "g_mem_kv", "w_mem_q", "w_mem_kv", "w_mem_o",
               "g_moe", "w_router_group", "b_router_group", "w_router_expert", "b_router_expert")


def kernel(x, mem, g_mix, w_in, b_forget, g_cq, g_ckv, w_uq, w_ukv, pe_k, pe_v, w_cmp_k1, w_cmp_k2, w_cmp_v1, w_cmp_v2, rel_bias, w_branch, w_out, g_mem_q, g_mem_kv, w_mem_q, w_mem_kv, w_mem_o, g_moe, w_router_group, b_router_group, w_router_expert, b_router_expert, w_exp_gate, w_exp_up, w_exp_down, g_final):
    B, S, D = x.shape
    T = B * S
    stacked = dict(g_mix=g_mix, w_in=w_in, b_forget=b_forget, g_cq=g_cq, g_ckv=g_ckv, w_uq=w_uq, w_ukv=w_ukv,
                   pe_k=pe_k, pe_v=pe_v, w_cmp_k1=w_cmp_k1, w_cmp_k2=w_cmp_k2, w_cmp_v1=w_cmp_v1, w_cmp_v2=w_cmp_v2,
                   w_branch=w_branch, w_out=w_out, g_mem_q=g_mem_q, g_mem_kv=g_mem_kv, w_mem_q=w_mem_q,
                   w_mem_kv=w_mem_kv, w_mem_o=w_mem_o, g_moe=g_moe, w_router_group=w_router_group,
                   b_router_group=b_router_group, w_router_expert=w_router_expert, b_router_expert=b_router_expert,
                   w_exp_gate=w_exp_gate, w_exp_up=w_exp_up, w_exp_down=w_exp_down)
    tabs = _position_tables(S, rel_bias.astype(F32) * LOG2E)
    h = x.reshape(T, D).astype(F32)
    mem2 = mem.reshape(-1, D).astype(F32)
    for l in range(w_in.shape[0]):
        p = {k: stacked[k][l] for k in _LAYER_KEYS}
        h = _token_mixers(h, p, w_in.astype(F32), l, tabs, B, S)
        h = _memory_block(h, mem2, p, B, S)
        h = _moe_block(h, p, (w_exp_gate, w_exp_up, w_exp_down), l)
    return rmsnorm_rows(h, g_final, tm=min(T, 512)).reshape(B, S, D)
```

```python
import functools
import math

import jax
import jax.numpy as jnp
import numpy as np
from jax import lax
from jax.experimental import pallas as pl
from jax.experimental.pallas import tpu as pltpu

F32 = jnp.float32
BF16 = jnp.bfloat16

EPS = 1e-6
NEG_INF = -1e30
LOG2E = math.log2(math.e)
LANE = 128

FOX_HEADS, FOX_DH = 8, 128
MLA_HEADS, MLA_NOPE, MLA_ROPE, MLA_DV = 8, 128, 64, 128
MLA_Q_RANK, MLA_KV_RANK = 768, 512
ROPE_THETA = 10000.0
NSA_HEADS, NSA_GROUPS, NSA_DK, NSA_DV = 8, 2, 192, 128
NSA_HPG = NSA_HEADS // NSA_GROUPS
NSA_DKP = 256
CMP_BLOCK, CMP_STRIDE, SEL_BLOCK, N_SEL, WINDOW = 32, 16, 64, 8, 512
SEL_FORCE = 1e6
REL_BUCKETS, REL_MAX_DIST = 32, 128
N_BRANCH, BRANCH_W = 3, 1024
MEM_HEADS, MEM_DH = 4, 128
N_GROUPS, EXPERTS_PER_GROUP, TOP_K = 4, 8, 2
N_EXPERTS = N_GROUPS * EXPERTS_PER_GROUP
FORGET_COLS, NSA_GATE_COLS = FOX_HEADS, NSA_HEADS * 3

CB_CKV, CB_KR, CB_CQ = 0, 4, 6
CB_NKC, CB_NQ = 12, 16
CB_FQ, CB_FK, CB_FV = 32, 40, 48
CB_NKS, CB_NKW = 56, 60
CB_NVC, CB_NVS, CB_NVW = 64, 66, 68
CB_MG = 72
Z_BLOCKS = 120

ATT_T = 256
ATT_HG = 8
NSA_T = 256
NSA_CMP_T = 512
MOE_TM = 512
MOE_DC = 512
VMEM_LIMIT = 56 * 1024 * 1024


def _cparams(sem):
    return pltpu.CompilerParams(dimension_semantics=sem, vmem_limit_bytes=VMEM_LIMIT)


def _sigmoid(x):
    return 1.0 / (1.0 + jnp.exp(-x))


def _rms_mm_kernel(x_ref, g_ref, w_ref, o_ref, u_ref):
    @pl.when(pl.program_id(1) == 0)
    def _():
        x = x_ref[...].astype(F32)
        ms = jnp.mean(x * x, axis=-1, keepdims=True)
        u_ref[...] = (x * lax.rsqrt(ms + EPS) * g_ref[...]).astype(u_ref.dtype)

    o_ref[...] = jnp.dot(u_ref[...], w_ref[...], preferred_element_type=F32).astype(o_ref.dtype)


def rms_matmul(x, xcol, g, w, out_dtype, tm, tn):
    T = x.shape[0]
    K, N = w.shape
    return pl.pallas_call(
        _rms_mm_kernel,
        out_shape=jax.ShapeDtypeStruct((T, N), out_dtype),
        grid=(T // tm, N // tn),
        in_specs=[pl.BlockSpec((tm, K), lambda i, j: (i, xcol)),
                  pl.BlockSpec((1, K), lambda i, j: (0, 0)),
                  pl.BlockSpec((K, tn), lambda i, j: (0, j))],
        out_specs=pl.BlockSpec((tm, tn), lambda i, j: (i, j)),
        scratch_shapes=[pltpu.VMEM((tm, K), BF16)],
        compiler_params=_cparams(("parallel", "arbitrary")),
        name="rms_matmul",
    )(x, g.reshape(1, K).astype(F32), w)


def _in_proj_kernel(x_ref, g_ref, w_ref, ws_ref, o_ref, os_ref, u_ref):
    @pl.when(pl.program_id(1) == 0)
    def _():
        x = x_ref[...]
        ms = jnp.mean(x * x, axis=-1, keepdims=True)
        u = x * lax.rsqrt(ms + EPS) * g_ref[...]
        u_ref[...] = u.astype(u_ref.dtype)
        os_ref[...] = jnp.dot(u, ws_ref[...], preferred_element_type=F32, precision=lax.Precision.HIGHEST)

    o_ref[...] = jnp.dot(u_ref[...], w_ref[...], preferred_element_type=F32).astype(o_ref.dtype)


def input_projection(x, g, w, w_small, tm, tn):
    T, K = x.shape
    N = w.shape[1]
    Ns = w_small.shape[1]
    return pl.pallas_call(
        _in_proj_kernel,
        out_shape=(jax.ShapeDtypeStruct((T, N), BF16), jax.ShapeDtypeStruct((T, Ns), F32)),
        grid=(T // tm, N // tn),
        in_specs=[pl.BlockSpec((tm, K), lambda i, j: (i, 0)),
                  pl.BlockSpec((1, K), lambda i, j: (0, 0)),
                  pl.BlockSpec((K, tn), lambda i, j: (0, j)),
                  pl.BlockSpec((K, Ns), lambda i, j: (0, 0))],
        out_specs=(pl.BlockSpec((tm, tn), lambda i, j: (i, j)), pl.BlockSpec((tm, Ns), lambda i, j: (i, 0))),
        scratch_shapes=[pltpu.VMEM((tm, K), BF16)],
        compiler_params=_cparams(("parallel", "arbitrary")),
        name="input_projection",
    )(x, g.reshape(1, K).astype(F32), w, w_small)


def _mla_q_kernel(x_ref, g_ref, w_ref, cos_ref, sin_ref, qn_ref, qr_ref):
    x = x_ref[...].astype(F32)
    ms = jnp.mean(x * x, axis=-1, keepdims=True)
    u = (x * lax.rsqrt(ms + EPS) * g_ref[...]).astype(BF16)
    y = jnp.dot(u, w_ref[...], preferred_element_type=F32)
    hw = MLA_HEADS * LANE
    qn_ref[...] = y[:, :hw].astype(qn_ref.dtype)
    cos = cos_ref[...]
    sin = sin_ref[...]
    for h in range(MLA_HEADS):
        a = y[:, hw + h * LANE: hw + (h + 1) * LANE]
        b = y[:, 2 * hw + h * LANE: 2 * hw + (h + 1) * LANE]
        qr_ref[:, h * LANE:(h + 1) * LANE] = (a * cos + b * sin).astype(qr_ref.dtype)


def mla_q_proj(z, g, w, cos, sin, S, tm):
    T = z.shape[0]
    K, N = w.shape
    hw = MLA_HEADS * LANE
    nsb = S // tm
    return pl.pallas_call(
        _mla_q_kernel,
        out_shape=(jax.ShapeDtypeStruct((T, hw), BF16), jax.ShapeDtypeStruct((T, hw), BF16)),
        grid=(T // tm,),
        in_specs=[pl.BlockSpec((tm, K), lambda i: (i, CB_CQ * LANE // MLA_Q_RANK)),
                  pl.BlockSpec((1, K), lambda i: (0, 0)),
                  pl.BlockSpec((K, N), lambda i: (0, 0)),
                  pl.BlockSpec((tm, LANE), lambda i: (i % nsb, 0)),
                  pl.BlockSpec((tm, LANE), lambda i: (i % nsb, 0))],
        out_specs=(pl.BlockSpec((tm, hw), lambda i: (i, 0)), pl.BlockSpec((tm, hw), lambda i: (i, 0))),
        compiler_params=_cparams(("parallel",)),
        name="mla_q_proj",
    )(z, g.reshape(1, K).astype(F32), w, cos, sin)


def _causal_attn_kernel(*refs, t, hg, two_part, decay):
    refs = list(refs)
    q_ref, k_ref, v_ref = refs[:3]
    pos = 3
    if two_part:
        q2_ref, k2_ref = refs[pos:pos + 2]
        pos += 2
    if decay:
        ck_ref = refs[pos]
        pos += 1
    o_ref = refs[pos]
    if two_part:
        kcat_ref = refs[pos + 1]
    qi = pl.program_id(2)
    dn = (((1,), (1,)), ((), ()))

    if two_part:
        @pl.when(qi == 0)
        def _():
            for j in range(hg):
                kcat_ref[j, :, :LANE] = k_ref[0, :, j * LANE:(j + 1) * LANE]
                kcat_ref[j, :, LANE:] = k2_ref[0]

    qs = []
    for j in range(hg):
        qj = q_ref[0, :, j * LANE:(j + 1) * LANE]
        if two_part:
            qj = jnp.concatenate([qj, q2_ref[0, :, j * LANE:(j + 1) * LANE]], axis=1)
        qs.append(qj)

    def step(kb, carry, masked, width=1):
        off = pl.multiple_of(kb * t, t)
        tk = width * t
        heads = range(hg)
        ss = []
        for j in heads:
            k = kcat_ref[j, pl.ds(off, tk), :] if two_part else k_ref[0, pl.ds(off, tk), j * LANE:(j + 1) * LANE]
            ss.append(lax.dot_general(k, qs[j], dn, preferred_element_type=F32))
        if decay:
            ss = [ss[j] - ck_ref[0, 0, pl.ds(off, tk), j:j + 1] for j in heads]
        if masked:
            r = lax.broadcasted_iota(jnp.int32, (tk, t), 0)
            c = lax.broadcasted_iota(jnp.int32, (tk, t), 1)
            ss = [jnp.where(r <= c, s, NEG_INF) for s in ss]
        ms = [jnp.maximum(carry[j][0], jnp.max(ss[j], axis=0, keepdims=True)) for j in heads]
        ps = [jnp.exp2(ss[j] - ms[j]) for j in heads]
        out = []
        for j in heads:
            m, l, acc = carry[j]
            a = jnp.exp2(m - ms[j])
            l = a * l + jnp.sum(ps[j], axis=0, keepdims=True)
            v = v_ref[0, pl.ds(off, tk), j * LANE:(j + 1) * LANE]
            acc = a * acc + lax.dot_general(v, ps[j].astype(BF16), (((0,), (0,)), ((), ())),
                                            preferred_element_type=F32)
            out.append((ms[j], l, acc))
        return tuple(out)

    init = tuple((jnp.full((1, t), NEG_INF, F32), jnp.zeros((1, t), F32), jnp.zeros((LANE, t), F32))
                 for _ in range(hg))
    carry = lax.fori_loop(0, qi // 2, lambda kp, c: step(2 * kp, c, False, width=2), init)
    carry = lax.cond(qi % 2 == 1, lambda c: step(qi - 1, c, False), lambda c: c, carry)
    carry = step(qi, carry, True)
    for j in range(hg):
        _, l, acc = carry[j]
        o_ref[0, :, j * LANE:(j + 1) * LANE] = (acc / l).T.astype(o_ref.dtype)


def causal_attention(q, qcb, k, kcb, v, vcb, heads, q2=None, k2=None, cum=None):
    B, S, _ = q.shape
    t, hg = ATT_T, ATT_HG
    w = hg * LANE
    two_part, decay = q2 is not None, cum is not None
    in_specs = [pl.BlockSpec((1, t, w), lambda b, h, i: (b, i, qcb // hg + h)),
                pl.BlockSpec((1, S, w), lambda b, h, i: (b, 0, kcb // hg + h)),
                pl.BlockSpec((1, S, w), lambda b, h, i: (b, 0, vcb // hg + h))]
    args = [q, k, v]
    scratch = []
    if two_part:
        in_specs += [pl.BlockSpec((1, t, w), lambda b, h, i: (b, i, h)),
                     pl.BlockSpec((1, S, LANE), lambda b, h, i: (b, 0, 0))]
        args += [q2, k2]
        scratch = [pltpu.VMEM((hg, S, 2 * LANE), BF16)]
    if decay:
        in_specs += [pl.BlockSpec((1, 1, S, hg), lambda b, h, i: (b, h, 0, 0))]
        args += [jnp.transpose(cum.reshape(B, S, heads // hg, hg), (0, 2, 1, 3))]
    return pl.pallas_call(
        functools.partial(_causal_attn_kernel, t=t, hg=hg, two_part=two_part, decay=decay),
        out_shape=jax.ShapeDtypeStruct((B, S, heads * LANE), BF16),
        grid=(B, heads // hg, S // t),
        in_specs=in_specs,
        out_specs=pl.BlockSpec((1, t, w), lambda b, h, i: (b, i, h)),
        scratch_shapes=scratch,
        compiler_params=_cparams(("parallel", "parallel", "arbitrary")),
        name="causal_attention",
    )(*args)


def _gelu_tanh(x):
    return 0.5 * x * (1.0 + jnp.tanh(math.sqrt(2.0 / math.pi) * (x + 0.044715 * (x * x * x))))


def _compress_kernel(x_ref, w1a_ref, w1b_ref, pe_ref, w2_ref, o_ref):
    x = x_ref[0]
    a = jnp.dot(x, w1a_ref[...], preferred_element_type=F32)
    b = jnp.dot(x, w1b_ref[...], preferred_element_type=F32)
    nc = a.shape[0]
    b_next = pltpu.roll(b, nc - 1, 0)
    pe_term = jnp.dot(pe_ref[0:1, :], w1a_ref[...], preferred_element_type=F32) + \
        jnp.dot(pe_ref[1:2, :], w1b_ref[...], preferred_element_type=F32)
    hid = _gelu_tanh(a + b_next + pe_term)
    o_ref[0] = jnp.dot(hid.astype(BF16), w2_ref[...], preferred_element_type=F32).astype(o_ref.dtype)


def compress(x, w1a, w1b, pe2, w2):
    BG, NC, KD = x.shape
    dp = w2.shape[1]
    return pl.pallas_call(
        _compress_kernel,
        out_shape=jax.ShapeDtypeStruct((BG, NC, dp), BF16),
        grid=(BG,),
        in_specs=[pl.BlockSpec((1, NC, KD), lambda i: (i, 0, 0)),
                  pl.BlockSpec((KD, dp), lambda i: (0, 0)),
                  pl.BlockSpec((KD, dp), lambda i: (0, 0)),
                  pl.BlockSpec((2, KD), lambda i: (0, 0)),
                  pl.BlockSpec((dp, dp), lambda i: (0, 0))],
        out_specs=pl.BlockSpec((1, NC, dp), lambda i: (i, 0, 0)),
        compiler_params=_cparams(("parallel",)),
        name="nsa_compress",
    )(x, w1a, w1b, pe2, w2)


def _nsa_cmp_kernel(q_ref, kc_ref, vc_ref, bias_ref, ov_ref, gate_ref, o_ref, ind_ref, *, t, n_sel):
    qi = pl.program_id(2)
    ncp = kc_ref.shape[1]
    kc = kc_ref[0]
    vc = vc_ref[0]
    row = lax.broadcasted_iota(jnp.int32, (t, ncp), 0) + qi * t
    col = lax.broadcasted_iota(jnp.int32, (t, ncp), 1)
    valid = row >= CMP_STRIDE * col + (CMP_BLOCK - 1)
    dn = (((1,), (1,)), ((), ()))
    heads = range(NSA_HPG)
    ss = [lax.dot_general(q_ref[0, :, h * NSA_DKP:(h + 1) * NSA_DKP], kc, dn, preferred_element_type=F32)
          for h in heads]
    ss = [jnp.where(valid, ss[h] + bias_ref[h], NEG_INF) for h in heads]
    es = [jnp.exp2(ss[h] - jnp.max(ss[h], axis=-1, keepdims=True)) for h in heads]
    ps = [jnp.where(valid, es[h] / jnp.sum(es[h], axis=-1, keepdims=True), 0.0) for h in heads]
    gates = _sigmoid(gate_ref[0, 0])
    for h in heads:
        o = jnp.dot(ps[h].astype(BF16), vc, preferred_element_type=F32)
        o_ref[0, :, h * NSA_DV:(h + 1) * NSA_DV] = (gates[:, 3 * h:3 * h + 1] * o).astype(o_ref.dtype)
    psum = functools.reduce(lambda x, y: x + y, ps)
    p_hi = psum.astype(BF16)
    p_lo = (psum - p_hi.astype(F32)).astype(BF16)
    imp = jnp.dot(p_hi, ov_ref[...], preferred_element_type=F32) + jnp.dot(p_lo, ov_ref[...], preferred_element_type=F32)
    nbp = imp.shape[1]
    blk = lax.broadcasted_iota(jnp.int32, (t, nbp), 1)
    cur = (lax.broadcasted_iota(jnp.int32, (t, nbp), 0) + qi * t) // SEL_BLOCK
    forced = (blk == 0) | (blk == cur) | (blk == cur - 1)
    score = jnp.where(blk <= cur, imp + jnp.where(forced, SEL_FORCE, 0.0), NEG_INF)
    sel = jnp.zeros((t, nbp), F32)
    for _ in range(n_sel):
        mx = jnp.max(score, axis=-1, keepdims=True)
        first = jnp.min(jnp.where(score == mx, blk, nbp), axis=-1, keepdims=True)
        hit = blk == first
        sel = jnp.where(hit, 1.0, sel)
        score = jnp.where(hit, -jnp.inf, score)
    ind_ref[0, 0] = sel.astype(ind_ref.dtype)


def nsa_cmp_select(z3, kc, vc, bias_c, overlap, gates, n_sel):
    B, S, _ = z3.shape
    G = NSA_GROUPS
    t = min(NSA_CMP_T, S)
    ncp = kc.shape[1]
    nbp = overlap.shape[1]
    qw = NSA_HPG * NSA_DKP
    ow = NSA_HPG * NSA_DV
    return pl.pallas_call(
        functools.partial(_nsa_cmp_kernel, t=t, n_sel=n_sel),
        out_shape=(jax.ShapeDtypeStruct((B, S, G * ow), BF16), jax.ShapeDtypeStruct((B, G, S, nbp), BF16)),
        grid=(B, G, S // t),
        in_specs=[pl.BlockSpec((1, t, qw), lambda b, g, i: (b, i, CB_NQ * LANE // qw + g)),
                  pl.BlockSpec((1, ncp, NSA_DKP), lambda b, g, i: (b, 0, g)),
                  pl.BlockSpec((1, ncp, NSA_DV), lambda b, g, i: (b, 0, g)),
                  pl.BlockSpec((NSA_HPG, t, ncp), lambda b, g, i: (g, i, 0)),
                  pl.BlockSpec((ncp, nbp), lambda b, g, i: (0, 0)),
                  pl.BlockSpec((1, 1, t, LANE), lambda b, g, i: (b, g, i, 0))],
        out_specs=(pl.BlockSpec((1, t, ow), lambda b, g, i: (b, i, g)),
                   pl.BlockSpec((1, 1, t, nbp), lambda b, g, i: (b, g, i, 0))),
        compiler_params=_cparams(("parallel", "parallel", "arbitrary")),
        name="nsa_cmp_select",
    )(z3, kc, vc, bias_c, overlap, gates)


def _nsa_sw_kernel(q_ref, ks_ref, vs_ref, kw_ref, vw_ref, ind_ref, e_ref, bias_ref, gate_ref, oc_ref, o_ref,
                   *, t):
    qi = pl.program_id(1)
    hp, G = NSA_HPG, NSA_GROUPS
    q4 = [jnp.concatenate([q_ref[0, :, (g * hp + h) * NSA_DKP:(g * hp + h + 1) * NSA_DKP] for h in range(hp)], axis=0)
          for g in range(G)]
    inds = [ind_ref[0, g] for g in range(G)]
    dn = (((1,), (1,)), ((), ()))

    def step(kb, carry, k_ref, v_ref, selected, width=1):
        off = pl.multiple_of(kb * t, t)
        tk = width * t
        ri = lax.broadcasted_iota(jnp.int32, (tk, t), 0)
        ci = lax.broadcasted_iota(jnp.int32, (tk, t), 1)
        d = (qi - kb) * t + ci - ri
        near = (d >= 0) if selected else (d >= 0) & (d < WINDOW)
        groups = range(G)
        ss = [lax.dot_general(k_ref[0, pl.ds(off, tk), g * NSA_DKP:(g + 1) * NSA_DKP], q4[g], dn,
                              preferred_element_type=F32) for g in groups]
        negs = []
        for g in groups:
            mask = near
            if selected:
                hit = jnp.dot(e_ref[pl.ds(off, tk), :], inds[g], preferred_element_type=F32)
                mask = near & (hit > 0.5)
            neg = jnp.where(mask, 0.0, NEG_INF)
            negs.append(jnp.concatenate([neg] * hp, axis=1))
        bias = [jnp.concatenate([bias_ref[g, jnp.minimum(qi - kb - w, 2)] for w in range(width)], axis=0)
                if width > 1 else bias_ref[g, jnp.minimum(qi - kb, 2)] for g in groups]
        ss = [ss[g] + bias[g] + negs[g] for g in groups]
        ms = [jnp.maximum(carry[g][0], jnp.max(ss[g], axis=0, keepdims=True)) for g in groups]
        ps = [jnp.exp2(ss[g] - ms[g]) for g in groups]
        out = []
        for g in groups:
            m, l, acc = carry[g]
            a = jnp.exp2(m - ms[g])
            l = a * l + jnp.sum(ps[g], axis=0, keepdims=True)
            v = v_ref[0, pl.ds(off, tk), g * NSA_DV:(g + 1) * NSA_DV]
            acc = a * acc + lax.dot_general(v, ps[g].astype(BF16), (((0,), (0,)), ((), ())),
                                            preferred_element_type=F32)
            out.append((ms[g], l, acc))
        return tuple(out)

    init = tuple((jnp.full((1, hp * t), NEG_INF, F32), jnp.zeros((1, hp * t), F32), jnp.zeros((NSA_DV, hp * t), F32))
                 for _ in range(G))
    sel = lax.fori_loop(0, (qi + 1) // 2, lambda kp, c: step(2 * kp, c, ks_ref, vs_ref, True, width=2), init)
    sel = lax.cond(qi % 2 == 0, lambda c: step(qi, c, ks_ref, vs_ref, True), lambda c: c, sel)
    lo = jnp.maximum(qi - WINDOW // t, 0)
    win = lax.fori_loop(lo, qi + 1, lambda kb, c: step(kb, c, kw_ref, vw_ref, False), init)
    for g in range(G):
        o_s = sel[g][2] / sel[g][1]
        o_w = win[g][2] / win[g][1]
        gates = _sigmoid(gate_ref[0, g])
        for h in range(hp):
            r = slice(h * t, (h + 1) * t)
            c = slice((g * hp + h) * NSA_DV, (g * hp + h + 1) * NSA_DV)
            o = gates[3 * h + 1:3 * h + 2, :] * o_s[:, r] + gates[3 * h + 2:3 * h + 3, :] * o_w[:, r]
            o_ref[0, :, c] = (oc_ref[0, :, c].astype(F32) + o.T).astype(o_ref.dtype)


def nsa_sel_win(z3, ind, expand, bias_sw, gates, o_cmp):
    B, S, _ = z3.shape
    G = NSA_GROUPS
    t = NSA_T
    qw = NSA_HEADS * NSA_DKP
    ow = NSA_HEADS * NSA_DV
    kw = G * NSA_DKP
    vw = G * NSA_DV
    nbp = ind.shape[2]
    kspec = lambda cb: pl.BlockSpec((1, S, kw), lambda b, i: (b, 0, cb * LANE // kw))
    vspec = lambda cb: pl.BlockSpec((1, S, vw), lambda b, i: (b, 0, cb * LANE // vw))
    return pl.pallas_call(
        functools.partial(_nsa_sw_kernel, t=t),
        out_shape=jax.ShapeDtypeStruct((B, S, ow), BF16),
        grid=(B, S // t),
        in_specs=[pl.BlockSpec((1, t, qw), lambda b, i: (b, i, CB_NQ * LANE // qw)),
                  kspec(CB_NKS), vspec(CB_NVS), kspec(CB_NKW), vspec(CB_NVW),
                  pl.BlockSpec((1, G, nbp, t), lambda b, i: (b, 0, 0, i)),
                  pl.BlockSpec((S, nbp), lambda b, i: (0, 0)),
                  pl.BlockSpec((G, 3, t, NSA_HPG * t), lambda b, i: (0, 0, 0, 0)),
                  pl.BlockSpec((1, G, gates.shape[2], t), lambda b, i: (b, 0, 0, i)),
                  pl.BlockSpec((1, t, ow), lambda b, i: (b, i, 0))],
        out_specs=pl.BlockSpec((1, t, ow), lambda b, i: (b, i, 0)),
        compiler_params=_cparams(("parallel", "arbitrary")),
        name="nsa_sel_win",
    )(z3, z3, z3, z3, z3, ind, expand, bias_sw, gates, o_cmp)


def _merge_kernel(of_ref, om_ref, on_ref, wb_ref, g0_ref, g1_ref, g2_ref, o_ref):
    acc = None
    for n, (o_r, g_r) in enumerate(((of_ref, g0_ref), (om_ref, g1_ref), (on_ref, g2_ref))):
        y = jnp.dot(o_r[...], wb_ref[n], preferred_element_type=F32)
        y = _sigmoid(g_r[...].astype(F32)) * y
        acc = y if acc is None else acc + y
    o_ref[...] = acc.astype(o_ref.dtype)


def merge_branches(o_fox, o_mla, o_nsa, wb, z, tm, tn):
    T = o_fox.shape[0]
    D = wb.shape[2]
    gspec = lambda n: pl.BlockSpec((tm, tn), lambda i, j: (i, (CB_MG * LANE + n * D) // tn + j))
    ospec = pl.BlockSpec((tm, BRANCH_W), lambda i, j: (i, 0))
    return pl.pallas_call(
        _merge_kernel,
        out_shape=jax.ShapeDtypeStruct((T, D), BF16),
        grid=(T // tm, D // tn),
        in_specs=[ospec, ospec, ospec,
                  pl.BlockSpec((N_BRANCH, BRANCH_W, tn), lambda i, j: (0, 0, j)),
                  gspec(0), gspec(1), gspec(2)],
        out_specs=pl.BlockSpec((tm, tn), lambda i, j: (i, j)),
        compiler_params=_cparams(("parallel", "arbitrary")),
        name="merge_branches",
    )(o_fox, o_mla, o_nsa, wb, z, z, z)


def _mm_res_kernel(a_ref, w_ref, r_ref, o_ref):
    o_ref[...] = r_ref[...] + jnp.dot(a_ref[...], w_ref[...], preferred_element_type=F32)


def matmul_residual(a, w, res, tm, tn):
    T, K = a.shape
    N = w.shape[1]
    return pl.pallas_call(
        _mm_res_kernel,
        out_shape=jax.ShapeDtypeStruct((T, N), F32),
        grid=(T // tm, N // tn),
        in_specs=[pl.BlockSpec((tm, K), lambda i, j: (i, 0)),
                  pl.BlockSpec((K, tn), lambda i, j: (0, j)),
                  pl.BlockSpec((tm, tn), lambda i, j: (i, j))],
        out_specs=pl.BlockSpec((tm, tn), lambda i, j: (i, j)),
        compiler_params=_cparams(("parallel", "arbitrary")),
        name="matmul_residual",
    )(a, w, res)


def _mem_attn_kernel(h_ref, g_ref, wq_ref, kv_ref, wo_ref, o_ref):
    x = h_ref[0]
    ms = jnp.mean(x * x, axis=-1, keepdims=True)
    u = (x * lax.rsqrt(ms + EPS) * g_ref[...]).astype(BF16)
    q = jnp.dot(u, wq_ref[...], preferred_element_type=F32).astype(BF16)
    dn = (((1,), (1,)), ((), ()))
    hw = MEM_HEADS * MEM_DH
    heads = range(MEM_HEADS)
    cs = [slice(h * MEM_DH, (h + 1) * MEM_DH) for h in heads]
    ss = [lax.dot_general(q[:, cs[h]], kv_ref[0, :, cs[h]], dn, preferred_element_type=F32) for h in heads]
    es = [jnp.exp2(ss[h] - jnp.max(ss[h], axis=-1, keepdims=True)) for h in heads]
    ps = [es[h] / jnp.sum(es[h], axis=-1, keepdims=True) for h in heads]
    outs = [jnp.dot(ps[h].astype(BF16), kv_ref[0, :, hw + h * MEM_DH: hw + (h + 1) * MEM_DH],
                    preferred_element_type=F32).astype(BF16) for h in heads]
    o = jnp.concatenate(outs, axis=1)
    o_ref[0] = x + jnp.dot(o, wo_ref[...], preferred_element_type=F32)


def memory_attention_block(h3, g, wq, kv, wo, tq):
    B, S, D = h3.shape
    M = kv.shape[1]
    hw = MEM_HEADS * MEM_DH
    return pl.pallas_call(
        _mem_attn_kernel,
        out_shape=jax.ShapeDtypeStruct((B, S, D), F32),
        grid=(B, S // tq),
        in_specs=[pl.BlockSpec((1, tq, D), lambda b, i: (b, i, 0)),
                  pl.BlockSpec((1, D), lambda b, i: (0, 0)),
                  pl.BlockSpec((D, hw), lambda b, i: (0, 0)),
                  pl.BlockSpec((1, M, 2 * hw), lambda b, i: (b, 0, 0)),
                  pl.BlockSpec((hw, D), lambda b, i: (0, 0))],
        out_specs=pl.BlockSpec((1, tq, D), lambda b, i: (b, i, 0)),
        compiler_params=_cparams(("parallel", "arbitrary")),
        name="memory_attention",
    )(h3, g.reshape(1, D).astype(F32), wq, kv, wo)


def _router_kernel(h_ref, g_ref, w_ref, b_ref, lg_ref, u_ref):
    x = h_ref[...]
    ms = jnp.mean(x * x, axis=-1, keepdims=True)
    u = x * lax.rsqrt(ms + EPS) * g_ref[...]
    u_ref[...] = u.astype(u_ref.dtype)
    lg_ref[...] = jnp.dot(u, w_ref[...], preferred_element_type=F32, precision=lax.Precision.HIGHEST) + b_ref[...]


def moe_router(h, g, w, b, tm):
    T, D = h.shape
    N = w.shape[1]
    return pl.pallas_call(
        _router_kernel,
        out_shape=(jax.ShapeDtypeStruct((T, N), F32), jax.ShapeDtypeStruct((T, D), BF16)),
        grid=(T // tm,),
        in_specs=[pl.BlockSpec((tm, D), lambda i: (i, 0)),
                  pl.BlockSpec((1, D), lambda i: (0, 0)),
                  pl.BlockSpec((D, N), lambda i: (0, 0)),
                  pl.BlockSpec((1, N), lambda i: (0, 0))],
        out_specs=(pl.BlockSpec((tm, N), lambda i: (i, 0)), pl.BlockSpec((tm, D), lambda i: (i, 0))),
        compiler_params=_cparams(("parallel",)),
        name="moe_router",
    )(h, g.reshape(1, D).astype(F32), w, b)


def _moe_kernel(be_ref, nu_ref, x_ref, wg_ref, wu_ref, wd_ref, o_ref, acc_ref):
    i = pl.program_id(0)
    j = pl.program_id(1)
    last = pl.num_programs(1) - 1
    used = i < nu_ref[0]

    @pl.when(used)
    def _():
        x = x_ref[...]
        a = jnp.dot(x, wg_ref[0, 0].astype(BF16), preferred_element_type=F32)
        b = jnp.dot(x, wu_ref[0, 0].astype(BF16), preferred_element_type=F32)
        hdn = (a * _sigmoid(a) * b).astype(BF16)
        y = jnp.dot(hdn, wd_ref[0, 0].astype(BF16), preferred_element_type=F32)

        @pl.when(j == 0)
        def _():
            acc_ref[...] = y

        @pl.when(j > 0)
        def _():
            acc_ref[...] += y

        @pl.when(j == last)
        def _():
            o_ref[...] = acc_ref[...].astype(o_ref.dtype)

    @pl.when(jnp.logical_not(used) & (j == last))
    def _():
        o_ref[...] = jnp.zeros(o_ref.shape, o_ref.dtype)


def moe_experts(blk_e, n_used, xr, wg, wu, wd, layer):
    P, D = xr.shape
    De = wg.shape[3]
    tm, dc = MOE_TM, MOE_DC
    nj = De // dc
    chunk = lambda i, s: jnp.where(i % 2 == 0, s, nj - 1 - s)
    jj = lambda i, s, nu: jnp.where(i < nu[0], chunk(i, s), chunk(nu[0] - 1, nj - 1))
    grid_spec = pltpu.PrefetchScalarGridSpec(
        num_scalar_prefetch=2,
        grid=(P // tm, nj),
        in_specs=[pl.BlockSpec((tm, D), lambda i, j, be, nu: (i, 0)),
                  pl.BlockSpec((1, 1, D, dc), lambda i, j, be, nu: (layer, be[i], 0, jj(i, j, nu))),
                  pl.BlockSpec((1, 1, D, dc), lambda i, j, be, nu: (layer, be[i], 0, jj(i, j, nu))),
                  pl.BlockSpec((1, 1, dc, D), lambda i, j, be, nu: (layer, be[i], jj(i, j, nu), 0))],
        out_specs=pl.BlockSpec((tm, D), lambda i, j, be, nu: (i, 0)),
        scratch_shapes=[pltpu.VMEM((tm, D), F32)],
    )
    return pl.pallas_call(
        _moe_kernel,
        out_shape=jax.ShapeDtypeStruct((P, D), BF16),
        grid_spec=grid_spec,
        compiler_params=_cparams(("arbitrary", "arbitrary")),
        name="moe_experts",
    )(blk_e, n_used, xr, wg, wu, wd)


def _rmsnorm_kernel(x_ref, g_ref, o_ref):
    x = x_ref[...]
    ms = jnp.mean(x * x, axis=-1, keepdims=True)
    o_ref[...] = x * lax.rsqrt(ms + EPS) * g_ref[...]


def rmsnorm_rows(x, g, tm):
    T, D = x.shape
    return pl.pallas_call(
        _rmsnorm_kernel,
        out_shape=jax.ShapeDtypeStruct((T, D), F32),
        grid=(T // tm,),
        in_specs=[pl.BlockSpec((tm, D), lambda i: (i, 0)), pl.BlockSpec((1, D), lambda i: (0, 0))],
        out_specs=pl.BlockSpec((tm, D), lambda i: (i, 0)),
        compiler_params=_cparams(("parallel",)),
        name="final_rmsnorm",
    )(x, g.reshape(1, D).astype(F32))


def _pad_cols(w, width):
    return jnp.pad(w, ((0, 0), (0, width - w.shape[1])))


def _w_in_segments(D):
    names = ("fq", "fk", "fv", "ff", "mcq", "mckv", "mkr", "nq", "nkc", "nvc", "nks", "nvs", "nkw", "nvw", "ngt", "mg")
    widths = (1024, 1024, 1024, FORGET_COLS, MLA_Q_RANK, MLA_KV_RANK, MLA_ROPE, NSA_HEADS * NSA_DK,
              NSA_GROUPS * NSA_DK, NSA_GROUPS * NSA_DV, NSA_GROUPS * NSA_DK, NSA_GROUPS * NSA_DV,
              NSA_GROUPS * NSA_DK, NSA_GROUPS * NSA_DV, NSA_GATE_COLS, N_BRANCH * D)
    src = dict(zip(names, np.cumsum((0,) + widths[:-1]).tolist()))
    wid = dict(zip(names, widths))
    segs = []
    plain = lambda name, cb, f=1.0: segs.append((cb * LANE, src[name], wid[name], f))

    def padded_k(name, cb, n, f=1.0):
        for i in range(n):
            segs.append((cb * LANE + i * NSA_DKP, src[name] + i * NSA_DK, NSA_DK, f))

    half = MLA_ROPE // 2
    plain("mckv", CB_CKV)
    plain("mkr", CB_KR)
    segs.append((CB_KR * LANE + MLA_ROPE, src["mkr"] + half, half, -1.0))
    segs.append((CB_KR * LANE + MLA_ROPE + half, src["mkr"], half, 1.0))
    plain("mcq", CB_CQ)
    padded_k("nkc", CB_NKC, NSA_GROUPS)
    padded_k("nq", CB_NQ, NSA_HEADS, NSA_DK ** -0.5 * LOG2E)
    plain("fq", CB_FQ, FOX_DH ** -0.5 * LOG2E)
    plain("fk", CB_FK)
    plain("fv", CB_FV)
    padded_k("nks", CB_NKS, NSA_GROUPS)
    padded_k("nkw", CB_NKW, NSA_GROUPS)
    plain("nvc", CB_NVC)
    plain("nvs", CB_NVS)
    plain("nvw", CB_NVW)
    plain("mg", CB_MG)
    return segs, src


def _pack_w_in_kernel(w_ref, o_ref, os_ref, *, segs, small_segs):
    x = w_ref[0]
    o_ref[...] = jnp.zeros(o_ref.shape, o_ref.dtype)
    for dst, s, n, f in segs:
        v = x[:, s:s + n]
        if f != 1.0:
            v = v * f
        o_ref[:, dst:dst + n] = v.astype(o_ref.dtype)
    os_ref[...] = jnp.zeros(os_ref.shape, os_ref.dtype)
    for dst, s, n in small_segs:
        os_ref[:, dst:dst + n] = x[:, s:s + n]


def _pack_w_in(w_all, layer):
    _, D, d_in = w_all.shape
    segs, src = _w_in_segments(D)
    small_segs = ((0, src["ff"], FORGET_COLS), (FORGET_COLS, src["ngt"], NSA_GATE_COLS))
    tr = 128
    return pl.pallas_call(
        functools.partial(_pack_w_in_kernel, segs=tuple(segs), small_segs=small_segs),
        out_shape=(jax.ShapeDtypeStruct((D, Z_BLOCKS * LANE), BF16), jax.ShapeDtypeStruct((D, LANE), F32)),
        grid=(D // tr,),
        in_specs=[pl.BlockSpec((1, tr, d_in), lambda i: (layer, i, 0))],
        out_specs=(pl.BlockSpec((tr, Z_BLOCKS * LANE), lambda i: (i, 0)), pl.BlockSpec((tr, LANE), lambda i: (i, 0))),
        compiler_params=_cparams(("parallel",)),
        name="pack_w_in",
    )(w_all)


def _pack_w_uq(w):
    K = w.shape[0]
    w3 = w.reshape(K, MLA_HEADS, MLA_NOPE + MLA_ROPE) * ((MLA_NOPE + MLA_ROPE) ** -0.5 * LOG2E)
    nope = w3[:, :, :MLA_NOPE].reshape(K, MLA_HEADS * MLA_NOPE)
    r = w3[:, :, MLA_NOPE:]
    half = MLA_ROPE // 2
    r_rot = jnp.concatenate([-r[:, :, half:], r[:, :, :half]], axis=2)
    padr = lambda a: jnp.pad(a, ((0, 0), (0, 0), (0, LANE - MLA_ROPE))).reshape(K, MLA_HEADS * LANE)
    return jnp.concatenate([nope, padr(r), padr(r_rot)], axis=1).astype(BF16)


def _pack_w_ukv(w):
    K = w.shape[0]
    w3 = w.reshape(K, MLA_HEADS, MLA_NOPE + MLA_DV)
    return jnp.concatenate([w3[:, :, :MLA_NOPE].reshape(K, -1), w3[:, :, MLA_NOPE:].reshape(K, -1)], axis=1).astype(BF16)


def _t5_bucket(dist):
    dist = jnp.maximum(dist, 0)
    exact = REL_BUCKETS // 2
    df = jnp.maximum(dist, 1).astype(F32)
    large = exact + (jnp.log(df / exact) / math.log(REL_MAX_DIST / exact) * (REL_BUCKETS - exact)).astype(jnp.int32)
    large = jnp.minimum(large, REL_BUCKETS - 1)
    return jnp.where(dist < exact, dist, large)


def _position_tables(S, rel_bias):
    t = NSA_T
    half = MLA_ROPE // 2
    inv = ROPE_THETA ** (-jnp.arange(half, dtype=F32) / half)
    ang = jnp.arange(S, dtype=F32)[:, None] * inv
    c, s = jnp.cos(ang), jnp.sin(ang)
    cos = _pad_cols(jnp.concatenate([c, c], axis=1), LANE)
    sin = _pad_cols(jnp.concatenate([s, s], axis=1), LANE)
    ncp = max(S // CMP_STRIDE, LANE)
    pos = jnp.arange(S)

    def bias_of(dist):
        onehot = jax.nn.one_hot(_t5_bucket(dist), REL_BUCKETS, dtype=F32)
        return jnp.einsum("...b,bh->h...", onehot, rel_bias, precision=lax.Precision.HIGHEST)

    bias_c = bias_of(pos[:, None] - (CMP_STRIDE * jnp.arange(ncp)[None, :] + CMP_BLOCK - 1))
    i = jnp.arange(t)
    bias_sw = jnp.stack([bias_of(k * t + i[:, None] - i[None, :]) for k in range(3)], axis=1)
    bias_sw = bias_sw.reshape(NSA_GROUPS, NSA_HPG, 3, t, t).transpose(0, 2, 4, 1, 3).reshape(NSA_GROUPS, 3, t, NSA_HPG * t)
    n_cmp = (S - CMP_BLOCK) // CMP_STRIDE + 1
    n_blk = S // SEL_BLOCK
    nbp = max(n_blk, LANE)
    cstart = CMP_STRIDE * jnp.arange(ncp)
    sstart = SEL_BLOCK * jnp.arange(nbp)
    ov = jnp.clip(jnp.minimum(cstart[:, None] + CMP_BLOCK, sstart[None, :] + SEL_BLOCK)
                  - jnp.maximum(cstart[:, None], sstart[None, :]), 0, None).astype(F32) / CMP_STRIDE
    ov = jnp.where((jnp.arange(ncp)[:, None] < n_cmp) & (jnp.arange(nbp)[None, :] < n_blk), ov, 0.0).astype(BF16)
    expand = ((pos[:, None] // SEL_BLOCK) == jnp.arange(nbp)[None, :]).astype(BF16)
    return cos, sin, bias_c, bias_sw, ov, expand


def _token_mixers(h, p, w_in_all, layer, tabs, B, S):
    T, D = h.shape
    cos, sin, bias_c, bias_sw, overlap, expand = tabs
    wz, w_small = _pack_w_in(w_in_all, layer)
    z, zs = input_projection(h, p["g_mix"], wz, w_small, tm=min(T, 1024), tn=1024)
    z3 = z.reshape(B, S, Z_BLOCKS * LANE)

    log_f = jax.nn.log_sigmoid(zs[:, :FORGET_COLS] + p["b_forget"].astype(F32)).reshape(B, S, FOX_HEADS)
    cum = jnp.cumsum(log_f, axis=1) * LOG2E
    o_fox = causal_attention(z3, CB_FQ, z3, CB_FK, z3, CB_FV, FOX_HEADS, cum=cum)

    q_nope, q_rope = mla_q_proj(z, p["g_cq"], _pack_w_uq(p["w_uq"]), cos, sin, S, tm=min(S, 512))
    kv = rms_matmul(z, CB_CKV * LANE // MLA_KV_RANK, p["g_ckv"], _pack_w_ukv(p["w_ukv"]), BF16, tm=min(T, 1024), tn=1024)
    kr = z3[:, :, CB_KR * LANE:(CB_KR + 1) * LANE].astype(F32)
    kr = kr[..., :MLA_ROPE] * cos[None, :, :MLA_ROPE] + kr[..., MLA_ROPE:] * sin[None, :, :MLA_ROPE]
    k_rope = jnp.pad(kr, ((0, 0), (0, 0), (0, LANE - MLA_ROPE))).astype(BF16)
    hw = MLA_HEADS * LANE
    kv3 = kv.reshape(B, S, 2 * hw)
    o_mla = causal_attention(q_nope.reshape(B, S, hw), 0, kv3, 0, kv3, MLA_HEADS, MLA_HEADS,
                             q2=q_rope.reshape(B, S, hw), k2=k_rope)

    G = NSA_GROUPS
    NC = S // CMP_STRIDE
    ncp = bias_c.shape[2]

    def compress_branch(cb, dp, d, pe, w1, w2):
        x = z3[:, :, cb * LANE: cb * LANE + G * dp].reshape(B, NC, CMP_STRIDE * G * dp)
        eye = jnp.eye(G, dtype=F32)
        w1p = jnp.pad(w1.reshape(CMP_BLOCK, d, d), ((0, 0), (0, dp - d), (0, dp - d)))
        w1g = jnp.einsum("lij,gh->lgihj", w1p, eye).reshape(CMP_BLOCK, G * dp, G * dp).astype(BF16)
        w1a = w1g[:CMP_STRIDE].reshape(CMP_STRIDE * G * dp, G * dp)
        w1b = w1g[CMP_STRIDE:].reshape(CMP_STRIDE * G * dp, G * dp)
        pe_g = jnp.tile(jnp.pad(pe, ((0, 0), (0, dp - d)))[:, None, :], (1, G, 1))
        pe2 = pe_g.reshape(2, CMP_STRIDE * G * dp).astype(BF16)
        w2p = jnp.pad(w2, ((0, dp - d), (0, dp - d)))
        w2g = jnp.einsum("ij,gh->gihj", w2p, eye).reshape(G * dp, G * dp).astype(BF16)
        out = compress(x, w1a, w1b, pe2, w2g)
        return jnp.pad(out, ((0, 0), (0, ncp - NC), (0, 0)))

    kc = compress_branch(CB_NKC, NSA_DKP, NSA_DK, p["pe_k"], p["w_cmp_k1"], p["w_cmp_k2"])
    vc = compress_branch(CB_NVC, NSA_DV, NSA_DV, p["pe_v"], p["w_cmp_v1"], p["w_cmp_v2"])
    gl = zs[:, FORGET_COLS:FORGET_COLS + NSA_GATE_COLS].reshape(B, S, G, NSA_HPG * 3)
    gl = jnp.transpose(gl, (0, 2, 1, 3))
    gl_rows = jnp.pad(gl, ((0, 0), (0, 0), (0, 0), (0, LANE - NSA_HPG * 3)))
    gl_cols = jnp.pad(jnp.swapaxes(gl, 2, 3), ((0, 0), (0, 0), (0, 16 - NSA_HPG * 3), (0, 0)))
    o_cmp, ind = nsa_cmp_select(z3, kc, vc, bias_c, overlap, gl_rows, min(N_SEL, S // SEL_BLOCK))
    o_nsa = nsa_sel_win(z3, jnp.swapaxes(ind, 2, 3), expand, bias_sw, gl_cols, o_cmp)

    merged = merge_branches(o_fox.reshape(T, -1), o_mla.reshape(T, -1), o_nsa.reshape(T, -1),
                            p["w_branch"].astype(BF16), z, tm=min(T, 1024), tn=512)
    return matmul_residual(merged, p["w_out"].astype(BF16), h, tm=min(T, 1024), tn=512)


def _memory_block(h, mem2, p, B, S):
    T, D = h.shape
    kv = rms_matmul(mem2, 0, p["g_mem_kv"], p["w_mem_kv"].astype(BF16), BF16, tm=min(mem2.shape[0], 512), tn=512)
    out = memory_attention_block(h.reshape(B, S, D), p["g_mem_q"], (p["w_mem_q"] * (MEM_DH ** -0.5 * LOG2E)).astype(BF16),
                                 kv.reshape(B, -1, kv.shape[1]), p["w_mem_o"].astype(BF16), tq=min(S, 512))
    return out.reshape(T, D)


def _moe_block(h, p, experts, layer):
    T, D = h.shape
    tm = MOE_TM
    w_r = _pad_cols(jnp.concatenate([p["w_router_group"], p["w_router_expert"]], axis=1), LANE).astype(F32)
    b_r = _pad_cols(jnp.concatenate([p["b_router_group"], p["b_router_expert"]])[None, :], LANE).astype(F32)
    logits, u = moe_router(h, p["g_moe"], w_r, b_r, tm=min(T, 512))
    glog = logits[:, :N_GROUPS]
    gsel = jnp.argmax(glog, axis=-1).astype(jnp.int32)
    pg = jnp.max(jax.nn.softmax(glog, axis=-1), axis=-1, keepdims=True)
    elog = logits[:, N_GROUPS:N_GROUPS + N_EXPERTS].reshape(T, N_GROUPS, EXPERTS_PER_GROUP)
    elog = jnp.take_along_axis(elog, gsel[:, None, None], axis=1)[:, 0]
    eprob = jax.nn.softmax(elog, axis=-1)
    j0 = jnp.argmax(eprob, axis=-1).astype(jnp.int32)
    lane = jnp.arange(EXPERTS_PER_GROUP, dtype=jnp.int32)[None, :]
    j1 = jnp.argmax(jnp.where(lane == j0[:, None], -jnp.inf, eprob), axis=-1).astype(jnp.int32)
    top_j = jnp.stack([j0, j1], axis=-1)
    top_p = jnp.take_along_axis(eprob, top_j, axis=-1)
    top_p = top_p / jnp.sum(top_p, axis=-1, keepdims=True)
    weight = pg * top_p
    flat_e = (gsel[:, None] * EXPERTS_PER_GROUP + top_j.astype(jnp.int32)).reshape(-1)
    TK = T * TOP_K
    onehot = (flat_e[:, None] == jnp.arange(N_EXPERTS, dtype=jnp.int32)[None, :]).astype(jnp.int32)
    rank = jnp.sum((jnp.cumsum(onehot, axis=0) - onehot) * onehot, axis=1)
    counts = jnp.sum(onehot, axis=0)
    pcounts = ((counts + tm - 1) // tm) * tm
    pends = jnp.cumsum(pcounts)
    dest = (pends - pcounts)[flat_e] + rank
    P = TK + N_EXPERTS * tm
    n_rb = P // tm
    row_tok = (jnp.arange(P, dtype=jnp.int32) % T).at[dest].set(jnp.repeat(jnp.arange(T, dtype=jnp.int32), TOP_K))
    blk_e = jnp.sum((pends[None, :] <= (jnp.arange(n_rb, dtype=jnp.int32) * tm)[:, None]).astype(jnp.int32), axis=1)
    blk_e = jnp.minimum(blk_e, N_EXPERTS - 1).astype(jnp.int32)
    n_used = (pends[-1] // tm).astype(jnp.int32).reshape(1)
    xr = u[row_tok]
    y = moe_experts(blk_e, n_used, xr, experts[0].astype(F32), experts[1].astype(F32), experts[2].astype(F32), layer)
    d2 = dest.reshape(T, TOP_K)
    return h + (weight[:, 0:1] * y[d2[:, 0]].astype(F32) + weight[:, 1:2] * y[d2[:, 1]].astype(F32))


_LAYER_KEYS = ("g_mix", "w_in", "b_forget", "g_cq", "g_ckv", "w_uq", "w_ukv", "pe_k", "pe_v", "w_cmp_k1", "w_cmp_k2",
               "w_cmp_v1", "w_cmp_v2", "w_branch", "w_out", "g_mem_q", "g_mem_kv", "w_mem_q", "w_mem_kv", "w_mem_o",
               "g_moe", "w_router_group", "b_router_group", "w_router_expert", "b_router_expert")


def kernel(x, mem, g_mix, w_in, b_forget, g_cq, g_ckv, w_uq, w_ukv, pe_k, pe_v, w_cmp_k1, w_cmp_k2, w_cmp_v1, w_cmp_v2, rel_bias, w_branch, w_out, g_mem_q, g_mem_kv, w_mem_q, w_mem_kv, w_mem_o, g_moe, w_router_group, b_router_group, w_router_expert, b_router_expert, w_exp_gate, w_exp_up, w_exp_down, g_final):
    B, S, D = x.shape
    T = B * S
    stacked = dict(g_mix=g_mix, w_in=w_in, b_forget=b_forget, g_cq=g_cq, g_ckv=g_ckv, w_uq=w_uq, w_ukv=w_ukv,
                   pe_k=pe_k, pe_v=pe_v, w_cmp_k1=w_cmp_k1, w_cmp_k2=w_cmp_k2, w_cmp_v1=w_cmp_v1, w_cmp_v2=w_cmp_v2,
                   w_branch=w_branch, w_out=w_out, g_mem_q=g_mem_q, g_mem_kv=g_mem_kv, w_mem_q=w_mem_q,
                   w_mem_kv=w_mem_kv, w_mem_o=w_mem_o, g_moe=g_moe, w_router_group=w_router_group,
                   b_router_group=b_router_group, w_router_expert=w_router_expert, b_router_expert=b_router_expert,
                   w_exp_gate=w_exp_gate, w_exp_up=w_exp_up, w_exp_down=w_exp_down)
    tabs = _position_tables(S, rel_bias.astype(F32) * LOG2E)
    h = x.reshape(T, D).astype(F32)
    mem2 = mem.reshape(-1, D).astype(F32)
    for l in range(w_in.shape[0]):
        p = {k: stacked[k][l] for k in _LAYER_KEYS}
        h = _token_mixers(h, p, w_in.astype(F32), l, tabs, B, S)
        h = _memory_block(h, mem2, p, B, S)
        h = _moe_block(h, p, (w_exp_gate, w_exp_up, w_exp_down), l)
    return rmsnorm_rows(h, g_final, tm=min(T, 512)).reshape(B, S, D)
```

```python
import functools
import math

import jax
import jax.numpy as jnp
import numpy as np
from jax import lax
from jax.experimental import pallas as pl
from jax.experimental.pallas import tpu as pltpu

F32 = jnp.float32
BF16 = jnp.bfloat16

EPS = 1e-6
NEG_INF = -1e30
LOG2E = math.log2(math.e)
LANE = 128

FOX_HEADS, FOX_DH = 8, 128
MLA_HEADS, MLA_NOPE, MLA_ROPE, MLA_DV = 8, 128, 64, 128
MLA_Q_RANK, MLA_KV_RANK = 768, 512
ROPE_THETA = 10000.0
NSA_HEADS, NSA_GROUPS, NSA_DK, NSA_DV = 8, 2, 192, 128
NSA_HPG = NSA_HEADS // NSA_GROUPS
NSA_DKP = 256
CMP_BLOCK, CMP_STRIDE, SEL_BLOCK, N_SEL, WINDOW = 32, 16, 64, 8, 512
SEL_FORCE = 1e6
REL_BUCKETS, REL_MAX_DIST = 32, 128
N_BRANCH, BRANCH_W = 3, 1024
MEM_HEADS, MEM_DH = 4, 128
N_GROUPS, EXPERTS_PER_GROUP, TOP_K = 4, 8, 2
N_EXPERTS = N_GROUPS * EXPERTS_PER_GROUP
FORGET_COLS, NSA_GATE_COLS = FOX_HEADS, NSA_HEADS * 3

CB_CKV, CB_KR, CB_CQ = 0, 4, 6
CB_NKC, CB_NQ = 12, 16
CB_FQ, CB_FK, CB_FV = 32, 40, 48
CB_NKS, CB_NKW = 56, 60
CB_NVC, CB_NVS, CB_NVW = 64, 66, 68
CB_MG = 72
Z_BLOCKS = 120

ATT_T = 256
ATT_HG = 8
NSA_T = 256
NSA_CMP_T = 512
MOE_TM = 512
MOE_DC = 512
VMEM_LIMIT = 56 * 1024 * 1024


def _cparams(sem):
    return pltpu.CompilerParams(dimension_semantics=sem, vmem_limit_bytes=VMEM_LIMIT)


def _sigmoid(x):
    return 1.0 / (1.0 + jnp.exp(-x))


def _rms_mm_kernel(x_ref, g_ref, w_ref, o_ref, u_ref):
    @pl.when(pl.program_id(1) == 0)
    def _():
        x = x_ref[...].astype(F32)
        ms = jnp.mean(x * x, axis=-1, keepdims=True)
        u_ref[...] = (x * lax.rsqrt(ms + EPS) * g_ref[...]).astype(u_ref.dtype)

    o_ref[...] = jnp.dot(u_ref[...], w_ref[...], preferred_element_type=F32).astype(o_ref.dtype)


def rms_matmul(x, xcol, g, w, out_dtype, tm, tn):
    T = x.shape[0]
    K, N = w.shape
    return pl.pallas_call(
        _rms_mm_kernel,
        out_shape=jax.ShapeDtypeStruct((T, N), out_dtype),
        grid=(T // tm, N // tn),
        in_specs=[pl.BlockSpec((tm, K), lambda i, j: (i, xcol)),
                  pl.BlockSpec((1, K), lambda i, j: (0, 0)),
                  pl.BlockSpec((K, tn), lambda i, j: (0, j))],
        out_specs=pl.BlockSpec((tm, tn), lambda i, j: (i, j)),
        scratch_shapes=[pltpu.VMEM((tm, K), BF16)],
        compiler_params=_cparams(("parallel", "arbitrary")),
        name="rms_matmul",
    )(x, g.reshape(1, K).astype(F32), w)


def _in_proj_kernel(x_ref, g_ref, w_ref, ws_ref, o_ref, os_ref, u_ref):
    @pl.when(pl.program_id(1) == 0)
    def _():
        x = x_ref[...]
        ms = jnp.mean(x * x, axis=-1, keepdims=True)
        u = x * lax.rsqrt(ms + EPS) * g_ref[...]
        u_ref[...] = u.astype(u_ref.dtype)
        os_ref[...] = jnp.dot(u, ws_ref[...], preferred_element_type=F32, precision=lax.Precision.HIGHEST)

    o_ref[...] = jnp.dot(u_ref[...], w_ref[...], preferred_element_type=F32).astype(o_ref.dtype)


def input_projection(x, g, w, w_small, tm, tn):
    T, K = x.shape
    N = w.shape[1]
    Ns = w_small.shape[1]
    return pl.pallas_call(
        _in_proj_kernel,
        out_shape=(jax.ShapeDtypeStruct((T, N), BF16), jax.ShapeDtypeStruct((T, Ns), F32)),
        grid=(T // tm, N // tn),
        in_specs=[pl.BlockSpec((tm, K), lambda i, j: (i, 0)),
                  pl.BlockSpec((1, K), lambda i, j: (0, 0)),
                  pl.BlockSpec((K, tn), lambda i, j: (0, j)),
                  pl.BlockSpec((K, Ns), lambda i, j: (0, 0))],
        out_specs=(pl.BlockSpec((tm, tn), lambda i, j: (i, j)), pl.BlockSpec((tm, Ns), lambda i, j: (i, 0))),
        scratch_shapes=[pltpu.VMEM((tm, K), BF16)],
        compiler_params=_cparams(("parallel", "arbitrary")),
        name="input_projection",
    )(x, g.reshape(1, K).astype(F32), w, w_small)


def _mla_q_kernel(x_ref, g_ref, w_ref, cos_ref, sin_ref, qn_ref, qr_ref):
    x = x_ref[...].astype(F32)
    ms = jnp.mean(x * x, axis=-1, keepdims=True)
    u = (x * lax.rsqrt(ms + EPS) * g_ref[...]).astype(BF16)
    y = jnp.dot(u, w_ref[...], preferred_element_type=F32)
    hw = MLA_HEADS * LANE
    qn_ref[...] = y[:, :hw].astype(qn_ref.dtype)
    cos = cos_ref[...]
    sin = sin_ref[...]
    for h in range(MLA_HEADS):
        a = y[:, hw + h * LANE: hw + (h + 1) * LANE]
        b = y[:, 2 * hw + h * LANE: 2 * hw + (h + 1) * LANE]
        qr_ref[:, h * LANE:(h + 1) * LANE] = (a * cos + b * sin).astype(qr_ref.dtype)


def mla_q_proj(z, g, w, cos, sin, S, tm):
    T = z.shape[0]
    K, N = w.shape
    hw = MLA_HEADS * LANE
    nsb = S // tm
    return pl.pallas_call(
        _mla_q_kernel,
        out_shape=(jax.ShapeDtypeStruct((T, hw), BF16), jax.ShapeDtypeStruct((T, hw), BF16)),
        grid=(T // tm,),
        in_specs=[pl.BlockSpec((tm, K), lambda i: (i, CB_CQ * LANE // MLA_Q_RANK)),
                  pl.BlockSpec((1, K), lambda i: (0, 0)),
                  pl.BlockSpec((K, N), lambda i: (0, 0)),
                  pl.BlockSpec((tm, LANE), lambda i: (i % nsb, 0)),
                  pl.BlockSpec((tm, LANE), lambda i: (i % nsb, 0))],
        out_specs=(pl.BlockSpec((tm, hw), lambda i: (i, 0)), pl.BlockSpec((tm, hw), lambda i: (i, 0))),
        compiler_params=_cparams(("parallel",)),
        name="mla_q_proj",
    )(z, g.reshape(1, K).astype(F32), w, cos, sin)


def _causal_attn_kernel(*refs, t, hg, two_part, decay):
    refs = list(refs)
    q_ref, k_ref, v_ref = refs[:3]
    pos = 3
    if two_part:
        q2_ref, k2_ref = refs[pos:pos + 2]
        pos += 2
    if decay:
        ck_ref = refs[pos]
        pos += 1
    o_ref = refs[pos]
    if two_part:
        kcat_ref = refs[pos + 1]
    qi = pl.program_id(2)
    dn = (((1,), (1,)), ((), ()))

    if two_part:
        @pl.when(qi == 0)
        def _():
            for j in range(hg):
                kcat_ref[j, :, :LANE] = k_ref[0, :, j * LANE:(j + 1) * LANE]
                kcat_ref[j, :, LANE:] = k2_ref[0]

    qs = []
    for j in range(hg):
        qj = q_ref[0, :, j * LANE:(j + 1) * LANE]
        if two_part:
            qj = jnp.concatenate([qj, q2_ref[0, :, j * LANE:(j + 1) * LANE]], axis=1)
        qs.append(qj)

    def step(kb, carry, masked, width=1):
        off = pl.multiple_of(kb * t, t)
        tk = width * t
        heads = range(hg)
        ss = []
        for j in heads:
            k = kcat_ref[j, pl.ds(off, tk), :] if two_part else k_ref[0, pl.ds(off, tk), j * LANE:(j + 1) * LANE]
            ss.append(lax.dot_general(k, qs[j], dn, preferred_element_type=F32))
        if decay:
            ss = [ss[j] - ck_ref[0, 0, pl.ds(off, tk), j:j + 1] for j in heads]
        if masked:
            r = lax.broadcasted_iota(jnp.int32, (tk, t), 0)
            c = lax.broadcasted_iota(jnp.int32, (tk, t), 1)
            ss = [jnp.where(r <= c, s, NEG_INF) for s in ss]
        ms = [jnp.maximum(carry[j][0], jnp.max(ss[j], axis=0, keepdims=True)) for j in heads]
        ps = [jnp.exp2(ss[j] - ms[j]) for j in heads]
        out = []
        for j in heads:
            m, l, acc = carry[j]
            a = jnp.exp2(m - ms[j])
            l = a * l + jnp.sum(ps[j], axis=0, keepdims=True)
            v = v_ref[0, pl.ds(off, tk), j * LANE:(j + 1) * LANE]
            acc = a * acc + lax.dot_general(v, ps[j].astype(BF16), (((0,), (0,)), ((), ())),
                                            preferred_element_type=F32)
            out.append((ms[j], l, acc))
        return tuple(out)

    init = tuple((jnp.full((1, t), NEG_INF, F32), jnp.zeros((1, t), F32), jnp.zeros((LANE, t), F32))
                 for _ in range(hg))
    carry = lax.fori_loop(0, qi // 2, lambda kp, c: step(2 * kp, c, False, width=2), init)
    carry = lax.cond(qi % 2 == 1, lambda c: step(qi - 1, c, False), lambda c: c, carry)
    carry = step(qi, carry, True)
    for j in range(hg):
        _, l, acc = carry[j]
        o_ref[0, :, j * LANE:(j + 1) * LANE] = (acc / l).T.astype(o_ref.dtype)


def causal_attention(q, qcb, k, kcb, v, vcb, heads, q2=None, k2=None, cum=None):
    B, S, _ = q.shape
    t, hg = ATT_T, ATT_HG
    w = hg * LANE
    two_part, decay = q2 is not None, cum is not None
    in_specs = [pl.BlockSpec((1, t, w), lambda b, h, i: (b, i, qcb // hg + h)),
                pl.BlockSpec((1, S, w), lambda b, h, i: (b, 0, kcb // hg + h)),
                pl.BlockSpec((1, S, w), lambda b, h, i: (b, 0, vcb // hg + h))]
    args = [q, k, v]
    scratch = []
    if two_part:
        in_specs += [pl.BlockSpec((1, t, w), lambda b, h, i: (b, i, h)),
                     pl.BlockSpec((1, S, LANE), lambda b, h, i: (b, 0, 0))]
        args += [q2, k2]
        scratch = [pltpu.VMEM((hg, S, 2 * LANE), BF16)]
    if decay:
        in_specs += [pl.BlockSpec((1, 1, S, hg), lambda b, h, i: (b, h, 0, 0))]
        args += [jnp.transpose(cum.reshape(B, S, heads // hg, hg), (0, 2, 1, 3))]
    return pl.pallas_call(
        functools.partial(_causal_attn_kernel, t=t, hg=hg, two_part=two_part, decay=decay),
        out_shape=jax.ShapeDtypeStruct((B, S, heads * LANE), BF16),
        grid=(B, heads // hg, S // t),
        in_specs=in_specs,
        out_specs=pl.BlockSpec((1, t, w), lambda b, h, i: (b, i, h)),
        scratch_shapes=scratch,
        compiler_params=_cparams(("parallel", "parallel", "arbitrary")),
        name="causal_attention",
    )(*args)


def _gelu_tanh(x):
    return 0.5 * x * (1.0 + jnp.tanh(math.sqrt(2.0 / math.pi) * (x + 0.044715 * (x * x * x))))


def _compress_kernel(x_ref, w1a_ref, w1b_ref, pe_ref, w2_ref, o_ref):
    x = x_ref[0]
    a = jnp.dot(x, w1a_ref[...], preferred_element_type=F32)
    b = jnp.dot(x, w1b_ref[...], preferred_element_type=F32)
    nc = a.shape[0]
    b_next = pltpu.roll(b, nc - 1, 0)
    pe_term = jnp.dot(pe_ref[0:1, :], w1a_ref[...], preferred_element_type=F32) + \
        jnp.dot(pe_ref[1:2, :], w1b_ref[...], preferred_element_type=F32)
    hid = _gelu_tanh(a + b_next + pe_term)
    o_ref[0] = jnp.dot(hid.astype(BF16), w2_ref[...], preferred_element_type=F32).astype(o_ref.dtype)


def compress(x, w1a, w1b, pe2, w2):
    BG, NC, KD = x.shape
    dp = w2.shape[1]
    return pl.pallas_call(
        _compress_kernel,
        out_shape=jax.ShapeDtypeStruct((BG, NC, dp), BF16),
        grid=(BG,),
        in_specs=[pl.BlockSpec((1, NC, KD), lambda i: (i, 0, 0)),
                  pl.BlockSpec((KD, dp), lambda i: (0, 0)),
                  pl.BlockSpec((KD, dp), lambda i: (0, 0)),
                  pl.BlockSpec((2, KD), lambda i: (0, 0)),
                  pl.BlockSpec((dp, dp), lambda i: (0, 0))],
        out_specs=pl.BlockSpec((1, NC, dp), lambda i: (i, 0, 0)),
        compiler_params=_cparams(("parallel",)),
        name="nsa_compress",
    )(x, w1a, w1b, pe2, w2)


def _nsa_cmp_kernel(q_ref, kc_ref, vc_ref, bias_ref, ov_ref, gate_ref, o_ref, ind_ref, *, t, n_sel, n_rows):
    qi = pl.program_id(2)
    ncp = kc_ref.shape[1]
    kc = kc_ref[0]
    vc = vc_ref[0]
    pos = lax.broadcasted_iota(jnp.int32, (ncp, t), 1) + qi * t
    cblk = lax.broadcasted_iota(jnp.int32, (ncp, t), 0)
    valid = pos >= CMP_STRIDE * cblk + (CMP_BLOCK - 1)
    dn = (((1,), (1,)), ((), ()))
    tn = (((0,), (0,)), ((), ()))
    heads = range(NSA_HPG)
    ss = [lax.dot_general(kc, q_ref[0, :, h * NSA_DKP:(h + 1) * NSA_DKP], dn, preferred_element_type=F32)
          for h in heads]
    ss = [jnp.where(valid, ss[h] + bias_ref[h], NEG_INF) for h in heads]
    es = [jnp.exp2(ss[h] - jnp.max(ss[h], axis=0, keepdims=True)) for h in heads]
    ps = [jnp.where(valid, es[h] / jnp.sum(es[h], axis=0, keepdims=True), 0.0) for h in heads]
    gates = _sigmoid(gate_ref[0, 0])
    for h in heads:
        o = lax.dot_general(vc, ps[h].astype(BF16), tn, preferred_element_type=F32)
        o_ref[0, :, h * NSA_DV:(h + 1) * NSA_DV] = (gates[3 * h:3 * h + 1, :] * o).T.astype(o_ref.dtype)
    psum = functools.reduce(lambda x, y: x + y, ps)
    p_hi = psum.astype(BF16)
    p_lo = (psum - p_hi.astype(F32)).astype(BF16)
    imp = lax.dot_general(ov_ref[...], p_hi, tn, preferred_element_type=F32) + \
        lax.dot_general(ov_ref[...], p_lo, tn, preferred_element_type=F32)
    nbp = imp.shape[0]
    imp = imp[:n_rows]
    blk = lax.broadcasted_iota(jnp.int32, (n_rows, t), 0)
    cur = (lax.broadcasted_iota(jnp.int32, (n_rows, t), 1) + qi * t) // SEL_BLOCK
    forced = (blk == 0) | (blk == cur) | (blk == cur - 1)
    score = jnp.where(blk <= cur, imp + jnp.where(forced, SEL_FORCE, 0.0), NEG_INF)
    sel = jnp.zeros((n_rows, t), F32)
    for _ in range(n_sel):
        mx = jnp.max(score, axis=0, keepdims=True)
        first = jnp.min(jnp.where(score == mx, blk, nbp), axis=0, keepdims=True)
        hit = blk == first
        sel = jnp.where(hit, 1.0, sel)
        score = jnp.where(hit, -jnp.inf, score)
    ind_ref[0, 0] = jnp.zeros(ind_ref.shape[2:], ind_ref.dtype)
    ind_ref[0, 0, :n_rows, :] = sel.astype(ind_ref.dtype)


def nsa_cmp_select(z3, kc, vc, bias_c, overlap, gates, n_sel):
    B, S, _ = z3.shape
    G = NSA_GROUPS
    t = min(NSA_CMP_T, S)
    ncp = kc.shape[1]
    nbp = overlap.shape[1]
    qw = NSA_HPG * NSA_DKP
    ow = NSA_HPG * NSA_DV
    n_rows = min(nbp, -(-(S // SEL_BLOCK) // 16) * 16)
    return pl.pallas_call(
        functools.partial(_nsa_cmp_kernel, t=t, n_sel=n_sel, n_rows=n_rows),
        out_shape=(jax.ShapeDtypeStruct((B, S, G * ow), BF16), jax.ShapeDtypeStruct((B, G, nbp, S), BF16)),
        grid=(B, G, S // t),
        in_specs=[pl.BlockSpec((1, t, qw), lambda b, g, i: (b, i, CB_NQ * LANE // qw + g)),
                  pl.BlockSpec((1, ncp, NSA_DKP), lambda b, g, i: (b, 0, g)),
                  pl.BlockSpec((1, ncp, NSA_DV), lambda b, g, i: (b, 0, g)),
                  pl.BlockSpec((NSA_HPG, ncp, t), lambda b, g, i: (g, 0, i)),
                  pl.BlockSpec((ncp, nbp), lambda b, g, i: (0, 0)),
                  pl.BlockSpec((1, 1, gates.shape[2], t), lambda b, g, i: (b, g, 0, i))],
        out_specs=(pl.BlockSpec((1, t, ow), lambda b, g, i: (b, i, g)),
                   pl.BlockSpec((1, 1, nbp, t), lambda b, g, i: (b, g, 0, i))),
        compiler_params=_cparams(("parallel", "parallel", "arbitrary")),
        name="nsa_cmp_select",
    )(z3, kc, vc, bias_c, overlap, gates)


def _nsa_sw_kernel(q_ref, ks_ref, vs_ref, kw_ref, vw_ref, ind_ref, e_ref, bias_ref, gate_ref, oc_ref, o_ref,
                   *, t):
    qi = pl.program_id(1)
    hp, G = NSA_HPG, NSA_GROUPS
    q4 = [jnp.concatenate([q_ref[0, :, (g * hp + h) * NSA_DKP:(g * hp + h + 1) * NSA_DKP] for h in range(hp)], axis=0)
          for g in range(G)]
    inds = [ind_ref[0, g] for g in range(G)]
    dn = (((1,), (1,)), ((), ()))

    def step(kb, carry, k_ref, v_ref, selected, width=1):
        off = pl.multiple_of(kb * t, t)
        tk = width * t
        ri = lax.broadcasted_iota(jnp.int32, (tk, t), 0)
        ci = lax.broadcasted_iota(jnp.int32, (tk, t), 1)
        d = (qi - kb) * t + ci - ri
        near = (d >= 0) if selected else (d >= 0) & (d < WINDOW)
        groups = range(G)
        ss = [lax.dot_general(k_ref[0, pl.ds(off, tk), g * NSA_DKP:(g + 1) * NSA_DKP], q4[g], dn,
                              preferred_element_type=F32) for g in groups]
        negs = []
        for g in groups:
            mask = near
            if selected:
                hit = jnp.dot(e_ref[pl.ds(off, tk), :], inds[g], preferred_element_type=F32)
                mask = near & (hit > 0.5)
            neg = jnp.where(mask, 0.0, NEG_INF)
            negs.append(jnp.concatenate([neg] * hp, axis=1))
        bias = [jnp.concatenate([bias_ref[g, jnp.minimum(qi - kb - w, 2)] for w in range(width)], axis=0)
                if width > 1 else bias_ref[g, jnp.minimum(qi - kb, 2)] for g in groups]
        ss = [ss[g] + bias[g] + negs[g] for g in groups]
        ms = [jnp.maximum(carry[g][0], jnp.max(ss[g], axis=0, keepdims=True)) for g in groups]
        ps = [jnp.exp2(ss[g] - ms[g]) for g in groups]
        out = []
        for g in groups:
            m, l, acc = carry[g]
            a = jnp.exp2(m - ms[g])
            l = a * l + jnp.sum(ps[g], axis=0, keepdims=True)
            v = v_ref[0, pl.ds(off, tk), g * NSA_DV:(g + 1) * NSA_DV]
            acc = a * acc + lax.dot_general(v, ps[g].astype(BF16), (((0,), (0,)), ((), ())),
                                            preferred_element_type=F32)
            out.append((ms[g], l, acc))
        return tuple(out)

    init = tuple((jnp.full((1, hp * t), NEG_INF, F32), jnp.zeros((1, hp * t), F32), jnp.zeros((NSA_DV, hp * t), F32))
                 for _ in range(G))
    sel = lax.fori_loop(0, (qi + 1) // 2, lambda kp, c: step(2 * kp, c, ks_ref, vs_ref, True, width=2), init)
    sel = lax.cond(qi % 2 == 0, lambda c: step(qi, c, ks_ref, vs_ref, True), lambda c: c, sel)
    lo = jnp.maximum(qi - WINDOW // t, 0)
    win = lax.fori_loop(lo, qi + 1, lambda kb, c: step(kb, c, kw_ref, vw_ref, False), init)
    for g in range(G):
        o_s = sel[g][2] / sel[g][1]
        o_w = win[g][2] / win[g][1]
        gates = _sigmoid(gate_ref[0, g])
        for h in range(hp):
            r = slice(h * t, (h + 1) * t)
            c = slice((g * hp + h) * NSA_DV, (g * hp + h + 1) * NSA_DV)
            o = gates[3 * h + 1:3 * h + 2, :] * o_s[:, r] + gates[3 * h + 2:3 * h + 3, :] * o_w[:, r]
            o_ref[0, :, c] = (oc_ref[0, :, c].astype(F32) + o.T).astype(o_ref.dtype)


def nsa_sel_win(z3, ind, expand, bias_sw, gates, o_cmp):
    B, S, _ = z3.shape
    G = NSA_GROUPS
    t = NSA_T
    qw = NSA_HEADS * NSA_DKP
    ow = NSA_HEADS * NSA_DV
    kw = G * NSA_DKP
    vw = G * NSA_DV
    nbp = ind.shape[2]
    kspec = lambda cb: pl.BlockSpec((1, S, kw), lambda b, i: (b, 0, cb * LANE // kw))
    vspec = lambda cb: pl.BlockSpec((1, S, vw), lambda b, i: (b, 0, cb * LANE // vw))
    return pl.pallas_call(
        functools.partial(_nsa_sw_kernel, t=t),
        out_shape=jax.ShapeDtypeStruct((B, S, ow), BF16),
        grid=(B, S // t),
        in_specs=[pl.BlockSpec((1, t, qw), lambda b, i: (b, i, CB_NQ * LANE // qw)),
                  kspec(CB_NKS), vspec(CB_NVS), kspec(CB_NKW), vspec(CB_NVW),
                  pl.BlockSpec((1, G, nbp, t), lambda b, i: (b, 0, 0, i)),
                  pl.BlockSpec((S, nbp), lambda b, i: (0, 0)),
                  pl.BlockSpec((G, 3, t, NSA_HPG * t), lambda b, i: (0, 0, 0, 0)),
                  pl.BlockSpec((1, G, gates.shape[2], t), lambda b, i: (b, 0, 0, i)),
                  pl.BlockSpec((1, t, ow), lambda b, i: (b, i, 0))],
        out_specs=pl.BlockSpec((1, t, ow), lambda b, i: (b, i, 0)),
        compiler_params=_cparams(("parallel", "arbitrary")),
        name="nsa_sel_win",
    )(z3, z3, z3, z3, z3, ind, expand, bias_sw, gates, o_cmp)


def _merge_kernel(of_ref, om_ref, on_ref, wb_ref, g0_ref, g1_ref, g2_ref, o_ref):
    acc = None
    for n, (o_r, g_r) in enumerate(((of_ref, g0_ref), (om_ref, g1_ref), (on_ref, g2_ref))):
        y = jnp.dot(o_r[...], wb_ref[n], preferred_element_type=F32)
        y = _sigmoid(g_r[...].astype(F32)) * y
        acc = y if acc is None else acc + y
    o_ref[...] = acc.astype(o_ref.dtype)


def merge_branches(o_fox, o_mla, o_nsa, wb, z, tm, tn):
    T = o_fox.shape[0]
    D = wb.shape[2]
    gspec = lambda n: pl.BlockSpec((tm, tn), lambda i, j: (i, (CB_MG * LANE + n * D) // tn + j))
    ospec = pl.BlockSpec((tm, BRANCH_W), lambda i, j: (i, 0))
    return pl.pallas_call(
        _merge_kernel,
        out_shape=jax.ShapeDtypeStruct((T, D), BF16),
        grid=(T // tm, D // tn),
        in_specs=[ospec, ospec, ospec,
                  pl.BlockSpec((N_BRANCH, BRANCH_W, tn), lambda i, j: (0, 0, j)),
                  gspec(0), gspec(1), gspec(2)],
        out_specs=pl.BlockSpec((tm, tn), lambda i, j: (i, j)),
        compiler_params=_cparams(("parallel", "arbitrary")),
        name="merge_branches",
    )(o_fox, o_mla, o_nsa, wb, z, z, z)


def _mm_res_kernel(a_ref, w_ref, r_ref, o_ref):
    o_ref[...] = r_ref[...] + jnp.dot(a_ref[...], w_ref[...], preferred_element_type=F32)


def matmul_residual(a, w, res, tm, tn):
    T, K = a.shape
    N = w.shape[1]
    return pl.pallas_call(
        _mm_res_kernel,
        out_shape=jax.ShapeDtypeStruct((T, N), F32),
        grid=(T // tm, N // tn),
        in_specs=[pl.BlockSpec((tm, K), lambda i, j: (i, 0)),
                  pl.BlockSpec((K, tn), lambda i, j: (0, j)),
                  pl.BlockSpec((tm, tn), lambda i, j: (i, j))],
        out_specs=pl.BlockSpec((tm, tn), lambda i, j: (i, j)),
        compiler_params=_cparams(("parallel", "arbitrary")),
        name="matmul_residual",
    )(a, w, res)


def _mem_attn_kernel(h_ref, g_ref, wq_ref, kv_ref, wo_ref, gr_ref, wr_ref, br_ref, o_ref, lg_ref, u_ref):
    x = h_ref[0]
    ms = jnp.mean(x * x, axis=-1, keepdims=True)
    u = (x * lax.rsqrt(ms + EPS) * g_ref[...]).astype(BF16)
    q = jnp.dot(u, wq_ref[...], preferred_element_type=F32).astype(BF16)
    dn = (((1,), (1,)), ((), ()))
    hw = MEM_HEADS * MEM_DH
    heads = range(MEM_HEADS)
    cs = [slice(h * MEM_DH, (h + 1) * MEM_DH) for h in heads]
    ss = [lax.dot_general(q[:, cs[h]], kv_ref[0, :, cs[h]], dn, preferred_element_type=F32) for h in heads]
    es = [jnp.exp2(ss[h] - jnp.max(ss[h], axis=-1, keepdims=True)) for h in heads]
    ps = [es[h] / jnp.sum(es[h], axis=-1, keepdims=True) for h in heads]
    outs = [jnp.dot(ps[h].astype(BF16), kv_ref[0, :, hw + h * MEM_DH: hw + (h + 1) * MEM_DH],
                    preferred_element_type=F32).astype(BF16) for h in heads]
    o = jnp.concatenate(outs, axis=1)
    hn = x + jnp.dot(o, wo_ref[...], preferred_element_type=F32)
    o_ref[0] = hn
    ms2 = jnp.mean(hn * hn, axis=-1, keepdims=True)
    u2 = hn * lax.rsqrt(ms2 + EPS) * gr_ref[...]
    u_ref[0] = u2.astype(u_ref.dtype)
    lg_ref[0] = jnp.dot(u2, wr_ref[...], preferred_element_type=F32, precision=lax.Precision.HIGHEST) + br_ref[...]


def memory_attention_block(h3, g, wq, kv, wo, g_r, w_r, b_r, tq):
    B, S, D = h3.shape
    M = kv.shape[1]
    hw = MEM_HEADS * MEM_DH
    N = w_r.shape[1]
    const = lambda shape: pl.BlockSpec(shape, lambda b, i: (0,) * len(shape))
    row = lambda width: pl.BlockSpec((1, tq, width), lambda b, i: (b, i, 0))
    return pl.pallas_call(
        _mem_attn_kernel,
        out_shape=(jax.ShapeDtypeStruct((B, S, D), F32), jax.ShapeDtypeStruct((B, S, N), F32),
                   jax.ShapeDtypeStruct((B, S, D), BF16)),
        grid=(B, S // tq),
        in_specs=[row(D), const((1, D)), const((D, hw)),
                  pl.BlockSpec((1, M, 2 * hw), lambda b, i: (b, 0, 0)),
                  const((hw, D)), const((1, D)), const((D, N)), const((1, N))],
        out_specs=(row(D), row(N), row(D)),
        compiler_params=_cparams(("parallel", "arbitrary")),
        name="memory_attention",
    )(h3, g.reshape(1, D).astype(F32), wq, kv, wo, g_r.reshape(1, D).astype(F32), w_r, b_r)


def _moe_kernel(be_ref, nu_ref, x_ref, wg_ref, wu_ref, wd_ref, o_ref, acc_ref):
    i = pl.program_id(0)
    j = pl.program_id(1)
    used = i < nu_ref[0]

    @pl.when((i == 0) & (j == 0))
    def _():
        acc_ref[...] = jnp.zeros(acc_ref.shape, acc_ref.dtype)

    @pl.when(used)
    def _():
        x = x_ref[...]
        a = jnp.dot(x, wg_ref[0, 0].astype(BF16), preferred_element_type=F32)
        b = jnp.dot(x, wu_ref[0, 0].astype(BF16), preferred_element_type=F32)
        hdn = (a * _sigmoid(a) * b).astype(BF16)
        y = jnp.dot(hdn, wd_ref[0, 0].astype(BF16), preferred_element_type=F32)
        tot = y + jnp.where(j > 0, acc_ref[...], 0.0)
        acc_ref[...] = tot
        o_ref[...] = tot.astype(o_ref.dtype)

    @pl.when(jnp.logical_not(used))
    def _():
        o_ref[...] = jnp.zeros(o_ref.shape, o_ref.dtype)


def moe_experts(blk_e, n_used, xr, wg, wu, wd, layer):
    P, D = xr.shape
    De = wg.shape[3]
    tm, dc = MOE_TM, MOE_DC
    nj = De // dc
    chunk = lambda i, s: jnp.where(i % 2 == 0, s, nj - 1 - s)
    jj = lambda i, s, nu: jnp.where(i < nu[0], chunk(i, s), chunk(nu[0] - 1, nj - 1))
    grid_spec = pltpu.PrefetchScalarGridSpec(
        num_scalar_prefetch=2,
        grid=(P // tm, nj),
        in_specs=[pl.BlockSpec((tm, D), lambda i, j, be, nu: (i, 0)),
                  pl.BlockSpec((1, 1, D, dc), lambda i, j, be, nu: (layer, be[i], 0, jj(i, j, nu))),
                  pl.BlockSpec((1, 1, D, dc), lambda i, j, be, nu: (layer, be[i], 0, jj(i, j, nu))),
                  pl.BlockSpec((1, 1, dc, D), lambda i, j, be, nu: (layer, be[i], jj(i, j, nu), 0))],
        out_specs=pl.BlockSpec((tm, D), lambda i, j, be, nu: (i, 0)),
        scratch_shapes=[pltpu.VMEM((tm, D), F32)],
    )
    return pl.pallas_call(
        _moe_kernel,
        out_shape=jax.ShapeDtypeStruct((P, D), BF16),
        grid_spec=grid_spec,
        compiler_params=_cparams(("arbitrary", "arbitrary")),
        name="moe_experts",
    )(blk_e, n_used, xr, wg, wu, wd)


def _rmsnorm_kernel(x_ref, g_ref, o_ref):
    x = x_ref[...]
    ms = jnp.mean(x * x, axis=-1, keepdims=True)
    o_ref[...] = x * lax.rsqrt(ms + EPS) * g_ref[...]


def rmsnorm_rows(x, g, tm):
    T, D = x.shape
    return pl.pallas_call(
        _rmsnorm_kernel,
        out_shape=jax.ShapeDtypeStruct((T, D), F32),
        grid=(T // tm,),
        in_specs=[pl.BlockSpec((tm, D), lambda i: (i, 0)), pl.BlockSpec((1, D), lambda i: (0, 0))],
        out_specs=pl.BlockSpec((tm, D), lambda i: (i, 0)),
        compiler_params=_cparams(("parallel",)),
        name="final_rmsnorm",
    )(x, g.reshape(1, D).astype(F32))


def _pad_cols(w, width):
    return jnp.pad(w, ((0, 0), (0, width - w.shape[1])))


def _w_in_segments(D):
    names = ("fq", "fk", "fv", "ff", "mcq", "mckv", "mkr", "nq", "nkc", "nvc", "nks", "nvs", "nkw", "nvw", "ngt", "mg")
    widths = (1024, 1024, 1024, FORGET_COLS, MLA_Q_RANK, MLA_KV_RANK, MLA_ROPE, NSA_HEADS * NSA_DK,
              NSA_GROUPS * NSA_DK, NSA_GROUPS * NSA_DV, NSA_GROUPS * NSA_DK, NSA_GROUPS * NSA_DV,
              NSA_GROUPS * NSA_DK, NSA_GROUPS * NSA_DV, NSA_GATE_COLS, N_BRANCH * D)
    src = dict(zip(names, np.cumsum((0,) + widths[:-1]).tolist()))
    wid = dict(zip(names, widths))
    segs = []
    plain = lambda name, cb, f=1.0: segs.append((cb * LANE, src[name], wid[name], f))

    def padded_k(name, cb, n, f=1.0):
        for i in range(n):
            segs.append((cb * LANE + i * NSA_DKP, src[name] + i * NSA_DK, NSA_DK, f))

    half = MLA_ROPE // 2
    plain("mckv", CB_CKV)
    plain("mkr", CB_KR)
    segs.append((CB_KR * LANE + MLA_ROPE, src["mkr"] + half, half, -1.0))
    segs.append((CB_KR * LANE + MLA_ROPE + half, src["mkr"], half, 1.0))
    plain("mcq", CB_CQ)
    padded_k("nkc", CB_NKC, NSA_GROUPS)
    padded_k("nq", CB_NQ, NSA_HEADS, NSA_DK ** -0.5 * LOG2E)
    plain("fq", CB_FQ, FOX_DH ** -0.5 * LOG2E)
    plain("fk", CB_FK)
    plain("fv", CB_FV)
    padded_k("nks", CB_NKS, NSA_GROUPS)
    padded_k("nkw", CB_NKW, NSA_GROUPS)
    plain("nvc", CB_NVC)
    plain("nvs", CB_NVS)
    plain("nvw", CB_NVW)
    plain("mg", CB_MG)
    return segs, src


def _pack_w_in_kernel(w_ref, o_ref, os_ref, *, segs, small_segs):
    x = w_ref[0]
    o_ref[...] = jnp.zeros(o_ref.shape, o_ref.dtype)
    for dst, s, n, f in segs:
        v = x[:, s:s + n]
        if f != 1.0:
            v = v * f
        o_ref[:, dst:dst + n] = v.astype(o_ref.dtype)
    os_ref[...] = jnp.zeros(os_ref.shape, os_ref.dtype)
    for dst, s, n in small_segs:
        os_ref[:, dst:dst + n] = x[:, s:s + n]


def _pack_w_in(w_all, layer):
    _, D, d_in = w_all.shape
    segs, src = _w_in_segments(D)
    small_segs = ((0, src["ff"], FORGET_COLS), (FORGET_COLS, src["ngt"], NSA_GATE_COLS))
    tr = 128
    return pl.pallas_call(
        functools.partial(_pack_w_in_kernel, segs=tuple(segs), small_segs=small_segs),
        out_shape=(jax.ShapeDtypeStruct((D, Z_BLOCKS * LANE), BF16), jax.ShapeDtypeStruct((D, LANE), F32)),
        grid=(D // tr,),
        in_specs=[pl.BlockSpec((1, tr, d_in), lambda i: (layer, i, 0))],
        out_specs=(pl.BlockSpec((tr, Z_BLOCKS * LANE), lambda i: (i, 0)), pl.BlockSpec((tr, LANE), lambda i: (i, 0))),
        compiler_params=_cparams(("parallel",)),
        name="pack_w_in",
    )(w_all)


def _pack_w_uq(w):
    K = w.shape[0]
    w3 = w.reshape(K, MLA_HEADS, MLA_NOPE + MLA_ROPE) * ((MLA_NOPE + MLA_ROPE) ** -0.5 * LOG2E)
    nope = w3[:, :, :MLA_NOPE].reshape(K, MLA_HEADS * MLA_NOPE)
    r = w3[:, :, MLA_NOPE:]
    half = MLA_ROPE // 2
    r_rot = jnp.concatenate([-r[:, :, half:], r[:, :, :half]], axis=2)
    padr = lambda a: jnp.pad(a, ((0, 0), (0, 0), (0, LANE - MLA_ROPE))).reshape(K, MLA_HEADS * LANE)
    return jnp.concatenate([nope, padr(r), padr(r_rot)], axis=1).astype(BF16)


def _pack_w_ukv(w):
    K = w.shape[0]
    w3 = w.reshape(K, MLA_HEADS, MLA_NOPE + MLA_DV)
    return jnp.concatenate([w3[:, :, :MLA_NOPE].reshape(K, -1), w3[:, :, MLA_NOPE:].reshape(K, -1)], axis=1).astype(BF16)


def _t5_bucket(dist):
    dist = jnp.maximum(dist, 0)
    exact = REL_BUCKETS // 2
    df = jnp.maximum(dist, 1).astype(F32)
    large = exact + (jnp.log(df / exact) / math.log(REL_MAX_DIST / exact) * (REL_BUCKETS - exact)).astype(jnp.int32)
    large = jnp.minimum(large, REL_BUCKETS - 1)
    return jnp.where(dist < exact, dist, large)


def _position_tables(S, rel_bias):
    t = NSA_T
    half = MLA_ROPE // 2
    inv = ROPE_THETA ** (-jnp.arange(half, dtype=F32) / half)
    ang = jnp.arange(S, dtype=F32)[:, None] * inv
    c, s = jnp.cos(ang), jnp.sin(ang)
    cos = _pad_cols(jnp.concatenate([c, c], axis=1), LANE)
    sin = _pad_cols(jnp.concatenate([s, s], axis=1), LANE)
    ncp = max(S // CMP_STRIDE, LANE)
    pos = jnp.arange(S)

    def bias_of(dist):
        onehot = jax.nn.one_hot(_t5_bucket(dist), REL_BUCKETS, dtype=F32)
        return jnp.einsum("...b,bh->h...", onehot, rel_bias, precision=lax.Precision.HIGHEST)

    bias_c = bias_of(pos[None, :] - (CMP_STRIDE * jnp.arange(ncp)[:, None] + CMP_BLOCK - 1))
    i = jnp.arange(t)
    bias_sw = jnp.stack([bias_of(k * t + i[:, None] - i[None, :]) for k in range(3)], axis=1)
    bias_sw = bias_sw.reshape(NSA_GROUPS, NSA_HPG, 3, t, t).transpose(0, 2, 4, 1, 3).reshape(NSA_GROUPS, 3, t, NSA_HPG * t)
    n_cmp = (S - CMP_BLOCK) // CMP_STRIDE + 1
    n_blk = S // SEL_BLOCK
    nbp = max(n_blk, LANE)
    cstart = CMP_STRIDE * jnp.arange(ncp)
    sstart = SEL_BLOCK * jnp.arange(nbp)
    ov = jnp.clip(jnp.minimum(cstart[:, None] + CMP_BLOCK, sstart[None, :] + SEL_BLOCK)
                  - jnp.maximum(cstart[:, None], sstart[None, :]), 0, None).astype(F32) / CMP_STRIDE
    ov = jnp.where((jnp.arange(ncp)[:, None] < n_cmp) & (jnp.arange(nbp)[None, :] < n_blk), ov, 0.0).astype(BF16)
    expand = ((pos[:, None] // SEL_BLOCK) == jnp.arange(nbp)[None, :]).astype(BF16)
    return cos, sin, bias_c, bias_sw, ov, expand


def _token_mixers(h, p, w_in_all, layer, tabs, B, S):
    T, D = h.shape
    cos, sin, bias_c, bias_sw, overlap, expand = tabs
    wz, w_small = _pack_w_in(w_in_all, layer)
    z, zs = input_projection(h, p["g_mix"], wz, w_small, tm=min(T, 1024), tn=1024)
    z3 = z.reshape(B, S, Z_BLOCKS * LANE)

    log_f = jax.nn.log_sigmoid(zs[:, :FORGET_COLS] + p["b_forget"].astype(F32)).reshape(B, S, FOX_HEADS)
    cum = jnp.cumsum(log_f, axis=1) * LOG2E
    o_fox = causal_attention(z3, CB_FQ, z3, CB_FK, z3, CB_FV, FOX_HEADS, cum=cum)

    q_nope, q_rope = mla_q_proj(z, p["g_cq"], _pack_w_uq(p["w_uq"]), cos, sin, S, tm=min(S, 512))
    kv = rms_matmul(z, CB_CKV * LANE // MLA_KV_RANK, p["g_ckv"], _pack_w_ukv(p["w_ukv"]), BF16, tm=min(T, 1024), tn=1024)
    kr = z3[:, :, CB_KR * LANE:(CB_KR + 1) * LANE].astype(F32)
    kr = kr[..., :MLA_ROPE] * cos[None, :, :MLA_ROPE] + kr[..., MLA_ROPE:] * sin[None, :, :MLA_ROPE]
    k_rope = jnp.pad(kr, ((0, 0), (0, 0), (0, LANE - MLA_ROPE))).astype(BF16)
    hw = MLA_HEADS * LANE
    kv3 = kv.reshape(B, S, 2 * hw)
    o_mla = causal_attention(q_nope.reshape(B, S, hw), 0, kv3, 0, kv3, MLA_HEADS, MLA_HEADS,
                             q2=q_rope.reshape(B, S, hw), k2=k_rope)

    G = NSA_GROUPS
    NC = S // CMP_STRIDE
    ncp = bias_c.shape[1]

    def compress_branch(cb, dp, d, pe, w1, w2):
        x = z3[:, :, cb * LANE: cb * LANE + G * dp].reshape(B, NC, CMP_STRIDE * G * dp)
        eye = jnp.eye(G, dtype=F32)
        w1p = jnp.pad(w1.reshape(CMP_BLOCK, d, d), ((0, 0), (0, dp - d), (0, dp - d)))
        w1g = jnp.einsum("lij,gh->lgihj", w1p, eye).reshape(CMP_BLOCK, G * dp, G * dp).astype(BF16)
        w1a = w1g[:CMP_STRIDE].reshape(CMP_STRIDE * G * dp, G * dp)
        w1b = w1g[CMP_STRIDE:].reshape(CMP_STRIDE * G * dp, G * dp)
        pe_g = jnp.tile(jnp.pad(pe, ((0, 0), (0, dp - d)))[:, None, :], (1, G, 1))
        pe2 = pe_g.reshape(2, CMP_STRIDE * G * dp).astype(BF16)
        w2p = jnp.pad(w2, ((0, dp - d), (0, dp - d)))
        w2g = jnp.einsum("ij,gh->gihj", w2p, eye).reshape(G * dp, G * dp).astype(BF16)
        out = compress(x, w1a, w1b, pe2, w2g)
        return jnp.pad(out, ((0, 0), (0, ncp - NC), (0, 0)))

    kc = compress_branch(CB_NKC, NSA_DKP, NSA_DK, p["pe_k"], p["w_cmp_k1"], p["w_cmp_k2"])
    vc = compress_branch(CB_NVC, NSA_DV, NSA_DV, p["pe_v"], p["w_cmp_v1"], p["w_cmp_v2"])
    gl = zs[:, FORGET_COLS:FORGET_COLS + NSA_GATE_COLS].reshape(B, S, G, NSA_HPG * 3)
    gl = jnp.transpose(gl, (0, 2, 1, 3))
    gl_cols = jnp.pad(jnp.swapaxes(gl, 2, 3), ((0, 0), (0, 0), (0, 16 - NSA_HPG * 3), (0, 0)))
    o_cmp, ind = nsa_cmp_select(z3, kc, vc, bias_c, overlap, gl_cols, min(N_SEL, S // SEL_BLOCK))
    o_nsa = nsa_sel_win(z3, ind, expand, bias_sw, gl_cols, o_cmp)

    merged = merge_branches(o_fox.reshape(T, -1), o_mla.reshape(T, -1), o_nsa.reshape(T, -1),
                            p["w_branch"].astype(BF16), z, tm=min(T, 1024), tn=512)
    return matmul_residual(merged, p["w_out"].astype(BF16), h, tm=min(T, 1024), tn=512)


def _memory_block(h, mem2, p, B, S):
    T, D = h.shape
    kv = rms_matmul(mem2, 0, p["g_mem_kv"], p["w_mem_kv"].astype(BF16), BF16, tm=min(mem2.shape[0], 512), tn=512)
    w_r = _pad_cols(jnp.concatenate([p["w_router_group"], p["w_router_expert"]], axis=1), LANE).astype(F32)
    b_r = _pad_cols(jnp.concatenate([p["b_router_group"], p["b_router_expert"]])[None, :], LANE).astype(F32)
    out, logits, u = memory_attention_block(
        h.reshape(B, S, D), p["g_mem_q"], (p["w_mem_q"] * (MEM_DH ** -0.5 * LOG2E)).astype(BF16),
        kv.reshape(B, -1, kv.shape[1]), p["w_mem_o"].astype(BF16), p["g_moe"], w_r, b_r, tq=min(S, 512))
    return out.reshape(T, D), logits.reshape(T, -1), u.reshape(T, D)


def _moe_block(h, logits, u, experts, layer):
    T, D = h.shape
    tm = MOE_TM
    glog = logits[:, :N_GROUPS]
    gsel = jnp.argmax(glog, axis=-1).astype(jnp.int32)
    pg = jnp.max(jax.nn.softmax(glog, axis=-1), axis=-1, keepdims=True)
    elog = logits[:, N_GROUPS:N_GROUPS + N_EXPERTS].reshape(T, N_GROUPS, EXPERTS_PER_GROUP)
    elog = jnp.take_along_axis(elog, gsel[:, None, None], axis=1)[:, 0]
    eprob = jax.nn.softmax(elog, axis=-1)
    j0 = jnp.argmax(eprob, axis=-1).astype(jnp.int32)
    lane = jnp.arange(EXPERTS_PER_GROUP, dtype=jnp.int32)[None, :]
    j1 = jnp.argmax(jnp.where(lane == j0[:, None], -jnp.inf, eprob), axis=-1).astype(jnp.int32)
    top_j = jnp.stack([j0, j1], axis=-1)
    top_p = jnp.take_along_axis(eprob, top_j, axis=-1)
    top_p = top_p / jnp.sum(top_p, axis=-1, keepdims=True)
    weight = pg * top_p
    flat_e = (gsel[:, None] * EXPERTS_PER_GROUP + top_j.astype(jnp.int32)).reshape(-1)
    TK = T * TOP_K
    onehot = (flat_e[:, None] == jnp.arange(N_EXPERTS, dtype=jnp.int32)[None, :]).astype(jnp.int32)
    rank = jnp.sum((jnp.cumsum(onehot, axis=0) - onehot) * onehot, axis=1)
    counts = jnp.sum(onehot, axis=0)
    pcounts = ((counts + tm - 1) // tm) * tm
    pends = jnp.cumsum(pcounts)
    dest = (pends - pcounts)[flat_e] + rank
    P = TK + N_EXPERTS * tm
    n_rb = P // tm
    row_tok = (jnp.arange(P, dtype=jnp.int32) % T).at[dest].set(jnp.repeat(jnp.arange(T, dtype=jnp.int32), TOP_K))
    blk_e = jnp.sum((pends[None, :] <= (jnp.arange(n_rb, dtype=jnp.int32) * tm)[:, None]).astype(jnp.int32), axis=1)
    blk_e = jnp.minimum(blk_e, N_EXPERTS - 1).astype(jnp.int32)
    n_used = (pends[-1] // tm).astype(jnp.int32).reshape(1)
    xr = u[row_tok]
    y = moe_experts(blk_e, n_used, xr, experts[0].astype(F32), experts[1].astype(F32), experts[2].astype(F32), layer)
    d2 = dest.reshape(T, TOP_K)
    return h + (weight[:, 0:1] * y[d2[:, 0]].astype(F32) + weight[:, 1:2] * y[d2[:, 1]].astype(F32))


_LAYER_KEYS = ("g_mix", "w_in", "b_forget", "g_cq", "g_ckv", "w_uq", "w_ukv", "pe_k", "pe_v", "w_cmp_k1", "w_cmp_k2",
               "w_cmp_v1", "w_cmp_v2", "w_branch", "w_out", "g_mem_q", "g_mem_kv", "w_mem_q", "w_mem_kv", "w_mem_o",
               "g_moe", "w_router_group", "b_router_group", "w_router_expert", "b_router_expert")


def kernel(x, mem, g_mix, w_in, b_forget, g_cq, g_ckv, w_uq, w_ukv, pe_k, pe_v, w_cmp_k1, w_cmp_k2, w_cmp_v1, w_cmp_v2, rel_bias, w_branch, w_out, g_mem_q, g_mem_kv, w_mem_q, w_mem_kv, w_mem_o, g_moe, w_router_group, b_router_group, w_router_expert, b_router_expert, w_exp_gate, w_exp_up, w_exp_down, g_final):
    B, S, D = x.shape
    T = B * S
    stacked = dict(g_mix=g_mix, w_in=w_in, b_forget=b_forget, g_cq=g_cq, g_ckv=g_ckv, w_uq=w_uq, w_ukv=w_ukv,
                   pe_k=pe_k, pe_v=pe_v, w_cmp_k1=w_cmp_k1, w_cmp_k2=w_cmp_k2, w_cmp_v1=w_cmp_v1, w_cmp_v2=w_cmp_v2,
                   w_branch=w_branch, w_out=w_out, g_mem_q=g_mem_q, g_mem_kv=g_mem_kv, w_mem_q=w_mem_q,
                   w_mem_kv=w_mem_kv, w_mem_o=w_mem_o, g_moe=g_moe, w_router_group=w_router_group,
                   b_router_group=b_router_group, w_router_expert=w_router_expert, b_router_expert=b_router_expert,
                   w_exp_gate=w_exp_gate, w_exp_up=w_exp_up, w_exp_down=w_exp_down)
    tabs = _position_tables(S, rel_bias.astype(F32) * LOG2E)
    h = x.reshape(T, D).astype(F32)
    mem2 = mem.reshape(-1, D).astype(F32)
    for l in range(w_in.shape[0]):
        p = {k: stacked[k][l] for k in _LAYER_KEYS}
        h = _token_mixers(h, p, w_in.astype(F32), l, tabs, B, S)
        h, logits, u = _memory_block(h, mem2, p, B, S)
        h = _moe_block(h, logits, u, (w_exp_gate, w_exp_up, w_exp_down), l)
    return rmsnorm_rows(h, g_final, tm=min(T, 512)).reshape(B, S, D)
```

```python
import functools
import math

import jax
import jax.numpy as jnp
import numpy as np
from jax import lax
from jax.experimental import pallas as pl
from jax.experimental.pallas import tpu as pltpu

F32 = jnp.float32
BF16 = jnp.bfloat16

EPS = 1e-6
NEG_INF = -1e30
LOG2E = math.log2(math.e)
LANE = 128

FOX_HEADS, FOX_DH = 8, 128
MLA_HEADS, MLA_NOPE, MLA_ROPE, MLA_DV = 8, 128, 64, 128
MLA_Q_RANK, MLA_KV_RANK = 768, 512
ROPE_THETA = 10000.0
NSA_HEADS, NSA_GROUPS, NSA_DK, NSA_DV = 8, 2, 192, 128
NSA_HPG = NSA_HEADS // NSA_GROUPS
NSA_DKP = 256
CMP_BLOCK, CMP_STRIDE, SEL_BLOCK, N_SEL, WINDOW = 32, 16, 64, 8, 512
SEL_FORCE = 1e6
REL_BUCKETS, REL_MAX_DIST = 32, 128
N_BRANCH, BRANCH_W = 3, 1024
MEM_HEADS, MEM_DH = 4, 128
N_GROUPS, EXPERTS_PER_GROUP, TOP_K = 4, 8, 2
N_EXPERTS = N_GROUPS * EXPERTS_PER_GROUP
FORGET_COLS, NSA_GATE_COLS = FOX_HEADS, NSA_HEADS * 3

CB_CKV, CB_KR, CB_CQ = 0, 4, 6
CB_NKC, CB_NQ = 12, 16
CB_FQ, CB_FK, CB_FV = 32, 40, 48
CB_NKS, CB_NKW = 56, 60
CB_NVC, CB_NVS, CB_NVW = 64, 66, 68
CB_MG = 72
Z_BLOCKS = 120

ATT_T = 256
ATT_HG = 8
NSA_T = 256
NSA_CMP_T = 512
MOE_TM = 512
MOE_DC = 512
VMEM_LIMIT = 56 * 1024 * 1024


def _cparams(sem):
    return pltpu.CompilerParams(dimension_semantics=sem, vmem_limit_bytes=VMEM_LIMIT)


def _sigmoid(x):
    return 1.0 / (1.0 + jnp.exp(-x))


def _rms_mm_kernel(x_ref, g_ref, w_ref, o_ref, u_ref):
    @pl.when(pl.program_id(1) == 0)
    def _():
        x = x_ref[...].astype(F32)
        ms = jnp.mean(x * x, axis=-1, keepdims=True)
        u_ref[...] = (x * lax.rsqrt(ms + EPS) * g_ref[...]).astype(u_ref.dtype)

    o_ref[...] = jnp.dot(u_ref[...], w_ref[...], preferred_element_type=F32).astype(o_ref.dtype)


def rms_matmul(x, xcol, g, w, out_dtype, tm, tn):
    T = x.shape[0]
    K, N = w.shape
    return pl.pallas_call(
        _rms_mm_kernel,
        out_shape=jax.ShapeDtypeStruct((T, N), out_dtype),
        grid=(T // tm, N // tn),
        in_specs=[pl.BlockSpec((tm, K), lambda i, j: (i, xcol)),
                  pl.BlockSpec((1, K), lambda i, j: (0, 0)),
                  pl.BlockSpec((K, tn), lambda i, j: (0, j))],
        out_specs=pl.BlockSpec((tm, tn), lambda i, j: (i, j)),
        scratch_shapes=[pltpu.VMEM((tm, K), BF16)],
        compiler_params=_cparams(("parallel", "arbitrary")),
        name="rms_matmul",
    )(x, g.reshape(1, K).astype(F32), w)


def _in_proj_kernel(x_ref, g_ref, w_ref, ws_ref, o_ref, os_ref, u_ref):
    @pl.when(pl.program_id(1) == 0)
    def _():
        x = x_ref[...]
        ms = jnp.mean(x * x, axis=-1, keepdims=True)
        u = x * lax.rsqrt(ms + EPS) * g_ref[...]
        u_ref[...] = u.astype(u_ref.dtype)
        os_ref[...] = _dot3(u, ws_ref[...])

    o_ref[...] = jnp.dot(u_ref[...], w_ref[...], preferred_element_type=F32).astype(o_ref.dtype)


def input_projection(x, g, w, w_small, tm, tn):
    T, K = x.shape
    N = w.shape[1]
    Ns = w_small.shape[1]
    return pl.pallas_call(
        _in_proj_kernel,
        out_shape=(jax.ShapeDtypeStruct((T, N), BF16), jax.ShapeDtypeStruct((T, Ns), F32)),
        grid=(T // tm, N // tn),
        in_specs=[pl.BlockSpec((tm, K), lambda i, j: (i, 0)),
                  pl.BlockSpec((1, K), lambda i, j: (0, 0)),
                  pl.BlockSpec((K, tn), lambda i, j: (0, j)),
                  pl.BlockSpec((K, Ns), lambda i, j: (0, 0))],
        out_specs=(pl.BlockSpec((tm, tn), lambda i, j: (i, j)), pl.BlockSpec((tm, Ns), lambda i, j: (i, 0))),
        scratch_shapes=[pltpu.VMEM((tm, K), BF16)],
        compiler_params=_cparams(("parallel", "arbitrary")),
        name="input_projection",
    )(x, g.reshape(1, K).astype(F32), w, w_small)


def _mla_q_kernel(x_ref, g_ref, w_ref, cos_ref, sin_ref, qn_ref, qr_ref):
    x = x_ref[...].astype(F32)
    ms = jnp.mean(x * x, axis=-1, keepdims=True)
    u = (x * lax.rsqrt(ms + EPS) * g_ref[...]).astype(BF16)
    y = jnp.dot(u, w_ref[...], preferred_element_type=F32)
    hw = MLA_HEADS * LANE
    qn_ref[...] = y[:, :hw].astype(qn_ref.dtype)
    cos = cos_ref[...]
    sin = sin_ref[...]
    for h in range(MLA_HEADS):
        a = y[:, hw + h * LANE: hw + (h + 1) * LANE]
        b = y[:, 2 * hw + h * LANE: 2 * hw + (h + 1) * LANE]
        qr_ref[:, h * LANE:(h + 1) * LANE] = (a * cos + b * sin).astype(qr_ref.dtype)


def mla_q_proj(z, g, w, cos, sin, S, tm):
    T = z.shape[0]
    K, N = w.shape
    hw = MLA_HEADS * LANE
    nsb = S // tm
    return pl.pallas_call(
        _mla_q_kernel,
        out_shape=(jax.ShapeDtypeStruct((T, hw), BF16), jax.ShapeDtypeStruct((T, hw), BF16)),
        grid=(T // tm,),
        in_specs=[pl.BlockSpec((tm, K), lambda i: (i, CB_CQ * LANE // MLA_Q_RANK)),
                  pl.BlockSpec((1, K), lambda i: (0, 0)),
                  pl.BlockSpec((K, N), lambda i: (0, 0)),
                  pl.BlockSpec((tm, LANE), lambda i: (i % nsb, 0)),
                  pl.BlockSpec((tm, LANE), lambda i: (i % nsb, 0))],
        out_specs=(pl.BlockSpec((tm, hw), lambda i: (i, 0)), pl.BlockSpec((tm, hw), lambda i: (i, 0))),
        compiler_params=_cparams(("parallel",)),
        name="mla_q_proj",
    )(z, g.reshape(1, K).astype(F32), w, cos, sin)


def _causal_attn_kernel(*refs, t, hg, two_part, decay):
    refs = list(refs)
    q_ref, k_ref, v_ref = refs[:3]
    pos = 3
    if two_part:
        q2_ref, k2_ref = refs[pos:pos + 2]
        pos += 2
    if decay:
        ck_ref = refs[pos]
        pos += 1
    o_ref = refs[pos]
    if two_part:
        kcat_ref = refs[pos + 1]
    qi = pl.program_id(2)
    dn = (((1,), (1,)), ((), ()))

    if two_part:
        @pl.when(qi == 0)
        def _():
            for j in range(hg):
                kcat_ref[j, :, :LANE] = k_ref[0, :, j * LANE:(j + 1) * LANE]
                kcat_ref[j, :, LANE:] = k2_ref[0]

    qs = []
    for j in range(hg):
        qj = q_ref[0, :, j * LANE:(j + 1) * LANE]
        if two_part:
            qj = jnp.concatenate([qj, q2_ref[0, :, j * LANE:(j + 1) * LANE]], axis=1)
        qs.append(qj)

    def step(kb, carry, masked, width=1):
        off = pl.multiple_of(kb * t, t)
        tk = width * t
        heads = range(hg)
        ss = []
        for j in heads:
            k = kcat_ref[j, pl.ds(off, tk), :] if two_part else k_ref[0, pl.ds(off, tk), j * LANE:(j + 1) * LANE]
            ss.append(lax.dot_general(k, qs[j], dn, preferred_element_type=F32))
        if decay:
            ss = [ss[j] - ck_ref[0, 0, pl.ds(off, tk), j:j + 1] for j in heads]
        if masked:
            r = lax.broadcasted_iota(jnp.int32, (tk, t), 0)
            c = lax.broadcasted_iota(jnp.int32, (tk, t), 1)
            ss = [jnp.where(r <= c, s, NEG_INF) for s in ss]
        ms = [jnp.maximum(carry[j][0], jnp.max(ss[j], axis=0, keepdims=True)) for j in heads]
        ps = [jnp.exp2(ss[j] - ms[j]) for j in heads]
        out = []
        for j in heads:
            m, l, acc = carry[j]
            a = jnp.exp2(m - ms[j])
            l = a * l + jnp.sum(ps[j], axis=0, keepdims=True)
            v = v_ref[0, pl.ds(off, tk), j * LANE:(j + 1) * LANE]
            acc = a * acc + lax.dot_general(v, ps[j].astype(BF16), (((0,), (0,)), ((), ())),
                                            preferred_element_type=F32)
            out.append((ms[j], l, acc))
        return tuple(out)

    init = tuple((jnp.full((1, t), NEG_INF, F32), jnp.zeros((1, t), F32), jnp.zeros((LANE, t), F32))
                 for _ in range(hg))
    carry = lax.fori_loop(0, qi // 2, lambda kp, c: step(2 * kp, c, False, width=2), init)
    carry = lax.cond(qi % 2 == 1, lambda c: step(qi - 1, c, False), lambda c: c, carry)
    carry = step(qi, carry, True)
    for j in range(hg):
        _, l, acc = carry[j]
        o_ref[0, :, j * LANE:(j + 1) * LANE] = (acc / l).T.astype(o_ref.dtype)


def causal_attention(q, qcb, k, kcb, v, vcb, heads, q2=None, k2=None, cum=None):
    B, S, _ = q.shape
    t, hg = ATT_T, ATT_HG
    w = hg * LANE
    two_part, decay = q2 is not None, cum is not None
    in_specs = [pl.BlockSpec((1, t, w), lambda b, h, i: (b, i, qcb // hg + h)),
                pl.BlockSpec((1, S, w), lambda b, h, i: (b, 0, kcb // hg + h)),
                pl.BlockSpec((1, S, w), lambda b, h, i: (b, 0, vcb // hg + h))]
    args = [q, k, v]
    scratch = []
    if two_part:
        in_specs += [pl.BlockSpec((1, t, w), lambda b, h, i: (b, i, h)),
                     pl.BlockSpec((1, S, LANE), lambda b, h, i: (b, 0, 0))]
        args += [q2, k2]
        scratch = [pltpu.VMEM((hg, S, 2 * LANE), BF16)]
    if decay:
        in_specs += [pl.BlockSpec((1, 1, S, hg), lambda b, h, i: (b, h, 0, 0))]
        args += [jnp.transpose(cum.reshape(B, S, heads // hg, hg), (0, 2, 1, 3))]
    return pl.pallas_call(
        functools.partial(_causal_attn_kernel, t=t, hg=hg, two_part=two_part, decay=decay),
        out_shape=jax.ShapeDtypeStruct((B, S, heads * LANE), BF16),
        grid=(B, heads // hg, S // t),
        in_specs=in_specs,
        out_specs=pl.BlockSpec((1, t, w), lambda b, h, i: (b, i, h)),
        scratch_shapes=scratch,
        compiler_params=_cparams(("parallel", "parallel", "arbitrary")),
        name="causal_attention",
    )(*args)


def _gelu_tanh(x):
    return 0.5 * x * (1.0 + jnp.tanh(math.sqrt(2.0 / math.pi) * (x + 0.044715 * (x * x * x))))


def _compress_kernel(x_ref, w1a_ref, w1b_ref, pe_ref, w2_ref, o_ref):
    x = x_ref[0]
    a = jnp.dot(x, w1a_ref[...], preferred_element_type=F32)
    b = jnp.dot(x, w1b_ref[...], preferred_element_type=F32)
    nc = a.shape[0]
    b_next = pltpu.roll(b, nc - 1, 0)
    pe_term = jnp.dot(pe_ref[0:1, :], w1a_ref[...], preferred_element_type=F32) + \
        jnp.dot(pe_ref[1:2, :], w1b_ref[...], preferred_element_type=F32)
    hid = _gelu_tanh(a + b_next + pe_term)
    o_ref[0] = jnp.dot(hid.astype(BF16), w2_ref[...], preferred_element_type=F32).astype(o_ref.dtype)


def compress(x, w1a, w1b, pe2, w2):
    BG, NC, KD = x.shape
    dp = w2.shape[1]
    return pl.pallas_call(
        _compress_kernel,
        out_shape=jax.ShapeDtypeStruct((BG, NC, dp), BF16),
        grid=(BG,),
        in_specs=[pl.BlockSpec((1, NC, KD), lambda i: (i, 0, 0)),
                  pl.BlockSpec((KD, dp), lambda i: (0, 0)),
                  pl.BlockSpec((KD, dp), lambda i: (0, 0)),
                  pl.BlockSpec((2, KD), lambda i: (0, 0)),
                  pl.BlockSpec((dp, dp), lambda i: (0, 0))],
        out_specs=pl.BlockSpec((1, NC, dp), lambda i: (i, 0, 0)),
        compiler_params=_cparams(("parallel",)),
        name="nsa_compress",
    )(x, w1a, w1b, pe2, w2)


def _nsa_cmp_kernel(q_ref, kc_ref, vc_ref, bias_ref, ov_ref, gate_ref, o_ref, ind_ref, *, t, n_sel, n_rows):
    qi = pl.program_id(2)
    ncp = kc_ref.shape[1]
    kc = kc_ref[0]
    vc = vc_ref[0]
    pos = lax.broadcasted_iota(jnp.int32, (ncp, t), 1) + qi * t
    cblk = lax.broadcasted_iota(jnp.int32, (ncp, t), 0)
    valid = pos >= CMP_STRIDE * cblk + (CMP_BLOCK - 1)
    dn = (((1,), (1,)), ((), ()))
    tn = (((0,), (0,)), ((), ()))
    heads = range(NSA_HPG)
    ss = [lax.dot_general(kc, q_ref[0, :, h * NSA_DKP:(h + 1) * NSA_DKP], dn, preferred_element_type=F32)
          for h in heads]
    ss = [jnp.where(valid, ss[h] + bias_ref[h], NEG_INF) for h in heads]
    es = [jnp.exp2(ss[h] - jnp.max(ss[h], axis=0, keepdims=True)) for h in heads]
    ps = [jnp.where(valid, es[h] / jnp.sum(es[h], axis=0, keepdims=True), 0.0) for h in heads]
    gates = _sigmoid(gate_ref[0, 0])
    for h in heads:
        o = lax.dot_general(vc, ps[h].astype(BF16), tn, preferred_element_type=F32)
        o_ref[0, :, h * NSA_DV:(h + 1) * NSA_DV] = (gates[3 * h:3 * h + 1, :] * o).T.astype(o_ref.dtype)
    psum = functools.reduce(lambda x, y: x + y, ps)
    p_hi = psum.astype(BF16)
    p_lo = (psum - p_hi.astype(F32)).astype(BF16)
    imp = lax.dot_general(ov_ref[...], p_hi, tn, preferred_element_type=F32) + \
        lax.dot_general(ov_ref[...], p_lo, tn, preferred_element_type=F32)
    nbp = imp.shape[0]
    imp = imp[:n_rows]
    blk = lax.broadcasted_iota(jnp.int32, (n_rows, t), 0)
    cur = (lax.broadcasted_iota(jnp.int32, (n_rows, t), 1) + qi * t) // SEL_BLOCK
    forced = (blk == 0) | (blk == cur) | (blk == cur - 1)
    score = jnp.where(blk <= cur, imp + jnp.where(forced, SEL_FORCE, 0.0), NEG_INF)
    sel = jnp.zeros((n_rows, t), F32)
    for _ in range(n_sel):
        mx = jnp.max(score, axis=0, keepdims=True)
        first = jnp.min(jnp.where(score == mx, blk, nbp), axis=0, keepdims=True)
        hit = blk == first
        sel = jnp.where(hit, 1.0, sel)
        score = jnp.where(hit, -jnp.inf, score)
    ind_ref[0, 0] = jnp.zeros(ind_ref.shape[2:], ind_ref.dtype)
    ind_ref[0, 0, :n_rows, :] = sel.astype(ind_ref.dtype)


def nsa_cmp_select(z3, kc, vc, bias_c, overlap, gates, n_sel):
    B, S, _ = z3.shape
    G = NSA_GROUPS
    t = min(NSA_CMP_T, S)
    ncp = kc.shape[1]
    nbp = overlap.shape[1]
    qw = NSA_HPG * NSA_DKP
    ow = NSA_HPG * NSA_DV
    n_rows = min(nbp, -(-(S // SEL_BLOCK) // 16) * 16)
    return pl.pallas_call(
        functools.partial(_nsa_cmp_kernel, t=t, n_sel=n_sel, n_rows=n_rows),
        out_shape=(jax.ShapeDtypeStruct((B, S, G * ow), BF16), jax.ShapeDtypeStruct((B, G, nbp, S), BF16)),
        grid=(B, G, S // t),
        in_specs=[pl.BlockSpec((1, t, qw), lambda b, g, i: (b, i, CB_NQ * LANE // qw + g)),
                  pl.BlockSpec((1, ncp, NSA_DKP), lambda b, g, i: (b, 0, g)),
                  pl.BlockSpec((1, ncp, NSA_DV), lambda b, g, i: (b, 0, g)),
                  pl.BlockSpec((NSA_HPG, ncp, t), lambda b, g, i: (g, 0, i)),
                  pl.BlockSpec((ncp, nbp), lambda b, g, i: (0, 0)),
                  pl.BlockSpec((1, 1, gates.shape[2], t), lambda b, g, i: (b, g, 0, i))],
        out_specs=(pl.BlockSpec((1, t, ow), lambda b, g, i: (b, i, g)),
                   pl.BlockSpec((1, 1, nbp, t), lambda b, g, i: (b, g, 0, i))),
        compiler_params=_cparams(("parallel", "parallel", "arbitrary")),
        name="nsa_cmp_select",
    )(z3, kc, vc, bias_c, overlap, gates)


def _nsa_sw_kernel(q_ref, ks_ref, vs_ref, kw_ref, vw_ref, ind_ref, e_ref, bias_ref, gate_ref, oc_ref, o_ref,
                   *, t):
    qi = pl.program_id(1)
    hp, G = NSA_HPG, NSA_GROUPS
    q4 = [jnp.concatenate([q_ref[0, :, (g * hp + h) * NSA_DKP:(g * hp + h + 1) * NSA_DKP] for h in range(hp)], axis=0)
          for g in range(G)]
    inds = [ind_ref[0, g] for g in range(G)]
    dn = (((1,), (1,)), ((), ()))

    def step(kb, carry, k_ref, v_ref, selected, width=1):
        off = pl.multiple_of(kb * t, t)
        tk = width * t
        ri = lax.broadcasted_iota(jnp.int32, (tk, t), 0)
        ci = lax.broadcasted_iota(jnp.int32, (tk, t), 1)
        d = (qi - kb) * t + ci - ri
        near = (d >= 0) if selected else (d >= 0) & (d < WINDOW)
        groups = range(G)
        ss = [lax.dot_general(k_ref[0, pl.ds(off, tk), g * NSA_DKP:(g + 1) * NSA_DKP], q4[g], dn,
                              preferred_element_type=F32) for g in groups]
        negs = []
        for g in groups:
            mask = near
            if selected:
                hit = jnp.dot(e_ref[pl.ds(off, tk), :], inds[g], preferred_element_type=F32)
                mask = near & (hit > 0.5)
            neg = jnp.where(mask, 0.0, NEG_INF)
            negs.append(jnp.concatenate([neg] * hp, axis=1))
        bias = [jnp.concatenate([bias_ref[g, jnp.minimum(qi - kb - w, 2)] for w in range(width)], axis=0)
                if width > 1 else bias_ref[g, jnp.minimum(qi - kb, 2)] for g in groups]
        ss = [ss[g] + bias[g] + negs[g] for g in groups]
        ms = [jnp.maximum(carry[g][0], jnp.max(ss[g], axis=0, keepdims=True)) for g in groups]
        ps = [jnp.exp2(ss[g] - ms[g]) for g in groups]
        out = []
        for g in groups:
            m, l, acc = carry[g]
            a = jnp.exp2(m - ms[g])
            l = a * l + jnp.sum(ps[g], axis=0, keepdims=True)
            v = v_ref[0, pl.ds(off, tk), g * NSA_DV:(g + 1) * NSA_DV]
            acc = a * acc + lax.dot_general(v, ps[g].astype(BF16), (((0,), (0,)), ((), ())),
                                            preferred_element_type=F32)
            out.append((ms[g], l, acc))
        return tuple(out)

    init = tuple((jnp.full((1, hp * t), NEG_INF, F32), jnp.zeros((1, hp * t), F32), jnp.zeros((NSA_DV, hp * t), F32))
                 for _ in range(G))
    sel = lax.fori_loop(0, (qi + 1) // 2, lambda kp, c: step(2 * kp, c, ks_ref, vs_ref, True, width=2), init)
    sel = lax.cond(qi % 2 == 0, lambda c: step(qi, c, ks_ref, vs_ref, True), lambda c: c, sel)
    lo = jnp.maximum(qi - WINDOW // t, 0)
    win = lax.fori_loop(lo, qi + 1, lambda kb, c: step(kb, c, kw_ref, vw_ref, False), init)
    for g in range(G):
        o_s = sel[g][2] / sel[g][1]
        o_w = win[g][2] / win[g][1]
        gates = _sigmoid(gate_ref[0, g])
        for h in range(hp):
            r = slice(h * t, (h + 1) * t)
            c = slice((g * hp + h) * NSA_DV, (g * hp + h + 1) * NSA_DV)
            o = gates[3 * h + 1:3 * h + 2, :] * o_s[:, r] + gates[3 * h + 2:3 * h + 3, :] * o_w[:, r]
            o_ref[0, :, c] = (oc_ref[0, :, c].astype(F32) + o.T).astype(o_ref.dtype)


def nsa_sel_win(z3, ind, expand, bias_sw, gates, o_cmp):
    B, S, _ = z3.shape
    G = NSA_GROUPS
    t = NSA_T
    qw = NSA_HEADS * NSA_DKP
    ow = NSA_HEADS * NSA_DV
    kw = G * NSA_DKP
    vw = G * NSA_DV
    nbp = ind.shape[2]
    kspec = lambda cb: pl.BlockSpec((1, S, kw), lambda b, i: (b, 0, cb * LANE // kw))
    vspec = lambda cb: pl.BlockSpec((1, S, vw), lambda b, i: (b, 0, cb * LANE // vw))
    return pl.pallas_call(
        functools.partial(_nsa_sw_kernel, t=t),
        out_shape=jax.ShapeDtypeStruct((B, S, ow), BF16),
        grid=(B, S // t),
        in_specs=[pl.BlockSpec((1, t, qw), lambda b, i: (b, i, CB_NQ * LANE // qw)),
                  kspec(CB_NKS), vspec(CB_NVS), kspec(CB_NKW), vspec(CB_NVW),
                  pl.BlockSpec((1, G, nbp, t), lambda b, i: (b, 0, 0, i)),
                  pl.BlockSpec((S, nbp), lambda b, i: (0, 0)),
                  pl.BlockSpec((G, 3, t, NSA_HPG * t), lambda b, i: (0, 0, 0, 0)),
                  pl.BlockSpec((1, G, gates.shape[2], t), lambda b, i: (b, 0, 0, i)),
                  pl.BlockSpec((1, t, ow), lambda b, i: (b, i, 0))],
        out_specs=pl.BlockSpec((1, t, ow), lambda b, i: (b, i, 0)),
        compiler_params=_cparams(("parallel", "arbitrary")),
        name="nsa_sel_win",
    )(z3, z3, z3, z3, z3, ind, expand, bias_sw, gates, o_cmp)


def _merge_kernel(of_ref, om_ref, on_ref, wb_ref, g0_ref, g1_ref, g2_ref, o_ref):
    acc = None
    for n, (o_r, g_r) in enumerate(((of_ref, g0_ref), (om_ref, g1_ref), (on_ref, g2_ref))):
        y = jnp.dot(o_r[...], wb_ref[n], preferred_element_type=F32)
        y = _sigmoid(g_r[...].astype(F32)) * y
        acc = y if acc is None else acc + y
    o_ref[...] = acc.astype(o_ref.dtype)


def merge_branches(o_fox, o_mla, o_nsa, wb, z, tm, tn):
    T = o_fox.shape[0]
    D = wb.shape[2]
    gspec = lambda n: pl.BlockSpec((tm, tn), lambda i, j: (i, (CB_MG * LANE + n * D) // tn + j))
    ospec = pl.BlockSpec((tm, BRANCH_W), lambda i, j: (i, 0))
    return pl.pallas_call(
        _merge_kernel,
        out_shape=jax.ShapeDtypeStruct((T, D), BF16),
        grid=(T // tm, D // tn),
        in_specs=[ospec, ospec, ospec,
                  pl.BlockSpec((N_BRANCH, BRANCH_W, tn), lambda i, j: (0, 0, j)),
                  gspec(0), gspec(1), gspec(2)],
        out_specs=pl.BlockSpec((tm, tn), lambda i, j: (i, j)),
        compiler_params=_cparams(("parallel", "arbitrary")),
        name="merge_branches",
    )(o_fox, o_mla, o_nsa, wb, z, z, z)


def _mm_res_kernel(a_ref, w_ref, r_ref, o_ref):
    o_ref[...] = r_ref[...] + jnp.dot(a_ref[...], w_ref[...], preferred_element_type=F32)


def matmul_residual(a, w, res, tm, tn):
    T, K = a.shape
    N = w.shape[1]
    return pl.pallas_call(
        _mm_res_kernel,
        out_shape=jax.ShapeDtypeStruct((T, N), F32),
        grid=(T // tm, N // tn),
        in_specs=[pl.BlockSpec((tm, K), lambda i, j: (i, 0)),
                  pl.BlockSpec((K, tn), lambda i, j: (0, j)),
                  pl.BlockSpec((tm, tn), lambda i, j: (i, j))],
        out_specs=pl.BlockSpec((tm, tn), lambda i, j: (i, j)),
        compiler_params=_cparams(("parallel", "arbitrary")),
        name="matmul_residual",
    )(a, w, res)


def _mem_attn_kernel(h_ref, g_ref, wq_ref, kv_ref, wo_ref, o_ref):
    x = h_ref[0]
    ms = jnp.mean(x * x, axis=-1, keepdims=True)
    u = (x * lax.rsqrt(ms + EPS) * g_ref[...]).astype(BF16)
    q = jnp.dot(u, wq_ref[...], preferred_element_type=F32).astype(BF16)
    dn = (((1,), (1,)), ((), ()))
    hw = MEM_HEADS * MEM_DH
    heads = range(MEM_HEADS)
    cs = [slice(h * MEM_DH, (h + 1) * MEM_DH) for h in heads]
    ss = [lax.dot_general(q[:, cs[h]], kv_ref[0, :, cs[h]], dn, preferred_element_type=F32) for h in heads]
    es = [jnp.exp2(ss[h] - jnp.max(ss[h], axis=-1, keepdims=True)) for h in heads]
    ps = [es[h] / jnp.sum(es[h], axis=-1, keepdims=True) for h in heads]
    outs = [jnp.dot(ps[h].astype(BF16), kv_ref[0, :, hw + h * MEM_DH: hw + (h + 1) * MEM_DH],
                    preferred_element_type=F32).astype(BF16) for h in heads]
    o = jnp.concatenate(outs, axis=1)
    o_ref[0] = x + jnp.dot(o, wo_ref[...], preferred_element_type=F32)


def memory_attention_block(h3, g, wq, kv, wo, tq):
    B, S, D = h3.shape
    M = kv.shape[1]
    hw = MEM_HEADS * MEM_DH
    return pl.pallas_call(
        _mem_attn_kernel,
        out_shape=jax.ShapeDtypeStruct((B, S, D), F32),
        grid=(B, S // tq),
        in_specs=[pl.BlockSpec((1, tq, D), lambda b, i: (b, i, 0)),
                  pl.BlockSpec((1, D), lambda b, i: (0, 0)),
                  pl.BlockSpec((D, hw), lambda b, i: (0, 0)),
                  pl.BlockSpec((1, M, 2 * hw), lambda b, i: (b, 0, 0)),
                  pl.BlockSpec((hw, D), lambda b, i: (0, 0))],
        out_specs=pl.BlockSpec((1, tq, D), lambda b, i: (b, i, 0)),
        compiler_params=_cparams(("parallel", "arbitrary")),
        name="memory_attention",
    )(h3, g.reshape(1, D).astype(F32), wq, kv, wo)


def _dot3(a, b):
    a_hi = a.astype(BF16)
    a_lo = (a - a_hi.astype(F32)).astype(BF16)
    b_hi = b.astype(BF16)
    b_lo = (b - b_hi.astype(F32)).astype(BF16)
    dot = functools.partial(jnp.dot, preferred_element_type=F32)
    return dot(a_hi, b_hi) + (dot(a_lo, b_hi) + dot(a_hi, b_lo))


def _router_kernel(h_ref, g_ref, w_ref, b_ref, lg_ref, u_ref):
    x = h_ref[...]
    ms = jnp.mean(x * x, axis=-1, keepdims=True)
    u = x * lax.rsqrt(ms + EPS) * g_ref[...]
    u_ref[...] = u.astype(u_ref.dtype)
    lg_ref[...] = _dot3(u, w_ref[...]) + b_ref[...]


def moe_router(h, g, w, b, tm):
    T, D = h.shape
    N = w.shape[1]
    return pl.pallas_call(
        _router_kernel,
        out_shape=(jax.ShapeDtypeStruct((T, N), F32), jax.ShapeDtypeStruct((T, D), BF16)),
        grid=(T // tm,),
        in_specs=[pl.BlockSpec((tm, D), lambda i: (i, 0)),
                  pl.BlockSpec((1, D), lambda i: (0, 0)),
                  pl.BlockSpec((D, N), lambda i: (0, 0)),
                  pl.BlockSpec((1, N), lambda i: (0, 0))],
        out_specs=(pl.BlockSpec((tm, N), lambda i: (i, 0)), pl.BlockSpec((tm, D), lambda i: (i, 0))),
        compiler_params=_cparams(("parallel",)),
        name="moe_router",
    )(h, g.reshape(1, D).astype(F32), w, b)


def _moe_kernel(be_ref, nu_ref, x_ref, wg_ref, wu_ref, wd_ref, o_ref, acc_ref):
    i = pl.program_id(0)
    j = pl.program_id(1)
    used = i < nu_ref[0]

    @pl.when((i == 0) & (j == 0))
    def _():
        acc_ref[...] = jnp.zeros(acc_ref.shape, acc_ref.dtype)

    @pl.when(used)
    def _():
        x = x_ref[...]
        a = jnp.dot(x, wg_ref[0, 0].astype(BF16), preferred_element_type=F32)
        b = jnp.dot(x, wu_ref[0, 0].astype(BF16), preferred_element_type=F32)
        hdn = (a * _sigmoid(a) * b).astype(BF16)
        y = jnp.dot(hdn, wd_ref[0, 0].astype(BF16), preferred_element_type=F32)
        tot = y + jnp.where(j > 0, acc_ref[...], 0.0)
        acc_ref[...] = tot
        o_ref[...] = tot.astype(o_ref.dtype)

    @pl.when(jnp.logical_not(used))
    def _():
        o_ref[...] = jnp.zeros(o_ref.shape, o_ref.dtype)


def moe_experts(blk_e, n_used, xr, wg, wu, wd, layer):
    P, D = xr.shape
    De = wg.shape[3]
    tm, dc = MOE_TM, MOE_DC
    nj = De // dc
    chunk = lambda i, s: jnp.where(i % 2 == 0, s, nj - 1 - s)
    jj = lambda i, s, nu: jnp.where(i < nu[0], chunk(i, s), chunk(nu[0] - 1, nj - 1))
    grid_spec = pltpu.PrefetchScalarGridSpec(
        num_scalar_prefetch=2,
        grid=(P // tm, nj),
        in_specs=[pl.BlockSpec((tm, D), lambda i, j, be, nu: (i, 0)),
                  pl.BlockSpec((1, 1, D, dc), lambda i, j, be, nu: (layer, be[i], 0, jj(i, j, nu))),
                  pl.BlockSpec((1, 1, D, dc), lambda i, j, be, nu: (layer, be[i], 0, jj(i, j, nu))),
                  pl.BlockSpec((1, 1, dc, D), lambda i, j, be, nu: (layer, be[i], jj(i, j, nu), 0))],
        out_specs=pl.BlockSpec((tm, D), lambda i, j, be, nu: (i, 0)),
        scratch_shapes=[pltpu.VMEM((tm, D), F32)],
    )
    return pl.pallas_call(
        _moe_kernel,
        out_shape=jax.ShapeDtypeStruct((P, D), BF16),
        grid_spec=grid_spec,
        compiler_params=_cparams(("arbitrary", "arbitrary")),
        name="moe_experts",
    )(blk_e, n_used, xr, wg, wu, wd)


def _combine_kernel(h_ref, y0_ref, y1_ref, w_ref, g_ref, o_ref, *, final_norm):
    w = w_ref[...]
    x = h_ref[...] + (w[:, 0:1] * y0_ref[...].astype(F32) + w[:, 1:2] * y1_ref[...].astype(F32))
    if final_norm:
        ms = jnp.mean(x * x, axis=-1, keepdims=True)
        x = x * lax.rsqrt(ms + EPS) * g_ref[...]
    o_ref[...] = x


def moe_combine(h, y0, y1, w, g, tm):
    T, D = h.shape
    final_norm = g is not None
    gain = (g if final_norm else jnp.ones((D,), F32)).reshape(1, D).astype(F32)
    row = lambda width: pl.BlockSpec((tm, width), lambda i: (i, 0))
    return pl.pallas_call(
        functools.partial(_combine_kernel, final_norm=final_norm),
        out_shape=jax.ShapeDtypeStruct((T, D), F32),
        grid=(T // tm,),
        in_specs=[row(D), row(D), row(D), row(LANE), pl.BlockSpec((1, D), lambda i: (0, 0))],
        out_specs=row(D),
        compiler_params=_cparams(("parallel",)),
        name="moe_combine",
    )(h, y0, y1, _pad_cols(w, LANE), gain)


def _pad_cols(w, width):
    return jnp.pad(w, ((0, 0), (0, width - w.shape[1])))


def _w_in_segments(D):
    names = ("fq", "fk", "fv", "ff", "mcq", "mckv", "mkr", "nq", "nkc", "nvc", "nks", "nvs", "nkw", "nvw", "ngt", "mg")
    widths = (1024, 1024, 1024, FORGET_COLS, MLA_Q_RANK, MLA_KV_RANK, MLA_ROPE, NSA_HEADS * NSA_DK,
              NSA_GROUPS * NSA_DK, NSA_GROUPS * NSA_DV, NSA_GROUPS * NSA_DK, NSA_GROUPS * NSA_DV,
              NSA_GROUPS * NSA_DK, NSA_GROUPS * NSA_DV, NSA_GATE_COLS, N_BRANCH * D)
    src = dict(zip(names, np.cumsum((0,) + widths[:-1]).tolist()))
    wid = dict(zip(names, widths))
    segs = []
    plain = lambda name, cb, f=1.0: segs.append((cb * LANE, src[name], wid[name], f))

    def padded_k(name, cb, n, f=1.0):
        for i in range(n):
            segs.append((cb * LANE + i * NSA_DKP, src[name] + i * NSA_DK, NSA_DK, f))

    half = MLA_ROPE // 2
    plain("mckv", CB_CKV)
    plain("mkr", CB_KR)
    segs.append((CB_KR * LANE + MLA_ROPE, src["mkr"] + half, half, -1.0))
    segs.append((CB_KR * LANE + MLA_ROPE + half, src["mkr"], half, 1.0))
    plain("mcq", CB_CQ)
    padded_k("nkc", CB_NKC, NSA_GROUPS)
    padded_k("nq", CB_NQ, NSA_HEADS, NSA_DK ** -0.5 * LOG2E)
    plain("fq", CB_FQ, FOX_DH ** -0.5 * LOG2E)
    plain("fk", CB_FK)
    plain("fv", CB_FV)
    padded_k("nks", CB_NKS, NSA_GROUPS)
    padded_k("nkw", CB_NKW, NSA_GROUPS)
    plain("nvc", CB_NVC)
    plain("nvs", CB_NVS)
    plain("nvw", CB_NVW)
    plain("mg", CB_MG)
    return segs, src


def _pack_w_in_kernel(w_ref, o_ref, os_ref, *, segs, small_segs):
    x = w_ref[0]
    o_ref[...] = jnp.zeros(o_ref.shape, o_ref.dtype)
    for dst, s, n, f in segs:
        v = x[:, s:s + n]
        if f != 1.0:
            v = v * f
        o_ref[:, dst:dst + n] = v.astype(o_ref.dtype)
    os_ref[...] = jnp.zeros(os_ref.shape, os_ref.dtype)
    for dst, s, n in small_segs:
        os_ref[:, dst:dst + n] = x[:, s:s + n]


def _pack_w_in(w_all, layer):
    _, D, d_in = w_all.shape
    segs, src = _w_in_segments(D)
    small_segs = ((0, src["ff"], FORGET_COLS), (FORGET_COLS, src["ngt"], NSA_GATE_COLS))
    tr = 128
    return pl.pallas_call(
        functools.partial(_pack_w_in_kernel, segs=tuple(segs), small_segs=small_segs),
        out_shape=(jax.ShapeDtypeStruct((D, Z_BLOCKS * LANE), BF16), jax.ShapeDtypeStruct((D, LANE), F32)),
        grid=(D // tr,),
        in_specs=[pl.BlockSpec((1, tr, d_in), lambda i: (layer, i, 0))],
        out_specs=(pl.BlockSpec((tr, Z_BLOCKS * LANE), lambda i: (i, 0)), pl.BlockSpec((tr, LANE), lambda i: (i, 0))),
        compiler_params=_cparams(("parallel",)),
        name="pack_w_in",
    )(w_all)


def _pack_w_uq(w):
    K = w.shape[0]
    w3 = w.reshape(K, MLA_HEADS, MLA_NOPE + MLA_ROPE) * ((MLA_NOPE + MLA_ROPE) ** -0.5 * LOG2E)
    nope = w3[:, :, :MLA_NOPE].reshape(K, MLA_HEADS * MLA_NOPE)
    r = w3[:, :, MLA_NOPE:]
    half = MLA_ROPE // 2
    r_rot = jnp.concatenate([-r[:, :, half:], r[:, :, :half]], axis=2)
    padr = lambda a: jnp.pad(a, ((0, 0), (0, 0), (0, LANE - MLA_ROPE))).reshape(K, MLA_HEADS * LANE)
    return jnp.concatenate([nope, padr(r), padr(r_rot)], axis=1).astype(BF16)


def _pack_w_ukv(w):
    K = w.shape[0]
    w3 = w.reshape(K, MLA_HEADS, MLA_NOPE + MLA_DV)
    return jnp.concatenate([w3[:, :, :MLA_NOPE].reshape(K, -1), w3[:, :, MLA_NOPE:].reshape(K, -1)], axis=1).astype(BF16)


def _t5_bucket(dist):
    dist = jnp.maximum(dist, 0)
    exact = REL_BUCKETS // 2
    df = jnp.maximum(dist, 1).astype(F32)
    large = exact + (jnp.log(df / exact) / math.log(REL_MAX_DIST / exact) * (REL_BUCKETS - exact)).astype(jnp.int32)
    large = jnp.minimum(large, REL_BUCKETS - 1)
    return jnp.where(dist < exact, dist, large)


def _position_tables(S, rel_bias):
    t = NSA_T
    half = MLA_ROPE // 2
    inv = ROPE_THETA ** (-jnp.arange(half, dtype=F32) / half)
    ang = jnp.arange(S, dtype=F32)[:, None] * inv
    c, s = jnp.cos(ang), jnp.sin(ang)
    cos = _pad_cols(jnp.concatenate([c, c], axis=1), LANE)
    sin = _pad_cols(jnp.concatenate([s, s], axis=1), LANE)
    ncp = max(S // CMP_STRIDE, LANE)
    pos = jnp.arange(S)

    def bias_of(dist):
        onehot = jax.nn.one_hot(_t5_bucket(dist), REL_BUCKETS, dtype=F32)
        return jnp.einsum("...b,bh->h...", onehot, rel_bias, precision=lax.Precision.HIGHEST)

    bias_c = bias_of(pos[None, :] - (CMP_STRIDE * jnp.arange(ncp)[:, None] + CMP_BLOCK - 1))
    i = jnp.arange(t)
    bias_sw = jnp.stack([bias_of(k * t + i[:, None] - i[None, :]) for k in range(3)], axis=1)
    bias_sw = bias_sw.reshape(NSA_GROUPS, NSA_HPG, 3, t, t).transpose(0, 2, 4, 1, 3).reshape(NSA_GROUPS, 3, t, NSA_HPG * t)
    n_cmp = (S - CMP_BLOCK) // CMP_STRIDE + 1
    n_blk = S // SEL_BLOCK
    nbp = max(n_blk, LANE)
    cstart = CMP_STRIDE * jnp.arange(ncp)
    sstart = SEL_BLOCK * jnp.arange(nbp)
    ov = jnp.clip(jnp.minimum(cstart[:, None] + CMP_BLOCK, sstart[None, :] + SEL_BLOCK)
                  - jnp.maximum(cstart[:, None], sstart[None, :]), 0, None).astype(F32) / CMP_STRIDE
    ov = jnp.where((jnp.arange(ncp)[:, None] < n_cmp) & (jnp.arange(nbp)[None, :] < n_blk), ov, 0.0).astype(BF16)
    expand = ((pos[:, None] // SEL_BLOCK) == jnp.arange(nbp)[None, :]).astype(BF16)
    return cos, sin, bias_c, bias_sw, ov, expand


def _token_mixers(h, p, w_in_all, layer, tabs, B, S):
    T, D = h.shape
    cos, sin, bias_c, bias_sw, overlap, expand = tabs
    wz, w_small = _pack_w_in(w_in_all, layer)
    z, zs = input_projection(h, p["g_mix"], wz, w_small, tm=min(T, 1024), tn=1024)
    z3 = z.reshape(B, S, Z_BLOCKS * LANE)

    log_f = jax.nn.log_sigmoid(zs[:, :FORGET_COLS] + p["b_forget"].astype(F32)).reshape(B, S, FOX_HEADS)
    cum = jnp.cumsum(log_f, axis=1) * LOG2E
    o_fox = causal_attention(z3, CB_FQ, z3, CB_FK, z3, CB_FV, FOX_HEADS, cum=cum)

    q_nope, q_rope = mla_q_proj(z, p["g_cq"], _pack_w_uq(p["w_uq"]), cos, sin, S, tm=min(S, 512))
    kv = rms_matmul(z, CB_CKV * LANE // MLA_KV_RANK, p["g_ckv"], _pack_w_ukv(p["w_ukv"]), BF16, tm=min(T, 1024), tn=1024)
    kr = z3[:, :, CB_KR * LANE:(CB_KR + 1) * LANE].astype(F32)
    kr = kr[..., :MLA_ROPE] * cos[None, :, :MLA_ROPE] + kr[..., MLA_ROPE:] * sin[None, :, :MLA_ROPE]
    k_rope = jnp.pad(kr, ((0, 0), (0, 0), (0, LANE - MLA_ROPE))).astype(BF16)
    hw = MLA_HEADS * LANE
    kv3 = kv.reshape(B, S, 2 * hw)
    o_mla = causal_attention(q_nope.reshape(B, S, hw), 0, kv3, 0, kv3, MLA_HEADS, MLA_HEADS,
                             q2=q_rope.reshape(B, S, hw), k2=k_rope)

    G = NSA_GROUPS
    NC = S // CMP_STRIDE
    ncp = bias_c.shape[1]

    def compress_branch(cb, dp, d, pe, w1, w2):
        x = z3[:, :, cb * LANE: cb * LANE + G * dp].reshape(B, NC, CMP_STRIDE * G * dp)
        eye = jnp.eye(G, dtype=F32)
        w1p = jnp.pad(w1.reshape(CMP_BLOCK, d, d), ((0, 0), (0, dp - d), (0, dp - d)))
        w1g = jnp.einsum("lij,gh->lgihj", w1p, eye).reshape(CMP_BLOCK, G * dp, G * dp).astype(BF16)
        w1a = w1g[:CMP_STRIDE].reshape(CMP_STRIDE * G * dp, G * dp)
        w1b = w1g[CMP_STRIDE:].reshape(CMP_STRIDE * G * dp, G * dp)
        pe_g = jnp.tile(jnp.pad(pe, ((0, 0), (0, dp - d)))[:, None, :], (1, G, 1))
        pe2 = pe_g.reshape(2, CMP_STRIDE * G * dp).astype(BF16)
        w2p = jnp.pad(w2, ((0, dp - d), (0, dp - d)))
        w2g = jnp.einsum("ij,gh->gihj", w2p, eye).reshape(G * dp, G * dp).astype(BF16)
        out = compress(x, w1a, w1b, pe2, w2g)
        return jnp.pad(out, ((0, 0), (0, ncp - NC), (0, 0)))

    kc = compress_branch(CB_NKC, NSA_DKP, NSA_DK, p["pe_k"], p["w_cmp_k1"], p["w_cmp_k2"])
    vc = compress_branch(CB_NVC, NSA_DV, NSA_DV, p["pe_v"], p["w_cmp_v1"], p["w_cmp_v2"])
    gl = zs[:, FORGET_COLS:FORGET_COLS + NSA_GATE_COLS].reshape(B, S, G, NSA_HPG * 3)
    gl = jnp.transpose(gl, (0, 2, 1, 3))
    gl_cols = jnp.pad(jnp.swapaxes(gl, 2, 3), ((0, 0), (0, 0), (0, 16 - NSA_HPG * 3), (0, 0)))
    o_cmp, ind = nsa_cmp_select(z3, kc, vc, bias_c, overlap, gl_cols, min(N_SEL, S // SEL_BLOCK))
    o_nsa = nsa_sel_win(z3, ind, expand, bias_sw, gl_cols, o_cmp)

    merged = merge_branches(o_fox.reshape(T, -1), o_mla.reshape(T, -1), o_nsa.reshape(T, -1),
                            p["w_branch"].astype(BF16), z, tm=min(T, 1024), tn=512)
    return matmul_residual(merged, p["w_out"].astype(BF16), h, tm=min(T, 1024), tn=512)


def _memory_block(h, mem2, p, B, S):
    T, D = h.shape
    kv = rms_matmul(mem2, 0, p["g_mem_kv"], p["w_mem_kv"].astype(BF16), BF16, tm=min(mem2.shape[0], 512), tn=512)
    out = memory_attention_block(h.reshape(B, S, D), p["g_mem_q"], (p["w_mem_q"] * (MEM_DH ** -0.5 * LOG2E)).astype(BF16),
                                 kv.reshape(B, -1, kv.shape[1]), p["w_mem_o"].astype(BF16), tq=min(S, 512))
    return out.reshape(T, D)


def _moe_block(h, p, experts, layer, g_final=None):
    T, D = h.shape
    tm = MOE_TM
    w_r = _pad_cols(jnp.concatenate([p["w_router_group"], p["w_router_expert"]], axis=1), LANE).astype(F32)
    b_r = _pad_cols(jnp.concatenate([p["b_router_group"], p["b_router_expert"]])[None, :], LANE).astype(F32)
    logits, u = moe_router(h, p["g_moe"], w_r, b_r, tm=min(T, 512))
    glog = logits[:, :N_GROUPS]
    gsel = jnp.argmax(glog, axis=-1).astype(jnp.int32)
    pg = jnp.max(jax.nn.softmax(glog, axis=-1), axis=-1, keepdims=True)
    elog = logits[:, N_GROUPS:N_GROUPS + N_EXPERTS].reshape(T, N_GROUPS, EXPERTS_PER_GROUP)
    elog = jnp.take_along_axis(elog, gsel[:, None, None], axis=1)[:, 0]
    eprob = jax.nn.softmax(elog, axis=-1)
    j0 = jnp.argmax(eprob, axis=-1).astype(jnp.int32)
    lane = jnp.arange(EXPERTS_PER_GROUP, dtype=jnp.int32)[None, :]
    j1 = jnp.argmax(jnp.where(lane == j0[:, None], -jnp.inf, eprob), axis=-1).astype(jnp.int32)
    top_j = jnp.stack([j0, j1], axis=-1)
    top_p = jnp.take_along_axis(eprob, top_j, axis=-1)
    top_p = top_p / jnp.sum(top_p, axis=-1, keepdims=True)
    weight = pg * top_p
    flat_e = (gsel[:, None] * EXPERTS_PER_GROUP + top_j.astype(jnp.int32)).reshape(-1)
    TK = T * TOP_K
    onehot = (flat_e[:, None] == jnp.arange(N_EXPERTS, dtype=jnp.int32)[None, :]).astype(jnp.int32)
    rank = jnp.sum((jnp.cumsum(onehot, axis=0) - onehot) * onehot, axis=1)
    counts = jnp.sum(onehot, axis=0)
    pcounts = ((counts + tm - 1) // tm) * tm
    pends = jnp.cumsum(pcounts)
    dest = (pends - pcounts)[flat_e] + rank
    P = TK + N_EXPERTS * tm
    n_rb = P // tm
    row_tok = (jnp.arange(P, dtype=jnp.int32) % T).at[dest].set(jnp.repeat(jnp.arange(T, dtype=jnp.int32), TOP_K))
    blk_e = jnp.sum((pends[None, :] <= (jnp.arange(n_rb, dtype=jnp.int32) * tm)[:, None]).astype(jnp.int32), axis=1)
    blk_e = jnp.minimum(blk_e, N_EXPERTS - 1).astype(jnp.int32)
    n_used = (pends[-1] // tm).astype(jnp.int32).reshape(1)
    xr = u[row_tok]
    y = moe_experts(blk_e, n_used, xr, experts[0].astype(F32), experts[1].astype(F32), experts[2].astype(F32), layer)
    d2 = dest.reshape(T, TOP_K)
    return moe_combine(h, y[d2[:, 0]], y[d2[:, 1]], weight, g_final, tm=min(T, 512))


_LAYER_KEYS = ("g_mix", "w_in", "b_forget", "g_cq", "g_ckv", "w_uq", "w_ukv", "pe_k", "pe_v", "w_cmp_k1", "w_cmp_k2",
               "w_cmp_v1", "w_cmp_v2", "w_branch", "w_out", "g_mem_q", "g_mem_kv", "w_mem_q", "w_mem_kv", "w_mem_o",
               "g_moe", "w_router_group", "b_router_group", "w_router_expert", "b_router_expert")


def kernel(x, mem, g_mix, w_in, b_forget, g_cq, g_ckv, w_uq, w_ukv, pe_k, pe_v, w_cmp_k1, w_cmp_k2, w_cmp_v1, w_cmp_v2, rel_bias, w_branch, w_out, g_mem_q, g_mem_kv, w_mem_q, w_mem_kv, w_mem_o, g_moe, w_router_group, b_router_group, w_router_expert, b_router_expert, w_exp_gate, w_exp_up, w_exp_down, g_final):
    B, S, D = x.shape
    T = B * S
    stacked = dict(g_mix=g_mix, w_in=w_in, b_forget=b_forget, g_cq=g_cq, g_ckv=g_ckv, w_uq=w_uq, w_ukv=w_ukv,
                   pe_k=pe_k, pe_v=pe_v, w_cmp_k1=w_cmp_k1, w_cmp_k2=w_cmp_k2, w_cmp_v1=w_cmp_v1, w_cmp_v2=w_cmp_v2,
                   w_branch=w_branch, w_out=w_out, g_mem_q=g_mem_q, g_mem_kv=g_mem_kv, w_mem_q=w_mem_q,
                   w_mem_kv=w_mem_kv, w_mem_o=w_mem_o, g_moe=g_moe, w_router_group=w_router_group,
                   b_router_group=b_router_group, w_router_expert=w_router_expert, b_router_expert=b_router_expert,
                   w_exp_gate=w_exp_gate, w_exp_up=w_exp_up, w_exp_down=w_exp_down)
    tabs = _position_tables(S, rel_bias.astype(F32) * LOG2E)
    h = x.reshape(T, D).astype(F32)
    mem2 = mem.reshape(-1, D).astype(F32)
    depth = w_in.shape[0]
    assert depth >= 1
    for l in range(depth):
        p = {k: stacked[k][l] for k in _LAYER_KEYS}
        h = _token_mixers(h, p, w_in.astype(F32), l, tabs, B, S)
        h = _memory_block(h, mem2, p, B, S)
        h = _moe_block(h, p, (w_exp_gate, w_exp_up, w_exp_down), l, g_final if l == depth - 1 else None)
    return h.reshape(B, S, D)
```

```python
import functools
import math

import jax
import jax.numpy as jnp
import numpy as np
from jax import lax
from jax.experimental import pallas as pl
from jax.experimental.pallas import tpu as pltpu

F32 = jnp.float32
BF16 = jnp.bfloat16

EPS = 1e-6
NEG_INF = -1e30
LOG2E = math.log2(math.e)
LANE = 128

FOX_HEADS, FOX_DH = 8, 128
MLA_HEADS, MLA_NOPE, MLA_ROPE, MLA_DV = 8, 128, 64, 128
MLA_Q_RANK, MLA_KV_RANK = 768, 512
ROPE_THETA = 10000.0
NSA_HEADS, NSA_GROUPS, NSA_DK, NSA_DV = 8, 2, 192, 128
NSA_HPG = NSA_HEADS // NSA_GROUPS
NSA_DKP = 256
CMP_BLOCK, CMP_STRIDE, SEL_BLOCK, N_SEL, WINDOW = 32, 16, 64, 8, 512
SEL_FORCE = 1e6
REL_BUCKETS, REL_MAX_DIST = 32, 128
N_BRANCH, BRANCH_W = 3, 1024
MEM_HEADS, MEM_DH = 4, 128
N_GROUPS, EXPERTS_PER_GROUP, TOP_K = 4, 8, 2
N_EXPERTS = N_GROUPS * EXPERTS_PER_GROUP
FORGET_COLS, NSA_GATE_COLS = FOX_HEADS, NSA_HEADS * 3

CB_CKV, CB_KR, CB_CQ = 0, 4, 6
CB_NKC, CB_NQ = 12, 16
CB_FQ, CB_FK, CB_FV = 32, 40, 48
CB_NKS, CB_NKW = 56, 60
CB_NVC, CB_NVS, CB_NVW = 64, 66, 68
CB_MG = 72
Z_BLOCKS = 120

ATT_T = 256
ATT_HG = 8
NSA_T = 256
NSA_CMP_T = 512
MOE_TM = 512
MOE_DC = 512
VMEM_LIMIT = 56 * 1024 * 1024


def _cparams(sem):
    return pltpu.CompilerParams(dimension_semantics=sem, vmem_limit_bytes=VMEM_LIMIT)


def _sigmoid(x):
    return 1.0 / (1.0 + jnp.exp(-x))


def _rms_mm_kernel(x_ref, g_ref, w_ref, o_ref, u_ref):
    @pl.when(pl.program_id(1) == 0)
    def _():
        x = x_ref[...].astype(F32)
        ms = jnp.mean(x * x, axis=-1, keepdims=True)
        u_ref[...] = (x * lax.rsqrt(ms + EPS) * g_ref[...]).astype(u_ref.dtype)

    o_ref[...] = jnp.dot(u_ref[...], w_ref[...], preferred_element_type=F32).astype(o_ref.dtype)


def rms_matmul(x, xcol, g, w, out_dtype, tm, tn):
    T = x.shape[0]
    K, N = w.shape
    return pl.pallas_call(
        _rms_mm_kernel,
        out_shape=jax.ShapeDtypeStruct((T, N), out_dtype),
        grid=(T // tm, N // tn),
        in_specs=[pl.BlockSpec((tm, K), lambda i, j: (i, xcol)),
                  pl.BlockSpec((1, K), lambda i, j: (0, 0)),
                  pl.BlockSpec((K, tn), lambda i, j: (0, j))],
        out_specs=pl.BlockSpec((tm, tn), lambda i, j: (i, j)),
        scratch_shapes=[pltpu.VMEM((tm, K), BF16)],
        compiler_params=_cparams(("parallel", "arbitrary")),
        name="rms_matmul",
    )(x, g.reshape(1, K).astype(F32), w)


def _in_proj_kernel(x_ref, g_ref, w_ref, ws_ref, o_ref, os_ref, u_ref):
    @pl.when(pl.program_id(1) == 0)
    def _():
        x = x_ref[...]
        ms = jnp.mean(x * x, axis=-1, keepdims=True)
        u = x * lax.rsqrt(ms + EPS) * g_ref[...]
        u_ref[...] = u.astype(u_ref.dtype)
        os_ref[...] = _dot3(u, ws_ref[...])

    o_ref[...] = jnp.dot(u_ref[...], w_ref[...], preferred_element_type=F32).astype(o_ref.dtype)


def input_projection(x, g, w, w_small, tm, tn):
    T, K = x.shape
    N = w.shape[1]
    Ns = w_small.shape[1]
    return pl.pallas_call(
        _in_proj_kernel,
        out_shape=(jax.ShapeDtypeStruct((T, N), BF16), jax.ShapeDtypeStruct((T, Ns), F32)),
        grid=(T // tm, N // tn),
        in_specs=[pl.BlockSpec((tm, K), lambda i, j: (i, 0)),
                  pl.BlockSpec((1, K), lambda i, j: (0, 0)),
                  pl.BlockSpec((K, tn), lambda i, j: (0, j)),
                  pl.BlockSpec((K, Ns), lambda i, j: (0, 0))],
        out_specs=(pl.BlockSpec((tm, tn), lambda i, j: (i, j)), pl.BlockSpec((tm, Ns), lambda i, j: (i, 0))),
        scratch_shapes=[pltpu.VMEM((tm, K), BF16)],
        compiler_params=_cparams(("parallel", "arbitrary")),
        name="input_projection",
    )(x, g.reshape(1, K).astype(F32), w, w_small)


def _mla_q_kernel(x_ref, g_ref, w_ref, cos_ref, sin_ref, qn_ref, qr_ref):
    x = x_ref[...].astype(F32)
    ms = jnp.mean(x * x, axis=-1, keepdims=True)
    u = (x * lax.rsqrt(ms + EPS) * g_ref[...]).astype(BF16)
    y = jnp.dot(u, w_ref[...], preferred_element_type=F32)
    hw = MLA_HEADS * LANE
    qn_ref[...] = y[:, :hw].astype(qn_ref.dtype)
    cos = cos_ref[...]
    sin = sin_ref[...]
    for h in range(MLA_HEADS):
        a = y[:, hw + h * LANE: hw + (h + 1) * LANE]
        b = y[:, 2 * hw + h * LANE: 2 * hw + (h + 1) * LANE]
        qr_ref[:, h * LANE:(h + 1) * LANE] = (a * cos + b * sin).astype(qr_ref.dtype)


def mla_q_proj(z, g, w, cos, sin, S, tm):
    T = z.shape[0]
    K, N = w.shape
    hw = MLA_HEADS * LANE
    nsb = S // tm
    return pl.pallas_call(
        _mla_q_kernel,
        out_shape=(jax.ShapeDtypeStruct((T, hw), BF16), jax.ShapeDtypeStruct((T, hw), BF16)),
        grid=(T // tm,),
        in_specs=[pl.BlockSpec((tm, K), lambda i: (i, CB_CQ * LANE // MLA_Q_RANK)),
                  pl.BlockSpec((1, K), lambda i: (0, 0)),
                  pl.BlockSpec((K, N), lambda i: (0, 0)),
                  pl.BlockSpec((tm, LANE), lambda i: (i % nsb, 0)),
                  pl.BlockSpec((tm, LANE), lambda i: (i % nsb, 0))],
        out_specs=(pl.BlockSpec((tm, hw), lambda i: (i, 0)), pl.BlockSpec((tm, hw), lambda i: (i, 0))),
        compiler_params=_cparams(("parallel",)),
        name="mla_q_proj",
    )(z, g.reshape(1, K).astype(F32), w, cos, sin)


def _causal_attn_kernel(*refs, t, hg, two_part, decay):
    refs = list(refs)
    q_ref, k_ref, v_ref = refs[:3]
    pos = 3
    if two_part:
        q2_ref, k2_ref = refs[pos:pos + 2]
        pos += 2
    if decay:
        ck_ref = refs[pos]
        pos += 1
    o_ref = refs[pos]
    if two_part:
        kcat_ref = refs[pos + 1]
    qi = pl.program_id(2)
    dn = (((1,), (1,)), ((), ()))

    if two_part:
        @pl.when(qi == 0)
        def _():
            for j in range(hg):
                kcat_ref[j, :, :LANE] = k_ref[0, :, j * LANE:(j + 1) * LANE]
                kcat_ref[j, :, LANE:] = k2_ref[0]

    qs = []
    for j in range(hg):
        qj = q_ref[0, :, j * LANE:(j + 1) * LANE]
        if two_part:
            qj = jnp.concatenate([qj, q2_ref[0, :, j * LANE:(j + 1) * LANE]], axis=1)
        qs.append(qj)

    def step(kb, carry, masked, width=1):
        off = pl.multiple_of(kb * t, t)
        tk = width * t
        heads = range(hg)
        ss = []
        for j in heads:
            k = kcat_ref[j, pl.ds(off, tk), :] if two_part else k_ref[0, pl.ds(off, tk), j * LANE:(j + 1) * LANE]
            ss.append(lax.dot_general(k, qs[j], dn, preferred_element_type=F32))
        if decay:
            ss = [ss[j] - ck_ref[0, 0, pl.ds(off, tk), j:j + 1] for j in heads]
        if masked:
            r = lax.broadcasted_iota(jnp.int32, (tk, t), 0)
            c = lax.broadcasted_iota(jnp.int32, (tk, t), 1)
            ss = [jnp.where(r <= c, s, NEG_INF) for s in ss]
        ms = [jnp.maximum(carry[j][0], jnp.max(ss[j], axis=0, keepdims=True)) for j in heads]
        ps = [jnp.exp2(ss[j] - ms[j]) for j in heads]
        out = []
        for j in heads:
            m, l, acc = carry[j]
            a = jnp.exp2(m - ms[j])
            l = a * l + jnp.sum(ps[j], axis=0, keepdims=True)
            v = v_ref[0, pl.ds(off, tk), j * LANE:(j + 1) * LANE]
            acc = a * acc + lax.dot_general(v, ps[j].astype(BF16), (((0,), (0,)), ((), ())),
                                            preferred_element_type=F32)
            out.append((ms[j], l, acc))
        return tuple(out)

    init = tuple((jnp.full((1, t), NEG_INF, F32), jnp.zeros((1, t), F32), jnp.zeros((LANE, t), F32))
                 for _ in range(hg))
    carry = lax.fori_loop(0, qi // 2, lambda kp, c: step(2 * kp, c, False, width=2), init)
    carry = lax.cond(qi % 2 == 1, lambda c: step(qi - 1, c, False), lambda c: c, carry)
    carry = step(qi, carry, True)
    for j in range(hg):
        _, l, acc = carry[j]
        o_ref[0, :, j * LANE:(j + 1) * LANE] = (acc / l).T.astype(o_ref.dtype)


def causal_attention(q, qcb, k, kcb, v, vcb, heads, q2=None, k2=None, cum=None):
    B, S, _ = q.shape
    t, hg = ATT_T, ATT_HG
    w = hg * LANE
    two_part, decay = q2 is not None, cum is not None
    in_specs = [pl.BlockSpec((1, t, w), lambda b, h, i: (b, i, qcb // hg + h)),
                pl.BlockSpec((1, S, w), lambda b, h, i: (b, 0, kcb // hg + h)),
                pl.BlockSpec((1, S, w), lambda b, h, i: (b, 0, vcb // hg + h))]
    args = [q, k, v]
    scratch = []
    if two_part:
        in_specs += [pl.BlockSpec((1, t, w), lambda b, h, i: (b, i, h)),
                     pl.BlockSpec((1, S, LANE), lambda b, h, i: (b, 0, 0))]
        args += [q2, k2]
        scratch = [pltpu.VMEM((hg, S, 2 * LANE), BF16)]
    if decay:
        in_specs += [pl.BlockSpec((1, 1, S, hg), lambda b, h, i: (b, h, 0, 0))]
        args += [jnp.transpose(cum.reshape(B, S, heads // hg, hg), (0, 2, 1, 3))]
    return pl.pallas_call(
        functools.partial(_causal_attn_kernel, t=t, hg=hg, two_part=two_part, decay=decay),
        out_shape=jax.ShapeDtypeStruct((B, S, heads * LANE), BF16),
        grid=(B, heads // hg, S // t),
        in_specs=in_specs,
        out_specs=pl.BlockSpec((1, t, w), lambda b, h, i: (b, i, h)),
        scratch_shapes=scratch,
        compiler_params=_cparams(("parallel", "parallel", "arbitrary")),
        name="causal_attention",
    )(*args)


def _gelu_tanh(x):
    return 0.5 * x * (1.0 + jnp.tanh(math.sqrt(2.0 / math.pi) * (x + 0.044715 * (x * x * x))))


def _compress_kernel(x_ref, w1a_ref, w1b_ref, pe_ref, w2_ref, o_ref):
    x = x_ref[0]
    a = jnp.dot(x, w1a_ref[...], preferred_element_type=F32)
    b = jnp.dot(x, w1b_ref[...], preferred_element_type=F32)
    nc = a.shape[0]
    b_next = pltpu.roll(b, nc - 1, 0)
    pe_term = jnp.dot(pe_ref[0:1, :], w1a_ref[...], preferred_element_type=F32) + \
        jnp.dot(pe_ref[1:2, :], w1b_ref[...], preferred_element_type=F32)
    hid = _gelu_tanh(a + b_next + pe_term)
    o_ref[0] = jnp.dot(hid.astype(BF16), w2_ref[...], preferred_element_type=F32).astype(o_ref.dtype)


def compress(x, w1a, w1b, pe2, w2):
    BG, NC, KD = x.shape
    dp = w2.shape[1]
    return pl.pallas_call(
        _compress_kernel,
        out_shape=jax.ShapeDtypeStruct((BG, NC, dp), BF16),
        grid=(BG,),
        in_specs=[pl.BlockSpec((1, NC, KD), lambda i: (i, 0, 0)),
                  pl.BlockSpec((KD, dp), lambda i: (0, 0)),
                  pl.BlockSpec((KD, dp), lambda i: (0, 0)),
                  pl.BlockSpec((2, KD), lambda i: (0, 0)),
                  pl.BlockSpec((dp, dp), lambda i: (0, 0))],
        out_specs=pl.BlockSpec((1, NC, dp), lambda i: (i, 0, 0)),
        compiler_params=_cparams(("parallel",)),
        name="nsa_compress",
    )(x, w1a, w1b, pe2, w2)


def _nsa_cmp_kernel(q_ref, kc_ref, vc_ref, bias_ref, ov_ref, gate_ref, o_ref, ind_ref, *, t, n_sel, n_rows):
    qi = pl.program_id(2)
    ncp = kc_ref.shape[1]
    kc = kc_ref[0]
    vc = vc_ref[0]
    pos = lax.broadcasted_iota(jnp.int32, (ncp, t), 1) + qi * t
    cblk = lax.broadcasted_iota(jnp.int32, (ncp, t), 0)
    valid = pos >= CMP_STRIDE * cblk + (CMP_BLOCK - 1)
    dn = (((1,), (1,)), ((), ()))
    tn = (((0,), (0,)), ((), ()))
    heads = range(NSA_HPG)
    ss = [lax.dot_general(kc, q_ref[0, :, h * NSA_DKP:(h + 1) * NSA_DKP], dn, preferred_element_type=F32)
          for h in heads]
    ss = [jnp.where(valid, ss[h] + bias_ref[h], NEG_INF) for h in heads]
    es = [jnp.exp2(ss[h] - jnp.max(ss[h], axis=0, keepdims=True)) for h in heads]
    ps = [jnp.where(valid, es[h] / jnp.sum(es[h], axis=0, keepdims=True), 0.0) for h in heads]
    gates = _sigmoid(gate_ref[0, 0])
    for h in heads:
        o = lax.dot_general(vc, ps[h].astype(BF16), tn, preferred_element_type=F32)
        o_ref[0, :, h * NSA_DV:(h + 1) * NSA_DV] = (gates[3 * h:3 * h + 1, :] * o).T.astype(o_ref.dtype)
    psum = functools.reduce(lambda x, y: x + y, ps)
    p_hi = psum.astype(BF16)
    p_lo = (psum - p_hi.astype(F32)).astype(BF16)
    imp = lax.dot_general(ov_ref[...], p_hi, tn, preferred_element_type=F32) + \
        lax.dot_general(ov_ref[...], p_lo, tn, preferred_element_type=F32)
    nbp = imp.shape[0]
    imp = imp[:n_rows]
    blk = lax.broadcasted_iota(jnp.int32, (n_rows, t), 0)
    cur = (lax.broadcasted_iota(jnp.int32, (n_rows, t), 1) + qi * t) // SEL_BLOCK
    forced = (blk == 0) | (blk == cur) | (blk == cur - 1)
    score = jnp.where(blk <= cur, imp + jnp.where(forced, SEL_FORCE, 0.0), NEG_INF)
    sel = jnp.zeros((n_rows, t), F32)
    for _ in range(n_sel):
        mx = jnp.max(score, axis=0, keepdims=True)
        first = jnp.min(jnp.where(score == mx, blk, nbp), axis=0, keepdims=True)
        hit = blk == first
        sel = jnp.where(hit, 1.0, sel)
        score = jnp.where(hit, -jnp.inf, score)
    ind_ref[0, 0] = jnp.zeros(ind_ref.shape[2:], ind_ref.dtype)
    ind_ref[0, 0, :n_rows, :] = sel.astype(ind_ref.dtype)


def nsa_cmp_select(z3, kc, vc, bias_c, overlap, gates, n_sel):
    B, S, _ = z3.shape
    G = NSA_GROUPS
    t = min(NSA_CMP_T, S)
    ncp = kc.shape[1]
    nbp = overlap.shape[1]
    qw = NSA_HPG * NSA_DKP
    ow = NSA_HPG * NSA_DV
    n_rows = min(nbp, -(-(S // SEL_BLOCK) // 16) * 16)
    return pl.pallas_call(
        functools.partial(_nsa_cmp_kernel, t=t, n_sel=n_sel, n_rows=n_rows),
        out_shape=(jax.ShapeDtypeStruct((B, S, G * ow), BF16), jax.ShapeDtypeStruct((B, G, nbp, S), BF16)),
        grid=(B, G, S // t),
        in_specs=[pl.BlockSpec((1, t, qw), lambda b, g, i: (b, i, CB_NQ * LANE // qw + g)),
                  pl.BlockSpec((1, ncp, NSA_DKP), lambda b, g, i: (b, 0, g)),
                  pl.BlockSpec((1, ncp, NSA_DV), lambda b, g, i: (b, 0, g)),
                  pl.BlockSpec((NSA_HPG, ncp, t), lambda b, g, i: (g, 0, i)),
                  pl.BlockSpec((ncp, nbp), lambda b, g, i: (0, 0)),
                  pl.BlockSpec((1, 1, gates.shape[2], t), lambda b, g, i: (b, g, 0, i))],
        out_specs=(pl.BlockSpec((1, t, ow), lambda b, g, i: (b, i, g)),
                   pl.BlockSpec((1, 1, nbp, t), lambda b, g, i: (b, g, 0, i))),
        compiler_params=_cparams(("parallel", "parallel", "arbitrary")),
        name="nsa_cmp_select",
    )(z3, kc, vc, bias_c, overlap, gates)


def _nsa_sw_kernel(q_ref, ks_ref, vs_ref, kw_ref, vw_ref, ind_ref, e_ref, bias_ref, gate_ref, oc_ref, o_ref,
                   *, t):
    qi = pl.program_id(1)
    hp, G = NSA_HPG, NSA_GROUPS
    q4 = [jnp.concatenate([q_ref[0, :, (g * hp + h) * NSA_DKP:(g * hp + h + 1) * NSA_DKP] for h in range(hp)], axis=0)
          for g in range(G)]
    inds = [ind_ref[0, g] for g in range(G)]
    dn = (((1,), (1,)), ((), ()))

    def step(kb, carry, k_ref, v_ref, selected, width=1):
        off = pl.multiple_of(kb * t, t)
        tk = width * t
        ri = lax.broadcasted_iota(jnp.int32, (tk, t), 0)
        ci = lax.broadcasted_iota(jnp.int32, (tk, t), 1)
        d = (qi - kb) * t + ci - ri
        near = (d >= 0) if selected else (d >= 0) & (d < WINDOW)
        groups = range(G)
        ss = [lax.dot_general(k_ref[0, pl.ds(off, tk), g * NSA_DKP:(g + 1) * NSA_DKP], q4[g], dn,
                              preferred_element_type=F32) for g in groups]
        negs = []
        for g in groups:
            mask = near
            if selected:
                hit = jnp.dot(e_ref[pl.ds(off, tk), :], inds[g], preferred_element_type=F32)
                mask = near & (hit > 0.5)
            neg = jnp.where(mask, 0.0, NEG_INF)
            negs.append(jnp.concatenate([neg] * hp, axis=1))
        bias = [jnp.concatenate([bias_ref[g, jnp.minimum(qi - kb - w, 2)] for w in range(width)], axis=0)
                if width > 1 else bias_ref[g, jnp.minimum(qi - kb, 2)] for g in groups]
        ss = [ss[g] + bias[g] + negs[g] for g in groups]
        ms = [jnp.maximum(carry[g][0], jnp.max(ss[g], axis=0, keepdims=True)) for g in groups]
        ps = [jnp.exp2(ss[g] - ms[g]) for g in groups]
        out = []
        for g in groups:
            m, l, acc = carry[g]
            a = jnp.exp2(m - ms[g])
            l = a * l + jnp.sum(ps[g], axis=0, keepdims=True)
            v = v_ref[0, pl.ds(off, tk), g * NSA_DV:(g + 1) * NSA_DV]
            acc = a * acc + lax.dot_general(v, ps[g].astype(BF16), (((0,), (0,)), ((), ())),
                                            preferred_element_type=F32)
            out.append((ms[g], l, acc))
        return tuple(out)

    init = tuple((jnp.full((1, hp * t), NEG_INF, F32), jnp.zeros((1, hp * t), F32), jnp.zeros((NSA_DV, hp * t), F32))
                 for _ in range(G))
    sel = lax.fori_loop(0, (qi + 1) // 2, lambda kp, c: step(2 * kp, c, ks_ref, vs_ref, True, width=2), init)
    sel = lax.cond(qi % 2 == 0, lambda c: step(qi, c, ks_ref, vs_ref, True), lambda c: c, sel)
    lo = jnp.maximum(qi - WINDOW // t, 0)
    win = lax.fori_loop(lo, qi + 1, lambda kb, c: step(kb, c, kw_ref, vw_ref, False), init)
    for g in range(G):
        o_s = sel[g][2] / sel[g][1]
        o_w = win[g][2] / win[g][1]
        gates = _sigmoid(gate_ref[0, g])
        for h in range(hp):
            r = slice(h * t, (h + 1) * t)
            c = slice((g * hp + h) * NSA_DV, (g * hp + h + 1) * NSA_DV)
            o = gates[3 * h + 1:3 * h + 2, :] * o_s[:, r] + gates[3 * h + 2:3 * h + 3, :] * o_w[:, r]
            o_ref[0, :, c] = (oc_ref[0, :, c].astype(F32) + o.T).astype(o_ref.dtype)


def nsa_sel_win(z3, ind, expand, bias_sw, gates, o_cmp):
    B, S, _ = z3.shape
    G = NSA_GROUPS
    t = NSA_T
    qw = NSA_HEADS * NSA_DKP
    ow = NSA_HEADS * NSA_DV
    kw = G * NSA_DKP
    vw = G * NSA_DV
    nbp = ind.shape[2]
    kspec = lambda cb: pl.BlockSpec((1, S, kw), lambda b, i: (b, 0, cb * LANE // kw))
    vspec = lambda cb: pl.BlockSpec((1, S, vw), lambda b, i: (b, 0, cb * LANE // vw))
    return pl.pallas_call(
        functools.partial(_nsa_sw_kernel, t=t),
        out_shape=jax.ShapeDtypeStruct((B, S, ow), BF16),
        grid=(B, S // t),
        in_specs=[pl.BlockSpec((1, t, qw), lambda b, i: (b, i, CB_NQ * LANE // qw)),
                  kspec(CB_NKS), vspec(CB_NVS), kspec(CB_NKW), vspec(CB_NVW),
                  pl.BlockSpec((1, G, nbp, t), lambda b, i: (b, 0, 0, i)),
                  pl.BlockSpec((S, nbp), lambda b, i: (0, 0)),
                  pl.BlockSpec((G, 3, t, NSA_HPG * t), lambda b, i: (0, 0, 0, 0)),
                  pl.BlockSpec((1, G, gates.shape[2], t), lambda b, i: (b, 0, 0, i)),
                  pl.BlockSpec((1, t, ow), lambda b, i: (b, i, 0))],
        out_specs=pl.BlockSpec((1, t, ow), lambda b, i: (b, i, 0)),
        compiler_params=_cparams(("parallel", "arbitrary")),
        name="nsa_sel_win",
    )(z3, z3, z3, z3, z3, ind, expand, bias_sw, gates, o_cmp)


def _merge_kernel(of_ref, om_ref, on_ref, wb_ref, g0_ref, g1_ref, g2_ref, o_ref):
    acc = None
    for n, (o_r, g_r) in enumerate(((of_ref, g0_ref), (om_ref, g1_ref), (on_ref, g2_ref))):
        y = jnp.dot(o_r[...], wb_ref[n], preferred_element_type=F32)
        y = _sigmoid(g_r[...].astype(F32)) * y
        acc = y if acc is None else acc + y
    o_ref[...] = acc.astype(o_ref.dtype)


def merge_branches(o_fox, o_mla, o_nsa, wb, z, tm, tn):
    T = o_fox.shape[0]
    D = wb.shape[2]
    gspec = lambda n: pl.BlockSpec((tm, tn), lambda i, j: (i, (CB_MG * LANE + n * D) // tn + j))
    ospec = pl.BlockSpec((tm, BRANCH_W), lambda i, j: (i, 0))
    return pl.pallas_call(
        _merge_kernel,
        out_shape=jax.ShapeDtypeStruct((T, D), BF16),
        grid=(T // tm, D // tn),
        in_specs=[ospec, ospec, ospec,
                  pl.BlockSpec((N_BRANCH, BRANCH_W, tn), lambda i, j: (0, 0, j)),
                  gspec(0), gspec(1), gspec(2)],
        out_specs=pl.BlockSpec((tm, tn), lambda i, j: (i, j)),
        compiler_params=_cparams(("parallel", "arbitrary")),
        name="merge_branches",
    )(o_fox, o_mla, o_nsa, wb, z, z, z)


def _mm_res_kernel(a_ref, w_ref, r_ref, o_ref):
    o_ref[...] = r_ref[...] + jnp.dot(a_ref[...], w_ref[...], preferred_element_type=F32)


def matmul_residual(a, w, res, tm, tn):
    T, K = a.shape
    N = w.shape[1]
    return pl.pallas_call(
        _mm_res_kernel,
        out_shape=jax.ShapeDtypeStruct((T, N), F32),
        grid=(T // tm, N // tn),
        in_specs=[pl.BlockSpec((tm, K), lambda i, j: (i, 0)),
                  pl.BlockSpec((K, tn), lambda i, j: (0, j)),
                  pl.BlockSpec((tm, tn), lambda i, j: (i, j))],
        out_specs=pl.BlockSpec((tm, tn), lambda i, j: (i, j)),
        compiler_params=_cparams(("parallel", "arbitrary")),
        name="matmul_residual",
    )(a, w, res)


def _mem_attn_kernel(h_ref, g_ref, wq_ref, kv_ref, wo_ref, o_ref):
    x = h_ref[0]
    ms = jnp.mean(x * x, axis=-1, keepdims=True)
    u = (x * lax.rsqrt(ms + EPS) * g_ref[...]).astype(BF16)
    q = jnp.dot(u, wq_ref[...], preferred_element_type=F32).astype(BF16)
    dn = (((1,), (1,)), ((), ()))
    hw = MEM_HEADS * MEM_DH
    heads = range(MEM_HEADS)
    cs = [slice(h * MEM_DH, (h + 1) * MEM_DH) for h in heads]
    ss = [lax.dot_general(q[:, cs[h]], kv_ref[0, :, cs[h]], dn, preferred_element_type=F32) for h in heads]
    es = [jnp.exp2(ss[h] - jnp.max(ss[h], axis=-1, keepdims=True)) for h in heads]
    ps = [es[h] / jnp.sum(es[h], axis=-1, keepdims=True) for h in heads]
    outs = [jnp.dot(ps[h].astype(BF16), kv_ref[0, :, hw + h * MEM_DH: hw + (h + 1) * MEM_DH],
                    preferred_element_type=F32).astype(BF16) for h in heads]
    o = jnp.concatenate(outs, axis=1)
    o_ref[0] = x + jnp.dot(o, wo_ref[...], preferred_element_type=F32)


def memory_attention_block(h3, g, wq, kv, wo, tq):
    B, S, D = h3.shape
    M = kv.shape[1]
    hw = MEM_HEADS * MEM_DH
    return pl.pallas_call(
        _mem_attn_kernel,
        out_shape=jax.ShapeDtypeStruct((B, S, D), F32),
        grid=(B, S // tq),
        in_specs=[pl.BlockSpec((1, tq, D), lambda b, i: (b, i, 0)),
                  pl.BlockSpec((1, D), lambda b, i: (0, 0)),
                  pl.BlockSpec((D, hw), lambda b, i: (0, 0)),
                  pl.BlockSpec((1, M, 2 * hw), lambda b, i: (b, 0, 0)),
                  pl.BlockSpec((hw, D), lambda b, i: (0, 0))],
        out_specs=pl.BlockSpec((1, tq, D), lambda b, i: (b, i, 0)),
        compiler_params=_cparams(("parallel", "arbitrary")),
        name="memory_attention",
    )(h3, g.reshape(1, D).astype(F32), wq, kv, wo)


def _dot3(a, b):
    a_hi = a.astype(BF16)
    a_lo = (a - a_hi.astype(F32)).astype(BF16)
    b_hi = b.astype(BF16)
    b_lo = (b - b_hi.astype(F32)).astype(BF16)
    dot = functools.partial(jnp.dot, preferred_element_type=F32)
    return dot(a_hi, b_hi) + (dot(a_lo, b_hi) + dot(a_hi, b_lo))


def _router_kernel(h_ref, g_ref, w_ref, b_ref, lg_ref, u_ref):
    x = h_ref[...]
    ms = jnp.mean(x * x, axis=-1, keepdims=True)
    u = x * lax.rsqrt(ms + EPS) * g_ref[...]
    u_ref[...] = u.astype(u_ref.dtype)
    lg_ref[...] = _dot3(u, w_ref[...]) + b_ref[...]


def moe_router(h, g, w, b, tm):
    T, D = h.shape
    N = w.shape[1]
    return pl.pallas_call(
        _router_kernel,
        out_shape=(jax.ShapeDtypeStruct((T, N), F32), jax.ShapeDtypeStruct((T, D), BF16)),
        grid=(T // tm,),
        in_specs=[pl.BlockSpec((tm, D), lambda i: (i, 0)),
                  pl.BlockSpec((1, D), lambda i: (0, 0)),
                  pl.BlockSpec((D, N), lambda i: (0, 0)),
                  pl.BlockSpec((1, N), lambda i: (0, 0))],
        out_specs=(pl.BlockSpec((tm, N), lambda i: (i, 0)), pl.BlockSpec((tm, D), lambda i: (i, 0))),
        compiler_params=_cparams(("parallel",)),
        name="moe_router",
    )(h, g.reshape(1, D).astype(F32), w, b)


def _moe_kernel(be_ref, nv_ref, nu_ref, x_ref, wg_ref, wu_ref, wd_ref, o_ref, acc_ref):
    i = pl.program_id(0)
    j = pl.program_id(1)
    nv = nv_ref[i]
    tm = x_ref.shape[0]
    half = tm // 2

    @pl.when((i == 0) & (j == 0))
    def _():
        acc_ref[...] = jnp.zeros(acc_ref.shape, acc_ref.dtype)

    def run(rows):
        x = x_ref[:rows, :]
        a = jnp.dot(x, wg_ref[0, 0].astype(BF16), preferred_element_type=F32)
        b = jnp.dot(x, wu_ref[0, 0].astype(BF16), preferred_element_type=F32)
        hdn = (a * _sigmoid(a) * b).astype(BF16)
        y = jnp.dot(hdn, wd_ref[0, 0].astype(BF16), preferred_element_type=F32)
        tot = y + jnp.where(j > 0, acc_ref[:rows, :], 0.0)
        acc_ref[:rows, :] = tot
        o_ref[:rows, :] = tot.astype(o_ref.dtype)
        if rows < tm:
            o_ref[rows:, :] = jnp.zeros((tm - rows, o_ref.shape[1]), o_ref.dtype)

    pl.when(nv > half)(lambda: run(tm))
    pl.when((nv > 0) & (nv <= half))(lambda: run(half))

    @pl.when(nv == 0)
    def _():
        o_ref[...] = jnp.zeros(o_ref.shape, o_ref.dtype)


def moe_experts(blk_e, n_valid, n_used, xr, wg, wu, wd, layer):
    P, D = xr.shape
    De = wg.shape[3]
    tm, dc = MOE_TM, MOE_DC
    nj = De // dc
    chunk = lambda i, s: jnp.where(i % 2 == 0, s, nj - 1 - s)
    jj = lambda i, s, nu: jnp.where(i < nu[0], chunk(i, s), chunk(nu[0] - 1, nj - 1))
    grid_spec = pltpu.PrefetchScalarGridSpec(
        num_scalar_prefetch=3,
        grid=(P // tm, nj),
        in_specs=[pl.BlockSpec((tm, D), lambda i, j, be, nv, nu: (i, 0)),
                  pl.BlockSpec((1, 1, D, dc), lambda i, j, be, nv, nu: (layer, be[i], 0, jj(i, j, nu))),
                  pl.BlockSpec((1, 1, D, dc), lambda i, j, be, nv, nu: (layer, be[i], 0, jj(i, j, nu))),
                  pl.BlockSpec((1, 1, dc, D), lambda i, j, be, nv, nu: (layer, be[i], jj(i, j, nu), 0))],
        out_specs=pl.BlockSpec((tm, D), lambda i, j, be, nv, nu: (i, 0)),
        scratch_shapes=[pltpu.VMEM((tm, D), F32)],
    )
    return pl.pallas_call(
        _moe_kernel,
        out_shape=jax.ShapeDtypeStruct((P, D), BF16),
        grid_spec=grid_spec,
        compiler_params=_cparams(("arbitrary", "arbitrary")),
        name="moe_experts",
    )(blk_e, n_valid, n_used, xr, wg, wu, wd)


def _combine_kernel(h_ref, y0_ref, y1_ref, w_ref, g_ref, o_ref, *, final_norm):
    w = w_ref[...]
    x = h_ref[...] + (w[:, 0:1] * y0_ref[...].astype(F32) + w[:, 1:2] * y1_ref[...].astype(F32))
    if final_norm:
        ms = jnp.mean(x * x, axis=-1, keepdims=True)
        x = x * lax.rsqrt(ms + EPS) * g_ref[...]
    o_ref[...] = x


def moe_combine(h, y0, y1, w, g, tm):
    T, D = h.shape
    final_norm = g is not None
    gain = (g if final_norm else jnp.ones((D,), F32)).reshape(1, D).astype(F32)
    row = lambda width: pl.BlockSpec((tm, width), lambda i: (i, 0))
    return pl.pallas_call(
        functools.partial(_combine_kernel, final_norm=final_norm),
        out_shape=jax.ShapeDtypeStruct((T, D), F32),
        grid=(T // tm,),
        in_specs=[row(D), row(D), row(D), row(LANE), pl.BlockSpec((1, D), lambda i: (0, 0))],
        out_specs=row(D),
        compiler_params=_cparams(("parallel",)),
        name="moe_combine",
    )(h, y0, y1, _pad_cols(w, LANE), gain)


def _pad_cols(w, width):
    return jnp.pad(w, ((0, 0), (0, width - w.shape[1])))


def _w_in_segments(D):
    names = ("fq", "fk", "fv", "ff", "mcq", "mckv", "mkr", "nq", "nkc", "nvc", "nks", "nvs", "nkw", "nvw", "ngt", "mg")
    widths = (1024, 1024, 1024, FORGET_COLS, MLA_Q_RANK, MLA_KV_RANK, MLA_ROPE, NSA_HEADS * NSA_DK,
              NSA_GROUPS * NSA_DK, NSA_GROUPS * NSA_DV, NSA_GROUPS * NSA_DK, NSA_GROUPS * NSA_DV,
              NSA_GROUPS * NSA_DK, NSA_GROUPS * NSA_DV, NSA_GATE_COLS, N_BRANCH * D)
    src = dict(zip(names, np.cumsum((0,) + widths[:-1]).tolist()))
    wid = dict(zip(names, widths))
    segs = []
    plain = lambda name, cb, f=1.0: segs.append((cb * LANE, src[name], wid[name], f))

    def padded_k(name, cb, n, f=1.0):
        for i in range(n):
            segs.append((cb * LANE + i * NSA_DKP, src[name] + i * NSA_DK, NSA_DK, f))

    half = MLA_ROPE // 2
    plain("mckv", CB_CKV)
    plain("mkr", CB_KR)
    segs.append((CB_KR * LANE + MLA_ROPE, src["mkr"] + half, half, -1.0))
    segs.append((CB_KR * LANE + MLA_ROPE + half, src["mkr"], half, 1.0))
    plain("mcq", CB_CQ)
    padded_k("nkc", CB_NKC, NSA_GROUPS)
    padded_k("nq", CB_NQ, NSA_HEADS, NSA_DK ** -0.5 * LOG2E)
    plain("fq", CB_FQ, FOX_DH ** -0.5 * LOG2E)
    plain("fk", CB_FK)
    plain("fv", CB_FV)
    padded_k("nks", CB_NKS, NSA_GROUPS)
    padded_k("nkw", CB_NKW, NSA_GROUPS)
    plain("nvc", CB_NVC)
    plain("nvs", CB_NVS)
    plain("nvw", CB_NVW)
    plain("mg", CB_MG)
    return segs, src


def _pack_w_in_kernel(w_ref, o_ref, os_ref, *, segs, small_segs):
    d_in = w_ref.shape[1]

    def put(dst_ref, dst, s, n, f):
        for k in range(0, n, LANE):
            m = min(LANE, n - k)
            assert s + k + LANE <= d_in
            v = w_ref[0, s + k:s + k + LANE, :].T
            if f != 1.0:
                v = v * f
            dst_ref[:, dst + k:dst + k + m] = v[:, :m].astype(dst_ref.dtype)

    o_ref[...] = jnp.zeros(o_ref.shape, o_ref.dtype)
    for dst, s, n, f in segs:
        put(o_ref, dst, s, n, f)
    os_ref[...] = jnp.zeros(os_ref.shape, os_ref.dtype)
    for dst, s, n in small_segs:
        put(os_ref, dst, s, n, 1.0)


def _pack_w_in(w_all, layer):
    _, D, d_in = w_all.shape
    segs, src = _w_in_segments(D)
    small_segs = ((0, src["ff"], FORGET_COLS), (FORGET_COLS, src["ngt"], NSA_GATE_COLS))
    tr = LANE
    return pl.pallas_call(
        functools.partial(_pack_w_in_kernel, segs=tuple(segs), small_segs=small_segs),
        out_shape=(jax.ShapeDtypeStruct((D, Z_BLOCKS * LANE), BF16), jax.ShapeDtypeStruct((D, LANE), F32)),
        grid=(D // tr,),
        in_specs=[pl.BlockSpec((1, d_in, tr), lambda i: (layer, 0, i))],
        out_specs=(pl.BlockSpec((tr, Z_BLOCKS * LANE), lambda i: (i, 0)), pl.BlockSpec((tr, LANE), lambda i: (i, 0))),
        compiler_params=_cparams(("parallel",)),
        name="pack_w_in",
    )(jnp.swapaxes(w_all, 1, 2))


def _pack_w_uq(w):
    K = w.shape[0]
    w3 = w.reshape(K, MLA_HEADS, MLA_NOPE + MLA_ROPE) * ((MLA_NOPE + MLA_ROPE) ** -0.5 * LOG2E)
    nope = w3[:, :, :MLA_NOPE].reshape(K, MLA_HEADS * MLA_NOPE)
    r = w3[:, :, MLA_NOPE:]
    half = MLA_ROPE // 2
    r_rot = jnp.concatenate([-r[:, :, half:], r[:, :, :half]], axis=2)
    padr = lambda a: jnp.pad(a, ((0, 0), (0, 0), (0, LANE - MLA_ROPE))).reshape(K, MLA_HEADS * LANE)
    return jnp.concatenate([nope, padr(r), padr(r_rot)], axis=1).astype(BF16)


def _pack_w_ukv(w):
    K = w.shape[0]
    w3 = w.reshape(K, MLA_HEADS, MLA_NOPE + MLA_DV)
    return jnp.concatenate([w3[:, :, :MLA_NOPE].reshape(K, -1), w3[:, :, MLA_NOPE:].reshape(K, -1)], axis=1).astype(BF16)


def _t5_bucket(dist):
    dist = jnp.maximum(dist, 0)
    exact = REL_BUCKETS // 2
    df = jnp.maximum(dist, 1).astype(F32)
    large = exact + (jnp.log(df / exact) / math.log(REL_MAX_DIST / exact) * (REL_BUCKETS - exact)).astype(jnp.int32)
    large = jnp.minimum(large, REL_BUCKETS - 1)
    return jnp.where(dist < exact, dist, large)


def _position_tables(S, rel_bias):
    t = NSA_T
    half = MLA_ROPE // 2
    inv = ROPE_THETA ** (-jnp.arange(half, dtype=F32) / half)
    ang = jnp.arange(S, dtype=F32)[:, None] * inv
    c, s = jnp.cos(ang), jnp.sin(ang)
    cos = _pad_cols(jnp.concatenate([c, c], axis=1), LANE)
    sin = _pad_cols(jnp.concatenate([s, s], axis=1), LANE)
    ncp = max(S // CMP_STRIDE, LANE)
    pos = jnp.arange(S)

    def bias_of(dist):
        onehot = jax.nn.one_hot(_t5_bucket(dist), REL_BUCKETS, dtype=F32)
        return jnp.einsum("...b,bh->h...", onehot, rel_bias, precision=lax.Precision.HIGHEST)

    bias_c = bias_of(pos[None, :] - (CMP_STRIDE * jnp.arange(ncp)[:, None] + CMP_BLOCK - 1))
    i = jnp.arange(t)
    bias_sw = jnp.stack([bias_of(k * t + i[:, None] - i[None, :]) for k in range(3)], axis=1)
    bias_sw = bias_sw.reshape(NSA_GROUPS, NSA_HPG, 3, t, t).transpose(0, 2, 4, 1, 3).reshape(NSA_GROUPS, 3, t, NSA_HPG * t)
    n_cmp = (S - CMP_BLOCK) // CMP_STRIDE + 1
    n_blk = S // SEL_BLOCK
    nbp = max(n_blk, LANE)
    cstart = CMP_STRIDE * jnp.arange(ncp)
    sstart = SEL_BLOCK * jnp.arange(nbp)
    ov = jnp.clip(jnp.minimum(cstart[:, None] + CMP_BLOCK, sstart[None, :] + SEL_BLOCK)
                  - jnp.maximum(cstart[:, None], sstart[None, :]), 0, None).astype(F32) / CMP_STRIDE
    ov = jnp.where((jnp.arange(ncp)[:, None] < n_cmp) & (jnp.arange(nbp)[None, :] < n_blk), ov, 0.0).astype(BF16)
    expand = ((pos[:, None] // SEL_BLOCK) == jnp.arange(nbp)[None, :]).astype(BF16)
    return cos, sin, bias_c, bias_sw, ov, expand


def _token_mixers(h, p, w_in_all, layer, tabs, B, S):
    T, D = h.shape
    cos, sin, bias_c, bias_sw, overlap, expand = tabs
    wz, w_small = _pack_w_in(w_in_all, layer)
    z, zs = input_projection(h, p["g_mix"], wz, w_small, tm=min(T, 1024), tn=1024)
    z3 = z.reshape(B, S, Z_BLOCKS * LANE)

    log_f = jax.nn.log_sigmoid(zs[:, :FORGET_COLS] + p["b_forget"].astype(F32)).reshape(B, S, FOX_HEADS)
    cum = jnp.cumsum(log_f, axis=1) * LOG2E
    o_fox = causal_attention(z3, CB_FQ, z3, CB_FK, z3, CB_FV, FOX_HEADS, cum=cum)

    q_nope, q_rope = mla_q_proj(z, p["g_cq"], _pack_w_uq(p["w_uq"]), cos, sin, S, tm=min(S, 512))
    kv = rms_matmul(z, CB_CKV * LANE // MLA_KV_RANK, p["g_ckv"], _pack_w_ukv(p["w_ukv"]), BF16, tm=min(T, 1024), tn=1024)
    kr = z3[:, :, CB_KR * LANE:(CB_KR + 1) * LANE].astype(F32)
    kr = kr[..., :MLA_ROPE] * cos[None, :, :MLA_ROPE] + kr[..., MLA_ROPE:] * sin[None, :, :MLA_ROPE]
    k_rope = jnp.pad(kr, ((0, 0), (0, 0), (0, LANE - MLA_ROPE))).astype(BF16)
    hw = MLA_HEADS * LANE
    kv3 = kv.reshape(B, S, 2 * hw)
    o_mla = causal_attention(q_nope.reshape(B, S, hw), 0, kv3, 0, kv3, MLA_HEADS, MLA_HEADS,
                             q2=q_rope.reshape(B, S, hw), k2=k_rope)

    G = NSA_GROUPS
    NC = S // CMP_STRIDE
    ncp = bias_c.shape[1]

    def compress_branch(cb, dp, d, pe, w1, w2):
        x = z3[:, :, cb * LANE: cb * LANE + G * dp].reshape(B, NC, CMP_STRIDE * G * dp)
        eye = jnp.eye(G, dtype=F32)
        w1p = jnp.pad(w1.reshape(CMP_BLOCK, d, d), ((0, 0), (0, dp - d), (0, dp - d)))
        w1g = jnp.einsum("lij,gh->lgihj", w1p, eye).reshape(CMP_BLOCK, G * dp, G * dp).astype(BF16)
        w1a = w1g[:CMP_STRIDE].reshape(CMP_STRIDE * G * dp, G * dp)
        w1b = w1g[CMP_STRIDE:].reshape(CMP_STRIDE * G * dp, G * dp)
        pe_g = jnp.tile(jnp.pad(pe, ((0, 0), (0, dp - d)))[:, None, :], (1, G, 1))
        pe2 = pe_g.reshape(2, CMP_STRIDE * G * dp).astype(BF16)
        w2p = jnp.pad(w2, ((0, dp - d), (0, dp - d)))
        w2g = jnp.einsum("ij,gh->gihj", w2p, eye).reshape(G * dp, G * dp).astype(BF16)
        out = compress(x, w1a, w1b, pe2, w2g)
        return jnp.pad(out, ((0, 0), (0, ncp - NC), (0, 0)))

    kc = compress_branch(CB_NKC, NSA_DKP, NSA_DK, p["pe_k"], p["w_cmp_k1"], p["w_cmp_k2"])
    vc = compress_branch(CB_NVC, NSA_DV, NSA_DV, p["pe_v"], p["w_cmp_v1"], p["w_cmp_v2"])
    gl = zs[:, FORGET_COLS:FORGET_COLS + NSA_GATE_COLS].reshape(B, S, G, NSA_HPG * 3)
    gl = jnp.transpose(gl, (0, 2, 1, 3))
    gl_cols = jnp.pad(jnp.swapaxes(gl, 2, 3), ((0, 0), (0, 0), (0, 16 - NSA_HPG * 3), (0, 0)))
    o_cmp, ind = nsa_cmp_select(z3, kc, vc, bias_c, overlap, gl_cols, min(N_SEL, S // SEL_BLOCK))
    o_nsa = nsa_sel_win(z3, ind, expand, bias_sw, gl_cols, o_cmp)

    merged = merge_branches(o_fox.reshape(T, -1), o_mla.reshape(T, -1), o_nsa.reshape(T, -1),
                            p["w_branch"].astype(BF16), z, tm=min(T, 1024), tn=512)
    return matmul_residual(merged, p["w_out"].astype(BF16), h, tm=min(T, 1024), tn=512)


def _memory_block(h, mem2, p, B, S):
    T, D = h.shape
    kv = rms_matmul(mem2, 0, p["g_mem_kv"], p["w_mem_kv"].astype(BF16), BF16, tm=min(mem2.shape[0], 512), tn=512)
    out = memory_attention_block(h.reshape(B, S, D), p["g_mem_q"], (p["w_mem_q"] * (MEM_DH ** -0.5 * LOG2E)).astype(BF16),
                                 kv.reshape(B, -1, kv.shape[1]), p["w_mem_o"].astype(BF16), tq=min(S, 512))
    return out.reshape(T, D)


def _moe_block(h, p, experts, layer, g_final=None):
    T, D = h.shape
    tm = MOE_TM
    w_r = _pad_cols(jnp.concatenate([p["w_router_group"], p["w_router_expert"]], axis=1), LANE).astype(F32)
    b_r = _pad_cols(jnp.concatenate([p["b_router_group"], p["b_router_expert"]])[None, :], LANE).astype(F32)
    logits, u = moe_router(h, p["g_moe"], w_r, b_r, tm=min(T, 512))
    glog = logits[:, :N_GROUPS]
    gsel = jnp.argmax(glog, axis=-1).astype(jnp.int32)
    pg = jnp.max(jax.nn.softmax(glog, axis=-1), axis=-1, keepdims=True)
    elog = logits[:, N_GROUPS:N_GROUPS + N_EXPERTS].reshape(T, N_GROUPS, EXPERTS_PER_GROUP)
    elog = jnp.take_along_axis(elog, gsel[:, None, None], axis=1)[:, 0]
    eprob = jax.nn.softmax(elog, axis=-1)
    j0 = jnp.argmax(eprob, axis=-1).astype(jnp.int32)
    lane = jnp.arange(EXPERTS_PER_GROUP, dtype=jnp.int32)[None, :]
    j1 = jnp.argmax(jnp.where(lane == j0[:, None], -jnp.inf, eprob), axis=-1).astype(jnp.int32)
    top_j = jnp.stack([j0, j1], axis=-1)
    top_p = jnp.take_along_axis(eprob, top_j, axis=-1)
    top_p = top_p / jnp.sum(top_p, axis=-1, keepdims=True)
    weight = pg * top_p
    flat_e = (gsel[:, None] * EXPERTS_PER_GROUP + top_j.astype(jnp.int32)).reshape(-1)
    TK = T * TOP_K
    onehot = (flat_e[:, None] == jnp.arange(N_EXPERTS, dtype=jnp.int32)[None, :]).astype(jnp.int32)
    rank = jnp.sum((jnp.cumsum(onehot, axis=0) - onehot) * onehot, axis=1)
    counts = jnp.sum(onehot, axis=0)
    pcounts = ((counts + tm - 1) // tm) * tm
    pends = jnp.cumsum(pcounts)
    dest = (pends - pcounts)[flat_e] + rank
    P = TK + N_EXPERTS * tm
    n_rb = P // tm
    row_tok = (jnp.arange(P, dtype=jnp.int32) % T).at[dest].set(jnp.repeat(jnp.arange(T, dtype=jnp.int32), TOP_K))
    blk_e = jnp.sum((pends[None, :] <= (jnp.arange(n_rb, dtype=jnp.int32) * tm)[:, None]).astype(jnp.int32), axis=1)
    blk_e = jnp.minimum(blk_e, N_EXPERTS - 1).astype(jnp.int32)
    n_used = (pends[-1] // tm).astype(jnp.int32).reshape(1)
    row_end = pends - pcounts + counts
    n_valid = jnp.clip(row_end[blk_e] - jnp.arange(n_rb, dtype=jnp.int32) * tm, 0, tm).astype(jnp.int32)
    xr = u[row_tok]
    y = moe_experts(blk_e, n_valid, n_used, xr, experts[0].astype(F32), experts[1].astype(F32),
                    experts[2].astype(F32), layer)
    d2 = dest.reshape(T, TOP_K)
    return moe_combine(h, y[d2[:, 0]], y[d2[:, 1]], weight, g_final, tm=min(T, 512))


_LAYER_KEYS = ("g_mix", "w_in", "b_forget", "g_cq", "g_ckv", "w_uq", "w_ukv", "pe_k", "pe_v", "w_cmp_k1", "w_cmp_k2",
               "w_cmp_v1", "w_cmp_v2", "w_branch", "w_out", "g_mem_q", "g_mem_kv", "w_mem_q", "w_mem_kv", "w_mem_o",
               "g_moe", "w_router_group", "b_router_group", "w_router_expert", "b_router_expert")


def kernel(x, mem, g_mix, w_in, b_forget, g_cq, g_ckv, w_uq, w_ukv, pe_k, pe_v, w_cmp_k1, w_cmp_k2, w_cmp_v1, w_cmp_v2, rel_bias, w_branch, w_out, g_mem_q, g_mem_kv, w_mem_q, w_mem_kv, w_mem_o, g_moe, w_router_group, b_router_group, w_router_expert, b_router_expert, w_exp_gate, w_exp_up, w_exp_down, g_final):
    B, S, D = x.shape
    T = B * S
    stacked = dict(g_mix=g_mix, w_in=w_in, b_forget=b_forget, g_cq=g_cq, g_ckv=g_ckv, w_uq=w_uq, w_ukv=w_ukv,
                   pe_k=pe_k, pe_v=pe_v, w_cmp_k1=w_cmp_k1, w_cmp_k2=w_cmp_k2, w_cmp_v1=w_cmp_v1, w_cmp_v2=w_cmp_v2,
                   w_branch=w_branch, w_out=w_out, g_mem_q=g_mem_q, g_mem_kv=g_mem_kv, w_mem_q=w_mem_q,
                   w_mem_kv=w_mem_kv, w_mem_o=w_mem_o, g_moe=g_moe, w_router_group=w_router_group,
                   b_router_group=b_router_group, w_router_expert=w_router_expert, b_router_expert=b_router_expert,
                   w_exp_gate=w_exp_gate, w_exp_up=w_exp_up, w_exp_down=w_exp_down)
    tabs = _position_tables(S, rel_bias.astype(F32) * LOG2E)
    h = x.reshape(T, D).astype(F32)
    mem2 = mem.reshape(-1, D).astype(F32)
    depth = w_in.shape[0]
    assert depth >= 1
    for l in range(depth):
        p = {k: stacked[k][l] for k in _LAYER_KEYS}
        h = _token_mixers(h, p, w_in.astype(F32), l, tabs, B, S)
        h = _memory_block(h, mem2, p, B, S)
        h = _moe_block(h, p, (w_exp_gate, w_exp_up, w_exp_down), l, g_final if l == depth - 1 else None)
    return h.reshape(B, S, D)
```

```python
import functools
import math

import jax
import jax.numpy as jnp
import numpy as np
from jax import lax
from jax.experimental import pallas as pl
from jax.experimental.pallas import tpu as pltpu

F32 = jnp.float32
BF16 = jnp.bfloat16

EPS = 1e-6
NEG_INF = -1e30
LOG2E = math.log2(math.e)
LANE = 128

FOX_HEADS, FOX_DH = 8, 128
MLA_HEADS, MLA_NOPE, MLA_ROPE, MLA_DV = 8, 128, 64, 128
MLA_Q_RANK, MLA_KV_RANK = 768, 512
ROPE_THETA = 10000.0
NSA_HEADS, NSA_GROUPS, NSA_DK, NSA_DV = 8, 2, 192, 128
NSA_HPG = NSA_HEADS // NSA_GROUPS
NSA_DKP = 256
CMP_BLOCK, CMP_STRIDE, SEL_BLOCK, N_SEL, WINDOW = 32, 16, 64, 8, 512
SEL_FORCE = 1e6
REL_BUCKETS, REL_MAX_DIST = 32, 128
N_BRANCH, BRANCH_W = 3, 1024
MEM_HEADS, MEM_DH = 4, 128
N_GROUPS, EXPERTS_PER_GROUP, TOP_K = 4, 8, 2
N_EXPERTS = N_GROUPS * EXPERTS_PER_GROUP
FORGET_COLS, NSA_GATE_COLS = FOX_HEADS, NSA_HEADS * 3

CB_CKV, CB_KR, CB_CQ = 0, 4, 6
CB_NKC, CB_NQ = 12, 16
CB_FQ, CB_FK, CB_FV = 32, 40, 48
CB_NKS, CB_NKW = 56, 60
CB_NVC, CB_NVS, CB_NVW = 64, 66, 68
CB_MG = 72
Z_BLOCKS = 120

ATT_T = 256
ATT_HG = 8
NSA_T = 256
NSA_CMP_T = 512
MOE_TM = 768
MOE_SUB = 256
MOE_DC = 256
VMEM_LIMIT = 56 * 1024 * 1024


def _cparams(sem):
    return pltpu.CompilerParams(dimension_semantics=sem, vmem_limit_bytes=VMEM_LIMIT)


def _sigmoid(x):
    return 1.0 / (1.0 + jnp.exp(-x))


def _rms_mm_kernel(x_ref, g_ref, w_ref, o_ref, u_ref):
    @pl.when(pl.program_id(1) == 0)
    def _():
        x = x_ref[...].astype(F32)
        ms = jnp.mean(x * x, axis=-1, keepdims=True)
        u_ref[...] = (x * lax.rsqrt(ms + EPS) * g_ref[...]).astype(u_ref.dtype)

    o_ref[...] = jnp.dot(u_ref[...], w_ref[...], preferred_element_type=F32).astype(o_ref.dtype)


def rms_matmul(x, xcol, g, w, out_dtype, tm, tn):
    T = x.shape[0]
    K, N = w.shape
    return pl.pallas_call(
        _rms_mm_kernel,
        out_shape=jax.ShapeDtypeStruct((T, N), out_dtype),
        grid=(T // tm, N // tn),
        in_specs=[pl.BlockSpec((tm, K), lambda i, j: (i, xcol)),
                  pl.BlockSpec((1, K), lambda i, j: (0, 0)),
                  pl.BlockSpec((K, tn), lambda i, j: (0, j))],
        out_specs=pl.BlockSpec((tm, tn), lambda i, j: (i, j)),
        scratch_shapes=[pltpu.VMEM((tm, K), BF16)],
        compiler_params=_cparams(("parallel", "arbitrary")),
        name="rms_matmul",
    )(x, g.reshape(1, K).astype(F32), w)


def _in_proj_kernel(x_ref, g_ref, w_ref, ws_ref, o_ref, os_ref, u_ref):
    @pl.when(pl.program_id(1) == 0)
    def _():
        x = x_ref[...]
        ms = jnp.mean(x * x, axis=-1, keepdims=True)
        u = x * lax.rsqrt(ms + EPS) * g_ref[...]
        u_ref[...] = u.astype(u_ref.dtype)
        os_ref[...] = _dot3(u, ws_ref[...])

    o_ref[...] = jnp.dot(u_ref[...], w_ref[...], preferred_element_type=F32).astype(o_ref.dtype)


def input_projection(x, g, w, w_small, tm, tn):
    T, K = x.shape
    N = w.shape[1]
    Ns = w_small.shape[1]
    return pl.pallas_call(
        _in_proj_kernel,
        out_shape=(jax.ShapeDtypeStruct((T, N), BF16), jax.ShapeDtypeStruct((T, Ns), F32)),
        grid=(T // tm, N // tn),
        in_specs=[pl.BlockSpec((tm, K), lambda i, j: (i, 0)),
                  pl.BlockSpec((1, K), lambda i, j: (0, 0)),
                  pl.BlockSpec((K, tn), lambda i, j: (0, j)),
                  pl.BlockSpec((K, Ns), lambda i, j: (0, 0))],
        out_specs=(pl.BlockSpec((tm, tn), lambda i, j: (i, j)), pl.BlockSpec((tm, Ns), lambda i, j: (i, 0))),
        scratch_shapes=[pltpu.VMEM((tm, K), BF16)],
        compiler_params=_cparams(("parallel", "arbitrary")),
        name="input_projection",
    )(x, g.reshape(1, K).astype(F32), w, w_small)


def _mla_q_kernel(x_ref, g_ref, w_ref, cos_ref, sin_ref, qn_ref, qr_ref):
    x = x_ref[...].astype(F32)
    ms = jnp.mean(x * x, axis=-1, keepdims=True)
    u = (x * lax.rsqrt(ms + EPS) * g_ref[...]).astype(BF16)
    y = jnp.dot(u, w_ref[...], preferred_element_type=F32)
    hw = MLA_HEADS * LANE
    qn_ref[...] = y[:, :hw].astype(qn_ref.dtype)
    cos = cos_ref[...]
    sin = sin_ref[...]
    for h in range(MLA_HEADS):
        a = y[:, hw + h * LANE: hw + (h + 1) * LANE]
        b = y[:, 2 * hw + h * LANE: 2 * hw + (h + 1) * LANE]
        qr_ref[:, h * LANE:(h + 1) * LANE] = (a * cos + b * sin).astype(qr_ref.dtype)


def mla_q_proj(z, g, w, cos, sin, S, tm):
    T = z.shape[0]
    K, N = w.shape
    hw = MLA_HEADS * LANE
    nsb = S // tm
    return pl.pallas_call(
        _mla_q_kernel,
        out_shape=(jax.ShapeDtypeStruct((T, hw), BF16), jax.ShapeDtypeStruct((T, hw), BF16)),
        grid=(T // tm,),
        in_specs=[pl.BlockSpec((tm, K), lambda i: (i, CB_CQ * LANE // MLA_Q_RANK)),
                  pl.BlockSpec((1, K), lambda i: (0, 0)),
                  pl.BlockSpec((K, N), lambda i: (0, 0)),
                  pl.BlockSpec((tm, LANE), lambda i: (i % nsb, 0)),
                  pl.BlockSpec((tm, LANE), lambda i: (i % nsb, 0))],
        out_specs=(pl.BlockSpec((tm, hw), lambda i: (i, 0)), pl.BlockSpec((tm, hw), lambda i: (i, 0))),
        compiler_params=_cparams(("parallel",)),
        name="mla_q_proj",
    )(z, g.reshape(1, K).astype(F32), w, cos, sin)


def _causal_attn_kernel(*refs, t, hg, two_part, decay):
    refs = list(refs)
    q_ref, k_ref, v_ref = refs[:3]
    pos = 3
    if two_part:
        q2_ref, k2_ref = refs[pos:pos + 2]
        pos += 2
    if decay:
        ck_ref = refs[pos]
        pos += 1
    o_ref = refs[pos]
    if two_part:
        kcat_ref = refs[pos + 1]
    qi = pl.program_id(2)
    dn = (((1,), (1,)), ((), ()))

    if two_part:
        @pl.when(qi == 0)
        def _():
            for j in range(hg):
                kcat_ref[j, :, :LANE] = k_ref[0, :, j * LANE:(j + 1) * LANE]
                kcat_ref[j, :, LANE:] = k2_ref[0]

    qs = []
    for j in range(hg):
        qj = q_ref[0, :, j * LANE:(j + 1) * LANE]
        if two_part:
            qj = jnp.concatenate([qj, q2_ref[0, :, j * LANE:(j + 1) * LANE]], axis=1)
        qs.append(qj)

    def step(kb, carry, masked, width=1):
        off = pl.multiple_of(kb * t, t)
        tk = width * t
        heads = range(hg)
        ss = []
        for j in heads:
            k = kcat_ref[j, pl.ds(off, tk), :] if two_part else k_ref[0, pl.ds(off, tk), j * LANE:(j + 1) * LANE]
            ss.append(lax.dot_general(k, qs[j], dn, preferred_element_type=F32))
        if decay:
            ss = [ss[j] - ck_ref[0, 0, pl.ds(off, tk), j:j + 1] for j in heads]
        if masked:
            r = lax.broadcasted_iota(jnp.int32, (tk, t), 0)
            c = lax.broadcasted_iota(jnp.int32, (tk, t), 1)
            ss = [jnp.where(r <= c, s, NEG_INF) for s in ss]
        ms = [jnp.maximum(carry[j][0], jnp.max(ss[j], axis=0, keepdims=True)) for j in heads]
        ps = [jnp.exp2(ss[j] - ms[j]) for j in heads]
        out = []
        for j in heads:
            m, l, acc = carry[j]
            a = jnp.exp2(m - ms[j])
            l = a * l + jnp.sum(ps[j], axis=0, keepdims=True)
            v = v_ref[0, pl.ds(off, tk), j * LANE:(j + 1) * LANE]
            acc = a * acc + lax.dot_general(v, ps[j].astype(BF16), (((0,), (0,)), ((), ())),
                                            preferred_element_type=F32)
            out.append((ms[j], l, acc))
        return tuple(out)

    init = tuple((jnp.full((1, t), NEG_INF, F32), jnp.zeros((1, t), F32), jnp.zeros((LANE, t), F32))
                 for _ in range(hg))
    carry = lax.fori_loop(0, qi // 2, lambda kp, c: step(2 * kp, c, False, width=2), init)
    carry = lax.cond(qi % 2 == 1, lambda c: step(qi - 1, c, False), lambda c: c, carry)
    carry = step(qi, carry, True)
    for j in range(hg):
        _, l, acc = carry[j]
        o_ref[0, :, j * LANE:(j + 1) * LANE] = (acc / l).T.astype(o_ref.dtype)


def causal_attention(q, qcb, k, kcb, v, vcb, heads, q2=None, k2=None, cum=None):
    B, S, _ = q.shape
    t, hg = ATT_T, ATT_HG
    w = hg * LANE
    two_part, decay = q2 is not None, cum is not None
    in_specs = [pl.BlockSpec((1, t, w), lambda b, h, i: (b, i, qcb // hg + h)),
                pl.BlockSpec((1, S, w), lambda b, h, i: (b, 0, kcb // hg + h)),
                pl.BlockSpec((1, S, w), lambda b, h, i: (b, 0, vcb // hg + h))]
    args = [q, k, v]
    scratch = []
    if two_part:
        in_specs += [pl.BlockSpec((1, t, w), lambda b, h, i: (b, i, h)),
                     pl.BlockSpec((1, S, LANE), lambda b, h, i: (b, 0, 0))]
        args += [q2, k2]
        scratch = [pltpu.VMEM((hg, S, 2 * LANE), BF16)]
    if decay:
        in_specs += [pl.BlockSpec((1, 1, S, hg), lambda b, h, i: (b, h, 0, 0))]
        args += [jnp.transpose(cum.reshape(B, S, heads // hg, hg), (0, 2, 1, 3))]
    return pl.pallas_call(
        functools.partial(_causal_attn_kernel, t=t, hg=hg, two_part=two_part, decay=decay),
        out_shape=jax.ShapeDtypeStruct((B, S, heads * LANE), BF16),
        grid=(B, heads // hg, S // t),
        in_specs=in_specs,
        out_specs=pl.BlockSpec((1, t, w), lambda b, h, i: (b, i, h)),
        scratch_shapes=scratch,
        compiler_params=_cparams(("parallel", "parallel", "arbitrary")),
        name="causal_attention",
    )(*args)


def _gelu_tanh(x):
    return 0.5 * x * (1.0 + jnp.tanh(math.sqrt(2.0 / math.pi) * (x + 0.044715 * (x * x * x))))


def _compress_kernel(x_ref, w1a_ref, w1b_ref, pe_ref, w2_ref, o_ref):
    x = x_ref[0]
    a = jnp.dot(x, w1a_ref[...], preferred_element_type=F32)
    b = jnp.dot(x, w1b_ref[...], preferred_element_type=F32)
    nc = a.shape[0]
    b_next = pltpu.roll(b, nc - 1, 0)
    pe_term = jnp.dot(pe_ref[0:1, :], w1a_ref[...], preferred_element_type=F32) + \
        jnp.dot(pe_ref[1:2, :], w1b_ref[...], preferred_element_type=F32)
    hid = _gelu_tanh(a + b_next + pe_term)
    o_ref[0] = jnp.dot(hid.astype(BF16), w2_ref[...], preferred_element_type=F32).astype(o_ref.dtype)


def compress(x, w1a, w1b, pe2, w2):
    BG, NC, KD = x.shape
    dp = w2.shape[1]
    return pl.pallas_call(
        _compress_kernel,
        out_shape=jax.ShapeDtypeStruct((BG, NC, dp), BF16),
        grid=(BG,),
        in_specs=[pl.BlockSpec((1, NC, KD), lambda i: (i, 0, 0)),
                  pl.BlockSpec((KD, dp), lambda i: (0, 0)),
                  pl.BlockSpec((KD, dp), lambda i: (0, 0)),
                  pl.BlockSpec((2, KD), lambda i: (0, 0)),
                  pl.BlockSpec((dp, dp), lambda i: (0, 0))],
        out_specs=pl.BlockSpec((1, NC, dp), lambda i: (i, 0, 0)),
        compiler_params=_cparams(("parallel",)),
        name="nsa_compress",
    )(x, w1a, w1b, pe2, w2)


def _nsa_cmp_kernel(q_ref, kc_ref, vc_ref, bias_ref, ov_ref, gate_ref, o_ref, ind_ref, *, t, n_sel, n_rows):
    qi = pl.program_id(2)
    ncp = kc_ref.shape[1]
    kc = kc_ref[0]
    vc = vc_ref[0]
    pos = lax.broadcasted_iota(jnp.int32, (ncp, t), 1) + qi * t
    cblk = lax.broadcasted_iota(jnp.int32, (ncp, t), 0)
    valid = pos >= CMP_STRIDE * cblk + (CMP_BLOCK - 1)
    dn = (((1,), (1,)), ((), ()))
    tn = (((0,), (0,)), ((), ()))
    heads = range(NSA_HPG)
    ss = [lax.dot_general(kc, q_ref[0, :, h * NSA_DKP:(h + 1) * NSA_DKP], dn, preferred_element_type=F32)
          for h in heads]
    ss = [jnp.where(valid, ss[h] + bias_ref[h], NEG_INF) for h in heads]
    es = [jnp.exp2(ss[h] - jnp.max(ss[h], axis=0, keepdims=True)) for h in heads]
    ps = [jnp.where(valid, es[h] / jnp.sum(es[h], axis=0, keepdims=True), 0.0) for h in heads]
    gates = _sigmoid(gate_ref[0, 0])
    for h in heads:
        o = lax.dot_general(vc, ps[h].astype(BF16), tn, preferred_element_type=F32)
        o_ref[0, :, h * NSA_DV:(h + 1) * NSA_DV] = (gates[3 * h:3 * h + 1, :] * o).T.astype(o_ref.dtype)
    psum = functools.reduce(lambda x, y: x + y, ps)
    p_hi = psum.astype(BF16)
    p_lo = (psum - p_hi.astype(F32)).astype(BF16)
    imp = lax.dot_general(ov_ref[...], p_hi, tn, preferred_element_type=F32) + \
        lax.dot_general(ov_ref[...], p_lo, tn, preferred_element_type=F32)
    nbp = imp.shape[0]
    imp = imp[:n_rows]
    blk = lax.broadcasted_iota(jnp.int32, (n_rows, t), 0)
    cur = (lax.broadcasted_iota(jnp.int32, (n_rows, t), 1) + qi * t) // SEL_BLOCK
    forced = (blk == 0) | (blk == cur) | (blk == cur - 1)
    score = jnp.where(blk <= cur, imp + jnp.where(forced, SEL_FORCE, 0.0), NEG_INF)
    sel = jnp.zeros((n_rows, t), F32)
    for _ in range(n_sel):
        mx = jnp.max(score, axis=0, keepdims=True)
        first = jnp.min(jnp.where(score == mx, blk, nbp), axis=0, keepdims=True)
        hit = blk == first
        sel = jnp.where(hit, 1.0, sel)
        score = jnp.where(hit, -jnp.inf, score)
    ind_ref[0, 0] = jnp.zeros(ind_ref.shape[2:], ind_ref.dtype)
    ind_ref[0, 0, :n_rows, :] = sel.astype(ind_ref.dtype)


def nsa_cmp_select(z3, kc, vc, bias_c, overlap, gates, n_sel):
    B, S, _ = z3.shape
    G = NSA_GROUPS
    t = min(NSA_CMP_T, S)
    ncp = kc.shape[1]
    nbp = overlap.shape[1]
    qw = NSA_HPG * NSA_DKP
    ow = NSA_HPG * NSA_DV
    n_rows = min(nbp, -(-(S // SEL_BLOCK) // 16) * 16)
    return pl.pallas_call(
        functools.partial(_nsa_cmp_kernel, t=t, n_sel=n_sel, n_rows=n_rows),
        out_shape=(jax.ShapeDtypeStruct((B, S, G * ow), BF16), jax.ShapeDtypeStruct((B, G, nbp, S), BF16)),
        grid=(B, G, S // t),
        in_specs=[pl.BlockSpec((1, t, qw), lambda b, g, i: (b, i, CB_NQ * LANE // qw + g)),
                  pl.BlockSpec((1, ncp, NSA_DKP), lambda b, g, i: (b, 0, g)),
                  pl.BlockSpec((1, ncp, NSA_DV), lambda b, g, i: (b, 0, g)),
                  pl.BlockSpec((NSA_HPG, ncp, t), lambda b, g, i: (g, 0, i)),
                  pl.BlockSpec((ncp, nbp), lambda b, g, i: (0, 0)),
                  pl.BlockSpec((1, 1, gates.shape[2], t), lambda b, g, i: (b, g, 0, i))],
        out_specs=(pl.BlockSpec((1, t, ow), lambda b, g, i: (b, i, g)),
                   pl.BlockSpec((1, 1, nbp, t), lambda b, g, i: (b, g, 0, i))),
        compiler_params=_cparams(("parallel", "parallel", "arbitrary")),
        name="nsa_cmp_select",
    )(z3, kc, vc, bias_c, overlap, gates)


def _nsa_sw_kernel(q_ref, ks_ref, vs_ref, kw_ref, vw_ref, ind_ref, e_ref, bias_ref, gate_ref, oc_ref, o_ref,
                   *, t):
    qi = pl.program_id(1)
    hp, G = NSA_HPG, NSA_GROUPS
    q4 = [jnp.concatenate([q_ref[0, :, (g * hp + h) * NSA_DKP:(g * hp + h + 1) * NSA_DKP] for h in range(hp)], axis=0)
          for g in range(G)]
    inds = [ind_ref[0, g] for g in range(G)]
    dn = (((1,), (1,)), ((), ()))

    def step(kb, carry, k_ref, v_ref, selected, width=1):
        off = pl.multiple_of(kb * t, t)
        tk = width * t
        ri = lax.broadcasted_iota(jnp.int32, (tk, t), 0)
        ci = lax.broadcasted_iota(jnp.int32, (tk, t), 1)
        d = (qi - kb) * t + ci - ri
        near = (d >= 0) if selected else (d >= 0) & (d < WINDOW)
        groups = range(G)
        ss = [lax.dot_general(k_ref[0, pl.ds(off, tk), g * NSA_DKP:(g + 1) * NSA_DKP], q4[g], dn,
                              preferred_element_type=F32) for g in groups]
        negs = []
        for g in groups:
            mask = near
            if selected:
                hit = jnp.dot(e_ref[pl.ds(off, tk), :], inds[g], preferred_element_type=F32)
                mask = near & (hit > 0.5)
            neg = jnp.where(mask, 0.0, NEG_INF)
            negs.append(jnp.concatenate([neg] * hp, axis=1))
        bias = [jnp.concatenate([bias_ref[g, jnp.minimum(qi - kb - w, 2)] for w in range(width)], axis=0)
                if width > 1 else bias_ref[g, jnp.minimum(qi - kb, 2)] for g in groups]
        ss = [ss[g] + bias[g] + negs[g] for g in groups]
        ms = [jnp.maximum(carry[g][0], jnp.max(ss[g], axis=0, keepdims=True)) for g in groups]
        ps = [jnp.exp2(ss[g] - ms[g]) for g in groups]
        out = []
        for g in groups:
            m, l, acc = carry[g]
            a = jnp.exp2(m - ms[g])
            l = a * l + jnp.sum(ps[g], axis=0, keepdims=True)
            v = v_ref[0, pl.ds(off, tk), g * NSA_DV:(g + 1) * NSA_DV]
            acc = a * acc + lax.dot_general(v, ps[g].astype(BF16), (((0,), (0,)), ((), ())),
                                            preferred_element_type=F32)
            out.append((ms[g], l, acc))
        return tuple(out)

    init = tuple((jnp.full((1, hp * t), NEG_INF, F32), jnp.zeros((1, hp * t), F32), jnp.zeros((NSA_DV, hp * t), F32))
                 for _ in range(G))
    sel = lax.fori_loop(0, (qi + 1) // 2, lambda kp, c: step(2 * kp, c, ks_ref, vs_ref, True, width=2), init)
    sel = lax.cond(qi % 2 == 0, lambda c: step(qi, c, ks_ref, vs_ref, True), lambda c: c, sel)
    lo = jnp.maximum(qi - WINDOW // t, 0)
    win = lax.fori_loop(lo, qi + 1, lambda kb, c: step(kb, c, kw_ref, vw_ref, False), init)
    for g in range(G):
        o_s = sel[g][2] / sel[g][1]
        o_w = win[g][2] / win[g][1]
        gates = _sigmoid(gate_ref[0, g])
        for h in range(hp):
            r = slice(h * t, (h + 1) * t)
            c = slice((g * hp + h) * NSA_DV, (g * hp + h + 1) * NSA_DV)
            o = gates[3 * h + 1:3 * h + 2, :] * o_s[:, r] + gates[3 * h + 2:3 * h + 3, :] * o_w[:, r]
            o_ref[0, :, c] = (oc_ref[0, :, c].astype(F32) + o.T).astype(o_ref.dtype)


def nsa_sel_win(z3, ind, expand, bias_sw, gates, o_cmp):
    B, S, _ = z3.shape
    G = NSA_GROUPS
    t = NSA_T
    qw = NSA_HEADS * NSA_DKP
    ow = NSA_HEADS * NSA_DV
    kw = G * NSA_DKP
    vw = G * NSA_DV
    nbp = ind.shape[2]
    kspec = lambda cb: pl.BlockSpec((1, S, kw), lambda b, i: (b, 0, cb * LANE // kw))
    vspec = lambda cb: pl.BlockSpec((1, S, vw), lambda b, i: (b, 0, cb * LANE // vw))
    return pl.pallas_call(
        functools.partial(_nsa_sw_kernel, t=t),
        out_shape=jax.ShapeDtypeStruct((B, S, ow), BF16),
        grid=(B, S // t),
        in_specs=[pl.BlockSpec((1, t, qw), lambda b, i: (b, i, CB_NQ * LANE // qw)),
                  kspec(CB_NKS), vspec(CB_NVS), kspec(CB_NKW), vspec(CB_NVW),
                  pl.BlockSpec((1, G, nbp, t), lambda b, i: (b, 0, 0, i)),
                  pl.BlockSpec((S, nbp), lambda b, i: (0, 0)),
                  pl.BlockSpec((G, 3, t, NSA_HPG * t), lambda b, i: (0, 0, 0, 0)),
                  pl.BlockSpec((1, G, gates.shape[2], t), lambda b, i: (b, 0, 0, i)),
                  pl.BlockSpec((1, t, ow), lambda b, i: (b, i, 0))],
        out_specs=pl.BlockSpec((1, t, ow), lambda b, i: (b, i, 0)),
        compiler_params=_cparams(("parallel", "arbitrary")),
        name="nsa_sel_win",
    )(z3, z3, z3, z3, z3, ind, expand, bias_sw, gates, o_cmp)


def _merge_kernel(of_ref, om_ref, on_ref, wb_ref, g0_ref, g1_ref, g2_ref, o_ref):
    acc = None
    for n, (o_r, g_r) in enumerate(((of_ref, g0_ref), (om_ref, g1_ref), (on_ref, g2_ref))):
        y = jnp.dot(o_r[...], wb_ref[n], preferred_element_type=F32)
        y = _sigmoid(g_r[...].astype(F32)) * y
        acc = y if acc is None else acc + y
    o_ref[...] = acc.astype(o_ref.dtype)


def merge_branches(o_fox, o_mla, o_nsa, wb, z, tm, tn):
    T = o_fox.shape[0]
    D = wb.shape[2]
    gspec = lambda n: pl.BlockSpec((tm, tn), lambda i, j: (i, (CB_MG * LANE + n * D) // tn + j))
    ospec = pl.BlockSpec((tm, BRANCH_W), lambda i, j: (i, 0))
    return pl.pallas_call(
        _merge_kernel,
        out_shape=jax.ShapeDtypeStruct((T, D), BF16),
        grid=(T // tm, D // tn),
        in_specs=[ospec, ospec, ospec,
                  pl.BlockSpec((N_BRANCH, BRANCH_W, tn), lambda i, j: (0, 0, j)),
                  gspec(0), gspec(1), gspec(2)],
        out_specs=pl.BlockSpec((tm, tn), lambda i, j: (i, j)),
        compiler_params=_cparams(("parallel", "arbitrary")),
        name="merge_branches",
    )(o_fox, o_mla, o_nsa, wb, z, z, z)


def _mm_res_kernel(a_ref, w_ref, r_ref, o_ref):
    o_ref[...] = r_ref[...] + jnp.dot(a_ref[...], w_ref[...], preferred_element_type=F32)


def matmul_residual(a, w, res, tm, tn):
    T, K = a.shape
    N = w.shape[1]
    return pl.pallas_call(
        _mm_res_kernel,
        out_shape=jax.ShapeDtypeStruct((T, N), F32),
        grid=(T // tm, N // tn),
        in_specs=[pl.BlockSpec((tm, K), lambda i, j: (i, 0)),
                  pl.BlockSpec((K, tn), lambda i, j: (0, j)),
                  pl.BlockSpec((tm, tn), lambda i, j: (i, j))],
        out_specs=pl.BlockSpec((tm, tn), lambda i, j: (i, j)),
        compiler_params=_cparams(("parallel", "arbitrary")),
        name="matmul_residual",
    )(a, w, res)


def _mem_attn_kernel(h_ref, g_ref, wq_ref, kv_ref, wo_ref, o_ref):
    x = h_ref[0]
    ms = jnp.mean(x * x, axis=-1, keepdims=True)
    u = (x * lax.rsqrt(ms + EPS) * g_ref[...]).astype(BF16)
    q = jnp.dot(u, wq_ref[...], preferred_element_type=F32).astype(BF16)
    dn = (((1,), (1,)), ((), ()))
    hw = MEM_HEADS * MEM_DH
    heads = range(MEM_HEADS)
    cs = [slice(h * MEM_DH, (h + 1) * MEM_DH) for h in heads]
    ss = [lax.dot_general(q[:, cs[h]], kv_ref[0, :, cs[h]], dn, preferred_element_type=F32) for h in heads]
    es = [jnp.exp2(ss[h] - jnp.max(ss[h], axis=-1, keepdims=True)) for h in heads]
    ps = [es[h] / jnp.sum(es[h], axis=-1, keepdims=True) for h in heads]
    outs = [jnp.dot(ps[h].astype(BF16), kv_ref[0, :, hw + h * MEM_DH: hw + (h + 1) * MEM_DH],
                    preferred_element_type=F32).astype(BF16) for h in heads]
    o = jnp.concatenate(outs, axis=1)
    o_ref[0] = x + jnp.dot(o, wo_ref[...], preferred_element_type=F32)


def memory_attention_block(h3, g, wq, kv, wo, tq):
    B, S, D = h3.shape
    M = kv.shape[1]
    hw = MEM_HEADS * MEM_DH
    return pl.pallas_call(
        _mem_attn_kernel,
        out_shape=jax.ShapeDtypeStruct((B, S, D), F32),
        grid=(B, S // tq),
        in_specs=[pl.BlockSpec((1, tq, D), lambda b, i: (b, i, 0)),
                  pl.BlockSpec((1, D), lambda b, i: (0, 0)),
                  pl.BlockSpec((D, hw), lambda b, i: (0, 0)),
                  pl.BlockSpec((1, M, 2 * hw), lambda b, i: (b, 0, 0)),
                  pl.BlockSpec((hw, D), lambda b, i: (0, 0))],
        out_specs=pl.BlockSpec((1, tq, D), lambda b, i: (b, i, 0)),
        compiler_params=_cparams(("parallel", "arbitrary")),
        name="memory_attention",
    )(h3, g.reshape(1, D).astype(F32), wq, kv, wo)


def _dot3(a, b):
    a_hi = a.astype(BF16)
    a_lo = (a - a_hi.astype(F32)).astype(BF16)
    b_hi = b.astype(BF16)
    b_lo = (b - b_hi.astype(F32)).astype(BF16)
    dot = functools.partial(jnp.dot, preferred_element_type=F32)
    return dot(a_hi, b_hi) + (dot(a_lo, b_hi) + dot(a_hi, b_lo))


def _router_kernel(h_ref, g_ref, w_ref, b_ref, lg_ref, u_ref):
    x = h_ref[...]
    ms = jnp.mean(x * x, axis=-1, keepdims=True)
    u = x * lax.rsqrt(ms + EPS) * g_ref[...]
    u_ref[...] = u.astype(u_ref.dtype)
    lg_ref[...] = _dot3(u, w_ref[...]) + b_ref[...]


def moe_router(h, g, w, b, tm):
    T, D = h.shape
    N = w.shape[1]
    return pl.pallas_call(
        _router_kernel,
        out_shape=(jax.ShapeDtypeStruct((T, N), F32), jax.ShapeDtypeStruct((T, D), BF16)),
        grid=(T // tm,),
        in_specs=[pl.BlockSpec((tm, D), lambda i: (i, 0)),
                  pl.BlockSpec((1, D), lambda i: (0, 0)),
                  pl.BlockSpec((D, N), lambda i: (0, 0)),
                  pl.BlockSpec((1, N), lambda i: (0, 0))],
        out_specs=(pl.BlockSpec((tm, N), lambda i: (i, 0)), pl.BlockSpec((tm, D), lambda i: (i, 0))),
        compiler_params=_cparams(("parallel",)),
        name="moe_router",
    )(h, g.reshape(1, D).astype(F32), w, b)


def _moe_kernel(be_ref, nv_ref, nu_ref, x_ref, wg_ref, wu_ref, wd_ref, o_ref, acc_ref):
    i = pl.program_id(0)
    j = pl.program_id(1)
    nv = nv_ref[i]
    tm = x_ref.shape[0]

    @pl.when((i == 0) & (j == 0))
    def _():
        acc_ref[...] = jnp.zeros(acc_ref.shape, acc_ref.dtype)

    def run(rows):
        x = x_ref[:rows, :]
        a = jnp.dot(x, wg_ref[0, 0].astype(BF16), preferred_element_type=F32)
        b = jnp.dot(x, wu_ref[0, 0].astype(BF16), preferred_element_type=F32)
        hdn = (a * _sigmoid(a) * b).astype(BF16)
        y = jnp.dot(hdn, wd_ref[0, 0].astype(BF16), preferred_element_type=F32)
        tot = y + jnp.where(j > 0, acc_ref[:rows, :], 0.0)
        acc_ref[:rows, :] = tot
        o_ref[:rows, :] = tot.astype(o_ref.dtype)
        if rows < tm:
            o_ref[rows:, :] = jnp.zeros((tm - rows, o_ref.shape[1]), o_ref.dtype)

    for rows in range(MOE_SUB, tm + 1, MOE_SUB):
        pl.when((nv > rows - MOE_SUB) & (nv <= rows))(functools.partial(run, rows))

    @pl.when(nv == 0)
    def _():
        o_ref[...] = jnp.zeros(o_ref.shape, o_ref.dtype)


def moe_experts(blk_e, n_valid, n_used, xr, wg, wu, wd, layer):
    P, D = xr.shape
    De = wg.shape[3]
    tm, dc = MOE_TM, MOE_DC
    nj = De // dc
    chunk = lambda i, s: jnp.where(i % 2 == 0, s, nj - 1 - s)
    jj = lambda i, s, nu: jnp.where(i < nu[0], chunk(i, s), chunk(nu[0] - 1, nj - 1))
    grid_spec = pltpu.PrefetchScalarGridSpec(
        num_scalar_prefetch=3,
        grid=(P // tm, nj),
        in_specs=[pl.BlockSpec((tm, D), lambda i, j, be, nv, nu: (i, 0)),
                  pl.BlockSpec((1, 1, D, dc), lambda i, j, be, nv, nu: (layer, be[i], 0, jj(i, j, nu))),
                  pl.BlockSpec((1, 1, D, dc), lambda i, j, be, nv, nu: (layer, be[i], 0, jj(i, j, nu))),
                  pl.BlockSpec((1, 1, dc, D), lambda i, j, be, nv, nu: (layer, be[i], jj(i, j, nu), 0))],
        out_specs=pl.BlockSpec((tm, D), lambda i, j, be, nv, nu: (i, 0)),
        scratch_shapes=[pltpu.VMEM((tm, D), F32)],
    )
    return pl.pallas_call(
        _moe_kernel,
        out_shape=jax.ShapeDtypeStruct((P, D), BF16),
        grid_spec=grid_spec,
        compiler_params=_cparams(("arbitrary", "arbitrary")),
        name="moe_experts",
    )(blk_e, n_valid, n_used, xr, wg, wu, wd)


def _combine_kernel(h_ref, y0_ref, y1_ref, w_ref, g_ref, o_ref, *, final_norm):
    w = w_ref[...]
    x = h_ref[...] + (w[:, 0:1] * y0_ref[...].astype(F32) + w[:, 1:2] * y1_ref[...].astype(F32))
    if final_norm:
        ms = jnp.mean(x * x, axis=-1, keepdims=True)
        x = x * lax.rsqrt(ms + EPS) * g_ref[...]
    o_ref[...] = x


def moe_combine(h, y0, y1, w, g, tm):
    T, D = h.shape
    final_norm = g is not None
    gain = (g if final_norm else jnp.ones((D,), F32)).reshape(1, D).astype(F32)
    row = lambda width: pl.BlockSpec((tm, width), lambda i: (i, 0))
    return pl.pallas_call(
        functools.partial(_combine_kernel, final_norm=final_norm),
        out_shape=jax.ShapeDtypeStruct((T, D), F32),
        grid=(T // tm,),
        in_specs=[row(D), row(D), row(D), row(LANE), pl.BlockSpec((1, D), lambda i: (0, 0))],
        out_specs=row(D),
        compiler_params=_cparams(("parallel",)),
        name="moe_combine",
    )(h, y0, y1, _pad_cols(w, LANE), gain)


def _pad_cols(w, width):
    return jnp.pad(w, ((0, 0), (0, width - w.shape[1])))


def _w_in_segments(D):
    names = ("fq", "fk", "fv", "ff", "mcq", "mckv", "mkr", "nq", "nkc", "nvc", "nks", "nvs", "nkw", "nvw", "ngt", "mg")
    widths = (1024, 1024, 1024, FORGET_COLS, MLA_Q_RANK, MLA_KV_RANK, MLA_ROPE, NSA_HEADS * NSA_DK,
              NSA_GROUPS * NSA_DK, NSA_GROUPS * NSA_DV, NSA_GROUPS * NSA_DK, NSA_GROUPS * NSA_DV,
              NSA_GROUPS * NSA_DK, NSA_GROUPS * NSA_DV, NSA_GATE_COLS, N_BRANCH * D)
    src = dict(zip(names, np.cumsum((0,) + widths[:-1]).tolist()))
    wid = dict(zip(names, widths))
    segs = []
    plain = lambda name, cb, f=1.0: segs.append((cb * LANE, src[name], wid[name], f))

    def padded_k(name, cb, n, f=1.0):
        for i in range(n):
            segs.append((cb * LANE + i * NSA_DKP, src[name] + i * NSA_DK, NSA_DK, f))

    half = MLA_ROPE // 2
    plain("mckv", CB_CKV)
    plain("mkr", CB_KR)
    segs.append((CB_KR * LANE + MLA_ROPE, src["mkr"] + half, half, -1.0))
    segs.append((CB_KR * LANE + MLA_ROPE + half, src["mkr"], half, 1.0))
    plain("mcq", CB_CQ)
    padded_k("nkc", CB_NKC, NSA_GROUPS)
    padded_k("nq", CB_NQ, NSA_HEADS, NSA_DK ** -0.5 * LOG2E)
    plain("fq", CB_FQ, FOX_DH ** -0.5 * LOG2E)
    plain("fk", CB_FK)
    plain("fv", CB_FV)
    padded_k("nks", CB_NKS, NSA_GROUPS)
    padded_k("nkw", CB_NKW, NSA_GROUPS)
    plain("nvc", CB_NVC)
    plain("nvs", CB_NVS)
    plain("nvw", CB_NVW)
    plain("mg", CB_MG)
    return segs, src


def _pack_w_in_kernel(w_ref, o_ref, os_ref, *, segs, small_segs):
    d_in = w_ref.shape[1]

    def put(dst_ref, dst, s, n, f):
        for k in range(0, n, LANE):
            m = min(LANE, n - k)
            assert s + k + LANE <= d_in
            v = w_ref[0, s + k:s + k + LANE, :].T
            if f != 1.0:
                v = v * f
            dst_ref[:, dst + k:dst + k + m] = v[:, :m].astype(dst_ref.dtype)

    o_ref[...] = jnp.zeros(o_ref.shape, o_ref.dtype)
    for dst, s, n, f in segs:
        put(o_ref, dst, s, n, f)
    os_ref[...] = jnp.zeros(os_ref.shape, os_ref.dtype)
    for dst, s, n in small_segs:
        put(os_ref, dst, s, n, 1.0)


def _pack_w_in(w_all, layer):
    _, D, d_in = w_all.shape
    segs, src = _w_in_segments(D)
    small_segs = ((0, src["ff"], FORGET_COLS), (FORGET_COLS, src["ngt"], NSA_GATE_COLS))
    tr = LANE
    return pl.pallas_call(
        functools.partial(_pack_w_in_kernel, segs=tuple(segs), small_segs=small_segs),
        out_shape=(jax.ShapeDtypeStruct((D, Z_BLOCKS * LANE), BF16), jax.ShapeDtypeStruct((D, LANE), F32)),
        grid=(D // tr,),
        in_specs=[pl.BlockSpec((1, d_in, tr), lambda i: (layer, 0, i))],
        out_specs=(pl.BlockSpec((tr, Z_BLOCKS * LANE), lambda i: (i, 0)), pl.BlockSpec((tr, LANE), lambda i: (i, 0))),
        compiler_params=_cparams(("parallel",)),
        name="pack_w_in",
    )(jnp.swapaxes(w_all, 1, 2))


def _pack_w_uq(w):
    K = w.shape[0]
    w3 = w.reshape(K, MLA_HEADS, MLA_NOPE + MLA_ROPE) * ((MLA_NOPE + MLA_ROPE) ** -0.5 * LOG2E)
    nope = w3[:, :, :MLA_NOPE].reshape(K, MLA_HEADS * MLA_NOPE)
    r = w3[:, :, MLA_NOPE:]
    half = MLA_ROPE // 2
    r_rot = jnp.concatenate([-r[:, :, half:], r[:, :, :half]], axis=2)
    padr = lambda a: jnp.pad(a, ((0, 0), (0, 0), (0, LANE - MLA_ROPE))).reshape(K, MLA_HEADS * LANE)
    return jnp.concatenate([nope, padr(r), padr(r_rot)], axis=1).astype(BF16)


def _pack_w_ukv(w):
    K = w.shape[0]
    w3 = w.reshape(K, MLA_HEADS, MLA_NOPE + MLA_DV)
    return jnp.concatenate([w3[:, :, :MLA_NOPE].reshape(K, -1), w3[:, :, MLA_NOPE:].reshape(K, -1)], axis=1).astype(BF16)


def _t5_bucket(dist):
    dist = jnp.maximum(dist, 0)
    exact = REL_BUCKETS // 2
    df = jnp.maximum(dist, 1).astype(F32)
    large = exact + (jnp.log(df / exact) / math.log(REL_MAX_DIST / exact) * (REL_BUCKETS - exact)).astype(jnp.int32)
    large = jnp.minimum(large, REL_BUCKETS - 1)
    return jnp.where(dist < exact, dist, large)


def _position_tables(S, rel_bias):
    t = NSA_T
    half = MLA_ROPE // 2
    inv = ROPE_THETA ** (-jnp.arange(half, dtype=F32) / half)
    ang = jnp.arange(S, dtype=F32)[:, None] * inv
    c, s = jnp.cos(ang), jnp.sin(ang)
    cos = _pad_cols(jnp.concatenate([c, c], axis=1), LANE)
    sin = _pad_cols(jnp.concatenate([s, s], axis=1), LANE)
    ncp = max(S // CMP_STRIDE, LANE)
    pos = jnp.arange(S)

    def bias_of(dist):
        onehot = jax.nn.one_hot(_t5_bucket(dist), REL_BUCKETS, dtype=F32)
        return jnp.einsum("...b,bh->h...", onehot, rel_bias, precision=lax.Precision.HIGHEST)

    bias_c = bias_of(pos[None, :] - (CMP_STRIDE * jnp.arange(ncp)[:, None] + CMP_BLOCK - 1))
    i = jnp.arange(t)
    bias_sw = jnp.stack([bias_of(k * t + i[:, None] - i[None, :]) for k in range(3)], axis=1)
    bias_sw = bias_sw.reshape(NSA_GROUPS, NSA_HPG, 3, t, t).transpose(0, 2, 4, 1, 3).reshape(NSA_GROUPS, 3, t, NSA_HPG * t)
    n_cmp = (S - CMP_BLOCK) // CMP_STRIDE + 1
    n_blk = S // SEL_BLOCK
    nbp = max(n_blk, LANE)
    cstart = CMP_STRIDE * jnp.arange(ncp)
    sstart = SEL_BLOCK * jnp.arange(nbp)
    ov = jnp.clip(jnp.minimum(cstart[:, None] + CMP_BLOCK, sstart[None, :] + SEL_BLOCK)
                  - jnp.maximum(cstart[:, None], sstart[None, :]), 0, None).astype(F32) / CMP_STRIDE
    ov = jnp.where((jnp.arange(ncp)[:, None] < n_cmp) & (jnp.arange(nbp)[None, :] < n_blk), ov, 0.0).astype(BF16)
    expand = ((pos[:, None] // SEL_BLOCK) == jnp.arange(nbp)[None, :]).astype(BF16)
    return cos, sin, bias_c, bias_sw, ov, expand


def _token_mixers(h, p, w_in_all, layer, tabs, B, S):
    T, D = h.shape
    cos, sin, bias_c, bias_sw, overlap, expand = tabs
    wz, w_small = _pack_w_in(w_in_all, layer)
    z, zs = input_projection(h, p["g_mix"], wz, w_small, tm=min(T, 1024), tn=1024)
    z3 = z.reshape(B, S, Z_BLOCKS * LANE)

    log_f = jax.nn.log_sigmoid(zs[:, :FORGET_COLS] + p["b_forget"].astype(F32)).reshape(B, S, FOX_HEADS)
    cum = jnp.cumsum(log_f, axis=1) * LOG2E
    o_fox = causal_attention(z3, CB_FQ, z3, CB_FK, z3, CB_FV, FOX_HEADS, cum=cum)

    q_nope, q_rope = mla_q_proj(z, p["g_cq"], _pack_w_uq(p["w_uq"]), cos, sin, S, tm=min(S, 512))
    kv = rms_matmul(z, CB_CKV * LANE // MLA_KV_RANK, p["g_ckv"], _pack_w_ukv(p["w_ukv"]), BF16, tm=min(T, 1024), tn=1024)
    kr = z3[:, :, CB_KR * LANE:(CB_KR + 1) * LANE].astype(F32)
    kr = kr[..., :MLA_ROPE] * cos[None, :, :MLA_ROPE] + kr[..., MLA_ROPE:] * sin[None, :, :MLA_ROPE]
    k_rope = jnp.pad(kr, ((0, 0), (0, 0), (0, LANE - MLA_ROPE))).astype(BF16)
    hw = MLA_HEADS * LANE
    kv3 = kv.reshape(B, S, 2 * hw)
    o_mla = causal_attention(q_nope.reshape(B, S, hw), 0, kv3, 0, kv3, MLA_HEADS, MLA_HEADS,
                             q2=q_rope.reshape(B, S, hw), k2=k_rope)

    G = NSA_GROUPS
    NC = S // CMP_STRIDE
    ncp = bias_c.shape[1]

    def compress_branch(cb, dp, d, pe, w1, w2):
        x = z3[:, :, cb * LANE: cb * LANE + G * dp].reshape(B, NC, CMP_STRIDE * G * dp)
        eye = jnp.eye(G, dtype=F32)
        w1p = jnp.pad(w1.reshape(CMP_BLOCK, d, d), ((0, 0), (0, dp - d), (0, dp - d)))
        w1g = jnp.einsum("lij,gh->lgihj", w1p, eye).reshape(CMP_BLOCK, G * dp, G * dp).astype(BF16)
        w1a = w1g[:CMP_STRIDE].reshape(CMP_STRIDE * G * dp, G * dp)
        w1b = w1g[CMP_STRIDE:].reshape(CMP_STRIDE * G * dp, G * dp)
        pe_g = jnp.tile(jnp.pad(pe, ((0, 0), (0, dp - d)))[:, None, :], (1, G, 1))
        pe2 = pe_g.reshape(2, CMP_STRIDE * G * dp).astype(BF16)
        w2p = jnp.pad(w2, ((0, dp - d), (0, dp - d)))
        w2g = jnp.einsum("ij,gh->gihj", w2p, eye).reshape(G * dp, G * dp).astype(BF16)
        out = compress(x, w1a, w1b, pe2, w2g)
        return jnp.pad(out, ((0, 0), (0, ncp - NC), (0, 0)))

    kc = compress_branch(CB_NKC, NSA_DKP, NSA_DK, p["pe_k"], p["w_cmp_k1"], p["w_cmp_k2"])
    vc = compress_branch(CB_NVC, NSA_DV, NSA_DV, p["pe_v"], p["w_cmp_v1"], p["w_cmp_v2"])
    gl = zs[:, FORGET_COLS:FORGET_COLS + NSA_GATE_COLS].reshape(B, S, G, NSA_HPG * 3)
    gl = jnp.transpose(gl, (0, 2, 1, 3))
    gl_cols = jnp.pad(jnp.swapaxes(gl, 2, 3), ((0, 0), (0, 0), (0, 16 - NSA_HPG * 3), (0, 0)))
    o_cmp, ind = nsa_cmp_select(z3, kc, vc, bias_c, overlap, gl_cols, min(N_SEL, S // SEL_BLOCK))
    o_nsa = nsa_sel_win(z3, ind, expand, bias_sw, gl_cols, o_cmp)

    merged = merge_branches(o_fox.reshape(T, -1), o_mla.reshape(T, -1), o_nsa.reshape(T, -1),
                            p["w_branch"].astype(BF16), z, tm=min(T, 1024), tn=512)
    return matmul_residual(merged, p["w_out"].astype(BF16), h, tm=min(T, 1024), tn=512)


def _memory_block(h, mem2, p, B, S):
    T, D = h.shape
    kv = rms_matmul(mem2, 0, p["g_mem_kv"], p["w_mem_kv"].astype(BF16), BF16, tm=min(mem2.shape[0], 512), tn=512)
    out = memory_attention_block(h.reshape(B, S, D), p["g_mem_q"], (p["w_mem_q"] * (MEM_DH ** -0.5 * LOG2E)).astype(BF16),
                                 kv.reshape(B, -1, kv.shape[1]), p["w_mem_o"].astype(BF16), tq=min(S, 512))
    return out.reshape(T, D)


def _moe_block(h, p, experts, layer, g_final=None):
    T, D = h.shape
    tm = MOE_TM
    w_r = _pad_cols(jnp.concatenate([p["w_router_group"], p["w_router_expert"]], axis=1), LANE).astype(F32)
    b_r = _pad_cols(jnp.concatenate([p["b_router_group"], p["b_router_expert"]])[None, :], LANE).astype(F32)
    logits, u = moe_router(h, p["g_moe"], w_r, b_r, tm=min(T, 512))
    glog = logits[:, :N_GROUPS]
    gsel = jnp.argmax(glog, axis=-1).astype(jnp.int32)
    pg = jnp.max(jax.nn.softmax(glog, axis=-1), axis=-1, keepdims=True)
    elog = logits[:, N_GROUPS:N_GROUPS + N_EXPERTS].reshape(T, N_GROUPS, EXPERTS_PER_GROUP)
    elog = jnp.take_along_axis(elog, gsel[:, None, None], axis=1)[:, 0]
    eprob = jax.nn.softmax(elog, axis=-1)
    j0 = jnp.argmax(eprob, axis=-1).astype(jnp.int32)
    lane = jnp.arange(EXPERTS_PER_GROUP, dtype=jnp.int32)[None, :]
    j1 = jnp.argmax(jnp.where(lane == j0[:, None], -jnp.inf, eprob), axis=-1).astype(jnp.int32)
    top_j = jnp.stack([j0, j1], axis=-1)
    top_p = jnp.take_along_axis(eprob, top_j, axis=-1)
    top_p = top_p / jnp.sum(top_p, axis=-1, keepdims=True)
    weight = pg * top_p
    flat_e = (gsel[:, None] * EXPERTS_PER_GROUP + top_j.astype(jnp.int32)).reshape(-1)
    TK = T * TOP_K
    onehot = (flat_e[:, None] == jnp.arange(N_EXPERTS, dtype=jnp.int32)[None, :]).astype(jnp.int32)
    rank = jnp.sum((jnp.cumsum(onehot, axis=0) - onehot) * onehot, axis=1)
    counts = jnp.sum(onehot, axis=0)
    pcounts = ((counts + tm - 1) // tm) * tm
    pends = jnp.cumsum(pcounts)
    dest = (pends - pcounts)[flat_e] + rank
    n_rb = -(-TK // tm) + N_EXPERTS
    P = n_rb * tm
    row_tok = (jnp.arange(P, dtype=jnp.int32) % T).at[dest].set(jnp.repeat(jnp.arange(T, dtype=jnp.int32), TOP_K))
    blk_e = jnp.sum((pends[None, :] <= (jnp.arange(n_rb, dtype=jnp.int32) * tm)[:, None]).astype(jnp.int32), axis=1)
    blk_e = jnp.minimum(blk_e, N_EXPERTS - 1).astype(jnp.int32)
    n_used = (pends[-1] // tm).astype(jnp.int32).reshape(1)
    row_end = pends - pcounts + counts
    n_valid = jnp.clip(row_end[blk_e] - jnp.arange(n_rb, dtype=jnp.int32) * tm, 0, tm).astype(jnp.int32)
    xr = u[row_tok]
    y = moe_experts(blk_e, n_valid, n_used, xr, experts[0].astype(F32), experts[1].astype(F32),
                    experts[2].astype(F32), layer)
    d2 = dest.reshape(T, TOP_K)
    return moe_combine(h, y[d2[:, 0]], y[d2[:, 1]], weight, g_final, tm=min(T, 512))


_LAYER_KEYS = ("g_mix", "w_in", "b_forget", "g_cq", "g_ckv", "w_uq", "w_ukv", "pe_k", "pe_v", "w_cmp_k1", "w_cmp_k2",
               "w_cmp_v1", "w_cmp_v2", "w_branch", "w_out", "g_mem_q", "g_mem_kv", "w_mem_q", "w_mem_kv", "w_mem_o",
               "g_moe", "w_router_group", "b_router_group", "w_router_expert", "b_router_expert")


def kernel(x, mem, g_mix, w_in, b_forget, g_cq, g_ckv, w_uq, w_ukv, pe_k, pe_v, w_cmp_k1, w_cmp_k2, w_cmp_v1, w_cmp_v2, rel_bias, w_branch, w_out, g_mem_q, g_mem_kv, w_mem_q, w_mem_kv, w_mem_o, g_moe, w_router_group, b_router_group, w_router_expert, b_router_expert, w_exp_gate, w_exp_up, w_exp_down, g_final):
    B, S, D = x.shape
    T = B * S
    stacked = dict(g_mix=g_mix, w_in=w_in, b_forget=b_forget, g_cq=g_cq, g_ckv=g_ckv, w_uq=w_uq, w_ukv=w_ukv,
                   pe_k=pe_k, pe_v=pe_v, w_cmp_k1=w_cmp_k1, w_cmp_k2=w_cmp_k2, w_cmp_v1=w_cmp_v1, w_cmp_v2=w_cmp_v2,
                   w_branch=w_branch, w_out=w_out, g_mem_q=g_mem_q, g_mem_kv=g_mem_kv, w_mem_q=w_mem_q,
                   w_mem_kv=w_mem_kv, w_mem_o=w_mem_o, g_moe=g_moe, w_router_group=w_router_group,
                   b_router_group=b_router_group, w_router_expert=w_router_expert, b_router_expert=b_router_expert,
                   w_exp_gate=w_exp_gate, w_exp_up=w_exp_up, w_exp_down=w_exp_down)
    tabs = _position_tables(S, rel_bias.astype(F32) * LOG2E)
    h = x.reshape(T, D).astype(F32)
    mem2 = mem.reshape(-1, D).astype(F32)
    depth = w_in.shape[0]
    assert depth >= 1
    for l in range(depth):
        p = {k: stacked[k][l] for k in _LAYER_KEYS}
        h = _token_mixers(h, p, w_in.astype(F32), l, tabs, B, S)
        h = _memory_block(h, mem2, p, B, S)
        h = _moe_block(h, p, (w_exp_gate, w_exp_up, w_exp_down), l, g_final if l == depth - 1 else None)
    return h.reshape(B, S, D)
```

```python
import functools
import math

import jax
import jax.numpy as jnp
import numpy as np
from jax import lax
from jax.experimental import pallas as pl
from jax.experimental.pallas import tpu as pltpu

F32 = jnp.float32
BF16 = jnp.bfloat16

EPS = 1e-6
NEG_INF = -1e30
LOG2E = math.log2(math.e)
LANE = 128

FOX_HEADS, FOX_DH = 8, 128
MLA_HEADS, MLA_NOPE, MLA_ROPE, MLA_DV = 8, 128, 64, 128
MLA_Q_RANK, MLA_KV_RANK = 768, 512
ROPE_THETA = 10000.0
NSA_HEADS, NSA_GROUPS, NSA_DK, NSA_DV = 8, 2, 192, 128
NSA_HPG = NSA_HEADS // NSA_GROUPS
NSA_DKP = 256
CMP_BLOCK, CMP_STRIDE, SEL_BLOCK, N_SEL, WINDOW = 32, 16, 64, 8, 512
SEL_FORCE = 1e6
REL_BUCKETS, REL_MAX_DIST = 32, 128
N_BRANCH, BRANCH_W = 3, 1024
MEM_HEADS, MEM_DH = 4, 128
N_GROUPS, EXPERTS_PER_GROUP, TOP_K = 4, 8, 2
N_EXPERTS = N_GROUPS * EXPERTS_PER_GROUP
FORGET_COLS, NSA_GATE_COLS = FOX_HEADS, NSA_HEADS * 3

CB_CKV, CB_KR, CB_CQ = 0, 4, 6
CB_NKC, CB_NQ = 12, 16
CB_FQ, CB_FK, CB_FV = 32, 40, 48
CB_NKS, CB_NKW = 56, 60
CB_NVC, CB_NVS, CB_NVW = 64, 66, 68
CB_MG = 72
Z_BLOCKS = 120
IN_PROJ_TN = 8 * LANE

ATT_T = 256
ATT_HG = 8
NSA_T = 256
NSA_CMP_T = 512
MOE_TM = 512
MOE_SUB = 256
MOE_DC = 512
VMEM_LIMIT = 56 * 1024 * 1024


def _cparams(sem):
    return pltpu.CompilerParams(dimension_semantics=sem, vmem_limit_bytes=VMEM_LIMIT)


def _sigmoid(x):
    return 1.0 / (1.0 + jnp.exp(-x))


def _rms_mm_kernel(x_ref, g_ref, w_ref, o_ref, u_ref):
    @pl.when(pl.program_id(1) == 0)
    def _():
        x = x_ref[...].astype(F32)
        ms = jnp.mean(x * x, axis=-1, keepdims=True)
        u_ref[...] = (x * lax.rsqrt(ms + EPS) * g_ref[...]).astype(u_ref.dtype)

    o_ref[...] = jnp.dot(u_ref[...], w_ref[...], preferred_element_type=F32).astype(o_ref.dtype)


def rms_matmul(x, xcol, g, w, out_dtype, tm, tn):
    T = x.shape[0]
    K, N = w.shape
    return pl.pallas_call(
        _rms_mm_kernel,
        out_shape=jax.ShapeDtypeStruct((T, N), out_dtype),
        grid=(T // tm, N // tn),
        in_specs=[pl.BlockSpec((tm, K), lambda i, j: (i, xcol)),
                  pl.BlockSpec((1, K), lambda i, j: (0, 0)),
                  pl.BlockSpec((K, tn), lambda i, j: (0, j))],
        out_specs=pl.BlockSpec((tm, tn), lambda i, j: (i, j)),
        scratch_shapes=[pltpu.VMEM((tm, K), BF16)],
        compiler_params=_cparams(("parallel", "arbitrary")),
        name="rms_matmul",
    )(x, g.reshape(1, K).astype(F32), w)


def _in_proj_kernel(x_ref, g_ref, w_ref, ws_ref, o_ref, os_ref, u_ref):
    @pl.when(pl.program_id(1) == 0)
    def _():
        x = x_ref[...]
        ms = jnp.mean(x * x, axis=-1, keepdims=True)
        u = x * lax.rsqrt(ms + EPS) * g_ref[...]
        u_ref[...] = u.astype(u_ref.dtype)
        os_ref[...] = _dot3(u, ws_ref[...])

    o_ref[...] = jnp.dot(u_ref[...], w_ref[...], preferred_element_type=F32).astype(o_ref.dtype)


def input_projection(x, g, w, w_small, tm, tn):
    T, K = x.shape
    N = w.shape[1]
    Ns = w_small.shape[1]
    return pl.pallas_call(
        _in_proj_kernel,
        out_shape=(jax.ShapeDtypeStruct((T, N), BF16), jax.ShapeDtypeStruct((T, Ns), F32)),
        grid=(T // tm, N // tn),
        in_specs=[pl.BlockSpec((tm, K), lambda i, j: (i, 0)),
                  pl.BlockSpec((1, K), lambda i, j: (0, 0)),
                  pl.BlockSpec((K, tn), lambda i, j: (0, j)),
                  pl.BlockSpec((K, Ns), lambda i, j: (0, 0))],
        out_specs=(pl.BlockSpec((tm, tn), lambda i, j: (i, j)), pl.BlockSpec((tm, Ns), lambda i, j: (i, 0))),
        scratch_shapes=[pltpu.VMEM((tm, K), BF16)],
        compiler_params=_cparams(("parallel", "arbitrary")),
        name="input_projection",
    )(x, g.reshape(1, K).astype(F32), w, w_small)


def _mla_q_kernel(x_ref, g_ref, w_ref, cos_ref, sin_ref, qn_ref, qr_ref):
    x = x_ref[...].astype(F32)
    ms = jnp.mean(x * x, axis=-1, keepdims=True)
    u = (x * lax.rsqrt(ms + EPS) * g_ref[...]).astype(BF16)
    y = jnp.dot(u, w_ref[...], preferred_element_type=F32)
    hw = MLA_HEADS * LANE
    qn_ref[...] = y[:, :hw].astype(qn_ref.dtype)
    cos = cos_ref[...]
    sin = sin_ref[...]
    for h in range(MLA_HEADS):
        a = y[:, hw + h * LANE: hw + (h + 1) * LANE]
        b = y[:, 2 * hw + h * LANE: 2 * hw + (h + 1) * LANE]
        qr_ref[:, h * LANE:(h + 1) * LANE] = (a * cos + b * sin).astype(qr_ref.dtype)


def mla_q_proj(z, g, w, cos, sin, S, tm):
    T = z.shape[0]
    K, N = w.shape
    hw = MLA_HEADS * LANE
    nsb = S // tm
    return pl.pallas_call(
        _mla_q_kernel,
        out_shape=(jax.ShapeDtypeStruct((T, hw), BF16), jax.ShapeDtypeStruct((T, hw), BF16)),
        grid=(T // tm,),
        in_specs=[pl.BlockSpec((tm, K), lambda i: (i, CB_CQ * LANE // MLA_Q_RANK)),
                  pl.BlockSpec((1, K), lambda i: (0, 0)),
                  pl.BlockSpec((K, N), lambda i: (0, 0)),
                  pl.BlockSpec((tm, LANE), lambda i: (i % nsb, 0)),
                  pl.BlockSpec((tm, LANE), lambda i: (i % nsb, 0))],
        out_specs=(pl.BlockSpec((tm, hw), lambda i: (i, 0)), pl.BlockSpec((tm, hw), lambda i: (i, 0))),
        compiler_params=_cparams(("parallel",)),
        name="mla_q_proj",
    )(z, g.reshape(1, K).astype(F32), w, cos, sin)


def _causal_attn_kernel(*refs, t, hg, two_part, decay):
    refs = list(refs)
    q_ref, k_ref, v_ref = refs[:3]
    pos = 3
    if two_part:
        q2_ref, k2_ref = refs[pos:pos + 2]
        pos += 2
    if decay:
        ck_ref = refs[pos]
        pos += 1
    o_ref = refs[pos]
    if two_part:
        kcat_ref = refs[pos + 1]
    qi = pl.program_id(2)
    dn = (((1,), (1,)), ((), ()))

    if two_part:
        @pl.when(qi == 0)
        def _():
            for j in range(hg):
                kcat_ref[j, :, :LANE] = k_ref[0, :, j * LANE:(j + 1) * LANE]
                kcat_ref[j, :, LANE:] = k2_ref[0]

    qs = []
    for j in range(hg):
        qj = q_ref[0, :, j * LANE:(j + 1) * LANE]
        if two_part:
            qj = jnp.concatenate([qj, q2_ref[0, :, j * LANE:(j + 1) * LANE]], axis=1)
        qs.append(qj)

    def step(kb, carry, masked, width=1):
        off = pl.multiple_of(kb * t, t)
        tk = width * t
        heads = range(hg)
        ss = []
        for j in heads:
            k = kcat_ref[j, pl.ds(off, tk), :] if two_part else k_ref[0, pl.ds(off, tk), j * LANE:(j + 1) * LANE]
            ss.append(lax.dot_general(k, qs[j], dn, preferred_element_type=F32))
        if decay:
            ss = [ss[j] - ck_ref[0, 0, pl.ds(off, tk), j:j + 1] for j in heads]
        if masked:
            r = lax.broadcasted_iota(jnp.int32, (tk, t), 0)
            c = lax.broadcasted_iota(jnp.int32, (tk, t), 1)
            ss = [jnp.where(r <= c, s, NEG_INF) for s in ss]
        ms = [jnp.maximum(carry[j][0], jnp.max(ss[j], axis=0, keepdims=True)) for j in heads]
        ps = [jnp.exp2(ss[j] - ms[j]) for j in heads]
        out = []
        for j in heads:
            m, l, acc = carry[j]
            a = jnp.exp2(m - ms[j])
            l = a * l + jnp.sum(ps[j], axis=0, keepdims=True)
            v = v_ref[0, pl.ds(off, tk), j * LANE:(j + 1) * LANE]
            acc = a * acc + lax.dot_general(v, ps[j].astype(BF16), (((0,), (0,)), ((), ())),
                                            preferred_element_type=F32)
            out.append((ms[j], l, acc))
        return tuple(out)

    init = tuple((jnp.full((1, t), NEG_INF, F32), jnp.zeros((1, t), F32), jnp.zeros((LANE, t), F32))
                 for _ in range(hg))
    carry = lax.fori_loop(0, qi // 2, lambda kp, c: step(2 * kp, c, False, width=2), init)
    carry = lax.cond(qi % 2 == 1, lambda c: step(qi - 1, c, False), lambda c: c, carry)
    carry = step(qi, carry, True)
    for j in range(hg):
        _, l, acc = carry[j]
        o_ref[0, :, j * LANE:(j + 1) * LANE] = (acc / l).T.astype(o_ref.dtype)


def causal_attention(q, qcb, k, kcb, v, vcb, heads, q2=None, k2=None, cum=None):
    B, S, _ = q.shape
    t, hg = ATT_T, ATT_HG
    w = hg * LANE
    two_part, decay = q2 is not None, cum is not None
    in_specs = [pl.BlockSpec((1, t, w), lambda b, h, i: (b, i, qcb // hg + h)),
                pl.BlockSpec((1, S, w), lambda b, h, i: (b, 0, kcb // hg + h)),
                pl.BlockSpec((1, S, w), lambda b, h, i: (b, 0, vcb // hg + h))]
    args = [q, k, v]
    scratch = []
    if two_part:
        in_specs += [pl.BlockSpec((1, t, w), lambda b, h, i: (b, i, h)),
                     pl.BlockSpec((1, S, LANE), lambda b, h, i: (b, 0, 0))]
        args += [q2, k2]
        scratch = [pltpu.VMEM((hg, S, 2 * LANE), BF16)]
    if decay:
        in_specs += [pl.BlockSpec((1, 1, S, hg), lambda b, h, i: (b, h, 0, 0))]
        args += [jnp.transpose(cum.reshape(B, S, heads // hg, hg), (0, 2, 1, 3))]
    return pl.pallas_call(
        functools.partial(_causal_attn_kernel, t=t, hg=hg, two_part=two_part, decay=decay),
        out_shape=jax.ShapeDtypeStruct((B, S, heads * LANE), BF16),
        grid=(B, heads // hg, S // t),
        in_specs=in_specs,
        out_specs=pl.BlockSpec((1, t, w), lambda b, h, i: (b, i, h)),
        scratch_shapes=scratch,
        compiler_params=_cparams(("parallel", "parallel", "arbitrary")),
        name="causal_attention",
    )(*args)


def _gelu_tanh(x):
    return 0.5 * x * (1.0 + jnp.tanh(math.sqrt(2.0 / math.pi) * (x + 0.044715 * (x * x * x))))


def _compress_kernel(x_ref, w1a_ref, w1b_ref, pe_ref, w2_ref, o_ref):
    x = x_ref[0]
    a = jnp.dot(x, w1a_ref[...], preferred_element_type=F32)
    b = jnp.dot(x, w1b_ref[...], preferred_element_type=F32)
    nc = a.shape[0]
    b_next = pltpu.roll(b, nc - 1, 0)
    pe_term = jnp.dot(pe_ref[0:1, :], w1a_ref[...], preferred_element_type=F32) + \
        jnp.dot(pe_ref[1:2, :], w1b_ref[...], preferred_element_type=F32)
    hid = _gelu_tanh(a + b_next + pe_term)
    o_ref[0] = jnp.dot(hid.astype(BF16), w2_ref[...], preferred_element_type=F32).astype(o_ref.dtype)


def compress(x, w1a, w1b, pe2, w2):
    BG, NC, KD = x.shape
    dp = w2.shape[1]
    return pl.pallas_call(
        _compress_kernel,
        out_shape=jax.ShapeDtypeStruct((BG, NC, dp), BF16),
        grid=(BG,),
        in_specs=[pl.BlockSpec((1, NC, KD), lambda i: (i, 0, 0)),
                  pl.BlockSpec((KD, dp), lambda i: (0, 0)),
                  pl.BlockSpec((KD, dp), lambda i: (0, 0)),
                  pl.BlockSpec((2, KD), lambda i: (0, 0)),
                  pl.BlockSpec((dp, dp), lambda i: (0, 0))],
        out_specs=pl.BlockSpec((1, NC, dp), lambda i: (i, 0, 0)),
        compiler_params=_cparams(("parallel",)),
        name="nsa_compress",
    )(x, w1a, w1b, pe2, w2)


def _nsa_cmp_kernel(q_ref, kc_ref, vc_ref, bias_ref, ov_ref, gate_ref, o_ref, ind_ref, *, t, n_sel, n_rows):
    qi = pl.program_id(2)
    ncp = kc_ref.shape[1]
    kc = kc_ref[0]
    vc = vc_ref[0]
    pos = lax.broadcasted_iota(jnp.int32, (ncp, t), 1) + qi * t
    cblk = lax.broadcasted_iota(jnp.int32, (ncp, t), 0)
    valid = pos >= CMP_STRIDE * cblk + (CMP_BLOCK - 1)
    dn = (((1,), (1,)), ((), ()))
    tn = (((0,), (0,)), ((), ()))
    heads = range(NSA_HPG)
    ss = [lax.dot_general(kc, q_ref[0, :, h * NSA_DKP:(h + 1) * NSA_DKP], dn, preferred_element_type=F32)
          for h in heads]
    ss = [jnp.where(valid, ss[h] + bias_ref[h], NEG_INF) for h in heads]
    es = [jnp.exp2(ss[h] - jnp.max(ss[h], axis=0, keepdims=True)) for h in heads]
    ps = [jnp.where(valid, es[h] / jnp.sum(es[h], axis=0, keepdims=True), 0.0) for h in heads]
    gates = _sigmoid(gate_ref[0, 0])
    for h in heads:
        o = lax.dot_general(vc, ps[h].astype(BF16), tn, preferred_element_type=F32)
        o_ref[0, :, h * NSA_DV:(h + 1) * NSA_DV] = (gates[3 * h:3 * h + 1, :] * o).T.astype(o_ref.dtype)
    psum = functools.reduce(lambda x, y: x + y, ps)
    p_hi = psum.astype(BF16)
    p_lo = (psum - p_hi.astype(F32)).astype(BF16)
    imp = lax.dot_general(ov_ref[...], p_hi, tn, preferred_element_type=F32) + \
        lax.dot_general(ov_ref[...], p_lo, tn, preferred_element_type=F32)
    nbp = imp.shape[0]
    imp = imp[:n_rows]
    blk = lax.broadcasted_iota(jnp.int32, (n_rows, t), 0)
    cur = (lax.broadcasted_iota(jnp.int32, (n_rows, t), 1) + qi * t) // SEL_BLOCK
    forced = (blk == 0) | (blk == cur) | (blk == cur - 1)
    score = jnp.where(blk <= cur, imp + jnp.where(forced, SEL_FORCE, 0.0), NEG_INF)
    sel = jnp.zeros((n_rows, t), F32)
    for _ in range(n_sel):
        mx = jnp.max(score, axis=0, keepdims=True)
        first = jnp.min(jnp.where(score == mx, blk, nbp), axis=0, keepdims=True)
        hit = blk == first
        sel = jnp.where(hit, 1.0, sel)
        score = jnp.where(hit, -jnp.inf, score)
    ind_ref[0, 0] = jnp.zeros(ind_ref.shape[2:], ind_ref.dtype)
    ind_ref[0, 0, :n_rows, :] = sel.astype(ind_ref.dtype)


def nsa_cmp_select(z3, kc, vc, bias_c, overlap, gates, n_sel):
    B, S, _ = z3.shape
    G = NSA_GROUPS
    t = min(NSA_CMP_T, S)
    ncp = kc.shape[1]
    nbp = overlap.shape[1]
    qw = NSA_HPG * NSA_DKP
    ow = NSA_HPG * NSA_DV
    n_rows = min(nbp, -(-(S // SEL_BLOCK) // 16) * 16)
    return pl.pallas_call(
        functools.partial(_nsa_cmp_kernel, t=t, n_sel=n_sel, n_rows=n_rows),
        out_shape=(jax.ShapeDtypeStruct((B, S, G * ow), BF16), jax.ShapeDtypeStruct((B, G, nbp, S), BF16)),
        grid=(B, G, S // t),
        in_specs=[pl.BlockSpec((1, t, qw), lambda b, g, i: (b, i, CB_NQ * LANE // qw + g)),
                  pl.BlockSpec((1, ncp, NSA_DKP), lambda b, g, i: (b, 0, g)),
                  pl.BlockSpec((1, ncp, NSA_DV), lambda b, g, i: (b, 0, g)),
                  pl.BlockSpec((NSA_HPG, ncp, t), lambda b, g, i: (g, 0, i)),
                  pl.BlockSpec((ncp, nbp), lambda b, g, i: (0, 0)),
                  pl.BlockSpec((1, 1, gates.shape[2], t), lambda b, g, i: (b, g, 0, i))],
        out_specs=(pl.BlockSpec((1, t, ow), lambda b, g, i: (b, i, g)),
                   pl.BlockSpec((1, 1, nbp, t), lambda b, g, i: (b, g, 0, i))),
        compiler_params=_cparams(("parallel", "parallel", "arbitrary")),
        name="nsa_cmp_select",
    )(z3, kc, vc, bias_c, overlap, gates)


def _nsa_sw_kernel(q_ref, ks_ref, vs_ref, kw_ref, vw_ref, ind_ref, e_ref, bias_ref, gate_ref, oc_ref, o_ref,
                   *, t):
    qi = pl.program_id(1)
    hp, G = NSA_HPG, NSA_GROUPS
    q4 = [jnp.concatenate([q_ref[0, :, (g * hp + h) * NSA_DKP:(g * hp + h + 1) * NSA_DKP] for h in range(hp)], axis=0)
          for g in range(G)]
    inds = [ind_ref[0, g] for g in range(G)]
    dn = (((1,), (1,)), ((), ()))

    def step(kb, carry, k_ref, v_ref, selected, width=1):
        off = pl.multiple_of(kb * t, t)
        tk = width * t
        ri = lax.broadcasted_iota(jnp.int32, (tk, t), 0)
        ci = lax.broadcasted_iota(jnp.int32, (tk, t), 1)
        d = (qi - kb) * t + ci - ri
        near = (d >= 0) if selected else (d >= 0) & (d < WINDOW)
        groups = range(G)
        ss = [lax.dot_general(k_ref[0, pl.ds(off, tk), g * NSA_DKP:(g + 1) * NSA_DKP], q4[g], dn,
                              preferred_element_type=F32) for g in groups]
        negs = []
        for g in groups:
            mask = near
            if selected:
                hit = jnp.dot(e_ref[pl.ds(off, tk), :], inds[g], preferred_element_type=F32)
                mask = near & (hit > 0.5)
            neg = jnp.where(mask, 0.0, NEG_INF)
            negs.append(jnp.concatenate([neg] * hp, axis=1))
        bias = [jnp.concatenate([bias_ref[g, jnp.minimum(qi - kb - w, 2)] for w in range(width)], axis=0)
                if width > 1 else bias_ref[g, jnp.minimum(qi - kb, 2)] for g in groups]
        ss = [ss[g] + bias[g] + negs[g] for g in groups]
        ms = [jnp.maximum(carry[g][0], jnp.max(ss[g], axis=0, keepdims=True)) for g in groups]
        ps = [jnp.exp2(ss[g] - ms[g]) for g in groups]
        out = []
        for g in groups:
            m, l, acc = carry[g]
            a = jnp.exp2(m - ms[g])
            l = a * l + jnp.sum(ps[g], axis=0, keepdims=True)
            v = v_ref[0, pl.ds(off, tk), g * NSA_DV:(g + 1) * NSA_DV]
            acc = a * acc + lax.dot_general(v, ps[g].astype(BF16), (((0,), (0,)), ((), ())),
                                            preferred_element_type=F32)
            out.append((ms[g], l, acc))
        return tuple(out)

    init = tuple((jnp.full((1, hp * t), NEG_INF, F32), jnp.zeros((1, hp * t), F32), jnp.zeros((NSA_DV, hp * t), F32))
                 for _ in range(G))
    sel = lax.fori_loop(0, (qi + 1) // 2, lambda kp, c: step(2 * kp, c, ks_ref, vs_ref, True, width=2), init)
    sel = lax.cond(qi % 2 == 0, lambda c: step(qi, c, ks_ref, vs_ref, True), lambda c: c, sel)
    lo = jnp.maximum(qi - WINDOW // t, 0)
    win = lax.fori_loop(lo, qi + 1, lambda kb, c: step(kb, c, kw_ref, vw_ref, False), init)
    for g in range(G):
        o_s = sel[g][2] / sel[g][1]
        o_w = win[g][2] / win[g][1]
        gates = _sigmoid(gate_ref[0, g])
        for h in range(hp):
            r = slice(h * t, (h + 1) * t)
            c = slice((g * hp + h) * NSA_DV, (g * hp + h + 1) * NSA_DV)
            o = gates[3 * h + 1:3 * h + 2, :] * o_s[:, r] + gates[3 * h + 2:3 * h + 3, :] * o_w[:, r]
            o_ref[0, :, c] = (oc_ref[0, :, c].astype(F32) + o.T).astype(o_ref.dtype)


def nsa_sel_win(z3, ind, expand, bias_sw, gates, o_cmp):
    B, S, _ = z3.shape
    G = NSA_GROUPS
    t = NSA_T
    qw = NSA_HEADS * NSA_DKP
    ow = NSA_HEADS * NSA_DV
    kw = G * NSA_DKP
    vw = G * NSA_DV
    nbp = ind.shape[2]
    kspec = lambda cb: pl.BlockSpec((1, S, kw), lambda b, i: (b, 0, cb * LANE // kw))
    vspec = lambda cb: pl.BlockSpec((1, S, vw), lambda b, i: (b, 0, cb * LANE // vw))
    return pl.pallas_call(
        functools.partial(_nsa_sw_kernel, t=t),
        out_shape=jax.ShapeDtypeStruct((B, S, ow), BF16),
        grid=(B, S // t),
        in_specs=[pl.BlockSpec((1, t, qw), lambda b, i: (b, i, CB_NQ * LANE // qw)),
                  kspec(CB_NKS), vspec(CB_NVS), kspec(CB_NKW), vspec(CB_NVW),
                  pl.BlockSpec((1, G, nbp, t), lambda b, i: (b, 0, 0, i)),
                  pl.BlockSpec((S, nbp), lambda b, i: (0, 0)),
                  pl.BlockSpec((G, 3, t, NSA_HPG * t), lambda b, i: (0, 0, 0, 0)),
                  pl.BlockSpec((1, G, gates.shape[2], t), lambda b, i: (b, 0, 0, i)),
                  pl.BlockSpec((1, t, ow), lambda b, i: (b, i, 0))],
        out_specs=pl.BlockSpec((1, t, ow), lambda b, i: (b, i, 0)),
        compiler_params=_cparams(("parallel", "arbitrary")),
        name="nsa_sel_win",
    )(z3, z3, z3, z3, z3, ind, expand, bias_sw, gates, o_cmp)


def _merge_kernel(of_ref, om_ref, on_ref, wb_ref, g0_ref, g1_ref, g2_ref, o_ref):
    acc = None
    for n, (o_r, g_r) in enumerate(((of_ref, g0_ref), (om_ref, g1_ref), (on_ref, g2_ref))):
        y = jnp.dot(o_r[...], wb_ref[n], preferred_element_type=F32)
        y = _sigmoid(g_r[...].astype(F32)) * y
        acc = y if acc is None else acc + y
    o_ref[...] = acc.astype(o_ref.dtype)


def merge_branches(o_fox, o_mla, o_nsa, wb, z, tm, tn):
    T = o_fox.shape[0]
    D = wb.shape[2]
    gspec = lambda n: pl.BlockSpec((tm, tn), lambda i, j: (i, (CB_MG * LANE + n * D) // tn + j))
    ospec = pl.BlockSpec((tm, BRANCH_W), lambda i, j: (i, 0))
    return pl.pallas_call(
        _merge_kernel,
        out_shape=jax.ShapeDtypeStruct((T, D), BF16),
        grid=(T // tm, D // tn),
        in_specs=[ospec, ospec, ospec,
                  pl.BlockSpec((N_BRANCH, BRANCH_W, tn), lambda i, j: (0, 0, j)),
                  gspec(0), gspec(1), gspec(2)],
        out_specs=pl.BlockSpec((tm, tn), lambda i, j: (i, j)),
        compiler_params=_cparams(("parallel", "arbitrary")),
        name="merge_branches",
    )(o_fox, o_mla, o_nsa, wb, z, z, z)


def _mm_res_kernel(a_ref, w_ref, r_ref, o_ref):
    o_ref[...] = r_ref[...] + jnp.dot(a_ref[...], w_ref[...], preferred_element_type=F32)


def matmul_residual(a, w, res, tm, tn):
    T, K = a.shape
    N = w.shape[1]
    return pl.pallas_call(
        _mm_res_kernel,
        out_shape=jax.ShapeDtypeStruct((T, N), F32),
        grid=(T // tm, N // tn),
        in_specs=[pl.BlockSpec((tm, K), lambda i, j: (i, 0)),
                  pl.BlockSpec((K, tn), lambda i, j: (0, j)),
                  pl.BlockSpec((tm, tn), lambda i, j: (i, j))],
        out_specs=pl.BlockSpec((tm, tn), lambda i, j: (i, j)),
        compiler_params=_cparams(("parallel", "arbitrary")),
        name="matmul_residual",
    )(a, w, res)


def _mem_attn_kernel(h_ref, g_ref, wq_ref, kv_ref, wo_ref, o_ref):
    x = h_ref[0]
    ms = jnp.mean(x * x, axis=-1, keepdims=True)
    u = (x * lax.rsqrt(ms + EPS) * g_ref[...]).astype(BF16)
    q = jnp.dot(u, wq_ref[...], preferred_element_type=F32).astype(BF16)
    dn = (((1,), (1,)), ((), ()))
    hw = MEM_HEADS * MEM_DH
    heads = range(MEM_HEADS)
    cs = [slice(h * MEM_DH, (h + 1) * MEM_DH) for h in heads]
    ss = [lax.dot_general(q[:, cs[h]], kv_ref[0, :, cs[h]], dn, preferred_element_type=F32) for h in heads]
    es = [jnp.exp2(ss[h] - jnp.max(ss[h], axis=-1, keepdims=True)) for h in heads]
    ps = [es[h] / jnp.sum(es[h], axis=-1, keepdims=True) for h in heads]
    outs = [jnp.dot(ps[h].astype(BF16), kv_ref[0, :, hw + h * MEM_DH: hw + (h + 1) * MEM_DH],
                    preferred_element_type=F32).astype(BF16) for h in heads]
    o = jnp.concatenate(outs, axis=1)
    o_ref[0] = x + jnp.dot(o, wo_ref[...], preferred_element_type=F32)


def memory_attention_block(h3, g, wq, kv, wo, tq):
    B, S, D = h3.shape
    M = kv.shape[1]
    hw = MEM_HEADS * MEM_DH
    return pl.pallas_call(
        _mem_attn_kernel,
        out_shape=jax.ShapeDtypeStruct((B, S, D), F32),
        grid=(B, S // tq),
        in_specs=[pl.BlockSpec((1, tq, D), lambda b, i: (b, i, 0)),
                  pl.BlockSpec((1, D), lambda b, i: (0, 0)),
                  pl.BlockSpec((D, hw), lambda b, i: (0, 0)),
                  pl.BlockSpec((1, M, 2 * hw), lambda b, i: (b, 0, 0)),
                  pl.BlockSpec((hw, D), lambda b, i: (0, 0))],
        out_specs=pl.BlockSpec((1, tq, D), lambda b, i: (b, i, 0)),
        compiler_params=_cparams(("parallel", "arbitrary")),
        name="memory_attention",
    )(h3, g.reshape(1, D).astype(F32), wq, kv, wo)


def _dot3(a, b):
    a_hi = a.astype(BF16)
    a_lo = (a - a_hi.astype(F32)).astype(BF16)
    b_hi = b.astype(BF16)
    b_lo = (b - b_hi.astype(F32)).astype(BF16)
    dot = functools.partial(jnp.dot, preferred_element_type=F32)
    return dot(a_hi, b_hi) + (dot(a_lo, b_hi) + dot(a_hi, b_lo))


def _router_kernel(h_ref, g_ref, w_ref, b_ref, lg_ref, u_ref):
    x = h_ref[...]
    ms = jnp.mean(x * x, axis=-1, keepdims=True)
    u = x * lax.rsqrt(ms + EPS) * g_ref[...]
    u_ref[...] = u.astype(u_ref.dtype)
    lg_ref[...] = _dot3(u, w_ref[...]) + b_ref[...]


def moe_router(h, g, w, b, tm):
    T, D = h.shape
    N = w.shape[1]
    return pl.pallas_call(
        _router_kernel,
        out_shape=(jax.ShapeDtypeStruct((T, N), F32), jax.ShapeDtypeStruct((T, D), BF16)),
        grid=(T // tm,),
        in_specs=[pl.BlockSpec((tm, D), lambda i: (i, 0)),
                  pl.BlockSpec((1, D), lambda i: (0, 0)),
                  pl.BlockSpec((D, N), lambda i: (0, 0)),
                  pl.BlockSpec((1, N), lambda i: (0, 0))],
        out_specs=(pl.BlockSpec((tm, N), lambda i: (i, 0)), pl.BlockSpec((tm, D), lambda i: (i, 0))),
        compiler_params=_cparams(("parallel",)),
        name="moe_router",
    )(h, g.reshape(1, D).astype(F32), w, b)


def _route_kernel(lg_ref, tri_ref, o_ref, cnt_ref, carry_ref):
    @pl.when(pl.program_id(0) == 0)
    def _():
        carry_ref[...] = jnp.zeros(carry_ref.shape, carry_ref.dtype)

    lg = lg_ref[...]
    lane = lax.broadcasted_iota(jnp.int32, lg.shape, 1)
    first = lambda hit: jnp.min(jnp.where(hit, lane, LANE), axis=1, keepdims=True)
    gmask = lane < N_GROUPS
    gl = jnp.where(gmask, lg, -jnp.inf)
    gmax = jnp.max(gl, axis=1, keepdims=True)
    gsel = first(gl == gmax)
    pg = 1.0 / jnp.sum(jnp.where(gmask, jnp.exp(lg - gmax), 0.0), axis=1, keepdims=True)
    lo = N_GROUPS + gsel * EXPERTS_PER_GROUP
    emask = (lane >= lo) & (lane < lo + EXPERTS_PER_GROUP)
    el = jnp.where(emask, lg, -jnp.inf)
    emax = jnp.max(el, axis=1, keepdims=True)
    l0 = first(el == emax)
    esum = jnp.sum(jnp.where(emask, jnp.exp(lg - emax), 0.0), axis=1, keepdims=True)
    el2 = jnp.where(lane == l0, -jnp.inf, el)
    e2max = jnp.max(el2, axis=1, keepdims=True)
    l1 = first(el2 == e2max)
    p0 = 1.0 / esum
    p1 = jnp.exp(e2max - emax) / esum
    w0 = pg * (p0 / (p0 + p1))
    w1 = pg * (p1 / (p0 + p1))
    onehot = jnp.where((lane == l0) | (lane == l1), 1.0, 0.0)
    before = jnp.dot(tri_ref[...], onehot.astype(BF16), preferred_element_type=F32) + carry_ref[...]
    r0 = jnp.sum(jnp.where(lane == l0, before, 0.0), axis=1, keepdims=True)
    r1 = jnp.sum(jnp.where(lane == l1, before, 0.0), axis=1, keepdims=True)
    carry_ref[...] += jnp.sum(onehot, axis=0, keepdims=True)
    cnt_ref[...] = jnp.broadcast_to(carry_ref[...], cnt_ref.shape)
    vals = (l0 - N_GROUPS, l1 - N_GROUPS, r0, r1, w0, w1)
    out = jnp.zeros(lg.shape, F32)
    for k, v in enumerate(vals):
        out = jnp.where(lane == k, v.astype(F32), out)
    o_ref[...] = out


def moe_route(logits, tm):
    T, N = logits.shape
    tri = (jnp.arange(tm)[:, None] > jnp.arange(tm)[None, :]).astype(BF16)
    return pl.pallas_call(
        _route_kernel,
        out_shape=(jax.ShapeDtypeStruct((T, N), F32), jax.ShapeDtypeStruct((8, N), F32)),
        grid=(T // tm,),
        in_specs=[pl.BlockSpec((tm, N), lambda i: (i, 0)), pl.BlockSpec((tm, tm), lambda i: (0, 0))],
        out_specs=(pl.BlockSpec((tm, N), lambda i: (i, 0)), pl.BlockSpec((8, N), lambda i: (0, 0))),
        scratch_shapes=[pltpu.VMEM((1, N), F32)],
        compiler_params=_cparams(("arbitrary",)),
        name="moe_route",
    )(logits, tri)


def _moe_kernel(be_ref, nv_ref, nu_ref, x_ref, wg_ref, wu_ref, wd_ref, o_ref, acc_ref):
    i = pl.program_id(0)
    j = pl.program_id(1)
    nv = nv_ref[i]
    tm = x_ref.shape[0]

    @pl.when((i == 0) & (j == 0))
    def _():
        acc_ref[...] = jnp.zeros(acc_ref.shape, acc_ref.dtype)

    def run(rows):
        x = x_ref[:rows, :]
        a = jnp.dot(x, wg_ref[0, 0].astype(BF16), preferred_element_type=F32)
        b = jnp.dot(x, wu_ref[0, 0].astype(BF16), preferred_element_type=F32)
        hdn = (a * _sigmoid(a) * b).astype(BF16)
        y = jnp.dot(hdn, wd_ref[0, 0].astype(BF16), preferred_element_type=F32)
        tot = y + jnp.where(j > 0, acc_ref[:rows, :], 0.0)
        acc_ref[:rows, :] = tot
        o_ref[:rows, :] = tot.astype(o_ref.dtype)
        if rows < tm:
            o_ref[rows:, :] = jnp.zeros((tm - rows, o_ref.shape[1]), o_ref.dtype)

    for rows in range(MOE_SUB, tm + 1, MOE_SUB):
        pl.when((nv > rows - MOE_SUB) & (nv <= rows))(functools.partial(run, rows))

    @pl.when(nv == 0)
    def _():
        o_ref[...] = jnp.zeros(o_ref.shape, o_ref.dtype)


def moe_experts(blk_e, n_valid, n_used, xr, wg, wu, wd, layer):
    P, D = xr.shape
    De = wg.shape[3]
    tm, dc = MOE_TM, MOE_DC
    nj = De // dc
    chunk = lambda i, s: jnp.where(i % 2 == 0, s, nj - 1 - s)
    jj = lambda i, s, nu: jnp.where(i < nu[0], chunk(i, s), chunk(nu[0] - 1, nj - 1))
    grid_spec = pltpu.PrefetchScalarGridSpec(
        num_scalar_prefetch=3,
        grid=(P // tm, nj),
        in_specs=[pl.BlockSpec((tm, D), lambda i, j, be, nv, nu: (i, 0)),
                  pl.BlockSpec((1, 1, D, dc), lambda i, j, be, nv, nu: (layer, be[i], 0, jj(i, j, nu))),
                  pl.BlockSpec((1, 1, D, dc), lambda i, j, be, nv, nu: (layer, be[i], 0, jj(i, j, nu))),
                  pl.BlockSpec((1, 1, dc, D), lambda i, j, be, nv, nu: (layer, be[i], jj(i, j, nu), 0))],
        out_specs=pl.BlockSpec((tm, D), lambda i, j, be, nv, nu: (i, 0)),
        scratch_shapes=[pltpu.VMEM((tm, D), F32)],
    )
    return pl.pallas_call(
        _moe_kernel,
        out_shape=jax.ShapeDtypeStruct((P, D), BF16),
        grid_spec=grid_spec,
        compiler_params=_cparams(("arbitrary", "arbitrary")),
        name="moe_experts",
    )(blk_e, n_valid, n_used, xr, wg, wu, wd)


def _combine_kernel(h_ref, y0_ref, y1_ref, w_ref, g_ref, o_ref, *, final_norm):
    w = w_ref[...]
    x = h_ref[...] + (w[:, 0:1] * y0_ref[...].astype(F32) + w[:, 1:2] * y1_ref[...].astype(F32))
    if final_norm:
        ms = jnp.mean(x * x, axis=-1, keepdims=True)
        x = x * lax.rsqrt(ms + EPS) * g_ref[...]
    o_ref[...] = x


def moe_combine(h, y0, y1, w, g, tm):
    T, D = h.shape
    final_norm = g is not None
    gain = (g if final_norm else jnp.ones((D,), F32)).reshape(1, D).astype(F32)
    row = lambda width: pl.BlockSpec((tm, width), lambda i: (i, 0))
    return pl.pallas_call(
        functools.partial(_combine_kernel, final_norm=final_norm),
        out_shape=jax.ShapeDtypeStruct((T, D), F32),
        grid=(T // tm,),
        in_specs=[row(D), row(D), row(D), row(LANE), pl.BlockSpec((1, D), lambda i: (0, 0))],
        out_specs=row(D),
        compiler_params=_cparams(("parallel",)),
        name="moe_combine",
    )(h, y0, y1, _pad_cols(w, LANE), gain)


def _pad_cols(w, width):
    return jnp.pad(w, ((0, 0), (0, width - w.shape[1])))


def _w_in_segments(D):
    names = ("fq", "fk", "fv", "ff", "mcq", "mckv", "mkr", "nq", "nkc", "nvc", "nks", "nvs", "nkw", "nvw", "ngt", "mg")
    widths = (1024, 1024, 1024, FORGET_COLS, MLA_Q_RANK, MLA_KV_RANK, MLA_ROPE, NSA_HEADS * NSA_DK,
              NSA_GROUPS * NSA_DK, NSA_GROUPS * NSA_DV, NSA_GROUPS * NSA_DK, NSA_GROUPS * NSA_DV,
              NSA_GROUPS * NSA_DK, NSA_GROUPS * NSA_DV, NSA_GATE_COLS, N_BRANCH * D)
    src = dict(zip(names, np.cumsum((0,) + widths[:-1]).tolist()))
    wid = dict(zip(names, widths))
    segs = []
    plain = lambda name, cb, f=1.0: segs.append((cb * LANE, src[name], wid[name], f))

    def padded_k(name, cb, n, f=1.0):
        for i in range(n):
            segs.append((cb * LANE + i * NSA_DKP, src[name] + i * NSA_DK, NSA_DK, f))

    half = MLA_ROPE // 2
    plain("mckv", CB_CKV)
    plain("mkr", CB_KR)
    segs.append((CB_KR * LANE + MLA_ROPE, src["mkr"] + half, half, -1.0))
    segs.append((CB_KR * LANE + MLA_ROPE + half, src["mkr"], half, 1.0))
    plain("mcq", CB_CQ)
    padded_k("nkc", CB_NKC, NSA_GROUPS)
    padded_k("nq", CB_NQ, NSA_HEADS, NSA_DK ** -0.5 * LOG2E)
    plain("fq", CB_FQ, FOX_DH ** -0.5 * LOG2E)
    plain("fk", CB_FK)
    plain("fv", CB_FV)
    padded_k("nks", CB_NKS, NSA_GROUPS)
    padded_k("nkw", CB_NKW, NSA_GROUPS)
    plain("nvc", CB_NVC)
    plain("nvs", CB_NVS)
    plain("nvw", CB_NVW)
    plain("mg", CB_MG)
    return segs, src


def _pack_w_in_kernel(w_ref, o_ref, os_ref, *, segs, small_segs):
    d_in = w_ref.shape[1]

    def put(dst_ref, dst, s, n, f):
        for k in range(0, n, LANE):
            m = min(LANE, n - k)
            assert s + k + LANE <= d_in
            v = w_ref[0, s + k:s + k + LANE, :].T
            if f != 1.0:
                v = v * f
            dst_ref[:, dst + k:dst + k + m] = v[:, :m].astype(dst_ref.dtype)

    o_ref[...] = jnp.zeros(o_ref.shape, o_ref.dtype)
    for dst, s, n, f in segs:
        put(o_ref, dst, s, n, f)
    os_ref[...] = jnp.zeros(os_ref.shape, os_ref.dtype)
    for dst, s, n in small_segs:
        put(os_ref, dst, s, n, 1.0)


def _pack_w_in(w_all, layer):
    _, D, d_in = w_all.shape
    segs, src = _w_in_segments(D)
    small_segs = ((0, src["ff"], FORGET_COLS), (FORGET_COLS, src["ngt"], NSA_GATE_COLS))
    tr = LANE
    return pl.pallas_call(
        functools.partial(_pack_w_in_kernel, segs=tuple(segs), small_segs=small_segs),
        out_shape=(jax.ShapeDtypeStruct((D, Z_BLOCKS * LANE), BF16), jax.ShapeDtypeStruct((D, LANE), F32)),
        grid=(D // tr,),
        in_specs=[pl.BlockSpec((1, d_in, tr), lambda i: (layer, 0, i))],
        out_specs=(pl.BlockSpec((tr, Z_BLOCKS * LANE), lambda i: (i, 0)), pl.BlockSpec((tr, LANE), lambda i: (i, 0))),
        compiler_params=_cparams(("parallel",)),
        name="pack_w_in",
    )(jnp.swapaxes(w_all, 1, 2))


def _pack_w_uq(w):
    K = w.shape[0]
    w3 = w.reshape(K, MLA_HEADS, MLA_NOPE + MLA_ROPE) * ((MLA_NOPE + MLA_ROPE) ** -0.5 * LOG2E)
    nope = w3[:, :, :MLA_NOPE].reshape(K, MLA_HEADS * MLA_NOPE)
    r = w3[:, :, MLA_NOPE:]
    half = MLA_ROPE // 2
    r_rot = jnp.concatenate([-r[:, :, half:], r[:, :, :half]], axis=2)
    padr = lambda a: jnp.pad(a, ((0, 0), (0, 0), (0, LANE - MLA_ROPE))).reshape(K, MLA_HEADS * LANE)
    return jnp.concatenate([nope, padr(r), padr(r_rot)], axis=1).astype(BF16)


def _pack_w_ukv(w):
    K = w.shape[0]
    w3 = w.reshape(K, MLA_HEADS, MLA_NOPE + MLA_DV)
    return jnp.concatenate([w3[:, :, :MLA_NOPE].reshape(K, -1), w3[:, :, MLA_NOPE:].reshape(K, -1)], axis=1).astype(BF16)


def _t5_bucket(dist):
    dist = jnp.maximum(dist, 0)
    exact = REL_BUCKETS // 2
    df = jnp.maximum(dist, 1).astype(F32)
    large = exact + (jnp.log(df / exact) / math.log(REL_MAX_DIST / exact) * (REL_BUCKETS - exact)).astype(jnp.int32)
    large = jnp.minimum(large, REL_BUCKETS - 1)
    return jnp.where(dist < exact, dist, large)


def _position_tables(S, rel_bias):
    t = NSA_T
    half = MLA_ROPE // 2
    inv = ROPE_THETA ** (-jnp.arange(half, dtype=F32) / half)
    ang = jnp.arange(S, dtype=F32)[:, None] * inv
    c, s = jnp.cos(ang), jnp.sin(ang)
    cos = _pad_cols(jnp.concatenate([c, c], axis=1), LANE)
    sin = _pad_cols(jnp.concatenate([s, s], axis=1), LANE)
    ncp = max(S // CMP_STRIDE, LANE)
    pos = jnp.arange(S)

    def bias_of(dist):
        onehot = jax.nn.one_hot(_t5_bucket(dist), REL_BUCKETS, dtype=F32)
        return jnp.einsum("...b,bh->h...", onehot, rel_bias, precision=lax.Precision.HIGHEST)

    bias_c = bias_of(pos[None, :] - (CMP_STRIDE * jnp.arange(ncp)[:, None] + CMP_BLOCK - 1))
    i = jnp.arange(t)
    bias_sw = jnp.stack([bias_of(k * t + i[:, None] - i[None, :]) for k in range(3)], axis=1)
    bias_sw = bias_sw.reshape(NSA_GROUPS, NSA_HPG, 3, t, t).transpose(0, 2, 4, 1, 3).reshape(NSA_GROUPS, 3, t, NSA_HPG * t)
    n_cmp = (S - CMP_BLOCK) // CMP_STRIDE + 1
    n_blk = S // SEL_BLOCK
    nbp = max(n_blk, LANE)
    cstart = CMP_STRIDE * jnp.arange(ncp)
    sstart = SEL_BLOCK * jnp.arange(nbp)
    ov = jnp.clip(jnp.minimum(cstart[:, None] + CMP_BLOCK, sstart[None, :] + SEL_BLOCK)
                  - jnp.maximum(cstart[:, None], sstart[None, :]), 0, None).astype(F32) / CMP_STRIDE
    ov = jnp.where((jnp.arange(ncp)[:, None] < n_cmp) & (jnp.arange(nbp)[None, :] < n_blk), ov, 0.0).astype(BF16)
    expand = ((pos[:, None] // SEL_BLOCK) == jnp.arange(nbp)[None, :]).astype(BF16)
    return cos, sin, bias_c, bias_sw, ov, expand


def _token_mixers(h, p, w_in_all, layer, tabs, B, S):
    T, D = h.shape
    cos, sin, bias_c, bias_sw, overlap, expand = tabs
    wz, w_small = _pack_w_in(w_in_all, layer)
    z, zs = input_projection(h, p["g_mix"], wz, w_small, tm=min(T, 1024), tn=IN_PROJ_TN)
    z3 = z.reshape(B, S, Z_BLOCKS * LANE)

    log_f = jax.nn.log_sigmoid(zs[:, :FORGET_COLS] + p["b_forget"].astype(F32)).reshape(B, S, FOX_HEADS)
    cum = jnp.cumsum(log_f, axis=1) * LOG2E
    o_fox = causal_attention(z3, CB_FQ, z3, CB_FK, z3, CB_FV, FOX_HEADS, cum=cum)

    q_nope, q_rope = mla_q_proj(z, p["g_cq"], _pack_w_uq(p["w_uq"]), cos, sin, S, tm=min(S, 512))
    kv = rms_matmul(z, CB_CKV * LANE // MLA_KV_RANK, p["g_ckv"], _pack_w_ukv(p["w_ukv"]), BF16, tm=min(T, 1024), tn=1024)
    kr = z3[:, :, CB_KR * LANE:(CB_KR + 1) * LANE].astype(F32)
    kr = kr[..., :MLA_ROPE] * cos[None, :, :MLA_ROPE] + kr[..., MLA_ROPE:] * sin[None, :, :MLA_ROPE]
    k_rope = jnp.pad(kr, ((0, 0), (0, 0), (0, LANE - MLA_ROPE))).astype(BF16)
    hw = MLA_HEADS * LANE
    kv3 = kv.reshape(B, S, 2 * hw)
    o_mla = causal_attention(q_nope.reshape(B, S, hw), 0, kv3, 0, kv3, MLA_HEADS, MLA_HEADS,
                             q2=q_rope.reshape(B, S, hw), k2=k_rope)

    G = NSA_GROUPS
    NC = S // CMP_STRIDE
    ncp = bias_c.shape[1]

    def compress_branch(cb, dp, d, pe, w1, w2):
        x = z3[:, :, cb * LANE: cb * LANE + G * dp].reshape(B, NC, CMP_STRIDE * G * dp)
        eye = jnp.eye(G, dtype=F32)
        w1p = jnp.pad(w1.reshape(CMP_BLOCK, d, d), ((0, 0), (0, dp - d), (0, dp - d)))
        w1g = jnp.einsum("lij,gh->lgihj", w1p, eye).reshape(CMP_BLOCK, G * dp, G * dp).astype(BF16)
        w1a = w1g[:CMP_STRIDE].reshape(CMP_STRIDE * G * dp, G * dp)
        w1b = w1g[CMP_STRIDE:].reshape(CMP_STRIDE * G * dp, G * dp)
        pe_g = jnp.tile(jnp.pad(pe, ((0, 0), (0, dp - d)))[:, None, :], (1, G, 1))
        pe2 = pe_g.reshape(2, CMP_STRIDE * G * dp).astype(BF16)
        w2p = jnp.pad(w2, ((0, dp - d), (0, dp - d)))
        w2g = jnp.einsum("ij,gh->gihj", w2p, eye).reshape(G * dp, G * dp).astype(BF16)
        out = compress(x, w1a, w1b, pe2, w2g)
        return jnp.pad(out, ((0, 0), (0, ncp - NC), (0, 0)))

    kc = compress_branch(CB_NKC, NSA_DKP, NSA_DK, p["pe_k"], p["w_cmp_k1"], p["w_cmp_k2"])
    vc = compress_branch(CB_NVC, NSA_DV, NSA_DV, p["pe_v"], p["w_cmp_v1"], p["w_cmp_v2"])
    gl = zs[:, FORGET_COLS:FORGET_COLS + NSA_GATE_COLS].reshape(B, S, G, NSA_HPG * 3)
    gl = jnp.transpose(gl, (0, 2, 1, 3))
    gl_cols = jnp.pad(jnp.swapaxes(gl, 2, 3), ((0, 0), (0, 0), (0, 16 - NSA_HPG * 3), (0, 0)))
    o_cmp, ind = nsa_cmp_select(z3, kc, vc, bias_c, overlap, gl_cols, min(N_SEL, S // SEL_BLOCK))
    o_nsa = nsa_sel_win(z3, ind, expand, bias_sw, gl_cols, o_cmp)

    merged = merge_branches(o_fox.reshape(T, -1), o_mla.reshape(T, -1), o_nsa.reshape(T, -1),
                            p["w_branch"].astype(BF16), z, tm=min(T, 1024), tn=512)
    return matmul_residual(merged, p["w_out"].astype(BF16), h, tm=min(T, 1024), tn=1024)


def _memory_block(h, mem2, p, B, S):
    T, D = h.shape
    kv = rms_matmul(mem2, 0, p["g_mem_kv"], p["w_mem_kv"].astype(BF16), BF16, tm=min(mem2.shape[0], 512), tn=512)
    out = memory_attention_block(h.reshape(B, S, D), p["g_mem_q"], (p["w_mem_q"] * (MEM_DH ** -0.5 * LOG2E)).astype(BF16),
                                 kv.reshape(B, -1, kv.shape[1]), p["w_mem_o"].astype(BF16), tq=min(S, 512))
    return out.reshape(T, D)


def _moe_block(h, p, experts, layer, g_final=None):
    T, D = h.shape
    tm = MOE_TM
    w_r = _pad_cols(jnp.concatenate([p["w_router_group"], p["w_router_expert"]], axis=1), LANE).astype(F32)
    b_r = _pad_cols(jnp.concatenate([p["b_router_group"], p["b_router_expert"]])[None, :], LANE).astype(F32)
    logits, u = moe_router(h, p["g_moe"], w_r, b_r, tm=min(T, 512))
    routed, totals = moe_route(logits, tm=min(T, 512))
    assert TOP_K == 2
    flat_e = routed[:, 0:2].astype(jnp.int32).reshape(-1)
    rank = routed[:, 2:4].astype(jnp.int32).reshape(-1)
    weight = routed[:, 4:6]
    counts = totals[0, N_GROUPS:N_GROUPS + N_EXPERTS].astype(jnp.int32)
    TK = T * TOP_K
    pcounts = ((counts + tm - 1) // tm) * tm
    pends = jnp.cumsum(pcounts)
    dest = (pends - pcounts)[flat_e] + rank
    n_rb = -(-TK // tm) + N_EXPERTS
    P = n_rb * tm
    row_tok = (jnp.arange(P, dtype=jnp.int32) % T).at[dest].set(jnp.repeat(jnp.arange(T, dtype=jnp.int32), TOP_K))
    blk_e = jnp.sum((pends[None, :] <= (jnp.arange(n_rb, dtype=jnp.int32) * tm)[:, None]).astype(jnp.int32), axis=1)
    blk_e = jnp.minimum(blk_e, N_EXPERTS - 1).astype(jnp.int32)
    n_used = (pends[-1] // tm).astype(jnp.int32).reshape(1)
    row_end = pends - pcounts + counts
    n_valid = jnp.clip(row_end[blk_e] - jnp.arange(n_rb, dtype=jnp.int32) * tm, 0, tm).astype(jnp.int32)
    xr = u[row_tok]
    y = moe_experts(blk_e, n_valid, n_used, xr, experts[0].astype(F32), experts[1].astype(F32),
                    experts[2].astype(F32), layer)
    d2 = dest.reshape(T, TOP_K)
    return moe_combine(h, y[d2[:, 0]], y[d2[:, 1]], weight, g_final, tm=min(T, 512))


_LAYER_KEYS = ("g_mix", "w_in", "b_forget", "g_cq", "g_ckv", "w_uq", "w_ukv", "pe_k", "pe_v", "w_cmp_k1", "w_cmp_k2",
               "w_cmp_v1", "w_cmp_v2", "w_branch", "w_out", "g_mem_q", "g_mem_kv", "w_mem_q", "w_mem_kv", "w_mem_o",
               "g_moe", "w_router_group", "b_router_group", "w_router_expert", "b_router_expert")


def kernel(x, mem, g_mix, w_in, b_forget, g_cq, g_ckv, w_uq, w_ukv, pe_k, pe_v, w_cmp_k1, w_cmp_k2, w_cmp_v1, w_cmp_v2, rel_bias, w_branch, w_out, g_mem_q, g_mem_kv, w_mem_q, w_mem_kv, w_mem_o, g_moe, w_router_group, b_router_group, w_router_expert, b_router_expert, w_exp_gate, w_exp_up, w_exp_down, g_final):
    B, S, D = x.shape
    T = B * S
    stacked = dict(g_mix=g_mix, w_in=w_in, b_forget=b_forget, g_cq=g_cq, g_ckv=g_ckv, w_uq=w_uq, w_ukv=w_ukv,
                   pe_k=pe_k, pe_v=pe_v, w_cmp_k1=w_cmp_k1, w_cmp_k2=w_cmp_k2, w_cmp_v1=w_cmp_v1, w_cmp_v2=w_cmp_v2,
                   w_branch=w_branch, w_out=w_out, g_mem_q=g_mem_q, g_mem_kv=g_mem_kv, w_mem_q=w_mem_q,
                   w_mem_kv=w_mem_kv, w_mem_o=w_mem_o, g_moe=g_moe, w_router_group=w_router_group,
                   b_router_group=b_router_group, w_router_expert=w_router_expert, b_router_expert=b_router_expert,
                   w_exp_gate=w_exp_gate, w_exp_up=w_exp_up, w_exp_down=w_exp_down)
    tabs = _position_tables(S, rel_bias.astype(F32) * LOG2E)
    h = x.reshape(T, D).astype(F32)
    mem2 = mem.reshape(-1, D).astype(F32)
    depth = w_in.shape[0]
    assert depth >= 1
    for l in range(depth):
        p = {k: stacked[k][l] for k in _LAYER_KEYS}
        h = _token_mixers(h, p, w_in.astype(F32), l, tabs, B, S)
        h = _memory_block(h, mem2, p, B, S)
        h = _moe_block(h, p, (w_exp_gate, w_exp_up, w_exp_down), l, g_final if l == depth - 1 else None)
    return h.reshape(B, S, D)
```

```python
import functools
import math

import jax
import jax.numpy as jnp
import numpy as np
from jax import lax
from jax.experimental import pallas as pl
from jax.experimental.pallas import tpu as pltpu

F32 = jnp.float32
BF16 = jnp.bfloat16

EPS = 1e-6
NEG_INF = -1e30
LOG2E = math.log2(math.e)
LANE = 128

FOX_HEADS, FOX_DH = 8, 128
MLA_HEADS, MLA_NOPE, MLA_ROPE, MLA_DV = 8, 128, 64, 128
MLA_Q_RANK, MLA_KV_RANK = 768, 512
ROPE_THETA = 10000.0
NSA_HEADS, NSA_GROUPS, NSA_DK, NSA_DV = 8, 2, 192, 128
NSA_HPG = NSA_HEADS // NSA_GROUPS
NSA_DKP = 256
CMP_BLOCK, CMP_STRIDE, SEL_BLOCK, N_SEL, WINDOW = 32, 16, 64, 8, 512
SEL_FORCE = 1e6
REL_BUCKETS, REL_MAX_DIST = 32, 128
N_BRANCH, BRANCH_W = 3, 1024
MEM_HEADS, MEM_DH = 4, 128
N_GROUPS, EXPERTS_PER_GROUP, TOP_K = 4, 8, 2
N_EXPERTS = N_GROUPS * EXPERTS_PER_GROUP
FORGET_COLS, NSA_GATE_COLS = FOX_HEADS, NSA_HEADS * 3

CB_CKV, CB_KR, CB_CQ = 0, 4, 6
CB_NKC, CB_NQ = 12, 16
CB_FQ, CB_FK, CB_FV = 32, 40, 48
CB_NKS, CB_NKW = 56, 60
CB_NVC, CB_NVS, CB_NVW = 64, 66, 68
CB_MG = 72
Z_BLOCKS = 120
IN_PROJ_TN = 8 * LANE

ATT_T = 256
ATT_HG = 8
NSA_T = 256
NSA_CMP_T = 512
MOE_TM = 512
MOE_SUB = 256
MOE_DC = 512
VMEM_LIMIT = 56 * 1024 * 1024


def _cparams(sem):
    return pltpu.CompilerParams(dimension_semantics=sem, vmem_limit_bytes=VMEM_LIMIT)


def _sigmoid(x):
    return 1.0 / (1.0 + jnp.exp(-x))


def _rms_mm_kernel(x_ref, g_ref, w_ref, o_ref, u_ref):
    @pl.when(pl.program_id(1) == 0)
    def _():
        x = x_ref[...].astype(F32)
        ms = jnp.mean(x * x, axis=-1, keepdims=True)
        u_ref[...] = (x * lax.rsqrt(ms + EPS) * g_ref[...]).astype(u_ref.dtype)

    o_ref[...] = jnp.dot(u_ref[...], w_ref[...], preferred_element_type=F32).astype(o_ref.dtype)


def rms_matmul(x, xcol, g, w, out_dtype, tm, tn):
    T = x.shape[0]
    K, N = w.shape
    return pl.pallas_call(
        _rms_mm_kernel,
        out_shape=jax.ShapeDtypeStruct((T, N), out_dtype),
        grid=(T // tm, N // tn),
        in_specs=[pl.BlockSpec((tm, K), lambda i, j: (i, xcol)),
                  pl.BlockSpec((1, K), lambda i, j: (0, 0)),
                  pl.BlockSpec((K, tn), lambda i, j: (0, j))],
        out_specs=pl.BlockSpec((tm, tn), lambda i, j: (i, j)),
        scratch_shapes=[pltpu.VMEM((tm, K), BF16)],
        compiler_params=_cparams(("parallel", "arbitrary")),
        name="rms_matmul",
    )(x, g.reshape(1, K).astype(F32), w)


def _in_proj_kernel(x_ref, g_ref, w_ref, ws_ref, o_ref, os_ref, u_ref):
    @pl.when(pl.program_id(1) == 0)
    def _():
        x = x_ref[...]
        ms = jnp.mean(x * x, axis=-1, keepdims=True)
        u = x * lax.rsqrt(ms + EPS) * g_ref[...]
        u_ref[...] = u.astype(u_ref.dtype)
        os_ref[...] = _dot3(u, ws_ref[...])

    o_ref[...] = jnp.dot(u_ref[...], w_ref[...], preferred_element_type=F32).astype(o_ref.dtype)


def input_projection(x, g, w, w_small, tm, tn):
    T, K = x.shape
    N = w.shape[1]
    Ns = w_small.shape[1]
    return pl.pallas_call(
        _in_proj_kernel,
        out_shape=(jax.ShapeDtypeStruct((T, N), BF16), jax.ShapeDtypeStruct((T, Ns), F32)),
        grid=(T // tm, N // tn),
        in_specs=[pl.BlockSpec((tm, K), lambda i, j: (i, 0)),
                  pl.BlockSpec((1, K), lambda i, j: (0, 0)),
                  pl.BlockSpec((K, tn), lambda i, j: (0, j)),
                  pl.BlockSpec((K, Ns), lambda i, j: (0, 0))],
        out_specs=(pl.BlockSpec((tm, tn), lambda i, j: (i, j)), pl.BlockSpec((tm, Ns), lambda i, j: (i, 0))),
        scratch_shapes=[pltpu.VMEM((tm, K), BF16)],
        compiler_params=_cparams(("parallel", "arbitrary")),
        name="input_projection",
    )(x, g.reshape(1, K).astype(F32), w, w_small)


def _mla_q_kernel(x_ref, g_ref, w_ref, cos_ref, sin_ref, qn_ref, qr_ref):
    x = x_ref[...].astype(F32)
    ms = jnp.mean(x * x, axis=-1, keepdims=True)
    u = (x * lax.rsqrt(ms + EPS) * g_ref[...]).astype(BF16)
    y = jnp.dot(u, w_ref[...], preferred_element_type=F32)
    hw = MLA_HEADS * LANE
    qn_ref[...] = y[:, :hw].astype(qn_ref.dtype)
    cos = cos_ref[...]
    sin = sin_ref[...]
    for h in range(MLA_HEADS):
        a = y[:, hw + h * LANE: hw + (h + 1) * LANE]
        b = y[:, 2 * hw + h * LANE: 2 * hw + (h + 1) * LANE]
        qr_ref[:, h * LANE:(h + 1) * LANE] = (a * cos + b * sin).astype(qr_ref.dtype)


def mla_q_proj(z, g, w, cos, sin, S, tm):
    T = z.shape[0]
    K, N = w.shape
    hw = MLA_HEADS * LANE
    nsb = S // tm
    return pl.pallas_call(
        _mla_q_kernel,
        out_shape=(jax.ShapeDtypeStruct((T, hw), BF16), jax.ShapeDtypeStruct((T, hw), BF16)),
        grid=(T // tm,),
        in_specs=[pl.BlockSpec((tm, K), lambda i: (i, CB_CQ * LANE // MLA_Q_RANK)),
                  pl.BlockSpec((1, K), lambda i: (0, 0)),
                  pl.BlockSpec((K, N), lambda i: (0, 0)),
                  pl.BlockSpec((tm, LANE), lambda i: (i % nsb, 0)),
                  pl.BlockSpec((tm, LANE), lambda i: (i % nsb, 0))],
        out_specs=(pl.BlockSpec((tm, hw), lambda i: (i, 0)), pl.BlockSpec((tm, hw), lambda i: (i, 0))),
        compiler_params=_cparams(("parallel",)),
        name="mla_q_proj",
    )(z, g.reshape(1, K).astype(F32), w, cos, sin)


def _causal_attn_kernel(*refs, t, hg, two_part, decay):
    refs = list(refs)
    q_ref, k_ref, v_ref = refs[:3]
    pos = 3
    if two_part:
        q2_ref, k2_ref = refs[pos:pos + 2]
        pos += 2
    if decay:
        ck_ref = refs[pos]
        pos += 1
    o_ref = refs[pos]
    if two_part:
        kcat_ref = refs[pos + 1]
    qi = pl.program_id(2)
    dn = (((1,), (1,)), ((), ()))

    if two_part:
        @pl.when(qi == 0)
        def _():
            for j in range(hg):
                kcat_ref[j, :, :LANE] = k_ref[0, :, j * LANE:(j + 1) * LANE]
                kcat_ref[j, :, LANE:] = k2_ref[0]

    qs = []
    for j in range(hg):
        qj = q_ref[0, :, j * LANE:(j + 1) * LANE]
        if two_part:
            qj = jnp.concatenate([qj, q2_ref[0, :, j * LANE:(j + 1) * LANE]], axis=1)
        qs.append(qj)

    def step(kb, carry, masked, width=1):
        off = pl.multiple_of(kb * t, t)
        tk = width * t
        heads = range(hg)
        ss = []
        for j in heads:
            k = kcat_ref[j, pl.ds(off, tk), :] if two_part else k_ref[0, pl.ds(off, tk), j * LANE:(j + 1) * LANE]
            ss.append(lax.dot_general(k, qs[j], dn, preferred_element_type=F32))
        if decay:
            ss = [ss[j] - ck_ref[0, 0, pl.ds(off, tk), j:j + 1] for j in heads]
        if masked:
            r = lax.broadcasted_iota(jnp.int32, (tk, t), 0)
            c = lax.broadcasted_iota(jnp.int32, (tk, t), 1)
            ss = [jnp.where(r <= c + (width - 1) * t, s, NEG_INF) for s in ss]
        ms = [jnp.maximum(carry[j][0], jnp.max(ss[j], axis=0, keepdims=True)) for j in heads]
        ps = [jnp.exp2(ss[j] - ms[j]) for j in heads]
        out = []
        for j in heads:
            m, l, acc = carry[j]
            a = jnp.exp2(m - ms[j])
            l = a * l + jnp.sum(ps[j], axis=0, keepdims=True)
            v = v_ref[0, pl.ds(off, tk), j * LANE:(j + 1) * LANE]
            acc = a * acc + lax.dot_general(v, ps[j].astype(BF16), (((0,), (0,)), ((), ())),
                                            preferred_element_type=F32)
            out.append((ms[j], l, acc))
        return tuple(out)

    init = tuple((jnp.full((1, t), NEG_INF, F32), jnp.zeros((1, t), F32), jnp.zeros((LANE, t), F32))
                 for _ in range(hg))
    carry = lax.fori_loop(0, qi // 2, lambda kp, c: step(2 * kp, c, False, width=2), init)
    carry = lax.cond(qi % 2 == 1, lambda c: step(qi - 1, c, True, width=2), lambda c: step(qi, c, True), carry)
    for j in range(hg):
        _, l, acc = carry[j]
        o_ref[0, :, j * LANE:(j + 1) * LANE] = (acc / l).T.astype(o_ref.dtype)


def causal_attention(q, qcb, k, kcb, v, vcb, heads, q2=None, k2=None, cum=None):
    B, S, _ = q.shape
    t, hg = ATT_T, ATT_HG
    w = hg * LANE
    two_part, decay = q2 is not None, cum is not None
    in_specs = [pl.BlockSpec((1, t, w), lambda b, h, i: (b, i, qcb // hg + h)),
                pl.BlockSpec((1, S, w), lambda b, h, i: (b, 0, kcb // hg + h)),
                pl.BlockSpec((1, S, w), lambda b, h, i: (b, 0, vcb // hg + h))]
    args = [q, k, v]
    scratch = []
    if two_part:
        in_specs += [pl.BlockSpec((1, t, w), lambda b, h, i: (b, i, h)),
                     pl.BlockSpec((1, S, LANE), lambda b, h, i: (b, 0, 0))]
        args += [q2, k2]
        scratch = [pltpu.VMEM((hg, S, 2 * LANE), BF16)]
    if decay:
        in_specs += [pl.BlockSpec((1, 1, S, hg), lambda b, h, i: (b, h, 0, 0))]
        args += [jnp.transpose(cum.reshape(B, S, heads // hg, hg), (0, 2, 1, 3))]
    return pl.pallas_call(
        functools.partial(_causal_attn_kernel, t=t, hg=hg, two_part=two_part, decay=decay),
        out_shape=jax.ShapeDtypeStruct((B, S, heads * LANE), BF16),
        grid=(B, heads // hg, S // t),
        in_specs=in_specs,
        out_specs=pl.BlockSpec((1, t, w), lambda b, h, i: (b, i, h)),
        scratch_shapes=scratch,
        compiler_params=_cparams(("parallel", "parallel", "arbitrary")),
        name="causal_attention",
    )(*args)


def _gelu_tanh(x):
    return 0.5 * x * (1.0 + jnp.tanh(math.sqrt(2.0 / math.pi) * (x + 0.044715 * (x * x * x))))


def _compress_kernel(x_ref, w1a_ref, w1b_ref, pe_ref, w2_ref, o_ref):
    x = x_ref[0]
    a = jnp.dot(x, w1a_ref[...], preferred_element_type=F32)
    b = jnp.dot(x, w1b_ref[...], preferred_element_type=F32)
    nc = a.shape[0]
    b_next = pltpu.roll(b, nc - 1, 0)
    pe_term = jnp.dot(pe_ref[0:1, :], w1a_ref[...], preferred_element_type=F32) + \
        jnp.dot(pe_ref[1:2, :], w1b_ref[...], preferred_element_type=F32)
    hid = _gelu_tanh(a + b_next + pe_term)
    o_ref[0] = jnp.dot(hid.astype(BF16), w2_ref[...], preferred_element_type=F32).astype(o_ref.dtype)


def compress(x, w1a, w1b, pe2, w2):
    BG, NC, KD = x.shape
    dp = w2.shape[1]
    return pl.pallas_call(
        _compress_kernel,
        out_shape=jax.ShapeDtypeStruct((BG, NC, dp), BF16),
        grid=(BG,),
        in_specs=[pl.BlockSpec((1, NC, KD), lambda i: (i, 0, 0)),
                  pl.BlockSpec((KD, dp), lambda i: (0, 0)),
                  pl.BlockSpec((KD, dp), lambda i: (0, 0)),
                  pl.BlockSpec((2, KD), lambda i: (0, 0)),
                  pl.BlockSpec((dp, dp), lambda i: (0, 0))],
        out_specs=pl.BlockSpec((1, NC, dp), lambda i: (i, 0, 0)),
        compiler_params=_cparams(("parallel",)),
        name="nsa_compress",
    )(x, w1a, w1b, pe2, w2)


def _nsa_cmp_kernel(q_ref, kc_ref, vc_ref, bias_ref, ov_ref, gate_ref, o_ref, ind_ref, *, t, n_sel, n_rows):
    qi = pl.program_id(2)
    ncp = kc_ref.shape[1]
    kc = kc_ref[0]
    vc = vc_ref[0]
    pos = lax.broadcasted_iota(jnp.int32, (ncp, t), 1) + qi * t
    cblk = lax.broadcasted_iota(jnp.int32, (ncp, t), 0)
    valid = pos >= CMP_STRIDE * cblk + (CMP_BLOCK - 1)
    dn = (((1,), (1,)), ((), ()))
    tn = (((0,), (0,)), ((), ()))
    heads = range(NSA_HPG)
    ss = [lax.dot_general(kc, q_ref[0, :, h * NSA_DKP:(h + 1) * NSA_DKP], dn, preferred_element_type=F32)
          for h in heads]
    ss = [jnp.where(valid, ss[h] + bias_ref[h], NEG_INF) for h in heads]
    es = [jnp.exp2(ss[h] - jnp.max(ss[h], axis=0, keepdims=True)) for h in heads]
    ps = [jnp.where(valid, es[h] / jnp.sum(es[h], axis=0, keepdims=True), 0.0) for h in heads]
    gates = _sigmoid(gate_ref[0, 0])
    for h in heads:
        o = lax.dot_general(vc, ps[h].astype(BF16), tn, preferred_element_type=F32)
        o_ref[0, :, h * NSA_DV:(h + 1) * NSA_DV] = (gates[3 * h:3 * h + 1, :] * o).T.astype(o_ref.dtype)
    psum = functools.reduce(lambda x, y: x + y, ps)
    p_hi = psum.astype(BF16)
    p_lo = (psum - p_hi.astype(F32)).astype(BF16)
    imp = lax.dot_general(ov_ref[...], p_hi, tn, preferred_element_type=F32) + \
        lax.dot_general(ov_ref[...], p_lo, tn, preferred_element_type=F32)
    nbp = imp.shape[0]
    imp = imp[:n_rows]
    blk = lax.broadcasted_iota(jnp.int32, (n_rows, t), 0)
    cur = (lax.broadcasted_iota(jnp.int32, (n_rows, t), 1) + qi * t) // SEL_BLOCK
    forced = (blk == 0) | (blk == cur) | (blk == cur - 1)
    score = jnp.where(blk <= cur, imp + jnp.where(forced, SEL_FORCE, 0.0), NEG_INF)
    sel = jnp.zeros((n_rows, t), F32)
    for _ in range(n_sel):
        mx = jnp.max(score, axis=0, keepdims=True)
        first = jnp.min(jnp.where(score == mx, blk, nbp), axis=0, keepdims=True)
        hit = blk == first
        sel = jnp.where(hit, 1.0, sel)
        score = jnp.where(hit, -jnp.inf, score)
    ind_ref[0, 0] = jnp.zeros(ind_ref.shape[2:], ind_ref.dtype)
    ind_ref[0, 0, :n_rows, :] = sel.astype(ind_ref.dtype)


def nsa_cmp_select(z3, kc, vc, bias_c, overlap, gates, n_sel):
    B, S, _ = z3.shape
    G = NSA_GROUPS
    t = min(NSA_CMP_T, S)
    ncp = kc.shape[1]
    nbp = overlap.shape[1]
    qw = NSA_HPG * NSA_DKP
    ow = NSA_HPG * NSA_DV
    n_rows = min(nbp, -(-(S // SEL_BLOCK) // 16) * 16)
    return pl.pallas_call(
        functools.partial(_nsa_cmp_kernel, t=t, n_sel=n_sel, n_rows=n_rows),
        out_shape=(jax.ShapeDtypeStruct((B, S, G * ow), BF16), jax.ShapeDtypeStruct((B, G, nbp, S), BF16)),
        grid=(B, G, S // t),
        in_specs=[pl.BlockSpec((1, t, qw), lambda b, g, i: (b, i, CB_NQ * LANE // qw + g)),
                  pl.BlockSpec((1, ncp, NSA_DKP), lambda b, g, i: (b, 0, g)),
                  pl.BlockSpec((1, ncp, NSA_DV), lambda b, g, i: (b, 0, g)),
                  pl.BlockSpec((NSA_HPG, ncp, t), lambda b, g, i: (g, 0, i)),
                  pl.BlockSpec((ncp, nbp), lambda b, g, i: (0, 0)),
                  pl.BlockSpec((1, 1, gates.shape[2], t), lambda b, g, i: (b, g, 0, i))],
        out_specs=(pl.BlockSpec((1, t, ow), lambda b, g, i: (b, i, g)),
                   pl.BlockSpec((1, 1, nbp, t), lambda b, g, i: (b, g, 0, i))),
        compiler_params=_cparams(("parallel", "parallel", "arbitrary")),
        name="nsa_cmp_select",
    )(z3, kc, vc, bias_c, overlap, gates)


def _nsa_sw_kernel(q_ref, ks_ref, vs_ref, kw_ref, vw_ref, ind_ref, e_ref, bias_ref, gate_ref, oc_ref, o_ref,
                   *, t):
    qi = pl.program_id(1)
    hp, G = NSA_HPG, NSA_GROUPS
    q4 = [jnp.concatenate([q_ref[0, :, (g * hp + h) * NSA_DKP:(g * hp + h + 1) * NSA_DKP] for h in range(hp)], axis=0)
          for g in range(G)]
    inds = [ind_ref[0, g] for g in range(G)]
    dn = (((1,), (1,)), ((), ()))

    def step(kb, carry, k_ref, v_ref, selected, width=1):
        off = pl.multiple_of(kb * t, t)
        tk = width * t
        ri = lax.broadcasted_iota(jnp.int32, (tk, t), 0)
        ci = lax.broadcasted_iota(jnp.int32, (tk, t), 1)
        d = (qi - kb) * t + ci - ri
        near = (d >= 0) if selected else (d >= 0) & (d < WINDOW)
        groups = range(G)
        ss = [lax.dot_general(k_ref[0, pl.ds(off, tk), g * NSA_DKP:(g + 1) * NSA_DKP], q4[g], dn,
                              preferred_element_type=F32) for g in groups]
        negs = []
        for g in groups:
            mask = near
            if selected:
                hit = jnp.dot(e_ref[pl.ds(off, tk), :], inds[g], preferred_element_type=F32)
                mask = near & (hit > 0.5)
            neg = jnp.where(mask, 0.0, NEG_INF)
            negs.append(jnp.concatenate([neg] * hp, axis=1))
        bias = [jnp.concatenate([bias_ref[g, jnp.minimum(qi - kb - w, 2)] for w in range(width)], axis=0)
                if width > 1 else bias_ref[g, jnp.minimum(qi - kb, 2)] for g in groups]
        ss = [ss[g] + bias[g] + negs[g] for g in groups]
        ms = [jnp.maximum(carry[g][0], jnp.max(ss[g], axis=0, keepdims=True)) for g in groups]
        ps = [jnp.exp2(ss[g] - ms[g]) for g in groups]
        out = []
        for g in groups:
            m, l, acc = carry[g]
            a = jnp.exp2(m - ms[g])
            l = a * l + jnp.sum(ps[g], axis=0, keepdims=True)
            v = v_ref[0, pl.ds(off, tk), g * NSA_DV:(g + 1) * NSA_DV]
            acc = a * acc + lax.dot_general(v, ps[g].astype(BF16), (((0,), (0,)), ((), ())),
                                            preferred_element_type=F32)
            out.append((ms[g], l, acc))
        return tuple(out)

    init = tuple((jnp.full((1, hp * t), NEG_INF, F32), jnp.zeros((1, hp * t), F32), jnp.zeros((NSA_DV, hp * t), F32))
                 for _ in range(G))
    sel = lax.fori_loop(0, (qi + 1) // 2, lambda kp, c: step(2 * kp, c, ks_ref, vs_ref, True, width=2), init)
    sel = lax.cond(qi % 2 == 0, lambda c: step(qi, c, ks_ref, vs_ref, True), lambda c: c, sel)
    nwin = WINDOW // t + 1
    win_update = lambda width: (lambda c: step(qi - (width - 1), c, kw_ref, vw_ref, False, width=width))
    win = win_update(1)
    for width in range(2, nwin + 1):
        win = functools.partial(lax.cond, qi >= width - 1, win_update(width), win)
    win = win(init)
    for g in range(G):
        o_s = sel[g][2] / sel[g][1]
        o_w = win[g][2] / win[g][1]
        gates = _sigmoid(gate_ref[0, g])
        for h in range(hp):
            r = slice(h * t, (h + 1) * t)
            c = slice((g * hp + h) * NSA_DV, (g * hp + h + 1) * NSA_DV)
            o = gates[3 * h + 1:3 * h + 2, :] * o_s[:, r] + gates[3 * h + 2:3 * h + 3, :] * o_w[:, r]
            o_ref[0, :, c] = (oc_ref[0, :, c].astype(F32) + o.T).astype(o_ref.dtype)


def nsa_sel_win(z3, ind, expand, bias_sw, gates, o_cmp):
    B, S, _ = z3.shape
    G = NSA_GROUPS
    t = NSA_T
    qw = NSA_HEADS * NSA_DKP
    ow = NSA_HEADS * NSA_DV
    kw = G * NSA_DKP
    vw = G * NSA_DV
    nbp = ind.shape[2]
    kspec = lambda cb: pl.BlockSpec((1, S, kw), lambda b, i: (b, 0, cb * LANE // kw))
    vspec = lambda cb: pl.BlockSpec((1, S, vw), lambda b, i: (b, 0, cb * LANE // vw))
    return pl.pallas_call(
        functools.partial(_nsa_sw_kernel, t=t),
        out_shape=jax.ShapeDtypeStruct((B, S, ow), BF16),
        grid=(B, S // t),
        in_specs=[pl.BlockSpec((1, t, qw), lambda b, i: (b, i, CB_NQ * LANE // qw)),
                  kspec(CB_NKS), vspec(CB_NVS), kspec(CB_NKW), vspec(CB_NVW),
                  pl.BlockSpec((1, G, nbp, t), lambda b, i: (b, 0, 0, i)),
                  pl.BlockSpec((S, nbp), lambda b, i: (0, 0)),
                  pl.BlockSpec((G, 3, t, NSA_HPG * t), lambda b, i: (0, 0, 0, 0)),
                  pl.BlockSpec((1, G, gates.shape[2], t), lambda b, i: (b, 0, 0, i)),
                  pl.BlockSpec((1, t, ow), lambda b, i: (b, i, 0))],
        out_specs=pl.BlockSpec((1, t, ow), lambda b, i: (b, i, 0)),
        compiler_params=_cparams(("parallel", "arbitrary")),
        name="nsa_sel_win",
    )(z3, z3, z3, z3, z3, ind, expand, bias_sw, gates, o_cmp)


def _merge_kernel(of_ref, om_ref, on_ref, wb_ref, g0_ref, g1_ref, g2_ref, o_ref):
    acc = None
    for n, (o_r, g_r) in enumerate(((of_ref, g0_ref), (om_ref, g1_ref), (on_ref, g2_ref))):
        y = jnp.dot(o_r[...], wb_ref[n], preferred_element_type=F32)
        y = _sigmoid(g_r[...].astype(F32)) * y
        acc = y if acc is None else acc + y
    o_ref[...] = acc.astype(o_ref.dtype)


def merge_branches(o_fox, o_mla, o_nsa, wb, z, tm, tn):
    T = o_fox.shape[0]
    D = wb.shape[2]
    gspec = lambda n: pl.BlockSpec((tm, tn), lambda i, j: (i, (CB_MG * LANE + n * D) // tn + j))
    ospec = pl.BlockSpec((tm, BRANCH_W), lambda i, j: (i, 0))
    return pl.pallas_call(
        _merge_kernel,
        out_shape=jax.ShapeDtypeStruct((T, D), BF16),
        grid=(T // tm, D // tn),
        in_specs=[ospec, ospec, ospec,
                  pl.BlockSpec((N_BRANCH, BRANCH_W, tn), lambda i, j: (0, 0, j)),
                  gspec(0), gspec(1), gspec(2)],
        out_specs=pl.BlockSpec((tm, tn), lambda i, j: (i, j)),
        compiler_params=_cparams(("parallel", "arbitrary")),
        name="merge_branches",
    )(o_fox, o_mla, o_nsa, wb, z, z, z)


def _mm_res_kernel(a_ref, w_ref, r_ref, o_ref):
    o_ref[...] = r_ref[...] + jnp.dot(a_ref[...], w_ref[...], preferred_element_type=F32)


def matmul_residual(a, w, res, tm, tn):
    T, K = a.shape
    N = w.shape[1]
    return pl.pallas_call(
        _mm_res_kernel,
        out_shape=jax.ShapeDtypeStruct((T, N), F32),
        grid=(T // tm, N // tn),
        in_specs=[pl.BlockSpec((tm, K), lambda i, j: (i, 0)),
                  pl.BlockSpec((K, tn), lambda i, j: (0, j)),
                  pl.BlockSpec((tm, tn), lambda i, j: (i, j))],
        out_specs=pl.BlockSpec((tm, tn), lambda i, j: (i, j)),
        compiler_params=_cparams(("parallel", "arbitrary")),
        name="matmul_residual",
    )(a, w, res)


def _mem_attn_kernel(h_ref, g_ref, wq_ref, kv_ref, wo_ref, o_ref):
    x = h_ref[0]
    ms = jnp.mean(x * x, axis=-1, keepdims=True)
    u = (x * lax.rsqrt(ms + EPS) * g_ref[...]).astype(BF16)
    q = jnp.dot(u, wq_ref[...], preferred_element_type=F32).astype(BF16)
    dn = (((1,), (1,)), ((), ()))
    hw = MEM_HEADS * MEM_DH
    heads = range(MEM_HEADS)
    cs = [slice(h * MEM_DH, (h + 1) * MEM_DH) for h in heads]
    ss = [lax.dot_general(q[:, cs[h]], kv_ref[0, :, cs[h]], dn, preferred_element_type=F32) for h in heads]
    es = [jnp.exp2(ss[h] - jnp.max(ss[h], axis=-1, keepdims=True)) for h in heads]
    ps = [es[h] / jnp.sum(es[h], axis=-1, keepdims=True) for h in heads]
    outs = [jnp.dot(ps[h].astype(BF16), kv_ref[0, :, hw + h * MEM_DH: hw + (h + 1) * MEM_DH],
                    preferred_element_type=F32).astype(BF16) for h in heads]
    o = jnp.concatenate(outs, axis=1)
    o_ref[0] = x + jnp.dot(o, wo_ref[...], preferred_element_type=F32)


def memory_attention_block(h3, g, wq, kv, wo, tq):
    B, S, D = h3.shape
    M = kv.shape[1]
    hw = MEM_HEADS * MEM_DH
    return pl.pallas_call(
        _mem_attn_kernel,
        out_shape=jax.ShapeDtypeStruct((B, S, D), F32),
        grid=(B, S // tq),
        in_specs=[pl.BlockSpec((1, tq, D), lambda b, i: (b, i, 0)),
                  pl.BlockSpec((1, D), lambda b, i: (0, 0)),
                  pl.BlockSpec((D, hw), lambda b, i: (0, 0)),
                  pl.BlockSpec((1, M, 2 * hw), lambda b, i: (b, 0, 0)),
                  pl.BlockSpec((hw, D), lambda b, i: (0, 0))],
        out_specs=pl.BlockSpec((1, tq, D), lambda b, i: (b, i, 0)),
        compiler_params=_cparams(("parallel", "arbitrary")),
        name="memory_attention",
    )(h3, g.reshape(1, D).astype(F32), wq, kv, wo)


def _dot3(a, b):
    a_hi = a.astype(BF16)
    a_lo = (a - a_hi.astype(F32)).astype(BF16)
    b_hi = b.astype(BF16)
    b_lo = (b - b_hi.astype(F32)).astype(BF16)
    dot = functools.partial(jnp.dot, preferred_element_type=F32)
    return dot(a_hi, b_hi) + (dot(a_lo, b_hi) + dot(a_hi, b_lo))


def _router_kernel(h_ref, g_ref, w_ref, b_ref, lg_ref, u_ref):
    x = h_ref[...]
    ms = jnp.mean(x * x, axis=-1, keepdims=True)
    u = x * lax.rsqrt(ms + EPS) * g_ref[...]
    u_ref[...] = u.astype(u_ref.dtype)
    lg_ref[...] = _dot3(u, w_ref[...]) + b_ref[...]


def moe_router(h, g, w, b, tm):
    T, D = h.shape
    N = w.shape[1]
    return pl.pallas_call(
        _router_kernel,
        out_shape=(jax.ShapeDtypeStruct((T, N), F32), jax.ShapeDtypeStruct((T, D), BF16)),
        grid=(T // tm,),
        in_specs=[pl.BlockSpec((tm, D), lambda i: (i, 0)),
                  pl.BlockSpec((1, D), lambda i: (0, 0)),
                  pl.BlockSpec((D, N), lambda i: (0, 0)),
                  pl.BlockSpec((1, N), lambda i: (0, 0))],
        out_specs=(pl.BlockSpec((tm, N), lambda i: (i, 0)), pl.BlockSpec((tm, D), lambda i: (i, 0))),
        compiler_params=_cparams(("parallel",)),
        name="moe_router",
    )(h, g.reshape(1, D).astype(F32), w, b)


def _route_kernel(lg_ref, tri_ref, o_ref, cnt_ref, carry_ref):
    @pl.when(pl.program_id(0) == 0)
    def _():
        carry_ref[...] = jnp.zeros(carry_ref.shape, carry_ref.dtype)

    lg = lg_ref[...]
    lane = lax.broadcasted_iota(jnp.int32, lg.shape, 1)
    first = lambda hit: jnp.min(jnp.where(hit, lane, LANE), axis=1, keepdims=True)
    gmask = lane < N_GROUPS
    gl = jnp.where(gmask, lg, -jnp.inf)
    gmax = jnp.max(gl, axis=1, keepdims=True)
    gsel = first(gl == gmax)
    pg = 1.0 / jnp.sum(jnp.where(gmask, jnp.exp(lg - gmax), 0.0), axis=1, keepdims=True)
    lo = N_GROUPS + gsel * EXPERTS_PER_GROUP
    emask = (lane >= lo) & (lane < lo + EXPERTS_PER_GROUP)
    el = jnp.where(emask, lg, -jnp.inf)
    emax = jnp.max(el, axis=1, keepdims=True)
    l0 = first(el == emax)
    esum = jnp.sum(jnp.where(emask, jnp.exp(lg - emax), 0.0), axis=1, keepdims=True)
    el2 = jnp.where(lane == l0, -jnp.inf, el)
    e2max = jnp.max(el2, axis=1, keepdims=True)
    l1 = first(el2 == e2max)
    p0 = 1.0 / esum
    p1 = jnp.exp(e2max - emax) / esum
    w0 = pg * (p0 / (p0 + p1))
    w1 = pg * (p1 / (p0 + p1))
    onehot = jnp.where((lane == l0) | (lane == l1), 1.0, 0.0)
    before = jnp.dot(tri_ref[...], onehot.astype(BF16), preferred_element_type=F32) + carry_ref[...]
    r0 = jnp.sum(jnp.where(lane == l0, before, 0.0), axis=1, keepdims=True)
    r1 = jnp.sum(jnp.where(lane == l1, before, 0.0), axis=1, keepdims=True)
    carry_ref[...] += jnp.sum(onehot, axis=0, keepdims=True)
    cnt_ref[...] = jnp.broadcast_to(carry_ref[...], cnt_ref.shape)
    vals = (l0 - N_GROUPS, l1 - N_GROUPS, r0, r1, w0, w1)
    out = jnp.zeros(lg.shape, F32)
    for k, v in enumerate(vals):
        out = jnp.where(lane == k, v.astype(F32), out)
    o_ref[...] = out


def moe_route(logits, tm):
    T, N = logits.shape
    tri = (jnp.arange(tm)[:, None] > jnp.arange(tm)[None, :]).astype(BF16)
    return pl.pallas_call(
        _route_kernel,
        out_shape=(jax.ShapeDtypeStruct((T, N), F32), jax.ShapeDtypeStruct((8, N), F32)),
        grid=(T // tm,),
        in_specs=[pl.BlockSpec((tm, N), lambda i: (i, 0)), pl.BlockSpec((tm, tm), lambda i: (0, 0))],
        out_specs=(pl.BlockSpec((tm, N), lambda i: (i, 0)), pl.BlockSpec((8, N), lambda i: (0, 0))),
        scratch_shapes=[pltpu.VMEM((1, N), F32)],
        compiler_params=_cparams(("arbitrary",)),
        name="moe_route",
    )(logits, tri)


def _moe_kernel(be_ref, nv_ref, nu_ref, x_ref, wg_ref, wu_ref, wd_ref, o_ref, acc_ref):
    i = pl.program_id(0)
    j = pl.program_id(1)
    nv = nv_ref[i]
    tm = x_ref.shape[0]

    @pl.when((i == 0) & (j == 0))
    def _():
        acc_ref[...] = jnp.zeros(acc_ref.shape, acc_ref.dtype)

    def run(rows):
        x = x_ref[:rows, :]
        a = jnp.dot(x, wg_ref[0, 0].astype(BF16), preferred_element_type=F32)
        b = jnp.dot(x, wu_ref[0, 0].astype(BF16), preferred_element_type=F32)
        hdn = (a * _sigmoid(a) * b).astype(BF16)
        y = jnp.dot(hdn, wd_ref[0, 0].astype(BF16), preferred_element_type=F32)
        tot = y + jnp.where(j > 0, acc_ref[:rows, :], 0.0)
        acc_ref[:rows, :] = tot
        o_ref[:rows, :] = tot.astype(o_ref.dtype)
        if rows < tm:
            o_ref[rows:, :] = jnp.zeros((tm - rows, o_ref.shape[1]), o_ref.dtype)

    for rows in range(MOE_SUB, tm + 1, MOE_SUB):
        pl.when((nv > rows - MOE_SUB) & (nv <= rows))(functools.partial(run, rows))

    @pl.when(nv == 0)
    def _():
        o_ref[...] = jnp.zeros(o_ref.shape, o_ref.dtype)


def moe_experts(blk_e, n_valid, n_used, xr, wg, wu, wd, layer):
    P, D = xr.shape
    De = wg.shape[3]
    tm, dc = MOE_TM, MOE_DC
    nj = De // dc
    chunk = lambda i, s: jnp.where(i % 2 == 0, s, nj - 1 - s)
    jj = lambda i, s, nu: jnp.where(i < nu[0], chunk(i, s), chunk(nu[0] - 1, nj - 1))
    grid_spec = pltpu.PrefetchScalarGridSpec(
        num_scalar_prefetch=3,
        grid=(P // tm, nj),
        in_specs=[pl.BlockSpec((tm, D), lambda i, j, be, nv, nu: (i, 0)),
                  pl.BlockSpec((1, 1, D, dc), lambda i, j, be, nv, nu: (layer, be[i], 0, jj(i, j, nu))),
                  pl.BlockSpec((1, 1, D, dc), lambda i, j, be, nv, nu: (layer, be[i], 0, jj(i, j, nu))),
                  pl.BlockSpec((1, 1, dc, D), lambda i, j, be, nv, nu: (layer, be[i], jj(i, j, nu), 0))],
        out_specs=pl.BlockSpec((tm, D), lambda i, j, be, nv, nu: (i, 0)),
        scratch_shapes=[pltpu.VMEM((tm, D), F32)],
    )
    return pl.pallas_call(
        _moe_kernel,
        out_shape=jax.ShapeDtypeStruct((P, D), BF16),
        grid_spec=grid_spec,
        compiler_params=_cparams(("arbitrary", "arbitrary")),
        name="moe_experts",
    )(blk_e, n_valid, n_used, xr, wg, wu, wd)


def _combine_kernel(h_ref, y0_ref, y1_ref, w_ref, g_ref, o_ref, *, final_norm):
    w = w_ref[...]
    x = h_ref[...] + (w[:, 0:1] * y0_ref[...].astype(F32) + w[:, 1:2] * y1_ref[...].astype(F32))
    if final_norm:
        ms = jnp.mean(x * x, axis=-1, keepdims=True)
        x = x * lax.rsqrt(ms + EPS) * g_ref[...]
    o_ref[...] = x


def moe_combine(h, y0, y1, w, g, tm):
    T, D = h.shape
    final_norm = g is not None
    gain = (g if final_norm else jnp.ones((D,), F32)).reshape(1, D).astype(F32)
    row = lambda width: pl.BlockSpec((tm, width), lambda i: (i, 0))
    return pl.pallas_call(
        functools.partial(_combine_kernel, final_norm=final_norm),
        out_shape=jax.ShapeDtypeStruct((T, D), F32),
        grid=(T // tm,),
        in_specs=[row(D), row(D), row(D), row(LANE), pl.BlockSpec((1, D), lambda i: (0, 0))],
        out_specs=row(D),
        compiler_params=_cparams(("parallel",)),
        name="moe_combine",
    )(h, y0, y1, _pad_cols(w, LANE), gain)


def _pad_cols(w, width):
    return jnp.pad(w, ((0, 0), (0, width - w.shape[1])))


def _w_in_segments(D):
    names = ("fq", "fk", "fv", "ff", "mcq", "mckv", "mkr", "nq", "nkc", "nvc", "nks", "nvs", "nkw", "nvw", "ngt", "mg")
    widths = (1024, 1024, 1024, FORGET_COLS, MLA_Q_RANK, MLA_KV_RANK, MLA_ROPE, NSA_HEADS * NSA_DK,
              NSA_GROUPS * NSA_DK, NSA_GROUPS * NSA_DV, NSA_GROUPS * NSA_DK, NSA_GROUPS * NSA_DV,
              NSA_GROUPS * NSA_DK, NSA_GROUPS * NSA_DV, NSA_GATE_COLS, N_BRANCH * D)
    src = dict(zip(names, np.cumsum((0,) + widths[:-1]).tolist()))
    wid = dict(zip(names, widths))
    segs = []
    plain = lambda name, cb, f=1.0: segs.append((cb * LANE, src[name], wid[name], f))

    def padded_k(name, cb, n, f=1.0):
        for i in range(n):
            segs.append((cb * LANE + i * NSA_DKP, src[name] + i * NSA_DK, NSA_DK, f))

    half = MLA_ROPE // 2
    plain("mckv", CB_CKV)
    plain("mkr", CB_KR)
    segs.append((CB_KR * LANE + MLA_ROPE, src["mkr"] + half, half, -1.0))
    segs.append((CB_KR * LANE + MLA_ROPE + half, src["mkr"], half, 1.0))
    plain("mcq", CB_CQ)
    padded_k("nkc", CB_NKC, NSA_GROUPS)
    padded_k("nq", CB_NQ, NSA_HEADS, NSA_DK ** -0.5 * LOG2E)
    plain("fq", CB_FQ, FOX_DH ** -0.5 * LOG2E)
    plain("fk", CB_FK)
    plain("fv", CB_FV)
    padded_k("nks", CB_NKS, NSA_GROUPS)
    padded_k("nkw", CB_NKW, NSA_GROUPS)
    plain("nvc", CB_NVC)
    plain("nvs", CB_NVS)
    plain("nvw", CB_NVW)
    plain("mg", CB_MG)
    return segs, src


def _pack_w_in_kernel(w_ref, o_ref, os_ref, *, segs, small_segs):
    d_in = w_ref.shape[1]

    def put(dst_ref, dst, s, n, f):
        for k in range(0, n, LANE):
            m = min(LANE, n - k)
            assert s + k + LANE <= d_in
            v = w_ref[0, s + k:s + k + LANE, :].T
            if f != 1.0:
                v = v * f
            dst_ref[:, dst + k:dst + k + m] = v[:, :m].astype(dst_ref.dtype)

    o_ref[...] = jnp.zeros(o_ref.shape, o_ref.dtype)
    for dst, s, n, f in segs:
        put(o_ref, dst, s, n, f)
    os_ref[...] = jnp.zeros(os_ref.shape, os_ref.dtype)
    for dst, s, n in small_segs:
        put(os_ref, dst, s, n, 1.0)


def _pack_w_in(w_all, layer):
    _, D, d_in = w_all.shape
    segs, src = _w_in_segments(D)
    small_segs = ((0, src["ff"], FORGET_COLS), (FORGET_COLS, src["ngt"], NSA_GATE_COLS))
    tr = LANE
    return pl.pallas_call(
        functools.partial(_pack_w_in_kernel, segs=tuple(segs), small_segs=small_segs),
        out_shape=(jax.ShapeDtypeStruct((D, Z_BLOCKS * LANE), BF16), jax.ShapeDtypeStruct((D, LANE), F32)),
        grid=(D // tr,),
        in_specs=[pl.BlockSpec((1, d_in, tr), lambda i: (layer, 0, i))],
        out_specs=(pl.BlockSpec((tr, Z_BLOCKS * LANE), lambda i: (i, 0)), pl.BlockSpec((tr, LANE), lambda i: (i, 0))),
        compiler_params=_cparams(("parallel",)),
        name="pack_w_in",
    )(jnp.swapaxes(w_all, 1, 2))


def _pack_w_uq(w):
    K = w.shape[0]
    w3 = w.reshape(K, MLA_HEADS, MLA_NOPE + MLA_ROPE) * ((MLA_NOPE + MLA_ROPE) ** -0.5 * LOG2E)
    nope = w3[:, :, :MLA_NOPE].reshape(K, MLA_HEADS * MLA_NOPE)
    r = w3[:, :, MLA_NOPE:]
    half = MLA_ROPE // 2
    r_rot = jnp.concatenate([-r[:, :, half:], r[:, :, :half]], axis=2)
    padr = lambda a: jnp.pad(a, ((0, 0), (0, 0), (0, LANE - MLA_ROPE))).reshape(K, MLA_HEADS * LANE)
    return jnp.concatenate([nope, padr(r), padr(r_rot)], axis=1).astype(BF16)


def _pack_w_ukv(w):
    K = w.shape[0]
    w3 = w.reshape(K, MLA_HEADS, MLA_NOPE + MLA_DV)
    return jnp.concatenate([w3[:, :, :MLA_NOPE].reshape(K, -1), w3[:, :, MLA_NOPE:].reshape(K, -1)], axis=1).astype(BF16)


def _t5_bucket(dist):
    dist = jnp.maximum(dist, 0)
    exact = REL_BUCKETS // 2
    df = jnp.maximum(dist, 1).astype(F32)
    large = exact + (jnp.log(df / exact) / math.log(REL_MAX_DIST / exact) * (REL_BUCKETS - exact)).astype(jnp.int32)
    large = jnp.minimum(large, REL_BUCKETS - 1)
    return jnp.where(dist < exact, dist, large)


def _position_tables(S, rel_bias):
    t = NSA_T
    half = MLA_ROPE // 2
    inv = ROPE_THETA ** (-jnp.arange(half, dtype=F32) / half)
    ang = jnp.arange(S, dtype=F32)[:, None] * inv
    c, s = jnp.cos(ang), jnp.sin(ang)
    cos = _pad_cols(jnp.concatenate([c, c], axis=1), LANE)
    sin = _pad_cols(jnp.concatenate([s, s], axis=1), LANE)
    ncp = max(S // CMP_STRIDE, LANE)
    pos = jnp.arange(S)

    def bias_of(dist):
        onehot = jax.nn.one_hot(_t5_bucket(dist), REL_BUCKETS, dtype=F32)
        return jnp.einsum("...b,bh->h...", onehot, rel_bias, precision=lax.Precision.HIGHEST)

    bias_c = bias_of(pos[None, :] - (CMP_STRIDE * jnp.arange(ncp)[:, None] + CMP_BLOCK - 1))
    i = jnp.arange(t)
    bias_sw = jnp.stack([bias_of(k * t + i[:, None] - i[None, :]) for k in range(3)], axis=1)
    bias_sw = bias_sw.reshape(NSA_GROUPS, NSA_HPG, 3, t, t).transpose(0, 2, 4, 1, 3).reshape(NSA_GROUPS, 3, t, NSA_HPG * t)
    n_cmp = (S - CMP_BLOCK) // CMP_STRIDE + 1
    n_blk = S // SEL_BLOCK
    nbp = max(n_blk, LANE)
    cstart = CMP_STRIDE * jnp.arange(ncp)
    sstart = SEL_BLOCK * jnp.arange(nbp)
    ov = jnp.clip(jnp.minimum(cstart[:, None] + CMP_BLOCK, sstart[None, :] + SEL_BLOCK)
                  - jnp.maximum(cstart[:, None], sstart[None, :]), 0, None).astype(F32) / CMP_STRIDE
    ov = jnp.where((jnp.arange(ncp)[:, None] < n_cmp) & (jnp.arange(nbp)[None, :] < n_blk), ov, 0.0).astype(BF16)
    expand = ((pos[:, None] // SEL_BLOCK) == jnp.arange(nbp)[None, :]).astype(BF16)
    return cos, sin, bias_c, bias_sw, ov, expand


def _token_mixers(h, p, w_in_all, layer, tabs, B, S):
    T, D = h.shape
    cos, sin, bias_c, bias_sw, overlap, expand = tabs
    wz, w_small = _pack_w_in(w_in_all, layer)
    z, zs = input_projection(h, p["g_mix"], wz, w_small, tm=min(T, 1024), tn=IN_PROJ_TN)
    z3 = z.reshape(B, S, Z_BLOCKS * LANE)

    log_f = jax.nn.log_sigmoid(zs[:, :FORGET_COLS] + p["b_forget"].astype(F32)).reshape(B, S, FOX_HEADS)
    cum = jnp.cumsum(log_f, axis=1) * LOG2E
    o_fox = causal_attention(z3, CB_FQ, z3, CB_FK, z3, CB_FV, FOX_HEADS, cum=cum)

    q_nope, q_rope = mla_q_proj(z, p["g_cq"], _pack_w_uq(p["w_uq"]), cos, sin, S, tm=min(S, 512))
    kv = rms_matmul(z, CB_CKV * LANE // MLA_KV_RANK, p["g_ckv"], _pack_w_ukv(p["w_ukv"]), BF16, tm=min(T, 1024), tn=1024)
    kr = z3[:, :, CB_KR * LANE:(CB_KR + 1) * LANE].astype(F32)
    kr = kr[..., :MLA_ROPE] * cos[None, :, :MLA_ROPE] + kr[..., MLA_ROPE:] * sin[None, :, :MLA_ROPE]
    k_rope = jnp.pad(kr, ((0, 0), (0, 0), (0, LANE - MLA_ROPE))).astype(BF16)
    hw = MLA_HEADS * LANE
    kv3 = kv.reshape(B, S, 2 * hw)
    o_mla = causal_attention(q_nope.reshape(B, S, hw), 0, kv3, 0, kv3, MLA_HEADS, MLA_HEADS,
                             q2=q_rope.reshape(B, S, hw), k2=k_rope)

    G = NSA_GROUPS
    NC = S // CMP_STRIDE
    ncp = bias_c.shape[1]

    def compress_branch(cb, dp, d, pe, w1, w2):
        x = z3[:, :, cb * LANE: cb * LANE + G * dp].reshape(B, NC, CMP_STRIDE * G * dp)
        eye = jnp.eye(G, dtype=F32)
        w1p = jnp.pad(w1.reshape(CMP_BLOCK, d, d), ((0, 0), (0, dp - d), (0, dp - d)))
        w1g = jnp.einsum("lij,gh->lgihj", w1p, eye).reshape(CMP_BLOCK, G * dp, G * dp).astype(BF16)
        w1a = w1g[:CMP_STRIDE].reshape(CMP_STRIDE * G * dp, G * dp)
        w1b = w1g[CMP_STRIDE:].reshape(CMP_STRIDE * G * dp, G * dp)
        pe_g = jnp.tile(jnp.pad(pe, ((0, 0), (0, dp - d)))[:, None, :], (1, G, 1))
        pe2 = pe_g.reshape(2, CMP_STRIDE * G * dp).astype(BF16)
        w2p = jnp.pad(w2, ((0, dp - d), (0, dp - d)))
        w2g = jnp.einsum("ij,gh->gihj", w2p, eye).reshape(G * dp, G * dp).astype(BF16)
        out = compress(x, w1a, w1b, pe2, w2g)
        return jnp.pad(out, ((0, 0), (0, ncp - NC), (0, 0)))

    kc = compress_branch(CB_NKC, NSA_DKP, NSA_DK, p["pe_k"], p["w_cmp_k1"], p["w_cmp_k2"])
    vc = compress_branch(CB_NVC, NSA_DV, NSA_DV, p["pe_v"], p["w_cmp_v1"], p["w_cmp_v2"])
    gl = zs[:, FORGET_COLS:FORGET_COLS + NSA_GATE_COLS].reshape(B, S, G, NSA_HPG * 3)
    gl = jnp.transpose(gl, (0, 2, 1, 3))
    gl_cols = jnp.pad(jnp.swapaxes(gl, 2, 3), ((0, 0), (0, 0), (0, 16 - NSA_HPG * 3), (0, 0)))
    o_cmp, ind = nsa_cmp_select(z3, kc, vc, bias_c, overlap, gl_cols, min(N_SEL, S // SEL_BLOCK))
    o_nsa = nsa_sel_win(z3, ind, expand, bias_sw, gl_cols, o_cmp)

    merged = merge_branches(o_fox.reshape(T, -1), o_mla.reshape(T, -1), o_nsa.reshape(T, -1),
                            p["w_branch"].astype(BF16), z, tm=min(T, 1024), tn=512)
    return matmul_residual(merged, p["w_out"].astype(BF16), h, tm=min(T, 1024), tn=1024)


def _memory_block(h, mem2, p, B, S):
    T, D = h.shape
    kv = rms_matmul(mem2, 0, p["g_mem_kv"], p["w_mem_kv"].astype(BF16), BF16, tm=min(mem2.shape[0], 512), tn=512)
    out = memory_attention_block(h.reshape(B, S, D), p["g_mem_q"], (p["w_mem_q"] * (MEM_DH ** -0.5 * LOG2E)).astype(BF16),
                                 kv.reshape(B, -1, kv.shape[1]), p["w_mem_o"].astype(BF16), tq=min(S, 512))
    return out.reshape(T, D)


def _moe_block(h, p, experts, layer, g_final=None):
    T, D = h.shape
    tm = MOE_TM
    w_r = _pad_cols(jnp.concatenate([p["w_router_group"], p["w_router_expert"]], axis=1), LANE).astype(F32)
    b_r = _pad_cols(jnp.concatenate([p["b_router_group"], p["b_router_expert"]])[None, :], LANE).astype(F32)
    logits, u = moe_router(h, p["g_moe"], w_r, b_r, tm=min(T, 512))
    routed, totals = moe_route(logits, tm=min(T, 512))
    assert TOP_K == 2
    flat_e = routed[:, 0:2].astype(jnp.int32).reshape(-1)
    rank = routed[:, 2:4].astype(jnp.int32).reshape(-1)
    weight = routed[:, 4:6]
    counts = totals[0, N_GROUPS:N_GROUPS + N_EXPERTS].astype(jnp.int32)
    TK = T * TOP_K
    pcounts = ((counts + tm - 1) // tm) * tm
    pends = jnp.cumsum(pcounts)
    dest = (pends - pcounts)[flat_e] + rank
    n_rb = -(-TK // tm) + N_EXPERTS
    P = n_rb * tm
    row_tok = (jnp.arange(P, dtype=jnp.int32) % T).at[dest].set(jnp.repeat(jnp.arange(T, dtype=jnp.int32), TOP_K))
    blk_e = jnp.sum((pends[None, :] <= (jnp.arange(n_rb, dtype=jnp.int32) * tm)[:, None]).astype(jnp.int32), axis=1)
    blk_e = jnp.minimum(blk_e, N_EXPERTS - 1).astype(jnp.int32)
    n_used = (pends[-1] // tm).astype(jnp.int32).reshape(1)
    row_end = pends - pcounts + counts
    n_valid = jnp.clip(row_end[blk_e] - jnp.arange(n_rb, dtype=jnp.int32) * tm, 0, tm).astype(jnp.int32)
    xr = u[row_tok]
    y = moe_experts(blk_e, n_valid, n_used, xr, experts[0].astype(F32), experts[1].astype(F32),
                    experts[2].astype(F32), layer)
    d2 = dest.reshape(T, TOP_K)
    return moe_combine(h, y[d2[:, 0]], y[d2[:, 1]], weight, g_final, tm=min(T, 512))


_LAYER_KEYS = ("g_mix", "w_in", "b_forget", "g_cq", "g_ckv", "w_uq", "w_ukv", "pe_k", "pe_v", "w_cmp_k1", "w_cmp_k2",
               "w_cmp_v1", "w_cmp_v2", "w_branch", "w_out", "g_mem_q", "g_mem_kv", "w_mem_q", "w_mem_kv", "w_mem_o",
               "g_moe", "w_router_group", "b_router_group", "w_router_expert", "b_router_expert")


def kernel(x, mem, g_mix, w_in, b_forget, g_cq, g_ckv, w_uq, w_ukv, pe_k, pe_v, w_cmp_k1, w_cmp_k2, w_cmp_v1, w_cmp_v2, rel_bias, w_branch, w_out, g_mem_q, g_mem_kv, w_mem_q, w_mem_kv, w_mem_o, g_moe, w_router_group, b_router_group, w_router_expert, b_router_expert, w_exp_gate, w_exp_up, w_exp_down, g_final):
    B, S, D = x.shape
    T = B * S
    stacked = dict(g_mix=g_mix, w_in=w_in, b_forget=b_forget, g_cq=g_cq, g_ckv=g_ckv, w_uq=w_uq, w_ukv=w_ukv,
                   pe_k=pe_k, pe_v=pe_v, w_cmp_k1=w_cmp_k1, w_cmp_k2=w_cmp_k2, w_cmp_v1=w_cmp_v1, w_cmp_v2=w_cmp_v2,
                   w_branch=w_branch, w_out=w_out, g_mem_q=g_mem_q, g_mem_kv=g_mem_kv, w_mem_q=w_mem_q,
                   w_mem_kv=w_mem_kv, w_mem_o=w_mem_o, g_moe=g_moe, w_router_group=w_router_group,
                   b_router_group=b_router_group, w_router_expert=w_router_expert, b_router_expert=b_router_expert,
                   w_exp_gate=w_exp_gate, w_exp_up=w_exp_up, w_exp_down=w_exp_down)
    tabs = _position_tables(S, rel_bias.astype(F32) * LOG2E)
    h = x.reshape(T, D).astype(F32)
    mem2 = mem.reshape(-1, D).astype(F32)
    depth = w_in.shape[0]
    assert depth >= 1
    for l in range(depth):
        p = {k: stacked[k][l] for k in _LAYER_KEYS}
        h = _token_mixers(h, p, w_in.astype(F32), l, tabs, B, S)
        h = _memory_block(h, mem2, p, B, S)
        h = _moe_block(h, p, (w_exp_gate, w_exp_up, w_exp_down), l, g_final if l == depth - 1 else None)
    return h.reshape(B, S, D)
```

```python
import functools
import math

import jax
import jax.numpy as jnp
import numpy as np
from jax import lax
from jax.experimental import pallas as pl
from jax.experimental.pallas import tpu as pltpu

F32 = jnp.float32
BF16 = jnp.bfloat16

EPS = 1e-6
NEG_INF = -1e30
LOG2E = math.log2(math.e)
LANE = 128

FOX_HEADS, FOX_DH = 8, 128
MLA_HEADS, MLA_NOPE, MLA_ROPE, MLA_DV = 8, 128, 64, 128
MLA_Q_RANK, MLA_KV_RANK = 768, 512
ROPE_THETA = 10000.0
NSA_HEADS, NSA_GROUPS, NSA_DK, NSA_DV = 8, 2, 192, 128
NSA_HPG = NSA_HEADS // NSA_GROUPS
NSA_DKP = 256
CMP_BLOCK, CMP_STRIDE, SEL_BLOCK, N_SEL, WINDOW = 32, 16, 64, 8, 512
SEL_FORCE = 1e6
REL_BUCKETS, REL_MAX_DIST = 32, 128
N_BRANCH, BRANCH_W = 3, 1024
MEM_HEADS, MEM_DH = 4, 128
N_GROUPS, EXPERTS_PER_GROUP, TOP_K = 4, 8, 2
N_EXPERTS = N_GROUPS * EXPERTS_PER_GROUP
FORGET_COLS, NSA_GATE_COLS = FOX_HEADS, NSA_HEADS * 3

CB_CKV, CB_KR, CB_CQ = 0, 4, 6
CB_NKC, CB_NQ = 12, 16
CB_FQ, CB_FK, CB_FV = 32, 40, 48
CB_NKS, CB_NKW = 56, 60
CB_NVC, CB_NVS, CB_NVW = 64, 66, 68
CB_MG = 72
Z_BLOCKS = 120
IN_PROJ_TN = 8 * LANE

ATT_T = 256
ATT_HG = 8
NSA_T = 256
NSA_CMP_T = 512
MOE_TM = 512
MOE_SUB = 256
MOE_DC = 512
VMEM_LIMIT = 56 * 1024 * 1024


def _cparams(sem):
    return pltpu.CompilerParams(dimension_semantics=sem, vmem_limit_bytes=VMEM_LIMIT)


def _sigmoid(x):
    return 1.0 / (1.0 + jnp.exp(-x))


def _rms_mm_kernel(x_ref, g_ref, w_ref, o_ref, u_ref):
    @pl.when(pl.program_id(1) == 0)
    def _():
        x = x_ref[...].astype(F32)
        ms = jnp.mean(x * x, axis=-1, keepdims=True)
        u_ref[...] = (x * lax.rsqrt(ms + EPS) * g_ref[...]).astype(u_ref.dtype)

    o_ref[...] = jnp.dot(u_ref[...], w_ref[...], preferred_element_type=F32).astype(o_ref.dtype)


def rms_matmul(x, xcol, g, w, out_dtype, tm, tn):
    T = x.shape[0]
    K, N = w.shape
    return pl.pallas_call(
        _rms_mm_kernel,
        out_shape=jax.ShapeDtypeStruct((T, N), out_dtype),
        grid=(T // tm, N // tn),
        in_specs=[pl.BlockSpec((tm, K), lambda i, j: (i, xcol)),
                  pl.BlockSpec((1, K), lambda i, j: (0, 0)),
                  pl.BlockSpec((K, tn), lambda i, j: (0, j))],
        out_specs=pl.BlockSpec((tm, tn), lambda i, j: (i, j)),
        scratch_shapes=[pltpu.VMEM((tm, K), BF16)],
        compiler_params=_cparams(("parallel", "arbitrary")),
        name="rms_matmul",
    )(x, g.reshape(1, K).astype(F32), w)


def _in_proj_kernel(x_ref, g_ref, w_ref, ws_ref, o_ref, os_ref, u_ref):
    @pl.when(pl.program_id(1) == 0)
    def _():
        x = x_ref[...]
        ms = jnp.mean(x * x, axis=-1, keepdims=True)
        u = x * lax.rsqrt(ms + EPS) * g_ref[...]
        u_ref[...] = u.astype(u_ref.dtype)
        os_ref[...] = _dot3(u, ws_ref[...])

    o_ref[...] = jnp.dot(u_ref[...], w_ref[...], preferred_element_type=F32).astype(o_ref.dtype)


def input_projection(x, g, w, w_small, tm, tn):
    T, K = x.shape
    N = w.shape[1]
    Ns = w_small.shape[1]
    return pl.pallas_call(
        _in_proj_kernel,
        out_shape=(jax.ShapeDtypeStruct((T, N), BF16), jax.ShapeDtypeStruct((T, Ns), F32)),
        grid=(T // tm, N // tn),
        in_specs=[pl.BlockSpec((tm, K), lambda i, j: (i, 0)),
                  pl.BlockSpec((1, K), lambda i, j: (0, 0)),
                  pl.BlockSpec((K, tn), lambda i, j: (0, j)),
                  pl.BlockSpec((K, Ns), lambda i, j: (0, 0))],
        out_specs=(pl.BlockSpec((tm, tn), lambda i, j: (i, j)), pl.BlockSpec((tm, Ns), lambda i, j: (i, 0))),
        scratch_shapes=[pltpu.VMEM((tm, K), BF16)],
        compiler_params=_cparams(("parallel", "arbitrary")),
        name="input_projection",
    )(x, g.reshape(1, K).astype(F32), w, w_small)


def _mla_q_kernel(x_ref, g_ref, w_ref, cos_ref, sin_ref, qn_ref, qr_ref):
    x = x_ref[...].astype(F32)
    ms = jnp.mean(x * x, axis=-1, keepdims=True)
    u = (x * lax.rsqrt(ms + EPS) * g_ref[...]).astype(BF16)
    y = jnp.dot(u, w_ref[...], preferred_element_type=F32)
    hw = MLA_HEADS * LANE
    qn_ref[...] = y[:, :hw].astype(qn_ref.dtype)
    cos = cos_ref[...]
    sin = sin_ref[...]
    for h in range(MLA_HEADS):
        a = y[:, hw + h * LANE: hw + (h + 1) * LANE]
        b = y[:, 2 * hw + h * LANE: 2 * hw + (h + 1) * LANE]
        qr_ref[:, h * LANE:(h + 1) * LANE] = (a * cos + b * sin).astype(qr_ref.dtype)


def mla_q_proj(z, g, w, cos, sin, S, tm):
    T = z.shape[0]
    K, N = w.shape
    hw = MLA_HEADS * LANE
    nsb = S // tm
    return pl.pallas_call(
        _mla_q_kernel,
        out_shape=(jax.ShapeDtypeStruct((T, hw), BF16), jax.ShapeDtypeStruct((T, hw), BF16)),
        grid=(T // tm,),
        in_specs=[pl.BlockSpec((tm, K), lambda i: (i, CB_CQ * LANE // MLA_Q_RANK)),
                  pl.BlockSpec((1, K), lambda i: (0, 0)),
                  pl.BlockSpec((K, N), lambda i: (0, 0)),
                  pl.BlockSpec((tm, LANE), lambda i: (i % nsb, 0)),
                  pl.BlockSpec((tm, LANE), lambda i: (i % nsb, 0))],
        out_specs=(pl.BlockSpec((tm, hw), lambda i: (i, 0)), pl.BlockSpec((tm, hw), lambda i: (i, 0))),
        compiler_params=_cparams(("parallel",)),
        name="mla_q_proj",
    )(z, g.reshape(1, K).astype(F32), w, cos, sin)


def _causal_attn_kernel(*refs, t, hg, two_part, decay):
    refs = list(refs)
    q_ref, k_ref, v_ref = refs[:3]
    pos = 3
    if two_part:
        q2_ref, k2_ref = refs[pos:pos + 2]
        pos += 2
    if decay:
        ck_ref = refs[pos]
        pos += 1
    o_ref = refs[pos]
    if two_part:
        kcat_ref = refs[pos + 1]
    qi = pl.program_id(2)
    dn = (((1,), (1,)), ((), ()))

    if two_part:
        @pl.when(qi == 0)
        def _():
            for j in range(hg):
                kcat_ref[j, :, :LANE] = k_ref[0, :, j * LANE:(j + 1) * LANE]
                kcat_ref[j, :, LANE:] = k2_ref[0]

    qs = []
    for j in range(hg):
        qj = q_ref[0, :, j * LANE:(j + 1) * LANE]
        if two_part:
            qj = jnp.concatenate([qj, q2_ref[0, :, j * LANE:(j + 1) * LANE]], axis=1)
        qs.append(qj)

    def step(kb, carry, masked, width=1):
        off = pl.multiple_of(kb * t, t)
        tk = width * t
        heads = range(hg)
        ss = []
        for j in heads:
            k = kcat_ref[j, pl.ds(off, tk), :] if two_part else k_ref[0, pl.ds(off, tk), j * LANE:(j + 1) * LANE]
            ss.append(lax.dot_general(k, qs[j], dn, preferred_element_type=F32))
        if decay:
            ss = [ss[j] - ck_ref[0, 0, pl.ds(off, tk), j:j + 1] for j in heads]
        if masked:
            r = lax.broadcasted_iota(jnp.int32, (tk, t), 0)
            c = lax.broadcasted_iota(jnp.int32, (tk, t), 1)
            ss = [jnp.where(r <= c + (width - 1) * t, s, NEG_INF) for s in ss]
        ms = [jnp.maximum(carry[j][0], jnp.max(ss[j], axis=0, keepdims=True)) for j in heads]
        ps = [jnp.exp2(ss[j] - ms[j]) for j in heads]
        out = []
        for j in heads:
            m, l, acc = carry[j]
            a = jnp.exp2(m - ms[j])
            l = a * l + jnp.sum(ps[j], axis=0, keepdims=True)
            v = v_ref[0, pl.ds(off, tk), j * LANE:(j + 1) * LANE]
            acc = a * acc + lax.dot_general(v, ps[j].astype(BF16), (((0,), (0,)), ((), ())),
                                            preferred_element_type=F32)
            out.append((ms[j], l, acc))
        return tuple(out)

    init = tuple((jnp.full((1, t), NEG_INF, F32), jnp.zeros((1, t), F32), jnp.zeros((LANE, t), F32))
                 for _ in range(hg))
    carry = lax.fori_loop(0, qi // 2, lambda kp, c: step(2 * kp, c, False, width=2), init)
    carry = lax.cond(qi % 2 == 1, lambda c: step(qi - 1, c, True, width=2), lambda c: step(qi, c, True), carry)
    for j in range(hg):
        _, l, acc = carry[j]
        o_ref[0, :, j * LANE:(j + 1) * LANE] = (acc / l).T.astype(o_ref.dtype)


def causal_attention(q, qcb, k, kcb, v, vcb, heads, q2=None, k2=None, cum=None):
    B, S, _ = q.shape
    t, hg = ATT_T, ATT_HG
    w = hg * LANE
    two_part, decay = q2 is not None, cum is not None
    in_specs = [pl.BlockSpec((1, t, w), lambda b, h, i: (b, i, qcb // hg + h)),
                pl.BlockSpec((1, S, w), lambda b, h, i: (b, 0, kcb // hg + h)),
                pl.BlockSpec((1, S, w), lambda b, h, i: (b, 0, vcb // hg + h))]
    args = [q, k, v]
    scratch = []
    if two_part:
        in_specs += [pl.BlockSpec((1, t, w), lambda b, h, i: (b, i, h)),
                     pl.BlockSpec((1, S, LANE), lambda b, h, i: (b, 0, 0))]
        args += [q2, k2]
        scratch = [pltpu.VMEM((hg, S, 2 * LANE), BF16)]
    if decay:
        in_specs += [pl.BlockSpec((1, 1, S, hg), lambda b, h, i: (b, h, 0, 0))]
        args += [jnp.transpose(cum.reshape(B, S, heads // hg, hg), (0, 2, 1, 3))]
    return pl.pallas_call(
        functools.partial(_causal_attn_kernel, t=t, hg=hg, two_part=two_part, decay=decay),
        out_shape=jax.ShapeDtypeStruct((B, S, heads * LANE), BF16),
        grid=(B, heads // hg, S // t),
        in_specs=in_specs,
        out_specs=pl.BlockSpec((1, t, w), lambda b, h, i: (b, i, h)),
        scratch_shapes=scratch,
        compiler_params=_cparams(("parallel", "parallel", "arbitrary")),
        name="causal_attention",
    )(*args)


def _gelu_tanh(x):
    return 0.5 * x * (1.0 + jnp.tanh(math.sqrt(2.0 / math.pi) * (x + 0.044715 * (x * x * x))))


def _compress_kernel(x_ref, w1a_ref, w1b_ref, pe_ref, w2_ref, o_ref):
    x = x_ref[0]
    a = jnp.dot(x, w1a_ref[...], preferred_element_type=F32)
    b = jnp.dot(x, w1b_ref[...], preferred_element_type=F32)
    nc = a.shape[0]
    b_next = pltpu.roll(b, nc - 1, 0)
    pe_term = jnp.dot(pe_ref[0:1, :], w1a_ref[...], preferred_element_type=F32) + \
        jnp.dot(pe_ref[1:2, :], w1b_ref[...], preferred_element_type=F32)
    hid = _gelu_tanh(a + b_next + pe_term)
    o_ref[0] = jnp.dot(hid.astype(BF16), w2_ref[...], preferred_element_type=F32).astype(o_ref.dtype)


def compress(x, w1a, w1b, pe2, w2):
    BG, NC, KD = x.shape
    dp = w2.shape[1]
    return pl.pallas_call(
        _compress_kernel,
        out_shape=jax.ShapeDtypeStruct((BG, NC, dp), BF16),
        grid=(BG,),
        in_specs=[pl.BlockSpec((1, NC, KD), lambda i: (i, 0, 0)),
                  pl.BlockSpec((KD, dp), lambda i: (0, 0)),
                  pl.BlockSpec((KD, dp), lambda i: (0, 0)),
                  pl.BlockSpec((2, KD), lambda i: (0, 0)),
                  pl.BlockSpec((dp, dp), lambda i: (0, 0))],
        out_specs=pl.BlockSpec((1, NC, dp), lambda i: (i, 0, 0)),
        compiler_params=_cparams(("parallel",)),
        name="nsa_compress",
    )(x, w1a, w1b, pe2, w2)


def _nsa_cmp_kernel(q_ref, kc_ref, vc_ref, bias_ref, ov_ref, gate_ref, o_ref, ind_ref, *, t, n_sel, n_rows):
    qi = pl.program_id(2)
    ncp = kc_ref.shape[1]
    kc = kc_ref[0]
    vc = vc_ref[0]
    pos = lax.broadcasted_iota(jnp.int32, (ncp, t), 1) + qi * t
    cblk = lax.broadcasted_iota(jnp.int32, (ncp, t), 0)
    valid = pos >= CMP_STRIDE * cblk + (CMP_BLOCK - 1)
    dn = (((1,), (1,)), ((), ()))
    tn = (((0,), (0,)), ((), ()))
    heads = range(NSA_HPG)
    ss = [lax.dot_general(kc, q_ref[0, :, h * NSA_DKP:(h + 1) * NSA_DKP], dn, preferred_element_type=F32)
          for h in heads]
    ss = [jnp.where(valid, ss[h] + bias_ref[h], NEG_INF) for h in heads]
    es = [jnp.exp2(ss[h] - jnp.max(ss[h], axis=0, keepdims=True)) for h in heads]
    ps = [jnp.where(valid, es[h] / jnp.sum(es[h], axis=0, keepdims=True), 0.0) for h in heads]
    gates = _sigmoid(gate_ref[0, 0])
    for h in heads:
        o = lax.dot_general(vc, ps[h].astype(BF16), tn, preferred_element_type=F32)
        o_ref[0, :, h * NSA_DV:(h + 1) * NSA_DV] = (gates[3 * h:3 * h + 1, :] * o).T.astype(o_ref.dtype)
    psum = functools.reduce(lambda x, y: x + y, ps)
    p_hi = psum.astype(BF16)
    p_lo = (psum - p_hi.astype(F32)).astype(BF16)
    imp = lax.dot_general(ov_ref[...], p_hi, tn, preferred_element_type=F32) + \
        lax.dot_general(ov_ref[...], p_lo, tn, preferred_element_type=F32)
    nbp = imp.shape[0]
    imp = imp[:n_rows]
    blk = lax.broadcasted_iota(jnp.int32, (n_rows, t), 0)
    cur = (lax.broadcasted_iota(jnp.int32, (n_rows, t), 1) + qi * t) // SEL_BLOCK
    forced = (blk == 0) | (blk == cur) | (blk == cur - 1)
    score = jnp.where(blk <= cur, imp + jnp.where(forced, SEL_FORCE, 0.0), NEG_INF)
    sel = jnp.zeros((n_rows, t), F32)
    for _ in range(n_sel):
        mx = jnp.max(score, axis=0, keepdims=True)
        first = jnp.min(jnp.where(score == mx, blk, nbp), axis=0, keepdims=True)
        hit = blk == first
        sel = jnp.where(hit, 1.0, sel)
        score = jnp.where(hit, -jnp.inf, score)
    ind_ref[0, 0] = jnp.zeros(ind_ref.shape[2:], ind_ref.dtype)
    ind_ref[0, 0, :n_rows, :] = sel.astype(ind_ref.dtype)


def nsa_cmp_select(z3, kc, vc, bias_c, overlap, gates, n_sel):
    B, S, _ = z3.shape
    G = NSA_GROUPS
    t = min(NSA_CMP_T, S)
    ncp = kc.shape[1]
    nbp = overlap.shape[1]
    qw = NSA_HPG * NSA_DKP
    ow = NSA_HPG * NSA_DV
    n_rows = min(nbp, -(-(S // SEL_BLOCK) // 16) * 16)
    return pl.pallas_call(
        functools.partial(_nsa_cmp_kernel, t=t, n_sel=n_sel, n_rows=n_rows),
        out_shape=(jax.ShapeDtypeStruct((B, S, G * ow), BF16), jax.ShapeDtypeStruct((B, G, nbp, S), BF16)),
        grid=(B, G, S // t),
        in_specs=[pl.BlockSpec((1, t, qw), lambda b, g, i: (b, i, CB_NQ * LANE // qw + g)),
                  pl.BlockSpec((1, ncp, NSA_DKP), lambda b, g, i: (b, 0, g)),
                  pl.BlockSpec((1, ncp, NSA_DV), lambda b, g, i: (b, 0, g)),
                  pl.BlockSpec((NSA_HPG, ncp, t), lambda b, g, i: (g, 0, i)),
                  pl.BlockSpec((ncp, nbp), lambda b, g, i: (0, 0)),
                  pl.BlockSpec((1, 1, gates.shape[2], t), lambda b, g, i: (b, g, 0, i))],
        out_specs=(pl.BlockSpec((1, t, ow), lambda b, g, i: (b, i, g)),
                   pl.BlockSpec((1, 1, nbp, t), lambda b, g, i: (b, g, 0, i))),
        compiler_params=_cparams(("parallel", "parallel", "arbitrary")),
        name="nsa_cmp_select",
    )(z3, kc, vc, bias_c, overlap, gates)


def _nsa_sw_kernel(q_ref, ks_ref, vs_ref, kw_ref, vw_ref, ind_ref, e_ref, bias_ref, gate_ref, oc_ref, o_ref,
                   *, t):
    qi = pl.program_id(1)
    hp, G = NSA_HPG, NSA_GROUPS
    q4 = [jnp.concatenate([q_ref[0, :, (g * hp + h) * NSA_DKP:(g * hp + h + 1) * NSA_DKP] for h in range(hp)], axis=0)
          for g in range(G)]
    inds = [ind_ref[0, g] for g in range(G)]
    dn = (((1,), (1,)), ((), ()))

    def step(kb, carry, k_ref, v_ref, selected, width=1):
        off = pl.multiple_of(kb * t, t)
        tk = width * t
        ri = lax.broadcasted_iota(jnp.int32, (tk, t), 0)
        ci = lax.broadcasted_iota(jnp.int32, (tk, t), 1)
        d = (qi - kb) * t + ci - ri
        near = (d >= 0) if selected else (d >= 0) & (d < WINDOW)
        groups = range(G)
        ss = [lax.dot_general(k_ref[0, pl.ds(off, tk), g * NSA_DKP:(g + 1) * NSA_DKP], q4[g], dn,
                              preferred_element_type=F32) for g in groups]
        negs = []
        for g in groups:
            mask = near
            if selected:
                hit = jnp.dot(e_ref[pl.ds(off, tk), :], inds[g], preferred_element_type=F32)
                mask = near & (hit > 0.5)
            neg = jnp.where(mask, 0.0, NEG_INF)
            negs.append(jnp.concatenate([neg] * hp, axis=1))
        bias = [jnp.concatenate([bias_ref[g, jnp.minimum(qi - kb - w, 2)] for w in range(width)], axis=0)
                if width > 1 else bias_ref[g, jnp.minimum(qi - kb, 2)] for g in groups]
        ss = [ss[g] + bias[g] + negs[g] for g in groups]
        ms = [jnp.maximum(carry[g][0], jnp.max(ss[g], axis=0, keepdims=True)) for g in groups]
        ps = [jnp.exp2(ss[g] - ms[g]) for g in groups]
        out = []
        for g in groups:
            m, l, acc = carry[g]
            a = jnp.exp2(m - ms[g])
            l = a * l + jnp.sum(ps[g], axis=0, keepdims=True)
            v = v_ref[0, pl.ds(off, tk), g * NSA_DV:(g + 1) * NSA_DV]
            acc = a * acc + lax.dot_general(v, ps[g].astype(BF16), (((0,), (0,)), ((), ())),
                                            preferred_element_type=F32)
            out.append((ms[g], l, acc))
        return tuple(out)

    init = tuple((jnp.full((1, hp * t), NEG_INF, F32), jnp.zeros((1, hp * t), F32), jnp.zeros((NSA_DV, hp * t), F32))
                 for _ in range(G))
    sel = lax.fori_loop(0, (qi + 1) // 2, lambda kp, c: step(2 * kp, c, ks_ref, vs_ref, True, width=2), init)
    sel = lax.cond(qi % 2 == 0, lambda c: step(qi, c, ks_ref, vs_ref, True), lambda c: c, sel)
    nwin = WINDOW // t + 1
    win_update = lambda width: (lambda c: step(qi - (width - 1), c, kw_ref, vw_ref, False, width=width))
    win = win_update(1)
    for width in range(2, nwin + 1):
        win = functools.partial(lax.cond, qi >= width - 1, win_update(width), win)
    win = win(init)
    for g in range(G):
        o_s = sel[g][2] / sel[g][1]
        o_w = win[g][2] / win[g][1]
        gates = _sigmoid(gate_ref[0, g])
        for h in range(hp):
            r = slice(h * t, (h + 1) * t)
            c = slice((g * hp + h) * NSA_DV, (g * hp + h + 1) * NSA_DV)
            o = gates[3 * h + 1:3 * h + 2, :] * o_s[:, r] + gates[3 * h + 2:3 * h + 3, :] * o_w[:, r]
            o_ref[0, :, c] = (oc_ref[0, :, c].astype(F32) + o.T).astype(o_ref.dtype)


def nsa_sel_win(z3, ind, expand, bias_sw, gates, o_cmp):
    B, S, _ = z3.shape
    G = NSA_GROUPS
    t = NSA_T
    qw = NSA_HEADS * NSA_DKP
    ow = NSA_HEADS * NSA_DV
    kw = G * NSA_DKP
    vw = G * NSA_DV
    nbp = ind.shape[2]
    kspec = lambda cb: pl.BlockSpec((1, S, kw), lambda b, i: (b, 0, cb * LANE // kw))
    vspec = lambda cb: pl.BlockSpec((1, S, vw), lambda b, i: (b, 0, cb * LANE // vw))
    return pl.pallas_call(
        functools.partial(_nsa_sw_kernel, t=t),
        out_shape=jax.ShapeDtypeStruct((B, S, ow), BF16),
        grid=(B, S // t),
        in_specs=[pl.BlockSpec((1, t, qw), lambda b, i: (b, i, CB_NQ * LANE // qw)),
                  kspec(CB_NKS), vspec(CB_NVS), kspec(CB_NKW), vspec(CB_NVW),
                  pl.BlockSpec((1, G, nbp, t), lambda b, i: (b, 0, 0, i)),
                  pl.BlockSpec((S, nbp), lambda b, i: (0, 0)),
                  pl.BlockSpec((G, 3, t, NSA_HPG * t), lambda b, i: (0, 0, 0, 0)),
                  pl.BlockSpec((1, G, gates.shape[2], t), lambda b, i: (b, 0, 0, i)),
                  pl.BlockSpec((1, t, ow), lambda b, i: (b, i, 0))],
        out_specs=pl.BlockSpec((1, t, ow), lambda b, i: (b, i, 0)),
        compiler_params=_cparams(("parallel", "arbitrary")),
        name="nsa_sel_win",
    )(z3, z3, z3, z3, z3, ind, expand, bias_sw, gates, o_cmp)


def _merge_kernel(of_ref, om_ref, on_ref, wb_ref, g0_ref, g1_ref, g2_ref, o_ref):
    acc = None
    for n, (o_r, g_r) in enumerate(((of_ref, g0_ref), (om_ref, g1_ref), (on_ref, g2_ref))):
        y = jnp.dot(o_r[...], wb_ref[n], preferred_element_type=F32)
        y = _sigmoid(g_r[...].astype(F32)) * y
        acc = y if acc is None else acc + y
    o_ref[...] = acc.astype(o_ref.dtype)


def merge_branches(o_fox, o_mla, o_nsa, wb, z, tm, tn):
    T = o_fox.shape[0]
    D = wb.shape[2]
    gspec = lambda n: pl.BlockSpec((tm, tn), lambda i, j: (i, (CB_MG * LANE + n * D) // tn + j))
    ospec = pl.BlockSpec((tm, BRANCH_W), lambda i, j: (i, 0))
    return pl.pallas_call(
        _merge_kernel,
        out_shape=jax.ShapeDtypeStruct((T, D), BF16),
        grid=(T // tm, D // tn),
        in_specs=[ospec, ospec, ospec,
                  pl.BlockSpec((N_BRANCH, BRANCH_W, tn), lambda i, j: (0, 0, j)),
                  gspec(0), gspec(1), gspec(2)],
        out_specs=pl.BlockSpec((tm, tn), lambda i, j: (i, j)),
        compiler_params=_cparams(("parallel", "arbitrary")),
        name="merge_branches",
    )(o_fox, o_mla, o_nsa, wb, z, z, z)


def _mm_res_kernel(a_ref, w_ref, r_ref, o_ref):
    o_ref[...] = r_ref[...] + jnp.dot(a_ref[...], w_ref[...], preferred_element_type=F32)


def matmul_residual(a, w, res, tm, tn):
    T, K = a.shape
    N = w.shape[1]
    return pl.pallas_call(
        _mm_res_kernel,
        out_shape=jax.ShapeDtypeStruct((T, N), F32),
        grid=(T // tm, N // tn),
        in_specs=[pl.BlockSpec((tm, K), lambda i, j: (i, 0)),
                  pl.BlockSpec((K, tn), lambda i, j: (0, j)),
                  pl.BlockSpec((tm, tn), lambda i, j: (i, j))],
        out_specs=pl.BlockSpec((tm, tn), lambda i, j: (i, j)),
        compiler_params=_cparams(("parallel", "arbitrary")),
        name="matmul_residual",
    )(a, w, res)


def _mem_attn_kernel(h_ref, g_ref, wq_ref, kv_ref, wo_ref, o_ref):
    x = h_ref[0]
    ms = jnp.mean(x * x, axis=-1, keepdims=True)
    u = (x * lax.rsqrt(ms + EPS) * g_ref[...]).astype(BF16)
    q = jnp.dot(u, wq_ref[...], preferred_element_type=F32).astype(BF16)
    dn = (((1,), (1,)), ((), ()))
    hw = MEM_HEADS * MEM_DH
    heads = range(MEM_HEADS)
    cs = [slice(h * MEM_DH, (h + 1) * MEM_DH) for h in heads]
    ss = [lax.dot_general(q[:, cs[h]], kv_ref[0, :, cs[h]], dn, preferred_element_type=F32) for h in heads]
    es = [jnp.exp2(ss[h] - jnp.max(ss[h], axis=-1, keepdims=True)) for h in heads]
    ps = [es[h] / jnp.sum(es[h], axis=-1, keepdims=True) for h in heads]
    outs = [jnp.dot(ps[h].astype(BF16), kv_ref[0, :, hw + h * MEM_DH: hw + (h + 1) * MEM_DH],
                    preferred_element_type=F32).astype(BF16) for h in heads]
    o = jnp.concatenate(outs, axis=1)
    o_ref[0] = x + jnp.dot(o, wo_ref[...], preferred_element_type=F32)


def memory_attention_block(h3, g, wq, kv, wo, tq):
    B, S, D = h3.shape
    M = kv.shape[1]
    hw = MEM_HEADS * MEM_DH
    return pl.pallas_call(
        _mem_attn_kernel,
        out_shape=jax.ShapeDtypeStruct((B, S, D), F32),
        grid=(B, S // tq),
        in_specs=[pl.BlockSpec((1, tq, D), lambda b, i: (b, i, 0)),
                  pl.BlockSpec((1, D), lambda b, i: (0, 0)),
                  pl.BlockSpec((D, hw), lambda b, i: (0, 0)),
                  pl.BlockSpec((1, M, 2 * hw), lambda b, i: (b, 0, 0)),
                  pl.BlockSpec((hw, D), lambda b, i: (0, 0))],
        out_specs=pl.BlockSpec((1, tq, D), lambda b, i: (b, i, 0)),
        compiler_params=_cparams(("parallel", "arbitrary")),
        name="memory_attention",
    )(h3, g.reshape(1, D).astype(F32), wq, kv, wo)


def _dot3(a, b):
    a_hi = a.astype(BF16)
    a_lo = (a - a_hi.astype(F32)).astype(BF16)
    b_hi = b.astype(BF16)
    b_lo = (b - b_hi.astype(F32)).astype(BF16)
    dot = functools.partial(jnp.dot, preferred_element_type=F32)
    return dot(a_hi, b_hi) + (dot(a_lo, b_hi) + dot(a_hi, b_lo))


def _router_kernel(h_ref, g_ref, w_ref, b_ref, lg_ref, u_ref):
    x = h_ref[...]
    ms = jnp.mean(x * x, axis=-1, keepdims=True)
    u = x * lax.rsqrt(ms + EPS) * g_ref[...]
    u_ref[...] = u.astype(u_ref.dtype)
    lg_ref[...] = _dot3(u, w_ref[...]) + b_ref[...]


def moe_router(h, g, w, b, tm):
    T, D = h.shape
    N = w.shape[1]
    return pl.pallas_call(
        _router_kernel,
        out_shape=(jax.ShapeDtypeStruct((T, N), F32), jax.ShapeDtypeStruct((T, D), BF16)),
        grid=(T // tm,),
        in_specs=[pl.BlockSpec((tm, D), lambda i: (i, 0)),
                  pl.BlockSpec((1, D), lambda i: (0, 0)),
                  pl.BlockSpec((D, N), lambda i: (0, 0)),
                  pl.BlockSpec((1, N), lambda i: (0, 0))],
        out_specs=(pl.BlockSpec((tm, N), lambda i: (i, 0)), pl.BlockSpec((tm, D), lambda i: (i, 0))),
        compiler_params=_cparams(("parallel",)),
        name="moe_router",
    )(h, g.reshape(1, D).astype(F32), w, b)


def _route_kernel(lg_ref, tri_ref, o_ref, cnt_ref, carry_ref):
    @pl.when(pl.program_id(0) == 0)
    def _():
        carry_ref[...] = jnp.zeros(carry_ref.shape, carry_ref.dtype)

    lg = lg_ref[...]
    lane = lax.broadcasted_iota(jnp.int32, lg.shape, 1)
    first = lambda hit: jnp.min(jnp.where(hit, lane, LANE), axis=1, keepdims=True)
    gmask = lane < N_GROUPS
    gl = jnp.where(gmask, lg, -jnp.inf)
    gmax = jnp.max(gl, axis=1, keepdims=True)
    gsel = first(gl == gmax)
    pg = 1.0 / jnp.sum(jnp.where(gmask, jnp.exp(lg - gmax), 0.0), axis=1, keepdims=True)
    lo = N_GROUPS + gsel * EXPERTS_PER_GROUP
    emask = (lane >= lo) & (lane < lo + EXPERTS_PER_GROUP)
    el = jnp.where(emask, lg, -jnp.inf)
    emax = jnp.max(el, axis=1, keepdims=True)
    l0 = first(el == emax)
    esum = jnp.sum(jnp.where(emask, jnp.exp(lg - emax), 0.0), axis=1, keepdims=True)
    el2 = jnp.where(lane == l0, -jnp.inf, el)
    e2max = jnp.max(el2, axis=1, keepdims=True)
    l1 = first(el2 == e2max)
    p0 = 1.0 / esum
    p1 = jnp.exp(e2max - emax) / esum
    w0 = pg * (p0 / (p0 + p1))
    w1 = pg * (p1 / (p0 + p1))
    onehot = jnp.where((lane == l0) | (lane == l1), 1.0, 0.0)
    before = jnp.dot(tri_ref[...], onehot.astype(BF16), preferred_element_type=F32) + carry_ref[...]
    r0 = jnp.sum(jnp.where(lane == l0, before, 0.0), axis=1, keepdims=True)
    r1 = jnp.sum(jnp.where(lane == l1, before, 0.0), axis=1, keepdims=True)
    carry_ref[...] += jnp.sum(onehot, axis=0, keepdims=True)
    cnt_ref[...] = jnp.broadcast_to(carry_ref[...], cnt_ref.shape)
    vals = (l0 - N_GROUPS, l1 - N_GROUPS, r0, r1, w0, w1)
    out = jnp.zeros(lg.shape, F32)
    for k, v in enumerate(vals):
        out = jnp.where(lane == k, v.astype(F32), out)
    o_ref[...] = out


def moe_route(logits, tm):
    T, N = logits.shape
    tri = (jnp.arange(tm)[:, None] > jnp.arange(tm)[None, :]).astype(BF16)
    return pl.pallas_call(
        _route_kernel,
        out_shape=(jax.ShapeDtypeStruct((T, N), F32), jax.ShapeDtypeStruct((8, N), F32)),
        grid=(T // tm,),
        in_specs=[pl.BlockSpec((tm, N), lambda i: (i, 0)), pl.BlockSpec((tm, tm), lambda i: (0, 0))],
        out_specs=(pl.BlockSpec((tm, N), lambda i: (i, 0)), pl.BlockSpec((8, N), lambda i: (0, 0))),
        scratch_shapes=[pltpu.VMEM((1, N), F32)],
        compiler_params=_cparams(("arbitrary",)),
        name="moe_route",
    )(logits, tri)


def _moe_kernel(be_ref, nv_ref, nu_ref, x_ref, wg_ref, wu_ref, wd_ref, o_ref, acc_ref):
    i = pl.program_id(0)
    j = pl.program_id(1)
    nv = nv_ref[i]
    tm = x_ref.shape[0]

    @pl.when((i == 0) & (j == 0))
    def _():
        acc_ref[...] = jnp.zeros(acc_ref.shape, acc_ref.dtype)

    def run(rows):
        x = x_ref[:rows, :]
        a = jnp.dot(x, wg_ref[0, 0].astype(BF16), preferred_element_type=F32)
        b = jnp.dot(x, wu_ref[0, 0].astype(BF16), preferred_element_type=F32)
        hdn = (a * _sigmoid(a) * b).astype(BF16)
        y = jnp.dot(hdn, wd_ref[0, 0].astype(BF16), preferred_element_type=F32)
        tot = y + jnp.where(j > 0, acc_ref[:rows, :], 0.0)
        acc_ref[:rows, :] = tot
        o_ref[:rows, :] = tot.astype(o_ref.dtype)
        if rows < tm:
            o_ref[rows:, :] = jnp.zeros((tm - rows, o_ref.shape[1]), o_ref.dtype)

    for rows in range(MOE_SUB, tm + 1, MOE_SUB):
        pl.when((nv > rows - MOE_SUB) & (nv <= rows))(functools.partial(run, rows))

    @pl.when(nv == 0)
    def _():
        o_ref[...] = jnp.zeros(o_ref.shape, o_ref.dtype)


def moe_experts(blk_e, n_valid, n_used, xr, wg, wu, wd, layer):
    P, D = xr.shape
    De = wg.shape[3]
    tm, dc = MOE_TM, MOE_DC
    nj = De // dc
    chunk = lambda i, s: jnp.where(i % 2 == 0, s, nj - 1 - s)
    jj = lambda i, s, nu: jnp.where(i < nu[0], chunk(i, s), chunk(nu[0] - 1, nj - 1))
    grid_spec = pltpu.PrefetchScalarGridSpec(
        num_scalar_prefetch=3,
        grid=(P // tm, nj),
        in_specs=[pl.BlockSpec((tm, D), lambda i, j, be, nv, nu: (i, 0)),
                  pl.BlockSpec((1, 1, D, dc), lambda i, j, be, nv, nu: (layer, be[i], 0, jj(i, j, nu))),
                  pl.BlockSpec((1, 1, D, dc), lambda i, j, be, nv, nu: (layer, be[i], 0, jj(i, j, nu))),
                  pl.BlockSpec((1, 1, dc, D), lambda i, j, be, nv, nu: (layer, be[i], jj(i, j, nu), 0))],
        out_specs=pl.BlockSpec((tm, D), lambda i, j, be, nv, nu: (i, 0)),
        scratch_shapes=[pltpu.VMEM((tm, D), F32)],
    )
    return pl.pallas_call(
        _moe_kernel,
        out_shape=jax.ShapeDtypeStruct((P, D), BF16),
        grid_spec=grid_spec,
        compiler_params=_cparams(("arbitrary", "arbitrary")),
        name="moe_experts",
    )(blk_e, n_valid, n_used, xr, wg, wu, wd)


def _combine_kernel(h_ref, y0_ref, y1_ref, w_ref, g_ref, o_ref, *, final_norm):
    w = w_ref[...]
    x = h_ref[...] + (w[:, 0:1] * y0_ref[...].astype(F32) + w[:, 1:2] * y1_ref[...].astype(F32))
    if final_norm:
        ms = jnp.mean(x * x, axis=-1, keepdims=True)
        x = x * lax.rsqrt(ms + EPS) * g_ref[...]
    o_ref[...] = x


def moe_combine(h, y0, y1, w, g, tm):
    T, D = h.shape
    final_norm = g is not None
    gain = (g if final_norm else jnp.ones((D,), F32)).reshape(1, D).astype(F32)
    row = lambda width: pl.BlockSpec((tm, width), lambda i: (i, 0))
    return pl.pallas_call(
        functools.partial(_combine_kernel, final_norm=final_norm),
        out_shape=jax.ShapeDtypeStruct((T, D), F32),
        grid=(T // tm,),
        in_specs=[row(D), row(D), row(D), row(LANE), pl.BlockSpec((1, D), lambda i: (0, 0))],
        out_specs=row(D),
        compiler_params=_cparams(("parallel",)),
        name="moe_combine",
    )(h, y0, y1, _pad_cols(w, LANE), gain)


def _pad_cols(w, width):
    return jnp.pad(w, ((0, 0), (0, width - w.shape[1])))


def _w_in_segments(D):
    names = ("fq", "fk", "fv", "ff", "mcq", "mckv", "mkr", "nq", "nkc", "nvc", "nks", "nvs", "nkw", "nvw", "ngt", "mg")
    widths = (1024, 1024, 1024, FORGET_COLS, MLA_Q_RANK, MLA_KV_RANK, MLA_ROPE, NSA_HEADS * NSA_DK,
              NSA_GROUPS * NSA_DK, NSA_GROUPS * NSA_DV, NSA_GROUPS * NSA_DK, NSA_GROUPS * NSA_DV,
              NSA_GROUPS * NSA_DK, NSA_GROUPS * NSA_DV, NSA_GATE_COLS, N_BRANCH * D)
    src = dict(zip(names, np.cumsum((0,) + widths[:-1]).tolist()))
    wid = dict(zip(names, widths))
    segs = []
    plain = lambda name, cb, f=1.0: segs.append((cb * LANE, src[name], wid[name], f))

    def padded_k(name, cb, n, f=1.0):
        for i in range(n):
            segs.append((cb * LANE + i * NSA_DKP, src[name] + i * NSA_DK, NSA_DK, f))

    half = MLA_ROPE // 2
    plain("mckv", CB_CKV)
    plain("mkr", CB_KR)
    segs.append((CB_KR * LANE + MLA_ROPE, src["mkr"] + half, half, -1.0))
    segs.append((CB_KR * LANE + MLA_ROPE + half, src["mkr"], half, 1.0))
    plain("mcq", CB_CQ)
    padded_k("nkc", CB_NKC, NSA_GROUPS)
    padded_k("nq", CB_NQ, NSA_HEADS, NSA_DK ** -0.5 * LOG2E)
    plain("fq", CB_FQ, FOX_DH ** -0.5 * LOG2E)
    plain("fk", CB_FK)
    plain("fv", CB_FV)
    padded_k("nks", CB_NKS, NSA_GROUPS)
    padded_k("nkw", CB_NKW, NSA_GROUPS)
    plain("nvc", CB_NVC)
    plain("nvs", CB_NVS)
    plain("nvw", CB_NVW)
    plain("mg", CB_MG)
    return segs, src


def _pack_w_in_kernel(w_ref, o_ref, os_ref, *, segs, small_segs):
    d_in = w_ref.shape[1]

    def put(dst_ref, dst, s, n, f):
        for k in range(0, n, LANE):
            m = min(LANE, n - k)
            assert s + k + LANE <= d_in
            v = w_ref[0, s + k:s + k + LANE, :].T
            if f != 1.0:
                v = v * f
            dst_ref[:, dst + k:dst + k + m] = v[:, :m].astype(dst_ref.dtype)

    o_ref[...] = jnp.zeros(o_ref.shape, o_ref.dtype)
    for dst, s, n, f in segs:
        put(o_ref, dst, s, n, f)
    os_ref[...] = jnp.zeros(os_ref.shape, os_ref.dtype)
    for dst, s, n in small_segs:
        put(os_ref, dst, s, n, 1.0)


def _pack_w_in(w_all, layer):
    _, D, d_in = w_all.shape
    segs, src = _w_in_segments(D)
    small_segs = ((0, src["ff"], FORGET_COLS), (FORGET_COLS, src["ngt"], NSA_GATE_COLS))
    tr = LANE
    return pl.pallas_call(
        functools.partial(_pack_w_in_kernel, segs=tuple(segs), small_segs=small_segs),
        out_shape=(jax.ShapeDtypeStruct((D, Z_BLOCKS * LANE), BF16), jax.ShapeDtypeStruct((D, LANE), F32)),
        grid=(D // tr,),
        in_specs=[pl.BlockSpec((1, d_in, tr), lambda i: (layer, 0, i))],
        out_specs=(pl.BlockSpec((tr, Z_BLOCKS * LANE), lambda i: (i, 0)), pl.BlockSpec((tr, LANE), lambda i: (i, 0))),
        compiler_params=_cparams(("parallel",)),
        name="pack_w_in",
    )(jnp.swapaxes(w_all, 1, 2))


def _pack_w_uq(w):
    K = w.shape[0]
    w3 = w.reshape(K, MLA_HEADS, MLA_NOPE + MLA_ROPE) * ((MLA_NOPE + MLA_ROPE) ** -0.5 * LOG2E)
    nope = w3[:, :, :MLA_NOPE].reshape(K, MLA_HEADS * MLA_NOPE)
    r = w3[:, :, MLA_NOPE:]
    half = MLA_ROPE // 2
    r_rot = jnp.concatenate([-r[:, :, half:], r[:, :, :half]], axis=2)
    padr = lambda a: jnp.pad(a, ((0, 0), (0, 0), (0, LANE - MLA_ROPE))).reshape(K, MLA_HEADS * LANE)
    return jnp.concatenate([nope, padr(r), padr(r_rot)], axis=1).astype(BF16)


def _pack_w_ukv(w):
    K = w.shape[0]
    w3 = w.reshape(K, MLA_HEADS, MLA_NOPE + MLA_DV)
    return jnp.concatenate([w3[:, :, :MLA_NOPE].reshape(K, -1), w3[:, :, MLA_NOPE:].reshape(K, -1)], axis=1).astype(BF16)


def _t5_bucket(dist):
    dist = jnp.maximum(dist, 0)
    exact = REL_BUCKETS // 2
    df = jnp.maximum(dist, 1).astype(F32)
    large = exact + (jnp.log(df / exact) / math.log(REL_MAX_DIST / exact) * (REL_BUCKETS - exact)).astype(jnp.int32)
    large = jnp.minimum(large, REL_BUCKETS - 1)
    return jnp.where(dist < exact, dist, large)


def _position_tables(S, rel_bias):
    t = NSA_T
    half = MLA_ROPE // 2
    inv = ROPE_THETA ** (-jnp.arange(half, dtype=F32) / half)
    ang = jnp.arange(S, dtype=F32)[:, None] * inv
    c, s = jnp.cos(ang), jnp.sin(ang)
    cos = _pad_cols(jnp.concatenate([c, c], axis=1), LANE)
    sin = _pad_cols(jnp.concatenate([s, s], axis=1), LANE)
    ncp = max(S // CMP_STRIDE, LANE)
    pos = jnp.arange(S)

    def bias_of(dist):
        onehot = jax.nn.one_hot(_t5_bucket(dist), REL_BUCKETS, dtype=F32)
        return jnp.einsum("...b,bh->h...", onehot, rel_bias, precision=lax.Precision.HIGHEST)

    bias_c = bias_of(pos[None, :] - (CMP_STRIDE * jnp.arange(ncp)[:, None] + CMP_BLOCK - 1))
    i = jnp.arange(t)
    bias_sw = jnp.stack([bias_of(k * t + i[:, None] - i[None, :]) for k in range(3)], axis=1)
    bias_sw = bias_sw.reshape(NSA_GROUPS, NSA_HPG, 3, t, t).transpose(0, 2, 4, 1, 3).reshape(NSA_GROUPS, 3, t, NSA_HPG * t)
    n_cmp = (S - CMP_BLOCK) // CMP_STRIDE + 1
    n_blk = S // SEL_BLOCK
    nbp = max(n_blk, LANE)
    cstart = CMP_STRIDE * jnp.arange(ncp)
    sstart = SEL_BLOCK * jnp.arange(nbp)
    ov = jnp.clip(jnp.minimum(cstart[:, None] + CMP_BLOCK, sstart[None, :] + SEL_BLOCK)
                  - jnp.maximum(cstart[:, None], sstart[None, :]), 0, None).astype(F32) / CMP_STRIDE
    ov = jnp.where((jnp.arange(ncp)[:, None] < n_cmp) & (jnp.arange(nbp)[None, :] < n_blk), ov, 0.0).astype(BF16)
    expand = ((pos[:, None] // SEL_BLOCK) == jnp.arange(nbp)[None, :]).astype(BF16)
    return cos, sin, bias_c, bias_sw, ov, expand


def _token_mixers(h, p, w_in_all, layer, tabs, B, S):
    T, D = h.shape
    cos, sin, bias_c, bias_sw, overlap, expand = tabs
    wz, w_small = _pack_w_in(w_in_all, layer)
    z, zs = input_projection(h, p["g_mix"], wz, w_small, tm=min(T, 1024), tn=IN_PROJ_TN)
    z3 = z.reshape(B, S, Z_BLOCKS * LANE)

    log_f = jax.nn.log_sigmoid(zs[:, :FORGET_COLS] + p["b_forget"].astype(F32)).reshape(B, S, FOX_HEADS)
    cum = jnp.cumsum(log_f, axis=1) * LOG2E
    o_fox = causal_attention(z3, CB_FQ, z3, CB_FK, z3, CB_FV, FOX_HEADS, cum=cum)

    q_nope, q_rope = mla_q_proj(z, p["g_cq"], _pack_w_uq(p["w_uq"]), cos, sin, S, tm=min(S, 512))
    kv = rms_matmul(z, CB_CKV * LANE // MLA_KV_RANK, p["g_ckv"], _pack_w_ukv(p["w_ukv"]), BF16, tm=min(T, 1024), tn=1024)
    kr = z3[:, :, CB_KR * LANE:(CB_KR + 1) * LANE].astype(F32)
    kr = kr[..., :MLA_ROPE] * cos[None, :, :MLA_ROPE] + kr[..., MLA_ROPE:] * sin[None, :, :MLA_ROPE]
    k_rope = jnp.pad(kr, ((0, 0), (0, 0), (0, LANE - MLA_ROPE))).astype(BF16)
    hw = MLA_HEADS * LANE
    kv3 = kv.reshape(B, S, 2 * hw)
    o_mla = causal_attention(q_nope.reshape(B, S, hw), 0, kv3, 0, kv3, MLA_HEADS, MLA_HEADS,
                             q2=q_rope.reshape(B, S, hw), k2=k_rope)

    G = NSA_GROUPS
    NC = S // CMP_STRIDE
    ncp = bias_c.shape[1]

    def compress_branch(cb, dp, d, pe, w1, w2):
        x = z3[:, :, cb * LANE: cb * LANE + G * dp].reshape(B, NC, CMP_STRIDE * G * dp)
        eye = jnp.eye(G, dtype=F32)
        w1p = jnp.pad(w1.reshape(CMP_BLOCK, d, d), ((0, 0), (0, dp - d), (0, dp - d)))
        w1g = jnp.einsum("lij,gh->lgihj", w1p, eye).reshape(CMP_BLOCK, G * dp, G * dp).astype(BF16)
        w1a = w1g[:CMP_STRIDE].reshape(CMP_STRIDE * G * dp, G * dp)
        w1b = w1g[CMP_STRIDE:].reshape(CMP_STRIDE * G * dp, G * dp)
        pe_g = jnp.tile(jnp.pad(pe, ((0, 0), (0, dp - d)))[:, None, :], (1, G, 1))
        pe2 = pe_g.reshape(2, CMP_STRIDE * G * dp).astype(BF16)
        w2p = jnp.pad(w2, ((0, dp - d), (0, dp - d)))
        w2g = jnp.einsum("ij,gh->gihj", w2p, eye).reshape(G * dp, G * dp).astype(BF16)
        out = compress(x, w1a, w1b, pe2, w2g)
        return jnp.pad(out, ((0, 0), (0, ncp - NC), (0, 0)))

    kc = compress_branch(CB_NKC, NSA_DKP, NSA_DK, p["pe_k"], p["w_cmp_k1"], p["w_cmp_k2"])
    vc = compress_branch(CB_NVC, NSA_DV, NSA_DV, p["pe_v"], p["w_cmp_v1"], p["w_cmp_v2"])
    gl = zs[:, FORGET_COLS:FORGET_COLS + NSA_GATE_COLS].reshape(B, S, G, NSA_HPG * 3)
    gl = jnp.transpose(gl, (0, 2, 1, 3))
    gl_cols = jnp.pad(jnp.swapaxes(gl, 2, 3), ((0, 0), (0, 0), (0, 16 - NSA_HPG * 3), (0, 0)))
    o_cmp, ind = nsa_cmp_select(z3, kc, vc, bias_c, overlap, gl_cols, min(N_SEL, S // SEL_BLOCK))
    o_nsa = nsa_sel_win(z3, ind, expand, bias_sw, gl_cols, o_cmp)

    merged = merge_branches(o_fox.reshape(T, -1), o_mla.reshape(T, -1), o_nsa.reshape(T, -1),
                            p["w_branch"].astype(BF16), z, tm=min(T, 512), tn=1024)
    return matmul_residual(merged, p["w_out"].astype(BF16), h, tm=min(T, 1024), tn=1024)


def _memory_block(h, mem2, p, B, S):
    T, D = h.shape
    kv = rms_matmul(mem2, 0, p["g_mem_kv"], p["w_mem_kv"].astype(BF16), BF16, tm=min(mem2.shape[0], 512), tn=512)
    out = memory_attention_block(h.reshape(B, S, D), p["g_mem_q"], (p["w_mem_q"] * (MEM_DH ** -0.5 * LOG2E)).astype(BF16),
                                 kv.reshape(B, -1, kv.shape[1]), p["w_mem_o"].astype(BF16), tq=min(S, 512))
    return out.reshape(T, D)


def _moe_block(h, p, experts, layer, g_final=None):
    T, D = h.shape
    tm = MOE_TM
    w_r = _pad_cols(jnp.concatenate([p["w_router_group"], p["w_router_expert"]], axis=1), LANE).astype(F32)
    b_r = _pad_cols(jnp.concatenate([p["b_router_group"], p["b_router_expert"]])[None, :], LANE).astype(F32)
    logits, u = moe_router(h, p["g_moe"], w_r, b_r, tm=min(T, 512))
    routed, totals = moe_route(logits, tm=min(T, 512))
    assert TOP_K == 2
    flat_e = routed[:, 0:2].astype(jnp.int32).reshape(-1)
    rank = routed[:, 2:4].astype(jnp.int32).reshape(-1)
    weight = routed[:, 4:6]
    counts = totals[0, N_GROUPS:N_GROUPS + N_EXPERTS].astype(jnp.int32)
    TK = T * TOP_K
    pcounts = ((counts + tm - 1) // tm) * tm
    pends = jnp.cumsum(pcounts)
    dest = (pends - pcounts)[flat_e] + rank
    n_rb = -(-TK // tm) + N_EXPERTS
    P = n_rb * tm
    row_tok = (jnp.arange(P, dtype=jnp.int32) % T).at[dest].set(jnp.repeat(jnp.arange(T, dtype=jnp.int32), TOP_K))
    blk_e = jnp.sum((pends[None, :] <= (jnp.arange(n_rb, dtype=jnp.int32) * tm)[:, None]).astype(jnp.int32), axis=1)
    blk_e = jnp.minimum(blk_e, N_EXPERTS - 1).astype(jnp.int32)
    n_used = (pends[-1] // tm).astype(jnp.int32).reshape(1)
    row_end = pends - pcounts + counts
    n_valid = jnp.clip(row_end[blk_e] - jnp.arange(n_rb, dtype=jnp.int32) * tm, 0, tm).astype(jnp.int32)
    xr = u[row_tok]
    y = moe_experts(blk_e, n_valid, n_used, xr, experts[0].astype(F32), experts[1].astype(F32),
                    experts[2].astype(F32), layer)
    d2 = dest.reshape(T, TOP_K)
    return moe_combine(h, y[d2[:, 0]], y[d2[:, 1]], weight, g_final, tm=min(T, 512))


_LAYER_KEYS = ("g_mix", "w_in", "b_forget", "g_cq", "g_ckv", "w_uq", "w_ukv", "pe_k", "pe_v", "w_cmp_k1", "w_cmp_k2",
               "w_cmp_v1", "w_cmp_v2", "w_branch", "w_out", "g_mem_q", "g_mem_kv", "w_mem_q", "w_mem_kv", "w_mem_o",
               "g_moe", "w_router_group", "b_router_group", "w_router_expert", "b_router_expert")


def kernel(x, mem, g_mix, w_in, b_forget, g_cq, g_ckv, w_uq, w_ukv, pe_k, pe_v, w_cmp_k1, w_cmp_k2, w_cmp_v1, w_cmp_v2, rel_bias, w_branch, w_out, g_mem_q, g_mem_kv, w_mem_q, w_mem_kv, w_mem_o, g_moe, w_router_group, b_router_group, w_router_expert, b_router_expert, w_exp_gate, w_exp_up, w_exp_down, g_final):
    B, S, D = x.shape
    T = B * S
    stacked = dict(g_mix=g_mix, w_in=w_in, b_forget=b_forget, g_cq=g_cq, g_ckv=g_ckv, w_uq=w_uq, w_ukv=w_ukv,
                   pe_k=pe_k, pe_v=pe_v, w_cmp_k1=w_cmp_k1, w_cmp_k2=w_cmp_k2, w_cmp_v1=w_cmp_v1, w_cmp_v2=w_cmp_v2,
                   w_branch=w_branch, w_out=w_out, g_mem_q=g_mem_q, g_mem_kv=g_mem_kv, w_mem_q=w_mem_q,
                   w_mem_kv=w_mem_kv, w_mem_o=w_mem_o, g_moe=g_moe, w_router_group=w_router_group,
                   b_router_group=b_router_group, w_router_expert=w_router_expert, b_router_expert=b_router_expert,
                   w_exp_gate=w_exp_gate, w_exp_up=w_exp_up, w_exp_down=w_exp_down)
    tabs = _position_tables(S, rel_bias.astype(F32) * LOG2E)
    h = x.reshape(T, D).astype(F32)
    mem2 = mem.reshape(-1, D).astype(F32)
    depth = w_in.shape[0]
    assert depth >= 1
    for l in range(depth):
        p = {k: stacked[k][l] for k in _LAYER_KEYS}
        h = _token_mixers(h, p, w_in.astype(F32), l, tabs, B, S)
        h = _memory_block(h, mem2, p, B, S)
        h = _moe_block(h, p, (w_exp_gate, w_exp_up, w_exp_down), l, g_final if l == depth - 1 else None)
    return h.reshape(B, S, D)
```

```python
import functools
import math

import jax
import jax.numpy as jnp
import numpy as np
from jax import lax
from jax.experimental import pallas as pl
from jax.experimental.pallas import tpu as pltpu

F32 = jnp.float32
BF16 = jnp.bfloat16

EPS = 1e-6
NEG_INF = -1e30
LOG2E = math.log2(math.e)
LANE = 128

FOX_HEADS, FOX_DH = 8, 128
MLA_HEADS, MLA_NOPE, MLA_ROPE, MLA_DV = 8, 128, 64, 128
MLA_Q_RANK, MLA_KV_RANK = 768, 512
ROPE_THETA = 10000.0
NSA_HEADS, NSA_GROUPS, NSA_DK, NSA_DV = 8, 2, 192, 128
NSA_HPG = NSA_HEADS // NSA_GROUPS
NSA_DKP = 256
CMP_BLOCK, CMP_STRIDE, SEL_BLOCK, N_SEL, WINDOW = 32, 16, 64, 8, 512
SEL_FORCE = 1e6
REL_BUCKETS, REL_MAX_DIST = 32, 128
N_BRANCH, BRANCH_W = 3, 1024
MEM_HEADS, MEM_DH = 4, 128
N_GROUPS, EXPERTS_PER_GROUP, TOP_K = 4, 8, 2
N_EXPERTS = N_GROUPS * EXPERTS_PER_GROUP
FORGET_COLS, NSA_GATE_COLS = FOX_HEADS, NSA_HEADS * 3

CB_CKV, CB_KR, CB_CQ = 0, 4, 6
CB_NKC, CB_NQ = 12, 16
CB_FQ, CB_FK, CB_FV = 32, 40, 48
CB_NKS, CB_NKW = 56, 60
CB_NVC, CB_NVS, CB_NVW = 64, 66, 68
CB_MG = 72
Z_BLOCKS = 120
IN_PROJ_TN = 8 * LANE

ATT_T = 256
ATT_HG = 8
NSA_T = 256
NSA_CMP_T = 1024
MOE_TM = 512
MOE_SUB = 256
MOE_DC = 512
VMEM_LIMIT = 56 * 1024 * 1024


def _cparams(sem):
    return pltpu.CompilerParams(dimension_semantics=sem, vmem_limit_bytes=VMEM_LIMIT)


def _sigmoid(x):
    return 1.0 / (1.0 + jnp.exp(-x))


def _rms_mm_kernel(x_ref, g_ref, w_ref, o_ref, u_ref):
    @pl.when(pl.program_id(1) == 0)
    def _():
        x = x_ref[...].astype(F32)
        ms = jnp.mean(x * x, axis=-1, keepdims=True)
        u_ref[...] = (x * lax.rsqrt(ms + EPS) * g_ref[...]).astype(u_ref.dtype)

    o_ref[...] = jnp.dot(u_ref[...], w_ref[...], preferred_element_type=F32).astype(o_ref.dtype)


def rms_matmul(x, xcol, g, w, out_dtype, tm, tn):
    T = x.shape[0]
    K, N = w.shape
    return pl.pallas_call(
        _rms_mm_kernel,
        out_shape=jax.ShapeDtypeStruct((T, N), out_dtype),
        grid=(T // tm, N // tn),
        in_specs=[pl.BlockSpec((tm, K), lambda i, j: (i, xcol)),
                  pl.BlockSpec((1, K), lambda i, j: (0, 0)),
                  pl.BlockSpec((K, tn), lambda i, j: (0, j))],
        out_specs=pl.BlockSpec((tm, tn), lambda i, j: (i, j)),
        scratch_shapes=[pltpu.VMEM((tm, K), BF16)],
        compiler_params=_cparams(("parallel", "arbitrary")),
        name="rms_matmul",
    )(x, g.reshape(1, K).astype(F32), w)


def _in_proj_kernel(x_ref, g_ref, w_ref, ws_ref, o_ref, os_ref, u_ref):
    @pl.when(pl.program_id(1) == 0)
    def _():
        x = x_ref[...]
        ms = jnp.mean(x * x, axis=-1, keepdims=True)
        u = x * lax.rsqrt(ms + EPS) * g_ref[...]
        u_ref[...] = u.astype(u_ref.dtype)
        os_ref[...] = _dot3(u, ws_ref[...])

    o_ref[...] = jnp.dot(u_ref[...], w_ref[...], preferred_element_type=F32).astype(o_ref.dtype)


def input_projection(x, g, w, w_small, tm, tn):
    T, K = x.shape
    N = w.shape[1]
    Ns = w_small.shape[1]
    return pl.pallas_call(
        _in_proj_kernel,
        out_shape=(jax.ShapeDtypeStruct((T, N), BF16), jax.ShapeDtypeStruct((T, Ns), F32)),
        grid=(T // tm, N // tn),
        in_specs=[pl.BlockSpec((tm, K), lambda i, j: (i, 0)),
                  pl.BlockSpec((1, K), lambda i, j: (0, 0)),
                  pl.BlockSpec((K, tn), lambda i, j: (0, j)),
                  pl.BlockSpec((K, Ns), lambda i, j: (0, 0))],
        out_specs=(pl.BlockSpec((tm, tn), lambda i, j: (i, j)), pl.BlockSpec((tm, Ns), lambda i, j: (i, 0))),
        scratch_shapes=[pltpu.VMEM((tm, K), BF16)],
        compiler_params=_cparams(("parallel", "arbitrary")),
        name="input_projection",
    )(x, g.reshape(1, K).astype(F32), w, w_small)


def _mla_q_kernel(x_ref, g_ref, w_ref, cos_ref, sin_ref, qn_ref, qr_ref):
    x = x_ref[...].astype(F32)
    ms = jnp.mean(x * x, axis=-1, keepdims=True)
    u = (x * lax.rsqrt(ms + EPS) * g_ref[...]).astype(BF16)
    y = jnp.dot(u, w_ref[...], preferred_element_type=F32)
    hw = MLA_HEADS * LANE
    qn_ref[...] = y[:, :hw].astype(qn_ref.dtype)
    cos = cos_ref[...]
    sin = sin_ref[...]
    for h in range(MLA_HEADS):
        a = y[:, hw + h * LANE: hw + (h + 1) * LANE]
        b = y[:, 2 * hw + h * LANE: 2 * hw + (h + 1) * LANE]
        qr_ref[:, h * LANE:(h + 1) * LANE] = (a * cos + b * sin).astype(qr_ref.dtype)


def mla_q_proj(z, g, w, cos, sin, S, tm):
    T = z.shape[0]
    K, N = w.shape
    hw = MLA_HEADS * LANE
    nsb = S // tm
    return pl.pallas_call(
        _mla_q_kernel,
        out_shape=(jax.ShapeDtypeStruct((T, hw), BF16), jax.ShapeDtypeStruct((T, hw), BF16)),
        grid=(T // tm,),
        in_specs=[pl.BlockSpec((tm, K), lambda i: (i, CB_CQ * LANE // MLA_Q_RANK)),
                  pl.BlockSpec((1, K), lambda i: (0, 0)),
                  pl.BlockSpec((K, N), lambda i: (0, 0)),
                  pl.BlockSpec((tm, LANE), lambda i: (i % nsb, 0)),
                  pl.BlockSpec((tm, LANE), lambda i: (i % nsb, 0))],
        out_specs=(pl.BlockSpec((tm, hw), lambda i: (i, 0)), pl.BlockSpec((tm, hw), lambda i: (i, 0))),
        compiler_params=_cparams(("parallel",)),
        name="mla_q_proj",
    )(z, g.reshape(1, K).astype(F32), w, cos, sin)


def _causal_attn_kernel(*refs, t, hg, two_part, decay):
    refs = list(refs)
    q_ref, k_ref, v_ref = refs[:3]
    pos = 3
    if two_part:
        q2_ref, k2_ref = refs[pos:pos + 2]
        pos += 2
    if decay:
        ck_ref = refs[pos]
        pos += 1
    o_ref = refs[pos]
    if two_part:
        kcat_ref = refs[pos + 1]
    qi = pl.program_id(2)
    dn = (((1,), (1,)), ((), ()))

    if two_part:
        @pl.when(qi == 0)
        def _():
            for j in range(hg):
                kcat_ref[j, :, :LANE] = k_ref[0, :, j * LANE:(j + 1) * LANE]
                kcat_ref[j, :, LANE:] = k2_ref[0]

    qs = []
    for j in range(hg):
        qj = q_ref[0, :, j * LANE:(j + 1) * LANE]
        if two_part:
            qj = jnp.concatenate([qj, q2_ref[0, :, j * LANE:(j + 1) * LANE]], axis=1)
        qs.append(qj)

    def step(kb, carry, masked, width=1):
        off = pl.multiple_of(kb * t, t)
        tk = width * t
        heads = range(hg)
        ss = []
        for j in heads:
            k = kcat_ref[j, pl.ds(off, tk), :] if two_part else k_ref[0, pl.ds(off, tk), j * LANE:(j + 1) * LANE]
            ss.append(lax.dot_general(k, qs[j], dn, preferred_element_type=F32))
        if decay:
            ss = [ss[j] - ck_ref[0, 0, pl.ds(off, tk), j:j + 1] for j in heads]
        if masked:
            r = lax.broadcasted_iota(jnp.int32, (tk, t), 0)
            c = lax.broadcasted_iota(jnp.int32, (tk, t), 1)
            ss = [jnp.where(r <= c + (width - 1) * t, s, NEG_INF) for s in ss]
        ms = [jnp.maximum(carry[j][0], jnp.max(ss[j], axis=0, keepdims=True)) for j in heads]
        ps = [jnp.exp2(ss[j] - ms[j]) for j in heads]
        out = []
        for j in heads:
            m, l, acc = carry[j]
            a = jnp.exp2(m - ms[j])
            l = a * l + jnp.sum(ps[j], axis=0, keepdims=True)
            v = v_ref[0, pl.ds(off, tk), j * LANE:(j + 1) * LANE]
            acc = a * acc + lax.dot_general(v, ps[j].astype(BF16), (((0,), (0,)), ((), ())),
                                            preferred_element_type=F32)
            out.append((ms[j], l, acc))
        return tuple(out)

    init = tuple((jnp.full((1, t), NEG_INF, F32), jnp.zeros((1, t), F32), jnp.zeros((LANE, t), F32))
                 for _ in range(hg))
    carry = lax.fori_loop(0, qi // 2, lambda kp, c: step(2 * kp, c, False, width=2), init)
    carry = lax.cond(qi % 2 == 1, lambda c: step(qi - 1, c, True, width=2), lambda c: step(qi, c, True), carry)
    for j in range(hg):
        _, l, acc = carry[j]
        o_ref[0, :, j * LANE:(j + 1) * LANE] = (acc / l).T.astype(o_ref.dtype)


def causal_attention(q, qcb, k, kcb, v, vcb, heads, q2=None, k2=None, cum=None):
    B, S, _ = q.shape
    t, hg = ATT_T, ATT_HG
    w = hg * LANE
    two_part, decay = q2 is not None, cum is not None
    in_specs = [pl.BlockSpec((1, t, w), lambda b, h, i: (b, i, qcb // hg + h)),
                pl.BlockSpec((1, S, w), lambda b, h, i: (b, 0, kcb // hg + h)),
                pl.BlockSpec((1, S, w), lambda b, h, i: (b, 0, vcb // hg + h))]
    args = [q, k, v]
    scratch = []
    if two_part:
        in_specs += [pl.BlockSpec((1, t, w), lambda b, h, i: (b, i, h)),
                     pl.BlockSpec((1, S, LANE), lambda b, h, i: (b, 0, 0))]
        args += [q2, k2]
        scratch = [pltpu.VMEM((hg, S, 2 * LANE), BF16)]
    if decay:
        in_specs += [pl.BlockSpec((1, 1, S, hg), lambda b, h, i: (b, h, 0, 0))]
        args += [jnp.transpose(cum.reshape(B, S, heads // hg, hg), (0, 2, 1, 3))]
    return pl.pallas_call(
        functools.partial(_causal_attn_kernel, t=t, hg=hg, two_part=two_part, decay=decay),
        out_shape=jax.ShapeDtypeStruct((B, S, heads * LANE), BF16),
        grid=(B, heads // hg, S // t),
        in_specs=in_specs,
        out_specs=pl.BlockSpec((1, t, w), lambda b, h, i: (b, i, h)),
        scratch_shapes=scratch,
        compiler_params=_cparams(("parallel", "parallel", "arbitrary")),
        name="causal_attention",
    )(*args)


def _gelu_tanh(x):
    return 0.5 * x * (1.0 + jnp.tanh(math.sqrt(2.0 / math.pi) * (x + 0.044715 * (x * x * x))))


def _compress_kernel(x_ref, w1a_ref, w1b_ref, pe_ref, w2_ref, o_ref):
    x = x_ref[0]
    a = jnp.dot(x, w1a_ref[...], preferred_element_type=F32)
    b = jnp.dot(x, w1b_ref[...], preferred_element_type=F32)
    nc = a.shape[0]
    b_next = pltpu.roll(b, nc - 1, 0)
    pe_term = jnp.dot(pe_ref[0:1, :], w1a_ref[...], preferred_element_type=F32) + \
        jnp.dot(pe_ref[1:2, :], w1b_ref[...], preferred_element_type=F32)
    hid = _gelu_tanh(a + b_next + pe_term)
    o_ref[0] = jnp.dot(hid.astype(BF16), w2_ref[...], preferred_element_type=F32).astype(o_ref.dtype)


def compress(x, w1a, w1b, pe2, w2):
    BG, NC, KD = x.shape
    dp = w2.shape[1]
    return pl.pallas_call(
        _compress_kernel,
        out_shape=jax.ShapeDtypeStruct((BG, NC, dp), BF16),
        grid=(BG,),
        in_specs=[pl.BlockSpec((1, NC, KD), lambda i: (i, 0, 0)),
                  pl.BlockSpec((KD, dp), lambda i: (0, 0)),
                  pl.BlockSpec((KD, dp), lambda i: (0, 0)),
                  pl.BlockSpec((2, KD), lambda i: (0, 0)),
                  pl.BlockSpec((dp, dp), lambda i: (0, 0))],
        out_specs=pl.BlockSpec((1, NC, dp), lambda i: (i, 0, 0)),
        compiler_params=_cparams(("parallel",)),
        name="nsa_compress",
    )(x, w1a, w1b, pe2, w2)


def _nsa_cmp_kernel(q_ref, kc_ref, vc_ref, bias_ref, ov_ref, gate_ref, o_ref, ind_ref, *, t, n_sel, n_rows):
    qi = pl.program_id(2)
    ncp = kc_ref.shape[1]
    kc = kc_ref[0]
    vc = vc_ref[0]
    pos = lax.broadcasted_iota(jnp.int32, (ncp, t), 1) + qi * t
    cblk = lax.broadcasted_iota(jnp.int32, (ncp, t), 0)
    valid = pos >= CMP_STRIDE * cblk + (CMP_BLOCK - 1)
    dn = (((1,), (1,)), ((), ()))
    tn = (((0,), (0,)), ((), ()))
    heads = range(NSA_HPG)
    ss = [lax.dot_general(kc, q_ref[0, :, h * NSA_DKP:(h + 1) * NSA_DKP], dn, preferred_element_type=F32)
          for h in heads]
    ss = [jnp.where(valid, ss[h] + bias_ref[h], NEG_INF) for h in heads]
    es = [jnp.exp2(ss[h] - jnp.max(ss[h], axis=0, keepdims=True)) for h in heads]
    ps = [jnp.where(valid, es[h] / jnp.sum(es[h], axis=0, keepdims=True), 0.0) for h in heads]
    gates = _sigmoid(gate_ref[0, 0])
    for h in heads:
        o = lax.dot_general(vc, ps[h].astype(BF16), tn, preferred_element_type=F32)
        o_ref[0, :, h * NSA_DV:(h + 1) * NSA_DV] = (gates[3 * h:3 * h + 1, :] * o).T.astype(o_ref.dtype)
    psum = functools.reduce(lambda x, y: x + y, ps)
    p_hi = psum.astype(BF16)
    p_lo = (psum - p_hi.astype(F32)).astype(BF16)
    imp = lax.dot_general(ov_ref[...], p_hi, tn, preferred_element_type=F32) + \
        lax.dot_general(ov_ref[...], p_lo, tn, preferred_element_type=F32)
    nbp = imp.shape[0]
    imp = imp[:n_rows]
    blk = lax.broadcasted_iota(jnp.int32, (n_rows, t), 0)
    cur = (lax.broadcasted_iota(jnp.int32, (n_rows, t), 1) + qi * t) // SEL_BLOCK
    forced = (blk == 0) | (blk == cur) | (blk == cur - 1)
    score = jnp.where(blk <= cur, imp + jnp.where(forced, SEL_FORCE, 0.0), NEG_INF)
    sel = jnp.zeros((n_rows, t), F32)
    for _ in range(n_sel):
        mx = jnp.max(score, axis=0, keepdims=True)
        first = jnp.min(jnp.where(score == mx, blk, nbp), axis=0, keepdims=True)
        hit = blk == first
        sel = jnp.where(hit, 1.0, sel)
        score = jnp.where(hit, -jnp.inf, score)
    ind_ref[0, 0] = jnp.zeros(ind_ref.shape[2:], ind_ref.dtype)
    ind_ref[0, 0, :n_rows, :] = sel.astype(ind_ref.dtype)


def nsa_cmp_select(z3, kc, vc, bias_c, overlap, gates, n_sel):
    B, S, _ = z3.shape
    G = NSA_GROUPS
    t = min(NSA_CMP_T, S)
    ncp = kc.shape[1]
    nbp = overlap.shape[1]
    qw = NSA_HPG * NSA_DKP
    ow = NSA_HPG * NSA_DV
    n_rows = min(nbp, -(-(S // SEL_BLOCK) // 16) * 16)
    return pl.pallas_call(
        functools.partial(_nsa_cmp_kernel, t=t, n_sel=n_sel, n_rows=n_rows),
        out_shape=(jax.ShapeDtypeStruct((B, S, G * ow), BF16), jax.ShapeDtypeStruct((B, G, nbp, S), BF16)),
        grid=(B, G, S // t),
        in_specs=[pl.BlockSpec((1, t, qw), lambda b, g, i: (b, i, CB_NQ * LANE // qw + g)),
                  pl.BlockSpec((1, ncp, NSA_DKP), lambda b, g, i: (b, 0, g)),
                  pl.BlockSpec((1, ncp, NSA_DV), lambda b, g, i: (b, 0, g)),
                  pl.BlockSpec((NSA_HPG, ncp, t), lambda b, g, i: (g, 0, i)),
                  pl.BlockSpec((ncp, nbp), lambda b, g, i: (0, 0)),
                  pl.BlockSpec((1, 1, gates.shape[2], t), lambda b, g, i: (b, g, 0, i))],
        out_specs=(pl.BlockSpec((1, t, ow), lambda b, g, i: (b, i, g)),
                   pl.BlockSpec((1, 1, nbp, t), lambda b, g, i: (b, g, 0, i))),
        compiler_params=_cparams(("parallel", "parallel", "arbitrary")),
        name="nsa_cmp_select",
    )(z3, kc, vc, bias_c, overlap, gates)


def _nsa_sw_kernel(q_ref, ks_ref, vs_ref, kw_ref, vw_ref, ind_ref, e_ref, bias_ref, gate_ref, oc_ref, o_ref,
                   *, t):
    qi = pl.program_id(1)
    hp, G = NSA_HPG, NSA_GROUPS
    q4 = [jnp.concatenate([q_ref[0, :, (g * hp + h) * NSA_DKP:(g * hp + h + 1) * NSA_DKP] for h in range(hp)], axis=0)
          for g in range(G)]
    inds = [ind_ref[0, g] for g in range(G)]
    dn = (((1,), (1,)), ((), ()))

    def step(kb, carry, k_ref, v_ref, selected, width=1):
        off = pl.multiple_of(kb * t, t)
        tk = width * t
        ri = lax.broadcasted_iota(jnp.int32, (tk, t), 0)
        ci = lax.broadcasted_iota(jnp.int32, (tk, t), 1)
        d = (qi - kb) * t + ci - ri
        near = (d >= 0) if selected else (d >= 0) & (d < WINDOW)
        groups = range(G)
        ss = [lax.dot_general(k_ref[0, pl.ds(off, tk), g * NSA_DKP:(g + 1) * NSA_DKP], q4[g], dn,
                              preferred_element_type=F32) for g in groups]
        negs = []
        for g in groups:
            mask = near
            if selected:
                hit = jnp.dot(e_ref[pl.ds(off, tk), :], inds[g], preferred_element_type=F32)
                mask = near & (hit > 0.5)
            neg = jnp.where(mask, 0.0, NEG_INF)
            negs.append(jnp.concatenate([neg] * hp, axis=1))
        bias = [jnp.concatenate([bias_ref[g, jnp.minimum(qi - kb - w, 2)] for w in range(width)], axis=0)
                if width > 1 else bias_ref[g, jnp.minimum(qi - kb, 2)] for g in groups]
        ss = [ss[g] + bias[g] + negs[g] for g in groups]
        ms = [jnp.maximum(carry[g][0], jnp.max(ss[g], axis=0, keepdims=True)) for g in groups]
        ps = [jnp.exp2(ss[g] - ms[g]) for g in groups]
        out = []
        for g in groups:
            m, l, acc = carry[g]
            a = jnp.exp2(m - ms[g])
            l = a * l + jnp.sum(ps[g], axis=0, keepdims=True)
            v = v_ref[0, pl.ds(off, tk), g * NSA_DV:(g + 1) * NSA_DV]
            acc = a * acc + lax.dot_general(v, ps[g].astype(BF16), (((0,), (0,)), ((), ())),
                                            preferred_element_type=F32)
            out.append((ms[g], l, acc))
        return tuple(out)

    init = tuple((jnp.full((1, hp * t), NEG_INF, F32), jnp.zeros((1, hp * t), F32), jnp.zeros((NSA_DV, hp * t), F32))
                 for _ in range(G))
    sel = lax.fori_loop(0, (qi + 1) // 2, lambda kp, c: step(2 * kp, c, ks_ref, vs_ref, True, width=2), init)
    sel = lax.cond(qi % 2 == 0, lambda c: step(qi, c, ks_ref, vs_ref, True), lambda c: c, sel)
    nwin = WINDOW // t + 1
    win_update = lambda width: (lambda c: step(qi - (width - 1), c, kw_ref, vw_ref, False, width=width))
    win = win_update(1)
    for width in range(2, nwin + 1):
        win = functools.partial(lax.cond, qi >= width - 1, win_update(width), win)
    win = win(init)
    for g in range(G):
        o_s = sel[g][2] / sel[g][1]
        o_w = win[g][2] / win[g][1]
        gates = _sigmoid(gate_ref[0, g])
        for h in range(hp):
            r = slice(h * t, (h + 1) * t)
            c = slice((g * hp + h) * NSA_DV, (g * hp + h + 1) * NSA_DV)
            o = gates[3 * h + 1:3 * h + 2, :] * o_s[:, r] + gates[3 * h + 2:3 * h + 3, :] * o_w[:, r]
            o_ref[0, :, c] = (oc_ref[0, :, c].astype(F32) + o.T).astype(o_ref.dtype)


def nsa_sel_win(z3, ind, expand, bias_sw, gates, o_cmp):
    B, S, _ = z3.shape
    G = NSA_GROUPS
    t = NSA_T
    qw = NSA_HEADS * NSA_DKP
    ow = NSA_HEADS * NSA_DV
    kw = G * NSA_DKP
    vw = G * NSA_DV
    nbp = ind.shape[2]
    kspec = lambda cb: pl.BlockSpec((1, S, kw), lambda b, i: (b, 0, cb * LANE // kw))
    vspec = lambda cb: pl.BlockSpec((1, S, vw), lambda b, i: (b, 0, cb * LANE // vw))
    return pl.pallas_call(
        functools.partial(_nsa_sw_kernel, t=t),
        out_shape=jax.ShapeDtypeStruct((B, S, ow), BF16),
        grid=(B, S // t),
        in_specs=[pl.BlockSpec((1, t, qw), lambda b, i: (b, i, CB_NQ * LANE // qw)),
                  kspec(CB_NKS), vspec(CB_NVS), kspec(CB_NKW), vspec(CB_NVW),
                  pl.BlockSpec((1, G, nbp, t), lambda b, i: (b, 0, 0, i)),
                  pl.BlockSpec((S, nbp), lambda b, i: (0, 0)),
                  pl.BlockSpec((G, 3, t, NSA_HPG * t), lambda b, i: (0, 0, 0, 0)),
                  pl.BlockSpec((1, G, gates.shape[2], t), lambda b, i: (b, 0, 0, i)),
                  pl.BlockSpec((1, t, ow), lambda b, i: (b, i, 0))],
        out_specs=pl.BlockSpec((1, t, ow), lambda b, i: (b, i, 0)),
        compiler_params=_cparams(("parallel", "arbitrary")),
        name="nsa_sel_win",
    )(z3, z3, z3, z3, z3, ind, expand, bias_sw, gates, o_cmp)


def _merge_kernel(of_ref, om_ref, on_ref, wb_ref, g0_ref, g1_ref, g2_ref, o_ref):
    acc = None
    for n, (o_r, g_r) in enumerate(((of_ref, g0_ref), (om_ref, g1_ref), (on_ref, g2_ref))):
        y = jnp.dot(o_r[...], wb_ref[n], preferred_element_type=F32)
        y = _sigmoid(g_r[...].astype(F32)) * y
        acc = y if acc is None else acc + y
    o_ref[...] = acc.astype(o_ref.dtype)


def merge_branches(o_fox, o_mla, o_nsa, wb, z, tm, tn):
    T = o_fox.shape[0]
    D = wb.shape[2]
    gspec = lambda n: pl.BlockSpec((tm, tn), lambda i, j: (i, (CB_MG * LANE + n * D) // tn + j))
    ospec = pl.BlockSpec((tm, BRANCH_W), lambda i, j: (i, 0))
    return pl.pallas_call(
        _merge_kernel,
        out_shape=jax.ShapeDtypeStruct((T, D), BF16),
        grid=(T // tm, D // tn),
        in_specs=[ospec, ospec, ospec,
                  pl.BlockSpec((N_BRANCH, BRANCH_W, tn), lambda i, j: (0, 0, j)),
                  gspec(0), gspec(1), gspec(2)],
        out_specs=pl.BlockSpec((tm, tn), lambda i, j: (i, j)),
        compiler_params=_cparams(("parallel", "arbitrary")),
        name="merge_branches",
    )(o_fox, o_mla, o_nsa, wb, z, z, z)


def _mm_res_kernel(a_ref, w_ref, r_ref, o_ref):
    o_ref[...] = r_ref[...] + jnp.dot(a_ref[...], w_ref[...], preferred_element_type=F32)


def matmul_residual(a, w, res, tm, tn):
    T, K = a.shape
    N = w.shape[1]
    return pl.pallas_call(
        _mm_res_kernel,
        out_shape=jax.ShapeDtypeStruct((T, N), F32),
        grid=(T // tm, N // tn),
        in_specs=[pl.BlockSpec((tm, K), lambda i, j: (i, 0)),
                  pl.BlockSpec((K, tn), lambda i, j: (0, j)),
                  pl.BlockSpec((tm, tn), lambda i, j: (i, j))],
        out_specs=pl.BlockSpec((tm, tn), lambda i, j: (i, j)),
        compiler_params=_cparams(("parallel", "arbitrary")),
        name="matmul_residual",
    )(a, w, res)


def _mem_attn_kernel(h_ref, g_ref, wq_ref, kv_ref, wo_ref, o_ref):
    x = h_ref[0]
    ms = jnp.mean(x * x, axis=-1, keepdims=True)
    u = (x * lax.rsqrt(ms + EPS) * g_ref[...]).astype(BF16)
    q = jnp.dot(u, wq_ref[...], preferred_element_type=F32).astype(BF16)
    dn = (((1,), (1,)), ((), ()))
    hw = MEM_HEADS * MEM_DH
    heads = range(MEM_HEADS)
    cs = [slice(h * MEM_DH, (h + 1) * MEM_DH) for h in heads]
    ss = [lax.dot_general(q[:, cs[h]], kv_ref[0, :, cs[h]], dn, preferred_element_type=F32) for h in heads]
    es = [jnp.exp2(ss[h] - jnp.max(ss[h], axis=-1, keepdims=True)) for h in heads]
    ps = [es[h] / jnp.sum(es[h], axis=-1, keepdims=True) for h in heads]
    outs = [jnp.dot(ps[h].astype(BF16), kv_ref[0, :, hw + h * MEM_DH: hw + (h + 1) * MEM_DH],
                    preferred_element_type=F32).astype(BF16) for h in heads]
    o = jnp.concatenate(outs, axis=1)
    o_ref[0] = x + jnp.dot(o, wo_ref[...], preferred_element_type=F32)


def memory_attention_block(h3, g, wq, kv, wo, tq):
    B, S, D = h3.shape
    M = kv.shape[1]
    hw = MEM_HEADS * MEM_DH
    return pl.pallas_call(
        _mem_attn_kernel,
        out_shape=jax.ShapeDtypeStruct((B, S, D), F32),
        grid=(B, S // tq),
        in_specs=[pl.BlockSpec((1, tq, D), lambda b, i: (b, i, 0)),
                  pl.BlockSpec((1, D), lambda b, i: (0, 0)),
                  pl.BlockSpec((D, hw), lambda b, i: (0, 0)),
                  pl.BlockSpec((1, M, 2 * hw), lambda b, i: (b, 0, 0)),
                  pl.BlockSpec((hw, D), lambda b, i: (0, 0))],
        out_specs=pl.BlockSpec((1, tq, D), lambda b, i: (b, i, 0)),
        compiler_params=_cparams(("parallel", "arbitrary")),
        name="memory_attention",
    )(h3, g.reshape(1, D).astype(F32), wq, kv, wo)


def _dot3(a, b):
    a_hi = a.astype(BF16)
    a_lo = (a - a_hi.astype(F32)).astype(BF16)
    b_hi = b.astype(BF16)
    b_lo = (b - b_hi.astype(F32)).astype(BF16)
    dot = functools.partial(jnp.dot, preferred_element_type=F32)
    return dot(a_hi, b_hi) + (dot(a_lo, b_hi) + dot(a_hi, b_lo))


def _router_kernel(h_ref, g_ref, w_ref, b_ref, lg_ref, u_ref):
    x = h_ref[...]
    ms = jnp.mean(x * x, axis=-1, keepdims=True)
    u = x * lax.rsqrt(ms + EPS) * g_ref[...]
    u_ref[...] = u.astype(u_ref.dtype)
    lg_ref[...] = _dot3(u, w_ref[...]) + b_ref[...]


def moe_router(h, g, w, b, tm):
    T, D = h.shape
    N = w.shape[1]
    return pl.pallas_call(
        _router_kernel,
        out_shape=(jax.ShapeDtypeStruct((T, N), F32), jax.ShapeDtypeStruct((T, D), BF16)),
        grid=(T // tm,),
        in_specs=[pl.BlockSpec((tm, D), lambda i: (i, 0)),
                  pl.BlockSpec((1, D), lambda i: (0, 0)),
                  pl.BlockSpec((D, N), lambda i: (0, 0)),
                  pl.BlockSpec((1, N), lambda i: (0, 0))],
        out_specs=(pl.BlockSpec((tm, N), lambda i: (i, 0)), pl.BlockSpec((tm, D), lambda i: (i, 0))),
        compiler_params=_cparams(("parallel",)),
        name="moe_router",
    )(h, g.reshape(1, D).astype(F32), w, b)


def _route_kernel(lg_ref, tri_ref, o_ref, cnt_ref, carry_ref):
    @pl.when(pl.program_id(0) == 0)
    def _():
        carry_ref[...] = jnp.zeros(carry_ref.shape, carry_ref.dtype)

    lg = lg_ref[...]
    lane = lax.broadcasted_iota(jnp.int32, lg.shape, 1)
    first = lambda hit: jnp.min(jnp.where(hit, lane, LANE), axis=1, keepdims=True)
    gmask = lane < N_GROUPS
    gl = jnp.where(gmask, lg, -jnp.inf)
    gmax = jnp.max(gl, axis=1, keepdims=True)
    gsel = first(gl == gmax)
    pg = 1.0 / jnp.sum(jnp.where(gmask, jnp.exp(lg - gmax), 0.0), axis=1, keepdims=True)
    lo = N_GROUPS + gsel * EXPERTS_PER_GROUP
    emask = (lane >= lo) & (lane < lo + EXPERTS_PER_GROUP)
    el = jnp.where(emask, lg, -jnp.inf)
    emax = jnp.max(el, axis=1, keepdims=True)
    l0 = first(el == emax)
    esum = jnp.sum(jnp.where(emask, jnp.exp(lg - emax), 0.0), axis=1, keepdims=True)
    el2 = jnp.where(lane == l0, -jnp.inf, el)
    e2max = jnp.max(el2, axis=1, keepdims=True)
    l1 = first(el2 == e2max)
    p0 = 1.0 / esum
    p1 = jnp.exp(e2max - emax) / esum
    w0 = pg * (p0 / (p0 + p1))
    w1 = pg * (p1 / (p0 + p1))
    onehot = jnp.where((lane == l0) | (lane == l1), 1.0, 0.0)
    before = jnp.dot(tri_ref[...], onehot.astype(BF16), preferred_element_type=F32) + carry_ref[...]
    r0 = jnp.sum(jnp.where(lane == l0, before, 0.0), axis=1, keepdims=True)
    r1 = jnp.sum(jnp.where(lane == l1, before, 0.0), axis=1, keepdims=True)
    carry_ref[...] += jnp.sum(onehot, axis=0, keepdims=True)
    cnt_ref[...] = jnp.broadcast_to(carry_ref[...], cnt_ref.shape)
    vals = (l0 - N_GROUPS, l1 - N_GROUPS, r0, r1, w0, w1)
    out = jnp.zeros(lg.shape, F32)
    for k, v in enumerate(vals):
        out = jnp.where(lane == k, v.astype(F32), out)
    o_ref[...] = out


def moe_route(logits, tm):
    T, N = logits.shape
    tri = (jnp.arange(tm)[:, None] > jnp.arange(tm)[None, :]).astype(BF16)
    return pl.pallas_call(
        _route_kernel,
        out_shape=(jax.ShapeDtypeStruct((T, N), F32), jax.ShapeDtypeStruct((8, N), F32)),
        grid=(T // tm,),
        in_specs=[pl.BlockSpec((tm, N), lambda i: (i, 0)), pl.BlockSpec((tm, tm), lambda i: (0, 0))],
        out_specs=(pl.BlockSpec((tm, N), lambda i: (i, 0)), pl.BlockSpec((8, N), lambda i: (0, 0))),
        scratch_shapes=[pltpu.VMEM((1, N), F32)],
        compiler_params=_cparams(("arbitrary",)),
        name="moe_route",
    )(logits, tri)


def _moe_kernel(be_ref, nv_ref, nu_ref, x_ref, wg_ref, wu_ref, wd_ref, o_ref, acc_ref):
    i = pl.program_id(0)
    j = pl.program_id(1)
    nv = nv_ref[i]
    tm = x_ref.shape[0]

    @pl.when((i == 0) & (j == 0))
    def _():
        acc_ref[...] = jnp.zeros(acc_ref.shape, acc_ref.dtype)

    def run(rows):
        x = x_ref[:rows, :]
        a = jnp.dot(x, wg_ref[0, 0].astype(BF16), preferred_element_type=F32)
        b = jnp.dot(x, wu_ref[0, 0].astype(BF16), preferred_element_type=F32)
        hdn = (a * _sigmoid(a) * b).astype(BF16)
        y = jnp.dot(hdn, wd_ref[0, 0].astype(BF16), preferred_element_type=F32)
        tot = y + jnp.where(j > 0, acc_ref[:rows, :], 0.0)
        acc_ref[:rows, :] = tot
        o_ref[:rows, :] = tot.astype(o_ref.dtype)
        if rows < tm:
            o_ref[rows:, :] = jnp.zeros((tm - rows, o_ref.shape[1]), o_ref.dtype)

    for rows in range(MOE_SUB, tm + 1, MOE_SUB):
        pl.when((nv > rows - MOE_SUB) & (nv <= rows))(functools.partial(run, rows))

    @pl.when(nv == 0)
    def _():
        o_ref[...] = jnp.zeros(o_ref.shape, o_ref.dtype)


def moe_experts(blk_e, n_valid, n_used, xr, wg, wu, wd, layer):
    P, D = xr.shape
    De = wg.shape[3]
    tm, dc = MOE_TM, MOE_DC
    nj = De // dc
    chunk = lambda i, s: jnp.where(i % 2 == 0, s, nj - 1 - s)
    jj = lambda i, s, nu: jnp.where(i < nu[0], chunk(i, s), chunk(nu[0] - 1, nj - 1))
    grid_spec = pltpu.PrefetchScalarGridSpec(
        num_scalar_prefetch=3,
        grid=(P // tm, nj),
        in_specs=[pl.BlockSpec((tm, D), lambda i, j, be, nv, nu: (i, 0)),
                  pl.BlockSpec((1, 1, D, dc), lambda i, j, be, nv, nu: (layer, be[i], 0, jj(i, j, nu))),
                  pl.BlockSpec((1, 1, D, dc), lambda i, j, be, nv, nu: (layer, be[i], 0, jj(i, j, nu))),
                  pl.BlockSpec((1, 1, dc, D), lambda i, j, be, nv, nu: (layer, be[i], jj(i, j, nu), 0))],
        out_specs=pl.BlockSpec((tm, D), lambda i, j, be, nv, nu: (i, 0)),
        scratch_shapes=[pltpu.VMEM((tm, D), F32)],
    )
    return pl.pallas_call(
        _moe_kernel,
        out_shape=jax.ShapeDtypeStruct((P, D), BF16),
        grid_spec=grid_spec,
        compiler_params=_cparams(("arbitrary", "arbitrary")),
        name="moe_experts",
    )(blk_e, n_valid, n_used, xr, wg, wu, wd)


def _combine_kernel(h_ref, y0_ref, y1_ref, w_ref, g_ref, o_ref, *, final_norm):
    w = w_ref[...]
    x = h_ref[...] + (w[:, 0:1] * y0_ref[...].astype(F32) + w[:, 1:2] * y1_ref[...].astype(F32))
    if final_norm:
        ms = jnp.mean(x * x, axis=-1, keepdims=True)
        x = x * lax.rsqrt(ms + EPS) * g_ref[...]
    o_ref[...] = x


def moe_combine(h, y0, y1, w, g, tm):
    T, D = h.shape
    final_norm = g is not None
    gain = (g if final_norm else jnp.ones((D,), F32)).reshape(1, D).astype(F32)
    row = lambda width: pl.BlockSpec((tm, width), lambda i: (i, 0))
    return pl.pallas_call(
        functools.partial(_combine_kernel, final_norm=final_norm),
        out_shape=jax.ShapeDtypeStruct((T, D), F32),
        grid=(T // tm,),
        in_specs=[row(D), row(D), row(D), row(LANE), pl.BlockSpec((1, D), lambda i: (0, 0))],
        out_specs=row(D),
        compiler_params=_cparams(("parallel",)),
        name="moe_combine",
    )(h, y0, y1, _pad_cols(w, LANE), gain)


def _pad_cols(w, width):
    return jnp.pad(w, ((0, 0), (0, width - w.shape[1])))


def _w_in_segments(D):
    names = ("fq", "fk", "fv", "ff", "mcq", "mckv", "mkr", "nq", "nkc", "nvc", "nks", "nvs", "nkw", "nvw", "ngt", "mg")
    widths = (1024, 1024, 1024, FORGET_COLS, MLA_Q_RANK, MLA_KV_RANK, MLA_ROPE, NSA_HEADS * NSA_DK,
              NSA_GROUPS * NSA_DK, NSA_GROUPS * NSA_DV, NSA_GROUPS * NSA_DK, NSA_GROUPS * NSA_DV,
              NSA_GROUPS * NSA_DK, NSA_GROUPS * NSA_DV, NSA_GATE_COLS, N_BRANCH * D)
    src = dict(zip(names, np.cumsum((0,) + widths[:-1]).tolist()))
    wid = dict(zip(names, widths))
    segs = []
    plain = lambda name, cb, f=1.0: segs.append((cb * LANE, src[name], wid[name], f))

    def padded_k(name, cb, n, f=1.0):
        for i in range(n):
            segs.append((cb * LANE + i * NSA_DKP, src[name] + i * NSA_DK, NSA_DK, f))

    half = MLA_ROPE // 2
    plain("mckv", CB_CKV)
    plain("mkr", CB_KR)
    segs.append((CB_KR * LANE + MLA_ROPE, src["mkr"] + half, half, -1.0))
    segs.append((CB_KR * LANE + MLA_ROPE + half, src["mkr"], half, 1.0))
    plain("mcq", CB_CQ)
    padded_k("nkc", CB_NKC, NSA_GROUPS)
    padded_k("nq", CB_NQ, NSA_HEADS, NSA_DK ** -0.5 * LOG2E)
    plain("fq", CB_FQ, FOX_DH ** -0.5 * LOG2E)
    plain("fk", CB_FK)
    plain("fv", CB_FV)
    padded_k("nks", CB_NKS, NSA_GROUPS)
    padded_k("nkw", CB_NKW, NSA_GROUPS)
    plain("nvc", CB_NVC)
    plain("nvs", CB_NVS)
    plain("nvw", CB_NVW)
    plain("mg", CB_MG)
    return segs, src


def _pack_w_in_kernel(w_ref, o_ref, os_ref, *, segs, small_segs):
    d_in = w_ref.shape[1]

    def put(dst_ref, dst, s, n, f):
        for k in range(0, n, LANE):
            m = min(LANE, n - k)
            assert s + k + LANE <= d_in
            v = w_ref[0, s + k:s + k + LANE, :].T
            if f != 1.0:
                v = v * f
            dst_ref[:, dst + k:dst + k + m] = v[:, :m].astype(dst_ref.dtype)

    o_ref[...] = jnp.zeros(o_ref.shape, o_ref.dtype)
    for dst, s, n, f in segs:
        put(o_ref, dst, s, n, f)
    os_ref[...] = jnp.zeros(os_ref.shape, os_ref.dtype)
    for dst, s, n in small_segs:
        put(os_ref, dst, s, n, 1.0)


def _pack_w_in(w_all, layer):
    _, D, d_in = w_all.shape
    segs, src = _w_in_segments(D)
    small_segs = ((0, src["ff"], FORGET_COLS), (FORGET_COLS, src["ngt"], NSA_GATE_COLS))
    tr = LANE
    return pl.pallas_call(
        functools.partial(_pack_w_in_kernel, segs=tuple(segs), small_segs=small_segs),
        out_shape=(jax.ShapeDtypeStruct((D, Z_BLOCKS * LANE), BF16), jax.ShapeDtypeStruct((D, LANE), F32)),
        grid=(D // tr,),
        in_specs=[pl.BlockSpec((1, d_in, tr), lambda i: (layer, 0, i))],
        out_specs=(pl.BlockSpec((tr, Z_BLOCKS * LANE), lambda i: (i, 0)), pl.BlockSpec((tr, LANE), lambda i: (i, 0))),
        compiler_params=_cparams(("parallel",)),
        name="pack_w_in",
    )(jnp.swapaxes(w_all, 1, 2))


def _pack_w_uq(w):
    K = w.shape[0]
    w3 = w.reshape(K, MLA_HEADS, MLA_NOPE + MLA_ROPE) * ((MLA_NOPE + MLA_ROPE) ** -0.5 * LOG2E)
    nope = w3[:, :, :MLA_NOPE].reshape(K, MLA_HEADS * MLA_NOPE)
    r = w3[:, :, MLA_NOPE:]
    half = MLA_ROPE // 2
    r_rot = jnp.concatenate([-r[:, :, half:], r[:, :, :half]], axis=2)
    padr = lambda a: jnp.pad(a, ((0, 0), (0, 0), (0, LANE - MLA_ROPE))).reshape(K, MLA_HEADS * LANE)
    return jnp.concatenate([nope, padr(r), padr(r_rot)], axis=1).astype(BF16)


def _pack_w_ukv(w):
    K = w.shape[0]
    w3 = w.reshape(K, MLA_HEADS, MLA_NOPE + MLA_DV)
    return jnp.concatenate([w3[:, :, :MLA_NOPE].reshape(K, -1), w3[:, :, MLA_NOPE:].reshape(K, -1)], axis=1).astype(BF16)


def _t5_bucket(dist):
    dist = jnp.maximum(dist, 0)
    exact = REL_BUCKETS // 2
    df = jnp.maximum(dist, 1).astype(F32)
    large = exact + (jnp.log(df / exact) / math.log(REL_MAX_DIST / exact) * (REL_BUCKETS - exact)).astype(jnp.int32)
    large = jnp.minimum(large, REL_BUCKETS - 1)
    return jnp.where(dist < exact, dist, large)


def _position_tables(S, rel_bias):
    t = NSA_T
    half = MLA_ROPE // 2
    inv = ROPE_THETA ** (-jnp.arange(half, dtype=F32) / half)
    ang = jnp.arange(S, dtype=F32)[:, None] * inv
    c, s = jnp.cos(ang), jnp.sin(ang)
    cos = _pad_cols(jnp.concatenate([c, c], axis=1), LANE)
    sin = _pad_cols(jnp.concatenate([s, s], axis=1), LANE)
    ncp = max(S // CMP_STRIDE, LANE)
    pos = jnp.arange(S)

    def bias_of(dist):
        onehot = jax.nn.one_hot(_t5_bucket(dist), REL_BUCKETS, dtype=F32)
        return jnp.einsum("...b,bh->h...", onehot, rel_bias, precision=lax.Precision.HIGHEST)

    bias_c = bias_of(pos[None, :] - (CMP_STRIDE * jnp.arange(ncp)[:, None] + CMP_BLOCK - 1))
    i = jnp.arange(t)
    bias_sw = jnp.stack([bias_of(k * t + i[:, None] - i[None, :]) for k in range(3)], axis=1)
    bias_sw = bias_sw.reshape(NSA_GROUPS, NSA_HPG, 3, t, t).transpose(0, 2, 4, 1, 3).reshape(NSA_GROUPS, 3, t, NSA_HPG * t)
    n_cmp = (S - CMP_BLOCK) // CMP_STRIDE + 1
    n_blk = S // SEL_BLOCK
    nbp = max(n_blk, LANE)
    cstart = CMP_STRIDE * jnp.arange(ncp)
    sstart = SEL_BLOCK * jnp.arange(nbp)
    ov = jnp.clip(jnp.minimum(cstart[:, None] + CMP_BLOCK, sstart[None, :] + SEL_BLOCK)
                  - jnp.maximum(cstart[:, None], sstart[None, :]), 0, None).astype(F32) / CMP_STRIDE
    ov = jnp.where((jnp.arange(ncp)[:, None] < n_cmp) & (jnp.arange(nbp)[None, :] < n_blk), ov, 0.0).astype(BF16)
    expand = ((pos[:, None] // SEL_BLOCK) == jnp.arange(nbp)[None, :]).astype(BF16)
    return cos, sin, bias_c, bias_sw, ov, expand


def _token_mixers(h, p, w_in_all, layer, tabs, B, S):
    T, D = h.shape
    cos, sin, bias_c, bias_sw, overlap, expand = tabs
    wz, w_small = _pack_w_in(w_in_all, layer)
    z, zs = input_projection(h, p["g_mix"], wz, w_small, tm=min(T, 1024), tn=IN_PROJ_TN)
    z3 = z.reshape(B, S, Z_BLOCKS * LANE)

    log_f = jax.nn.log_sigmoid(zs[:, :FORGET_COLS] + p["b_forget"].astype(F32)).reshape(B, S, FOX_HEADS)
    cum = jnp.cumsum(log_f, axis=1) * LOG2E
    o_fox = causal_attention(z3, CB_FQ, z3, CB_FK, z3, CB_FV, FOX_HEADS, cum=cum)

    q_nope, q_rope = mla_q_proj(z, p["g_cq"], _pack_w_uq(p["w_uq"]), cos, sin, S, tm=min(S, 512))
    kv = rms_matmul(z, CB_CKV * LANE // MLA_KV_RANK, p["g_ckv"], _pack_w_ukv(p["w_ukv"]), BF16, tm=min(T, 1024), tn=1024)
    kr = z3[:, :, CB_KR * LANE:(CB_KR + 1) * LANE].astype(F32)
    kr = kr[..., :MLA_ROPE] * cos[None, :, :MLA_ROPE] + kr[..., MLA_ROPE:] * sin[None, :, :MLA_ROPE]
    k_rope = jnp.pad(kr, ((0, 0), (0, 0), (0, LANE - MLA_ROPE))).astype(BF16)
    hw = MLA_HEADS * LANE
    kv3 = kv.reshape(B, S, 2 * hw)
    o_mla = causal_attention(q_nope.reshape(B, S, hw), 0, kv3, 0, kv3, MLA_HEADS, MLA_HEADS,
                             q2=q_rope.reshape(B, S, hw), k2=k_rope)

    G = NSA_GROUPS
    NC = S // CMP_STRIDE
    ncp = bias_c.shape[1]

    def compress_branch(cb, dp, d, pe, w1, w2):
        x = z3[:, :, cb * LANE: cb * LANE + G * dp].reshape(B, NC, CMP_STRIDE * G * dp)
        eye = jnp.eye(G, dtype=F32)
        w1p = jnp.pad(w1.reshape(CMP_BLOCK, d, d), ((0, 0), (0, dp - d), (0, dp - d)))
        w1g = jnp.einsum("lij,gh->lgihj", w1p, eye).reshape(CMP_BLOCK, G * dp, G * dp).astype(BF16)
        w1a = w1g[:CMP_STRIDE].reshape(CMP_STRIDE * G * dp, G * dp)
        w1b = w1g[CMP_STRIDE:].reshape(CMP_STRIDE * G * dp, G * dp)
        pe_g = jnp.tile(jnp.pad(pe, ((0, 0), (0, dp - d)))[:, None, :], (1, G, 1))
        pe2 = pe_g.reshape(2, CMP_STRIDE * G * dp).astype(BF16)
        w2p = jnp.pad(w2, ((0, dp - d), (0, dp - d)))
        w2g = jnp.einsum("ij,gh->gihj", w2p, eye).reshape(G * dp, G * dp).astype(BF16)
        out = compress(x, w1a, w1b, pe2, w2g)
        return jnp.pad(out, ((0, 0), (0, ncp - NC), (0, 0)))

    kc = compress_branch(CB_NKC, NSA_DKP, NSA_DK, p["pe_k"], p["w_cmp_k1"], p["w_cmp_k2"])
    vc = compress_branch(CB_NVC, NSA_DV, NSA_DV, p["pe_v"], p["w_cmp_v1"], p["w_cmp_v2"])
    gl = zs[:, FORGET_COLS:FORGET_COLS + NSA_GATE_COLS].reshape(B, S, G, NSA_HPG * 3)
    gl = jnp.transpose(gl, (0, 2, 1, 3))
    gl_cols = jnp.pad(jnp.swapaxes(gl, 2, 3), ((0, 0), (0, 0), (0, 16 - NSA_HPG * 3), (0, 0)))
    o_cmp, ind = nsa_cmp_select(z3, kc, vc, bias_c, overlap, gl_cols, min(N_SEL, S // SEL_BLOCK))
    o_nsa = nsa_sel_win(z3, ind, expand, bias_sw, gl_cols, o_cmp)

    merged = merge_branches(o_fox.reshape(T, -1), o_mla.reshape(T, -1), o_nsa.reshape(T, -1),
                            p["w_branch"].astype(BF16), z, tm=min(T, 1024), tn=512)
    return matmul_residual(merged, p["w_out"].astype(BF16), h, tm=min(T, 512), tn=2048)


def _memory_block(h, mem2, p, B, S):
    T, D = h.shape
    kv = rms_matmul(mem2, 0, p["g_mem_kv"], p["w_mem_kv"].astype(BF16), BF16, tm=min(mem2.shape[0], 512), tn=512)
    out = memory_attention_block(h.reshape(B, S, D), p["g_mem_q"], (p["w_mem_q"] * (MEM_DH ** -0.5 * LOG2E)).astype(BF16),
                                 kv.reshape(B, -1, kv.shape[1]), p["w_mem_o"].astype(BF16), tq=min(S, 512))
    return out.reshape(T, D)


def _moe_block(h, p, experts, layer, g_final=None):
    T, D = h.shape
    tm = MOE_TM
    w_r = _pad_cols(jnp.concatenate([p["w_router_group"], p["w_router_expert"]], axis=1), LANE).astype(F32)
    b_r = _pad_cols(jnp.concatenate([p["b_router_group"], p["b_router_expert"]])[None, :], LANE).astype(F32)
    logits, u = moe_router(h, p["g_moe"], w_r, b_r, tm=min(T, 512))
    routed, totals = moe_route(logits, tm=min(T, 512))
    assert TOP_K == 2
    flat_e = routed[:, 0:2].astype(jnp.int32).reshape(-1)
    rank = routed[:, 2:4].astype(jnp.int32).reshape(-1)
    weight = routed[:, 4:6]
    counts = totals[0, N_GROUPS:N_GROUPS + N_EXPERTS].astype(jnp.int32)
    TK = T * TOP_K
    pcounts = ((counts + tm - 1) // tm) * tm
    pends = jnp.cumsum(pcounts)
    dest = (pends - pcounts)[flat_e] + rank
    n_rb = -(-TK // tm) + N_EXPERTS
    P = n_rb * tm
    row_tok = (jnp.arange(P, dtype=jnp.int32) % T).at[dest].set(jnp.repeat(jnp.arange(T, dtype=jnp.int32), TOP_K))
    blk_e = jnp.sum((pends[None, :] <= (jnp.arange(n_rb, dtype=jnp.int32) * tm)[:, None]).astype(jnp.int32), axis=1)
    blk_e = jnp.minimum(blk_e, N_EXPERTS - 1).astype(jnp.int32)
    n_used = (pends[-1] // tm).astype(jnp.int32).reshape(1)
    row_end = pends - pcounts + counts
    n_valid = jnp.clip(row_end[blk_e] - jnp.arange(n_rb, dtype=jnp.int32) * tm, 0, tm).astype(jnp.int32)
    xr = u[row_tok]
    y = moe_experts(blk_e, n_valid, n_used, xr, experts[0].astype(F32), experts[1].astype(F32),
                    experts[2].astype(F32), layer)
    d2 = dest.reshape(T, TOP_K)
    return moe_combine(h, y[d2[:, 0]], y[d2[:, 1]], weight, g_final, tm=min(T, 512))


_LAYER_KEYS = ("g_mix", "w_in", "b_forget", "g_cq", "g_ckv", "w_uq", "w_ukv", "pe_k", "pe_v", "w_cmp_k1", "w_cmp_k2",
               "w_cmp_v1", "w_cmp_v2", "w_branch", "w_out", "g_mem_q", "g_mem_kv", "w_mem_q", "w_mem_kv", "w_mem_o",
               "g_moe", "w_router_group", "b_router_group", "w_router_expert", "b_router_expert")


def kernel(x, mem, g_mix, w_in, b_forget, g_cq, g_ckv, w_uq, w_ukv, pe_k, pe_v, w_cmp_k1, w_cmp_k2, w_cmp_v1, w_cmp_v2, rel_bias, w_branch, w_out, g_mem_q, g_mem_kv, w_mem_q, w_mem_kv, w_mem_o, g_moe, w_router_group, b_router_group, w_router_expert, b_router_expert, w_exp_gate, w_exp_up, w_exp_down, g_final):
    B, S, D = x.shape
    T = B * S
    stacked = dict(g_mix=g_mix, w_in=w_in, b_forget=b_forget, g_cq=g_cq, g_ckv=g_ckv, w_uq=w_uq, w_ukv=w_ukv,
                   pe_k=pe_k, pe_v=pe_v, w_cmp_k1=w_cmp_k1, w_cmp_k2=w_cmp_k2, w_cmp_v1=w_cmp_v1, w_cmp_v2=w_cmp_v2,
                   w_branch=w_branch, w_out=w_out, g_mem_q=g_mem_q, g_mem_kv=g_mem_kv, w_mem_q=w_mem_q,
                   w_mem_kv=w_mem_kv, w_mem_o=w_mem_o, g_moe=g_moe, w_router_group=w_router_group,
                   b_router_group=b_router_group, w_router_expert=w_router_expert, b_router_expert=b_router_expert,
                   w_exp_gate=w_exp_gate, w_exp_up=w_exp_up, w_exp_down=w_exp_down)
    tabs = _position_tables(S, rel_bias.astype(F32) * LOG2E)
    h = x.reshape(T, D).astype(F32)
    mem2 = mem.reshape(-1, D).astype(F32)
    depth = w_in.shape[0]
    assert depth >= 1
    for l in range(depth):
        p = {k: stacked[k][l] for k in _LAYER_KEYS}
        h = _token_mixers(h, p, w_in.astype(F32), l, tabs, B, S)
        h = _memory_block(h, mem2, p, B, S)
        h = _moe_block(h, p, (w_exp_gate, w_exp_up, w_exp_down), l, g_final if l == depth - 1 else None)
    return h.reshape(B, S, D)
```
